```python
import math
import jax
import jax.numpy as jnp
from jax import lax
import numpy as np

D_MODEL = 2048
BATCH = 1
SEQ = 8192
DEPTH = 4

N_A_LAYERS = DEPTH // 2
N_B_LAYERS = DEPTH - N_A_LAYERS
RW_HEAD = 64
RW_HEADS = D_MODEL // RW_HEAD
LORA_DECAY = 96
LORA_A = 128
LORA_V = 64
LORA_G = 256
N_SHIFT_MIX = 6
GN_EPS = 64e-5
NSA_HEADS = 16
NSA_GROUPS = 4
NSA_REP = NSA_HEADS // NSA_GROUPS
NSA_HD = D_MODEL // NSA_HEADS
N_KV_PARTS = 6
CMP_STRIDE = 16
CMP_LEN = 2 * CMP_STRIDE
SEL_BLOCK = 64
SEL_TOP = 16
WINDOW = 512
QBLK = 128
N_BRANCH = 3
SEL_FORCE = 1e4
NEG_INF = -1e30
REL_BUCKETS = 32
REL_MAX_DIST = 128
D_FF = 4 * D_MODEL
NORM_EPS = 1e-6

kernel_name = 'rwkv7_nsa_yoco_hybrid'


def rmsnorm(x, g):
    xf = x.astype(jnp.float32)
    y = xf * lax.rsqrt(jnp.mean(xf * xf, axis=-1, keepdims=True) + NORM_EPS)
    return (y * g.astype(jnp.float32)).astype(x.dtype)


def t5_bucket(dist):
    n = jnp.maximum(dist, 0)
    max_exact = REL_BUCKETS // 2
    nf = jnp.maximum(n, 1).astype(jnp.float32)
    large = max_exact + (jnp.log(nf / max_exact) / math.log(REL_MAX_DIST / max_exact)
                         * (REL_BUCKETS - max_exact)).astype(jnp.int32)
    large = jnp.minimum(large, REL_BUCKETS - 1)
    return jnp.where(n < max_exact, n, large)


def sqrelu_mlp(xn, w_up, w_down):
    return jnp.square(jax.nn.relu(xn @ w_up)) @ w_down


def _rwkv7_step(S, inp):
    r, w, k, v, a, b = inp
    sa = jnp.einsum('bhij,bhj->bhi', S, a)
    S = S * w[:, :, None, :] + sa[..., None] * b[:, :, None, :] + v[..., None] * k[:, :, None, :]
    return S, jnp.einsum('bhij,bhj->bhi', S, r)


def rwkv7_time_mix(xn, mu, wr, wk, wv, wo, w0, w1, w2, a0, a1, a2, g1, g2,
                   k_k, k_a, r_k, lnx_w, lnx_b, v_first, v0, v1, v2):
    B, T, C = xn.shape
    H, N = RW_HEADS, RW_HEAD
    f32 = jnp.float32
    xx = jnp.pad(xn[:, :-1], ((0, 0), (1, 0), (0, 0))) - xn
    xr, xw, xk, xv, xa, xg = [xn + xx * mu[i] for i in range(N_SHIFT_MIX)]
    r = xr @ wr
    w = -jax.nn.softplus(-(w0 + jnp.tanh(xw @ w1) @ w2)) - 0.5
    k = xk @ wk
    v = xv @ wv
    if v_first is None:
        v_first = v
    else:
        v = v + (v_first - v) * jax.nn.sigmoid(v0 + (xv @ v1) @ v2)
    a = jax.nn.sigmoid(a0 + (xa @ a1) @ a2)
    g = jax.nn.sigmoid(xg @ g1) @ g2
    kk = (k * k_k).astype(f32).reshape(B, T, H, N)
    kk = kk * lax.rsqrt(jnp.maximum(jnp.sum(kk * kk, -1, keepdims=True), 1e-24))
    k = k * (1.0 + (a - 1.0) * k_a)

    def heads(u):
        return u.astype(f32).reshape(B, T, H, N)

    rf, kf, vf, af = heads(r), heads(k), heads(v), heads(a)
    decay = jnp.exp(-jnp.exp(heads(w)))
    xs = tuple(jnp.moveaxis(u, 1, 0) for u in (rf, decay, kf, vf, -kk, kk * af))
    S0 = jnp.zeros((B, H, N, N), f32)
    _, y = lax.scan(_rwkv7_step, S0, xs)
    y = jnp.moveaxis(y, 0, 1)
    mean = jnp.mean(y, -1, keepdims=True)
    var = jnp.mean(jnp.square(y - mean), -1, keepdims=True)
    y = ((y - mean) * lax.rsqrt(var + GN_EPS)).reshape(B, T, C) * lnx_w + lnx_b
    bonus = (jnp.sum(rf * kf * r_k, -1, keepdims=True) * vf).reshape(B, T, C)
    out = ((y + bonus) * g).astype(xn.dtype) @ wo
    return out, v_first


def compress_blocks(u, pe, w1, b1, w2):
    B, T, G, D = u.shape
    u16 = u.reshape(B, T // CMP_STRIDE, CMP_STRIDE, G, D)
    blocks = jnp.concatenate([u16[:, :-1], u16[:, 1:]], axis=2) + pe[None, None, :, None, :]
    flat = blocks.transpose(0, 1, 3, 2, 4).reshape(B, T // CMP_STRIDE - 1, G, CMP_LEN * D)
    return jax.nn.gelu(flat @ w1 + b1) @ w2


def nsa_shared_kv(h, kv_norm_g, w_kv, cmp_pe, cmp_w1, cmp_b1, cmp_w2):
    B, T, _ = h.shape
    G, D = NSA_GROUPS, NSA_HD
    kv = (rmsnorm(h, kv_norm_g) @ w_kv).reshape(B, T, N_KV_PARTS, G, D)
    kc = compress_blocks(kv[:, :, 0], cmp_pe[0], cmp_w1[0], cmp_b1[0], cmp_w2[0])
    vc = compress_blocks(kv[:, :, 1], cmp_pe[1], cmp_w1[1], cmp_b1[1], cmp_w2[1])
    n_s = T // SEL_BLOCK
    ks = kv[:, :, 2].reshape(B, n_s, SEL_BLOCK, G, D).transpose(0, 3, 1, 2, 4)
    vs = kv[:, :, 3].reshape(B, n_s, SEL_BLOCK, G, D).transpose(0, 3, 1, 2, 4)
    pad = ((0, 0), (WINDOW, 0), (0, 0), (0, 0))
    kw = jnp.pad(kv[:, :, 4], pad)
    vw = jnp.pad(kv[:, :, 5], pad)
    return kc, vc, ks, vs, kw, vw


def nsa_attention(hn, wq, wo, rel_bias, kc, vc, ks, vs, kw, vw):
    B, T, _ = hn.shape
    G, R, D = NSA_GROUPS, NSA_REP, NSA_HD
    f32 = jnp.float32
    n_c = kc.shape[1]
    n_s = ks.shape[2]
    top = min(SEL_TOP, n_s)
    n_qb = T // QBLK
    proj = hn @ wq
    q = (proj[..., :NSA_HEADS * D] * (D ** -0.5)).reshape(B, n_qb, QBLK, G, R, D).transpose(1, 0, 2, 3, 4, 5)
    gates = jax.nn.sigmoid(proj[..., NSA_HEADS * D:].astype(f32)).reshape(
        B, n_qb, QBLK, G, R, N_BRANCH).transpose(1, 0, 2, 3, 4, 5)
    table = rel_bias.astype(f32)
    table_g = table.reshape(REL_BUCKETS, G, R).transpose(1, 0, 2)
    cmp_end = jnp.arange(n_c) * CMP_STRIDE + (CMP_LEN - 1)
    blk_start = jnp.arange(n_s) * SEL_BLOCK
    overlap = (((cmp_end[:, None] - (CMP_LEN - 1)) <= (blk_start[None, :] + SEL_BLOCK - 1))
               & (cmp_end[:, None] >= blk_start[None, :])).astype(f32)
    b_idx = jnp.arange(B)[:, None, None, None]
    g_idx = jnp.arange(G)[None, :, None, None]
    sel_off = jnp.arange(SEL_BLOCK)

    def head_bias(dist):
        bias = table[t5_bucket(dist)]
        return bias.reshape(dist.shape + (G, R)).transpose(2, 3, 0, 1)

    def block(args):
        qb, gb, qi = args
        t = qi * QBLK + jnp.arange(QBLK)
        m_c = cmp_end[None, :] <= t[:, None]
        s_c = jnp.einsum('bqgrd,bcgd->bgrqc', qb, kc).astype(f32) + head_bias(t[:, None] - cmp_end[None, :])
        p_c = jnp.where(m_c, jax.nn.softmax(jnp.where(m_c, s_c, NEG_INF), axis=-1), 0.0)
        o_c = jnp.einsum('bgrqc,bcgd->bqgrd', p_c, vc)
        imp = jnp.einsum('bgrqc,cs->bgqs', p_c, overlap)
        blk_ok = blk_start[None, :] <= t[:, None]
        forced = (blk_start[None, :] == (t[:, None] // SEL_BLOCK) * SEL_BLOCK) | (blk_start[None, :] == 0)
        score = jnp.where(forced, SEL_FORCE, jnp.where(blk_ok, imp, -SEL_FORCE))
        _, idx = lax.top_k(score, top)
        k_sel = ks[b_idx, g_idx, idx]
        v_sel = vs[b_idx, g_idx, idx]
        dist_s = t[None, None, :, None, None] - (idx[..., None] * SEL_BLOCK + sel_off)
        m_s = dist_s >= 0
        bias_s = jnp.moveaxis(table_g[g_idx[..., None], t5_bucket(dist_s)], -1, 2)
        s_s = jnp.einsum('bqgrd,bgqkld->bgrqkl', qb, k_sel).astype(f32) + bias_s
        s_s = jnp.where(m_s[:, :, None], s_s, NEG_INF).reshape(B, G, R, QBLK, top * SEL_BLOCK)
        p_s = jax.nn.softmax(s_s, axis=-1).reshape(B, G, R, QBLK, top, SEL_BLOCK)
        o_s = jnp.einsum('bgrqkl,bgqkld->bqgrd', p_s, v_sel)
        kwb = lax.dynamic_slice_in_dim(kw, qi * QBLK, WINDOW + QBLK, axis=1)
        vwb = lax.dynamic_slice_in_dim(vw, qi * QBLK, WINDOW + QBLK, axis=1)
        kpos = qi * QBLK - WINDOW + jnp.arange(WINDOW + QBLK)
        dist_w = t[:, None] - kpos[None, :]
        m_w = (dist_w >= 0) & (dist_w < WINDOW) & (kpos[None, :] >= 0)
        s_w = jnp.einsum('bqgrd,bkgd->bgrqk', qb, kwb).astype(f32) + head_bias(dist_w)
        p_w = jax.nn.softmax(jnp.where(m_w, s_w, NEG_INF), axis=-1)
        o_w = jnp.einsum('bgrqk,bkgd->bqgrd', p_w, vwb)
        o = gb[..., 0:1] * o_c + gb[..., 1:2] * o_s + gb[..., 2:3] * o_w
        return o.astype(hn.dtype)

    out = lax.map(block, (q, gates, jnp.arange(n_qb)))
    out = out.transpose(1, 0, 2, 3, 4, 5).reshape(B, T, NSA_HEADS * D)
    return out @ wo


def setup_inputs(seed: int = 0) -> dict:
    key = jax.random.key(seed)
    keys = iter(jax.random.split(key, 48))
    f32 = jnp.float32

    def nrm(shape, scale):
        return jax.random.normal(next(keys), shape, f32) * scale

    def gain(shape):
        return 1.0 + 0.02 * jax.random.normal(next(keys), shape, f32)

    def unif(shape, lo, hi):
        return jax.random.uniform(next(keys), shape, f32, lo, hi)

    nA, nB, D = N_A_LAYERS, N_B_LAYERS, D_MODEL
    H, N = RW_HEADS, RW_HEAD
    HB, G, HD = NSA_HEADS, NSA_GROUPS, NSA_HD
    sD = D ** -0.5
    return {
        'x': nrm((BATCH, SEQ, D), 1.0),
        'a_norm_g': gain((nA, D)),
        'rw_mu': unif((nA, N_SHIFT_MIX, D), 0.0, 1.0),
        'rw_wr': nrm((nA, D, D), sD),
        'rw_wk': nrm((nA, D, D), sD),
        'rw_wv': nrm((nA, D, D), sD),
        'rw_wo': nrm((nA, D, D), 0.5 * sD),
        'rw_w0': unif((nA, D), -6.0, -1.0),
        'rw_w1': nrm((nA, D, LORA_DECAY), sD),
        'rw_w2': nrm((nA, LORA_DECAY, D), 0.1 * LORA_DECAY ** -0.5),
        'rw_a0': nrm((nA, D), 0.1),
        'rw_a1': nrm((nA, D, LORA_A), sD),
        'rw_a2': nrm((nA, LORA_A, D), 0.1 * LORA_A ** -0.5),
        'rw_v0': nrm((nA - 1, D), 0.1),
        'rw_v1': nrm((nA - 1, D, LORA_V), sD),
        'rw_v2': nrm((nA - 1, LORA_V, D), 0.1 * LORA_V ** -0.5),
        'rw_g1': nrm((nA, D, LORA_G), sD),
        'rw_g2': nrm((nA, LORA_G, D), LORA_G ** -0.5),
        'rw_kk': 0.85 + nrm((nA, D), 0.02),
        'rw_ka': gain((nA, D)),
        'rw_rk': nrm((nA, H, N), 0.1),
        'rw_lnx_w': gain((nA, D)),
        'rw_lnx_b': nrm((nA, D), 0.02),
        'kv_norm_g': gain((D,)),
        'w_kv': nrm((D, N_KV_PARTS * G * HD), sD),
        'cmp_pe': nrm((2, CMP_LEN, HD), 0.1),
        'cmp_w1': nrm((2, CMP_LEN * HD, HD), (CMP_LEN * HD) ** -0.5),
        'cmp_b1': nrm((2, HD), 0.02),
        'cmp_w2': nrm((2, HD, HD), HD ** -0.5),
        'b_norm_g': gain((nB, D)),
        'nsa_wq': nrm((nB, D, HB * HD + N_BRANCH * HB), sD),
        'nsa_wo': nrm((nB, HB * HD, D), 0.5 * (HB * HD) ** -0.5),
        'rel_bias': nrm((REL_BUCKETS, HB), 0.5),
        'm_norm_g': gain((DEPTH, D)),
        'mlp_up': nrm((DEPTH, D, D_FF), sD),
        'mlp_down': nrm((DEPTH, D_FF, D), 0.5 * D_FF ** -0.5),
        'final_norm_g': gain((D,)),
    }


def reference(x, a_norm_g, rw_mu, rw_wr, rw_wk, rw_wv, rw_wo, rw_w0, rw_w1, rw_w2,
              rw_a0, rw_a1, rw_a2, rw_v0, rw_v1, rw_v2, rw_g1, rw_g2, rw_kk, rw_ka, rw_rk,
              rw_lnx_w, rw_lnx_b, kv_norm_g, w_kv, cmp_pe, cmp_w1, cmp_b1, cmp_w2,
              b_norm_g, nsa_wq, nsa_wo, rel_bias, m_norm_g, mlp_up, mlp_down, final_norm_g):
    h = x
    v_first = None
    for layer in range(DEPTH):
        if layer < N_A_LAYERS:
            i = layer
            if i == 0:
                v0, v1, v2 = None, None, None
            else:
                v0, v1, v2 = rw_v0[i - 1], rw_v1[i - 1], rw_v2[i - 1]
            y, v_first = rwkv7_time_mix(
                rmsnorm(h, a_norm_g[i]), rw_mu[i], rw_wr[i], rw_wk[i], rw_wv[i], rw_wo[i],
                rw_w0[i], rw_w1[i], rw_w2[i], rw_a0[i], rw_a1[i], rw_a2[i], rw_g1[i], rw_g2[i],
                rw_kk[i], rw_ka[i], rw_rk[i], rw_lnx_w[i], rw_lnx_b[i], v_first, v0, v1, v2)
        else:
            j = layer - N_A_LAYERS
            if layer == N_A_LAYERS:
                kc, vc, ks, vs, kw, vw = nsa_shared_kv(h, kv_norm_g, w_kv, cmp_pe, cmp_w1, cmp_b1, cmp_w2)
            y = nsa_attention(rmsnorm(h, b_norm_g[j]), nsa_wq[j], nsa_wo[j], rel_bias,
                              kc, vc, ks, vs, kw, vw)
        h = h + y
        h = h + sqrelu_mlp(rmsnorm(h, m_norm_g[layer]), mlp_up[layer], mlp_down[layer])
    return rmsnorm(h, final_norm_g)
```

```python
import functools
import math

import numpy as np
import jax
import jax.numpy as jnp
from jax import lax
from jax.experimental import pallas as pl
from jax.experimental.pallas import tpu as pltpu

F32 = jnp.float32
BF16 = jnp.bfloat16

D_MODEL = 2048
DEPTH = 4
N_A_LAYERS = 2
RW_HEAD = 64
RW_HEADS = D_MODEL // RW_HEAD
N_SHIFT_MIX = 6
GN_EPS = 64e-5
NSA_HEADS = 16
NSA_GROUPS = 4
NSA_REP = NSA_HEADS // NSA_GROUPS
NSA_HD = D_MODEL // NSA_HEADS
N_KV_PARTS = 6
CMP_STRIDE = 16
CMP_LEN = 32
SEL_BLOCK = 64
SEL_TOP = 16
WINDOW = 512
QBLK = 128
N_BRANCH = 3
SEL_FORCE = 1e4
NEG_INF = -1e30
REL_BUCKETS = 32
REL_MAX_DIST = 128
NORM_EPS = 1e-6

VMEM_LIMIT_BYTES = 56 * 1024 * 1024
SCAN_CHUNK = 64
SCAN_HEADS = 4
CMP_BAND = 16
CMP_BAND_LO = 6
CMP_BIAS_K = 32


def _cparams(sem):
    return pltpu.CompilerParams(dimension_semantics=sem, vmem_limit_bytes=VMEM_LIMIT_BYTES)


def _split3(x):
    h = x.astype(BF16)
    r = x - h.astype(F32)
    m = r.astype(BF16)
    l = (r - m.astype(F32)).astype(BF16)
    return h, m, l


def _dot(a, b):
    return jnp.dot(a, b, preferred_element_type=F32)


def _dot_nt(a, b):
    return lax.dot_general(a, b, (((1,), (1,)), ((), ())), preferred_element_type=F32)


def _dot_tn(a, b):
    return lax.dot_general(a, b, (((0,), (0,)), ((), ())), preferred_element_type=F32)


def _dot_exact_lhs(a_bf16, x):
    h, m, l = _split3(x)
    return _dot(a_bf16, h) + _dot(a_bf16, m) + _dot(a_bf16, l)


def _dot_exact_rhs(x, b_bf16):
    h, m, l = _split3(x)
    return _dot(h, b_bf16) + _dot(m, b_bf16) + _dot(l, b_bf16)


def _dot_hi(a, b):
    ah = a.astype(BF16)
    al = (a - ah.astype(F32)).astype(BF16)
    bh = b.astype(BF16)
    bl = (b - bh.astype(F32)).astype(BF16)
    return _dot(ah, bh) + _dot(ah, bl) + _dot(al, bh)


def _matmul_kernel(x_ref, w_ref, *rest, act, scale, has_res, nk):
    if has_res:
        res_ref, o_ref, *scratch = rest
    else:
        o_ref, *scratch = rest

    def epilogue(acc):
        if act == "relu2":
            acc = jnp.square(jnp.maximum(acc, 0.0))
        elif act == "tanh":
            acc = jnp.tanh(acc)
        elif act == "sigmoid":
            acc = jax.nn.sigmoid(acc)
        if scale is not None:
            acc = acc * scale
        if has_res:
            acc = acc + res_ref[...]
        o_ref[...] = acc.astype(o_ref.dtype)

    if nk == 1:
        epilogue(_dot(x_ref[...], w_ref[...]))
    else:
        acc_ref, = scratch
        k = pl.program_id(2)

        @pl.when(k == 0)
        def _():
            acc_ref[...] = jnp.zeros_like(acc_ref)

        acc_ref[...] += _dot(x_ref[...], w_ref[...])

        @pl.when(k == nk - 1)
        def _():
            epilogue(acc_ref[...])


def _pick_tile(n, pref):
    t = min(n, pref)
    while n % t:
        t //= 2
    return t


def matmul(x, w, *, act=None, scale=None, residual=None, out_dtype=F32, tm=1024, tn=512, tk=2048):
    m, kdim = x.shape
    _, n = w.shape
    tm, tn, tk = _pick_tile(m, tm), _pick_tile(n, tn), _pick_tile(kdim, tk)
    nk = kdim // tk
    in_specs = [pl.BlockSpec((tm, tk), lambda j, i, k: (i, k)),
                pl.BlockSpec((tk, tn), lambda j, i, k: (k, j))]
    args = [x, w]
    if residual is not None:
        in_specs.append(pl.BlockSpec((tm, tn), lambda j, i, k: (i, j)))
        args.append(residual)
    return pl.pallas_call(
        functools.partial(_matmul_kernel, act=act, scale=scale, has_res=residual is not None, nk=nk),
        out_shape=jax.ShapeDtypeStruct((m, n), out_dtype),
        grid=(n // tn, m // tm, nk),
        in_specs=in_specs,
        out_specs=pl.BlockSpec((tm, tn), lambda j, i, k: (i, j)),
        scratch_shapes=[pltpu.VMEM((tm, tn), F32)] if nk > 1 else [],
        compiler_params=_cparams(("parallel", "parallel", "arbitrary")),
        name="matmul",
    )(*args)


def _rms(x, g):
    return x * lax.rsqrt(jnp.mean(x * x, axis=-1, keepdims=True) + NORM_EPS) * g


def _rmsnorm_kernel(h_ref, g_ref, o_ref):
    o_ref[...] = _rms(h_ref[...], g_ref[...]).astype(o_ref.dtype)


def rmsnorm(h, g, out_dtype=BF16, tm=512):
    t, d = h.shape
    tm = _pick_tile(t, tm)
    return pl.pallas_call(
        _rmsnorm_kernel,
        out_shape=jax.ShapeDtypeStruct((t, d), out_dtype),
        grid=(t // tm,),
        in_specs=[pl.BlockSpec((tm, d), lambda i: (i, 0)), pl.BlockSpec((1, d), lambda i: (0, 0))],
        out_specs=pl.BlockSpec((tm, d), lambda i: (i, 0)),
        compiler_params=_cparams(("parallel",)),
        name="rmsnorm",
    )(h, g.reshape(1, d))


def _rwkv_prep_kernel(h_ref, prev_ref, g_ref, mu_ref, *o_refs):
    g = g_ref[...]
    xn = _rms(h_ref[...], g)
    prev = _rms(prev_ref[...], g)[7:8, :] * (pl.program_id(0) > 0).astype(F32)
    row = lax.broadcasted_iota(jnp.int32, xn.shape, 0)
    shifted = jnp.where(row == 0, prev, pltpu.roll(xn, 1, axis=0))
    xx = shifted - xn
    for i, o_ref in enumerate(o_refs):
        o_ref[...] = (xn + xx * mu_ref[i:i + 1, :]).astype(o_ref.dtype)


def rwkv_prep(h, g, mu, tm=256):
    t, d = h.shape
    tm = _pick_tile(t, tm)
    blk = pl.BlockSpec((tm, d), lambda i: (i, 0))
    return pl.pallas_call(
        _rwkv_prep_kernel,
        out_shape=[jax.ShapeDtypeStruct((t, d), BF16)] * N_SHIFT_MIX,
        grid=(t // tm,),
        in_specs=[blk,
                  pl.BlockSpec((8, d), lambda i: (jnp.maximum(i * (tm // 8) - 1, 0), 0)),
                  pl.BlockSpec((1, d), lambda i: (0, 0)),
                  pl.BlockSpec((N_SHIFT_MIX, d), lambda i: (0, 0))],
        out_specs=[blk] * N_SHIFT_MIX,
        compiler_params=_cparams(("parallel",)),
        name="rwkv_prep",
    )(h, h, g.reshape(1, d), mu)


def _tri_inverse(a_strict, row, col):
    eye = (row == col).astype(F32)
    same_blk = (row >> 4) == (col >> 4)
    d = jnp.where(same_blk, a_strict, 0.0)
    e = a_strict - d
    p = eye + d
    x = d
    for _ in range(3):
        x = _dot_hi(x, x)
        p = p + _dot_hi(p, x)
    n1 = _dot_hi(p, e)
    n2 = _dot_hi(n1, n1)
    n3 = _dot_hi(n1, n2)
    return _dot_hi(eye + n1 + n2 + n3, p)


def _rwkv_scan_kernel(r_ref, lw_ref, k_ref, v_ref, a_ref, b_ref, y_ref, st_ref):
    @pl.when(pl.program_id(1) == 0)
    def _():
        st_ref[...] = jnp.zeros_like(st_ref)

    L = SCAN_CHUNK
    row = lax.broadcasted_iota(jnp.int32, (L, L), 0)
    col = lax.broadcasted_iota(jnp.int32, (L, L), 1)
    tri_incl = (col <= row).astype(BF16)
    ones = jnp.ones((L, RW_HEAD), BF16)
    for h in range(SCAN_HEADS):
        r, lw, k, v, a, b = (ref[h] for ref in (r_ref, lw_ref, k_ref, v_ref, a_ref, b_ref))
        st = st_ref[h]
        c = _dot_exact_lhs(tri_incl, lw)
        c_last = c[L - 1:L, :]
        hs, ms, ls = _split3(lw)
        c_col = _dot_tn(hs, ones) + _dot_tn(ms, ones) + _dot_tn(ls, ones)
        p_in = jnp.exp(c)
        p_inv = jnp.exp(-c)
        p_rest = jnp.exp(c_last - c)
        rt = (r * p_in).astype(BF16)
        at = (a * jnp.exp(c - lw)).astype(BF16)
        kt = (k * p_inv).astype(BF16)
        bt = (b * p_inv).astype(BF16)
        kh = (k * p_rest).astype(BF16)
        bh = (b * p_rest).astype(BF16)
        vb = v.astype(BF16)
        stb = st.astype(BF16)
        strict = col < row
        incl = col <= row
        a_ab = jnp.where(strict, _dot_nt(at, bt), 0.0)
        a_ak = jnp.where(strict, _dot_nt(at, kt), 0.0)
        a_rb = jnp.where(incl, _dot_nt(rt, bt), 0.0)
        a_rk = jnp.where(incl, _dot_nt(rt, kt), 0.0)
        t_inv = _tri_inverse(a_ab, row, col)
        rhs = _dot(at, stb) + _dot(a_ak.astype(BF16), vb)
        u = _dot_hi(t_inv, rhs)
        ub = u.astype(BF16)
        y_ref[h] = _dot(rt, stb) + _dot(a_rb.astype(BF16), ub) + _dot(a_rk.astype(BF16), vb)
        st_ref[h] = jnp.exp(c_col) * st + _dot_tn(bh, ub) + _dot_tn(kh, vb)


def rwkv_scan(r, lw, k, v, a, b):
    nh, t, n = r.shape
    blk = pl.BlockSpec((SCAN_HEADS, SCAN_CHUNK, n), lambda hg, ci: (hg, ci, 0))
    return pl.pallas_call(
        _rwkv_scan_kernel,
        out_shape=jax.ShapeDtypeStruct((nh, t, n), F32),
        grid=(nh // SCAN_HEADS, t // SCAN_CHUNK),
        in_specs=[blk] * 6,
        out_specs=blk,
        scratch_shapes=[pltpu.VMEM((SCAN_HEADS, n, n), F32)],
        compiler_params=_cparams(("parallel", "arbitrary")),
        name="rwkv_scan",
    )(r, lw, k, v, a, b)


def _compress_kernel(x_ref, w1_ref, c_ref, w2_ref, o_ref):
    ab = _dot(x_ref[...], w1_ref[...])
    a = ab[:, :NSA_HD]
    bsh = ab[:, NSA_HD:]
    n16 = a.shape[0]
    row = lax.broadcasted_iota(jnp.int32, a.shape, 0)
    b_next = jnp.where(row == n16 - 1, 0.0, pltpu.roll(bsh, n16 - 1, axis=0))
    hid = a + b_next + c_ref[...]
    hid = jax.nn.gelu(hid, approximate=True)
    o_ref[...] = _dot(hid.astype(BF16), w2_ref[...]).astype(o_ref.dtype)


def compress(x16, w1cat, cvec, w2):
    _, g, n16, kd = x16.shape
    d = NSA_HD
    return pl.pallas_call(
        _compress_kernel,
        out_shape=jax.ShapeDtypeStruct((2, g, n16, d), BF16),
        grid=(2, g),
        in_specs=[pl.BlockSpec((None, None, n16, kd), lambda p, gi: (p, gi, 0, 0)),
                  pl.BlockSpec((None, kd, 2 * d), lambda p, gi: (p, 0, 0)),
                  pl.BlockSpec((None, 1, d), lambda p, gi: (p, 0, 0)),
                  pl.BlockSpec((None, d, d), lambda p, gi: (p, 0, 0))],
        out_specs=pl.BlockSpec((None, None, n16, d), lambda p, gi: (p, gi, 0, 0)),
        compiler_params=_cparams(("parallel", "parallel")),
        name="nsa_compress",
    )(x16, w1cat, cvec, w2)


def _softmax_rows(s):
    mx = jnp.max(s, axis=-1, keepdims=True)
    e = jnp.exp(s - mx)
    return e / jnp.sum(e, axis=-1, keepdims=True)


def _nsa_kernel(q_ref, gate_ref, kc_ref, vc_ref, ks_ref, vs_ref, kw_ref, vw_ref,
                gs_ref, ov_ref, bs_ref, bw_ref, o_ref):
    qi = pl.program_id(1)
    R, Q, D = NSA_REP, QBLK, NSA_HD
    q = q_ref[...]
    qs = jnp.concatenate([q[:, r * D:(r + 1) * D] for r in range(R)], axis=0)
    n_cp = kc_ref.shape[0]

    t_rows = qi * Q + (lax.broadcasted_iota(jnp.int32, (R * Q, 1), 0) & (Q - 1))
    c_idx = lax.broadcasted_iota(jnp.int32, (1, n_cp), 1)
    m_off = qi * (Q // CMP_STRIDE) - lax.broadcasted_iota(jnp.int32, (CMP_BIAS_K, n_cp), 1) + CMP_BAND_LO
    k_idx = lax.broadcasted_iota(jnp.int32, (CMP_BIAS_K, n_cp), 0)
    in_band = ((k_idx < CMP_BAND) & (m_off == k_idx)) | ((k_idx == CMP_BAND) & (m_off >= CMP_BAND))
    sel_mat = jnp.where(in_band, 1.0, 0.0).astype(BF16)
    gs = gs_ref[...]
    bias_c = _dot(gs[0], sel_mat) + _dot(gs[1], sel_mat) + _dot(gs[2], sel_mat)
    m_c = (c_idx * CMP_STRIDE + (CMP_LEN - 1)) <= t_rows
    s_c = jnp.where(m_c, _dot_nt(qs, kc_ref[...]) + bias_c, NEG_INF)
    p_c = jnp.where(m_c, _softmax_rows(s_c), 0.0)
    o_c = _dot(p_c.astype(BF16), vc_ref[...])

    p_sum = p_c[0:Q] + p_c[Q:2 * Q] + p_c[2 * Q:3 * Q] + p_c[3 * Q:4 * Q]
    imp = _dot_exact_rhs(p_sum, ov_ref[...])
    n_sb = imp.shape[1]
    t_q = qi * Q + lax.broadcasted_iota(jnp.int32, (Q, 1), 0)
    s_io = lax.broadcasted_iota(jnp.int32, (Q, n_sb), 1)
    forced = (s_io == (t_q >> 6)) | (s_io == 0)
    score = jnp.where(forced, SEL_FORCE, jnp.where(s_io * SEL_BLOCK <= t_q, imp, -SEL_FORCE))
    sel = jnp.zeros((Q, n_sb), F32)
    s_f = s_io.astype(F32)
    for _ in range(SEL_TOP):
        mx = jnp.max(score, axis=-1, keepdims=True)
        first = jnp.min(jnp.where(score == mx, s_f, float(n_sb)), axis=-1, keepdims=True)
        pick = s_f == first
        sel = jnp.where(pick, 1.0, sel)
        score = jnp.where(pick, -3e38, score)
    sel_b = sel.astype(BF16)

    sb_row = lax.broadcasted_iota(jnp.int32, (n_sb, Q), 0)
    half = (lax.broadcasted_iota(jnp.int32, (n_sb, Q), 1) >= SEL_BLOCK).astype(jnp.int32)

    def sel_body(kb, carry):
        m_run, l_run, acc = carry
        start = pl.multiple_of(kb * Q, Q)
        kt = ks_ref[pl.ds(start, Q), :]
        vt = vs_ref[pl.ds(start, Q), :]
        expand = jnp.where(sb_row == 2 * kb + half, 1.0, 0.0).astype(BF16)
        madd = jnp.where(_dot(sel_b, expand) > 0.5, 0.0, NEG_INF)
        madd = jnp.concatenate([madd] * R, axis=0)
        s = _dot_nt(qs, kt) + bs_ref[jnp.minimum(qi - kb, 2)] + madd
        m_new = jnp.maximum(m_run, jnp.max(s, axis=-1, keepdims=True))
        alpha = jnp.exp(m_run - m_new)
        p = jnp.exp(s - m_new)
        l_new = alpha * l_run + jnp.sum(p, axis=-1, keepdims=True)
        acc_new = alpha * acc + _dot(p.astype(BF16), vt)
        return m_new, l_new, acc_new

    init = (jnp.full((R * Q, 1), NEG_INF, F32), jnp.zeros((R * Q, 1), F32), jnp.zeros((R * Q, D), F32))
    _, l_s, acc_s = lax.fori_loop(0, qi + 1, sel_body, init)
    o_s = acc_s / l_s

    w0 = pl.multiple_of(qi * Q, Q)
    kwb = kw_ref[pl.ds(w0, WINDOW + Q), :]
    vwb = vw_ref[pl.ds(w0, WINDOW + Q), :]
    j_io = lax.broadcasted_iota(jnp.int32, (1, WINDOW + Q), 1)
    s_w = jnp.where(j_io >= WINDOW - qi * Q, _dot_nt(qs, kwb) + bw_ref[...], NEG_INF)
    o_w = _dot(_softmax_rows(s_w).astype(BF16), vwb)

    gates = gate_ref[...]
    for r in range(R):
        rows = slice(r * Q, (r + 1) * Q)
        o = (gates[:, 3 * r:3 * r + 1] * o_c[rows] + gates[:, 3 * r + 1:3 * r + 2] * o_s[rows]
             + gates[:, 3 * r + 2:3 * r + 3] * o_w[rows])
        o_ref[:, r * D:(r + 1) * D] = o.astype(o_ref.dtype)


def nsa_attention(q, gates, kc, vc, ks, vs, kw, vw, gs, ov, bs, bw):
    t = q.shape[0]
    g, n_cp, d = kc.shape
    rq = NSA_REP * QBLK
    per_g = lambda shape: pl.BlockSpec((None,) + shape, lambda gi, qi: (gi,) + (0,) * len(shape))
    return pl.pallas_call(
        _nsa_kernel,
        out_shape=jax.ShapeDtypeStruct((t, NSA_HEADS * d), BF16),
        grid=(g, t // QBLK),
        in_specs=[pl.BlockSpec((QBLK, NSA_REP * d), lambda gi, qi: (qi, gi)),
                  pl.BlockSpec((None, QBLK, NSA_REP * N_BRANCH), lambda gi, qi: (gi, qi, 0)),
                  per_g((n_cp, d)), per_g((n_cp, d)),
                  per_g((t, d)), per_g((t, d)),
                  per_g((t + WINDOW, d)), per_g((t + WINDOW, d)),
                  per_g((3, rq, CMP_BIAS_K)),
                  pl.BlockSpec(ov.shape, lambda gi, qi: (0, 0)),
                  per_g((3, rq, QBLK)),
                  per_g((rq, WINDOW + QBLK))],
        out_specs=pl.BlockSpec((QBLK, NSA_REP * d), lambda gi, qi: (qi, gi)),
        compiler_params=_cparams(("parallel", "arbitrary")),
        name="nsa_attention",
    )(q, gates, kc, vc, ks, vs, kw, vw, gs, ov, bs, bw)


def _t5_bucket_np(n):
    n = np.maximum(n, 0)
    max_exact = REL_BUCKETS // 2
    nf = np.maximum(n, 1).astype(np.float64)
    large = max_exact + (np.log(nf / max_exact) / math.log(REL_MAX_DIST / max_exact)
                         * (REL_BUCKETS - max_exact)).astype(np.int64)
    large = np.minimum(large, REL_BUCKETS - 1)
    return np.where(n < max_exact, n, large).astype(np.int32)


def _nsa_bias_tables(rel_bias):
    G, R, Q = NSA_GROUPS, NSA_REP, QBLK
    table = rel_bias.astype(F32).reshape(REL_BUCKETS, G, R).transpose(1, 2, 0)
    qq = np.arange(Q)

    def lookup(dist):
        return table[:, :, _t5_bucket_np(dist)].reshape(G, R * Q, dist.shape[1])

    m = np.arange(CMP_BAND) - CMP_BAND_LO
    gs = lookup(qq[:, None] - (CMP_LEN - 1) + CMP_STRIDE * m[None, :])
    far = jnp.broadcast_to(table[:, :, REL_BUCKETS - 1][:, :, None, None], (G, R, Q, 1)).reshape(G, R * Q, 1)
    gs = jnp.concatenate([gs, far, jnp.zeros((G, R * Q, CMP_BIAS_K - CMP_BAND - 1), F32)], axis=-1)
    h, mid, l = _split3(gs)
    gs3 = jnp.stack([h, mid, l], axis=1)
    jj = np.arange(Q)
    d0 = qq[:, None] - jj[None, :]
    b0 = lookup(d0) + jnp.asarray(np.where(np.tile(d0, (R, 1)) >= 0, 0.0, NEG_INF), F32)[None]
    b1 = lookup(d0 + Q)
    b2 = lookup(d0 + 2 * Q)
    bs = jnp.stack([b0, b1, b2], axis=1)
    dw = qq[:, None] + WINDOW - np.arange(WINDOW + Q)[None, :]
    okw = np.tile((dw >= 0) & (dw < WINDOW), (R, 1))
    bw = lookup(dw) + jnp.asarray(np.where(okw, 0.0, NEG_INF), F32)[None]
    return gs3, bs, bw


def _overlap_table(n_cp, n_sb):
    c = np.arange(n_cp)[:, None] * CMP_STRIDE
    s = np.arange(n_sb)[None, :] * SEL_BLOCK
    return jnp.asarray(((c <= s + SEL_BLOCK - 1) & (c + CMP_LEN - 1 >= s)), BF16)


def _pad_cols(w, n):
    return jnp.pad(w, ((0, 0), (0, n - w.shape[1])))


def _pad_rows(w, n):
    return jnp.pad(w, ((0, n - w.shape[0]), (0, 0)))


def _heads(u):
    t = u.shape[0]
    return u.reshape(t, RW_HEADS, RW_HEAD).transpose(1, 0, 2)


def _rwkv_layer(h, norm_g, mu, wr, wk, wv, wo, w0, w1, w2, a0, a1, a2, g1, g2,
                k_k, k_a, r_k, lnx_w, lnx_b, v_first, v0, v1, v2):
    t = h.shape[0]
    bf = lambda w: w.astype(BF16)
    xr, xw, xk, xv, xa, xg = rwkv_prep(h, norm_g, mu)
    r = matmul(xr, bf(wr))
    k = matmul(xk, bf(wk))
    v = matmul(xv, bf(wv))
    lw_hid = matmul(xw, bf(_pad_cols(w1, 128)), act="tanh", out_dtype=BF16)
    w_lin = matmul(lw_hid, bf(_pad_rows(w2, 128)))
    a_lin = matmul(matmul(xa, bf(a1), out_dtype=BF16), bf(a2))
    g = matmul(matmul(xg, bf(g1), act="sigmoid", out_dtype=BF16), bf(g2))
    w = -jax.nn.softplus(-(w0 + w_lin)) - 0.5
    if v_first is None:
        v_first = v
    else:
        v_lin = matmul(matmul(xv, bf(_pad_cols(v1, 128)), out_dtype=BF16), bf(_pad_rows(v2, 128)))
        v = v + (v_first - v) * jax.nn.sigmoid(v0 + v_lin)
    a = jax.nn.sigmoid(a0 + a_lin)
    kk = (k * k_k).reshape(t, RW_HEADS, RW_HEAD)
    kk = (kk * lax.rsqrt(jnp.maximum(jnp.sum(kk * kk, -1, keepdims=True), 1e-24))).reshape(t, D_MODEL)
    k = k * (1.0 + (a - 1.0) * k_a)
    y = rwkv_scan(_heads(r), _heads(-jnp.exp(w)), _heads(k), _heads(v), _heads(-kk), _heads(kk * a))
    y = y.transpose(1, 0, 2)
    mean = jnp.mean(y, -1, keepdims=True)
    var = jnp.mean(jnp.square(y - mean), -1, keepdims=True)
    y = ((y - mean) * lax.rsqrt(var + GN_EPS)).reshape(t, D_MODEL) * lnx_w + lnx_b
    rk = (r * k).reshape(t, RW_HEADS, RW_HEAD) * r_k
    bonus = (jnp.sum(rk, -1, keepdims=True) * v.reshape(t, RW_HEADS, RW_HEAD)).reshape(t, D_MODEL)
    h = matmul(((y + bonus) * g).astype(BF16), bf(wo), residual=h)
    return h, v_first


def _mlp(h, norm_g, w_up, w_down):
    xn = rmsnorm(h, norm_g)
    hid = matmul(xn, w_up.astype(BF16), act="relu2", out_dtype=BF16)
    return matmul(hid, w_down.astype(BF16), residual=h)


def _nsa_shared_kv(h, kv_norm_g, w_kv, cmp_pe, cmp_w1, cmp_b1, cmp_w2):
    t = h.shape[0]
    G, D = NSA_GROUPS, NSA_HD
    kv = matmul(rmsnorm(h, kv_norm_g), w_kv.astype(BF16), out_dtype=BF16)
    kv = kv.reshape(t, N_KV_PARTS, G, D).transpose(1, 2, 0, 3)
    x16 = kv[0:2].reshape(2, G, t // CMP_STRIDE, CMP_STRIDE * D)
    half = CMP_STRIDE * D
    w1cat = jnp.concatenate([cmp_w1[:, :half], cmp_w1[:, half:]], axis=-1).astype(BF16)
    cvec = jnp.einsum("plk,plkd->pd", cmp_pe, cmp_w1.reshape(2, CMP_LEN, D, D),
                      precision=lax.Precision.HIGHEST) + cmp_b1
    kvc = compress(x16, w1cat, cvec.reshape(2, 1, D), cmp_w2.astype(BF16))
    pad = ((0, 0), (WINDOW, 0), (0, 0))
    return kvc[0], kvc[1], kv[2], kv[3], jnp.pad(kv[4], pad), jnp.pad(kv[5], pad)


def _nsa_layer(h, norm_g, wq, wo, kvs, tables):
    t = h.shape[0]
    hn = rmsnorm(h, norm_g)
    nq = NSA_HEADS * NSA_HD
    q = matmul(hn, wq[:, :nq].astype(BF16), scale=NSA_HD ** -0.5, out_dtype=BF16)
    gates = matmul(hn, _pad_cols(wq[:, nq:], 128).astype(BF16), act="sigmoid")
    gates = gates[:, :NSA_HEADS * N_BRANCH].reshape(t, NSA_GROUPS, NSA_REP * N_BRANCH).transpose(1, 0, 2)
    o = nsa_attention(q, gates, *kvs, *tables)
    return matmul(o, wo.astype(BF16), residual=h)


def kernel(x, a_norm_g, rw_mu, rw_wr, rw_wk, rw_wv, rw_wo, rw_w0, rw_w1, rw_w2, rw_a0, rw_a1, rw_a2,
           rw_v0, rw_v1, rw_v2, rw_g1, rw_g2, rw_kk, rw_ka, rw_rk, rw_lnx_w, rw_lnx_b, kv_norm_g, w_kv,
           cmp_pe, cmp_w1, cmp_b1, cmp_w2, b_norm_g, nsa_wq, nsa_wo, rel_bias, m_norm_g, mlp_up,
           mlp_down, final_norm_g):
    b, t, d = x.shape
    assert b == 1 and d == D_MODEL and t % QBLK == 0 and t // SEL_BLOCK <= 128
    h = x.reshape(t, d)
    v_first = None
    kvs = tables = None
    for layer in range(DEPTH):
        if layer < N_A_LAYERS:
            i = layer
            vp = (None, None, None) if i == 0 else (rw_v0[i - 1], rw_v1[i - 1], rw_v2[i - 1])
            h, v_first = _rwkv_layer(
                h, a_norm_g[i], rw_mu[i], rw_wr[i], rw_wk[i], rw_wv[i], rw_wo[i], rw_w0[i], rw_w1[i],
                rw_w2[i], rw_a0[i], rw_a1[i], rw_a2[i], rw_g1[i], rw_g2[i], rw_kk[i], rw_ka[i],
                rw_rk[i].reshape(1, RW_HEADS, RW_HEAD), rw_lnx_w[i], rw_lnx_b[i], v_first, *vp)
        else:
            j = layer - N_A_LAYERS
            if j == 0:
                kvs = _nsa_shared_kv(h, kv_norm_g, w_kv, cmp_pe, cmp_w1, cmp_b1, cmp_w2)
                gs3, bs, bw = _nsa_bias_tables(rel_bias)
                tables = (gs3, _overlap_table(t // CMP_STRIDE, 128), bs, bw)
            h = _nsa_layer(h, b_norm_g[j], nsa_wq[j], nsa_wo[j], kvs, tables)
        h = _mlp(h, m_norm_g[layer], mlp_up[layer], mlp_down[layer])
    return rmsnorm(h, final_norm_g, out_dtype=x.dtype).reshape(b, t, d)
```

```python
import functools
import math

import numpy as np
import jax
import jax.numpy as jnp
from jax import lax
from jax.experimental import pallas as pl
from jax.experimental.pallas import tpu as pltpu

F32 = jnp.float32
BF16 = jnp.bfloat16

D_MODEL = 2048
DEPTH = 4
N_A_LAYERS = 2
RW_HEAD = 64
RW_HEADS = D_MODEL // RW_HEAD
N_SHIFT_MIX = 6
GN_EPS = 64e-5
NSA_HEADS = 16
NSA_GROUPS = 4
NSA_REP = NSA_HEADS // NSA_GROUPS
NSA_HD = D_MODEL // NSA_HEADS
N_KV_PARTS = 6
CMP_STRIDE = 16
CMP_LEN = 32
SEL_BLOCK = 64
SEL_TOP = 16
WINDOW = 512
QBLK = 128
N_BRANCH = 3
SEL_FORCE = 1e4
NEG_INF = -1e30
REL_BUCKETS = 32
REL_MAX_DIST = 128
NORM_EPS = 1e-6

VMEM_LIMIT_BYTES = 56 * 1024 * 1024
SCAN_CHUNK = 64
SCAN_HEADS = 4
SCAN_WIDTH = SCAN_HEADS * RW_HEAD
SCAN_UNITS = 2
CMP_BAND = 16
CMP_BAND_LO = 6
CMP_BIAS_K = 32


def _cparams(sem):
    return pltpu.CompilerParams(dimension_semantics=sem, vmem_limit_bytes=VMEM_LIMIT_BYTES)


def _split3(x):
    h = x.astype(BF16)
    r = x - h.astype(F32)
    m = r.astype(BF16)
    l = (r - m.astype(F32)).astype(BF16)
    return h, m, l


def _dot(a, b):
    return jnp.dot(a, b, preferred_element_type=F32)


def _dot_nt(a, b):
    return lax.dot_general(a, b, (((1,), (1,)), ((), ())), preferred_element_type=F32)


def _dot_tn(a, b):
    return lax.dot_general(a, b, (((0,), (0,)), ((), ())), preferred_element_type=F32)


def _dot_exact_lhs(a_bf16, x):
    h, m, l = _split3(x)
    return _dot(a_bf16, h) + _dot(a_bf16, m) + _dot(a_bf16, l)


def _dot_exact_rhs(x, b_bf16):
    h, m, l = _split3(x)
    return _dot(h, b_bf16) + _dot(m, b_bf16) + _dot(l, b_bf16)


def _dot_hi(a, b):
    ah = a.astype(BF16)
    al = (a - ah.astype(F32)).astype(BF16)
    bh = b.astype(BF16)
    bl = (b - bh.astype(F32)).astype(BF16)
    return _dot(ah, bh) + _dot(ah, bl) + _dot(al, bh)


def _matmul_kernel(x_ref, w_ref, *rest, act, scale, has_res, nk):
    if has_res:
        res_ref, o_ref, *scratch = rest
    else:
        o_ref, *scratch = rest

    def epilogue(acc):
        if act == "relu2":
            acc = jnp.square(jnp.maximum(acc, 0.0))
        elif act == "tanh":
            acc = jnp.tanh(acc)
        elif act == "sigmoid":
            acc = jax.nn.sigmoid(acc)
        if scale is not None:
            acc = acc * scale
        if has_res:
            acc = acc + res_ref[...]
        o_ref[...] = acc.astype(o_ref.dtype)

    if nk == 1:
        epilogue(_dot(x_ref[...], w_ref[...]))
    else:
        acc_ref, = scratch
        k = pl.program_id(2)

        @pl.when(k == 0)
        def _():
            acc_ref[...] = jnp.zeros_like(acc_ref)

        acc_ref[...] += _dot(x_ref[...], w_ref[...])

        @pl.when(k == nk - 1)
        def _():
            epilogue(acc_ref[...])


def _pick_tile(n, pref):
    t = min(n, pref)
    while n % t:
        t //= 2
    return t


def matmul(x, w, *, act=None, scale=None, residual=None, out_dtype=F32, tm=1024, tn=512, tk=2048):
    m, kdim = x.shape
    _, n = w.shape
    tm, tn, tk = _pick_tile(m, tm), _pick_tile(n, tn), _pick_tile(kdim, tk)
    nk = kdim // tk
    in_specs = [pl.BlockSpec((tm, tk), lambda j, i, k: (i, k)),
                pl.BlockSpec((tk, tn), lambda j, i, k: (k, j))]
    args = [x, w]
    if residual is not None:
        in_specs.append(pl.BlockSpec((tm, tn), lambda j, i, k: (i, j)))
        args.append(residual)
    return pl.pallas_call(
        functools.partial(_matmul_kernel, act=act, scale=scale, has_res=residual is not None, nk=nk),
        out_shape=jax.ShapeDtypeStruct((m, n), out_dtype),
        grid=(n // tn, m // tm, nk),
        in_specs=in_specs,
        out_specs=pl.BlockSpec((tm, tn), lambda j, i, k: (i, j)),
        scratch_shapes=[pltpu.VMEM((tm, tn), F32)] if nk > 1 else [],
        compiler_params=_cparams(("parallel", "parallel", "arbitrary")),
        name="matmul",
    )(*args)


def _rms(x, g):
    return x * lax.rsqrt(jnp.mean(x * x, axis=-1, keepdims=True) + NORM_EPS) * g


def _rmsnorm_kernel(h_ref, g_ref, o_ref):
    o_ref[...] = _rms(h_ref[...], g_ref[...]).astype(o_ref.dtype)


def rmsnorm(h, g, out_dtype=BF16, tm=512):
    t, d = h.shape
    tm = _pick_tile(t, tm)
    return pl.pallas_call(
        _rmsnorm_kernel,
        out_shape=jax.ShapeDtypeStruct((t, d), out_dtype),
        grid=(t // tm,),
        in_specs=[pl.BlockSpec((tm, d), lambda i: (i, 0)), pl.BlockSpec((1, d), lambda i: (0, 0))],
        out_specs=pl.BlockSpec((tm, d), lambda i: (i, 0)),
        compiler_params=_cparams(("parallel",)),
        name="rmsnorm",
    )(h, g.reshape(1, d))


def _rwkv_prep_kernel(h_ref, prev_ref, g_ref, mu_ref, *o_refs):
    g = g_ref[...]
    xn = _rms(h_ref[...], g)
    prev = _rms(prev_ref[...], g)[7:8, :] * (pl.program_id(0) > 0).astype(F32)
    row = lax.broadcasted_iota(jnp.int32, xn.shape, 0)
    shifted = jnp.where(row == 0, prev, pltpu.roll(xn, 1, axis=0))
    xx = shifted - xn
    for i, o_ref in enumerate(o_refs):
        o_ref[...] = (xn + xx * mu_ref[i:i + 1, :]).astype(o_ref.dtype)


def rwkv_prep(h, g, mu, tm=256):
    t, d = h.shape
    tm = _pick_tile(t, tm)
    blk = pl.BlockSpec((tm, d), lambda i: (i, 0))
    return pl.pallas_call(
        _rwkv_prep_kernel,
        out_shape=[jax.ShapeDtypeStruct((t, d), BF16)] * N_SHIFT_MIX,
        grid=(t // tm,),
        in_specs=[blk,
                  pl.BlockSpec((8, d), lambda i: (jnp.maximum(i * (tm // 8) - 1, 0), 0)),
                  pl.BlockSpec((1, d), lambda i: (0, 0)),
                  pl.BlockSpec((N_SHIFT_MIX, d), lambda i: (0, 0))],
        out_specs=[blk] * N_SHIFT_MIX,
        compiler_params=_cparams(("parallel",)),
        name="rwkv_prep",
    )(h, h, g.reshape(1, d), mu)


def _head_mask():
    row = lax.broadcasted_iota(jnp.int32, (SCAN_WIDTH, SCAN_WIDTH), 0)
    col = lax.broadcasted_iota(jnp.int32, (SCAN_WIDTH, SCAN_WIDTH), 1)
    return (row >> 6) == (col >> 6)


def _block_diag(x, head_mask):
    return jnp.where(head_mask, jnp.concatenate([x] * SCAN_HEADS, axis=0), jnp.zeros((), x.dtype))


def _fold_rows(x_bd):
    L = SCAN_CHUNK
    return x_bd[0:L] + x_bd[L:2 * L] + x_bd[2 * L:3 * L] + x_bd[3 * L:4 * L]


def _tri_inverse_bd(a_bd):
    n = a_bd.shape[0]
    row = lax.broadcasted_iota(jnp.int32, (n, n), 0)
    col = lax.broadcasted_iota(jnp.int32, (n, n), 1)
    eye = jnp.where(row == col, 1.0, 0.0)
    d = jnp.where((row >> 4) == (col >> 4), a_bd, 0.0)
    e = a_bd - d
    p = eye + d
    x = d
    for _ in range(3):
        x = _dot_hi(x, x)
        p = p + _dot_hi(p, x)
    n1 = _dot_hi(p, e)
    n2 = _dot_hi(n1, n1)
    n3 = _dot_hi(n1, n2)
    return _dot_hi(eye + n1 + n2 + n3, p)


def _softplus(x):
    return jnp.maximum(x, 0.0) + jnp.log(1.0 + jnp.exp(-jnp.abs(x)))


def _rwkv_chunk_kernel(*refs, mix_v):
    if mix_v:
        r_ref, k_ref, v_ref, wl_ref, al_ref, vl_ref, vf_ref, par_ref, m_ref, c_ref, qp_ref, y0_ref, bo_ref = refs
    else:
        r_ref, k_ref, v_ref, wl_ref, al_ref, par_ref, m_ref, c_ref, qp_ref, y0_ref, bo_ref = refs
    L, W = SCAN_CHUNK, SCAN_WIDTH
    head_mask = _head_mask()
    ones_bd = jnp.where(head_mask, 1.0, 0.0).astype(BF16)
    tri_incl = jnp.where(lax.broadcasted_iota(jnp.int32, (L, L), 1)
                         <= lax.broadcasted_iota(jnp.int32, (L, L), 0), 1.0, 0.0).astype(BF16)
    j_row = lax.broadcasted_iota(jnp.int32, (L, W), 0)
    m_col = lax.broadcasted_iota(jnp.int32, (L, W), 1) & (L - 1)
    strict = m_col < j_row
    incl = m_col <= j_row
    for u in range(SCAN_UNITS):
        cols = slice(u * W, (u + 1) * W)
        par = par_ref[:, cols]
        w0, a0, v0, k_k, k_a, r_k = (par[i:i + 1] for i in range(6))
        r, k, v = r_ref[:, cols], k_ref[:, cols], v_ref[:, cols]
        lw = -jnp.exp(-_softplus(-(w0 + wl_ref[:, cols])) - 0.5)
        a_s = jax.nn.sigmoid(a0 + al_ref[:, cols])
        if mix_v:
            v = v + (vf_ref[:, cols] - v) * jax.nn.sigmoid(v0 + vl_ref[:, cols])
        kk = k * k_k
        kk = kk * lax.rsqrt(jnp.maximum(_dot_exact_rhs(kk * kk, ones_bd), 1e-24))
        k = k * (1.0 + (a_s - 1.0) * k_a)
        bo_ref[:, cols] = _dot_exact_rhs(r * k * r_k, ones_bd) * v
        c = _dot_exact_lhs(tri_incl, lw)
        c_last = c[L - 1:L, :]
        p_inv = jnp.exp(-c)
        p_rest = jnp.exp(c_last - c)
        rt = r * jnp.exp(c)
        at = (-kk * jnp.exp(c - lw)).astype(BF16)
        b = kk * a_s
        ar = jnp.concatenate([at, rt.astype(BF16)], axis=0)
        g_b = _dot_nt(ar, _block_diag((b * p_inv).astype(BF16), head_mask))
        g_k = _dot_nt(ar, _block_diag((k * p_inv).astype(BF16), head_mask))
        a_ab = jnp.where(strict, g_b[:L], 0.0)
        a_ak = jnp.where(strict, g_k[:L], 0.0).astype(BF16)
        a_rb = jnp.where(incl, g_b[L:], 0.0).astype(BF16)
        a_rk = jnp.where(incl, g_k[L:], 0.0).astype(BF16)
        t_inv = _tri_inverse_bd(_block_diag(a_ab, head_mask))
        t_hi = t_inv.astype(BF16)
        t_lo = (t_inv - t_hi.astype(F32)).astype(BF16)
        at_bd = _block_diag(at, head_mask)
        v_bd = _block_diag(v.astype(BF16), head_mask)
        w_bd = (_dot(t_hi, at_bd) + _dot(t_lo, at_bd)).astype(BF16)
        u0_bd = _dot_hi(t_inv, _dot(_block_diag(a_ak, head_mask), v_bd)).astype(BF16)
        qp_ref[:, cols] = (rt + _dot(a_rb, w_bd)).astype(qp_ref.dtype)
        y0_ref[:, cols] = _dot(a_rb, u0_bd) + _dot(a_rk, v_bd)
        bh_bd = _block_diag((b * p_rest).astype(BF16), head_mask)
        kh_bd = _block_diag((k * p_rest).astype(BF16), head_mask)
        decay_diag = jnp.where(m_col == j_row, jnp.exp(c_last), 0.0)
        m_ref[:, cols] = _fold_rows(_dot_tn(bh_bd, w_bd)) + decay_diag
        c_ref[:, cols] = _fold_rows(_dot_tn(bh_bd, u0_bd) + _dot_tn(kh_bd, v_bd))


def rwkv_chunks(r, k, v, w_lin, a_lin, v_lin, v_first, params):
    t, d = r.shape
    bw = SCAN_UNITS * SCAN_WIDTH
    blk = pl.BlockSpec((SCAN_CHUNK, bw), lambda ci, gi: (ci, gi))
    mix_v = v_lin is not None
    args = [r, k, v, w_lin, a_lin] + ([v_lin, v_first] if mix_v else []) + [params]
    f32 = jax.ShapeDtypeStruct((t, d), F32)
    return pl.pallas_call(
        functools.partial(_rwkv_chunk_kernel, mix_v=mix_v),
        out_shape=[f32, f32, jax.ShapeDtypeStruct((t, d), BF16), f32, f32],
        grid=(t // SCAN_CHUNK, d // bw),
        in_specs=[blk] * (len(args) - 1) + [pl.BlockSpec((8, bw), lambda ci, gi: (0, gi))],
        out_specs=[blk] * 5,
        compiler_params=_cparams(("parallel", "parallel")),
        name="rwkv_chunks",
    )(*args)


def _rwkv_state_kernel(m_ref, c_ref, qp_ref, y0_ref, bo_ref, g_ref, ln_ref, o_ref, st_ref):
    @pl.when(pl.program_id(1) == 0)
    def _():
        st_ref[...] = jnp.zeros_like(st_ref)

    W = SCAN_WIDTH
    head_mask = _head_mask()
    ones_bd = jnp.where(head_mask, 1.0, 0.0).astype(BF16)
    for u in range(SCAN_UNITS):
        cols = slice(u * W, (u + 1) * W)
        st = st_ref[u]
        y = _dot(qp_ref[:, cols], st.astype(BF16)) + y0_ref[:, cols]
        mean = _dot_exact_rhs(y, ones_bd) * (1.0 / RW_HEAD)
        dev = y - mean
        var = _dot_exact_rhs(dev * dev, ones_bd) * (1.0 / RW_HEAD)
        ln = ln_ref[:, cols]
        out = (dev * lax.rsqrt(var + GN_EPS) * ln[0:1] + ln[1:2] + bo_ref[:, cols]) * g_ref[:, cols]
        o_ref[:, cols] = out.astype(o_ref.dtype)
        st_ref[u] = _dot_hi(_block_diag(m_ref[:, cols], head_mask), st) + _block_diag(c_ref[:, cols], head_mask)


def rwkv_state_scan(m, c, qp, y0, bonus, g, ln_params):
    t, d = m.shape
    bw = SCAN_UNITS * SCAN_WIDTH
    blk = pl.BlockSpec((SCAN_CHUNK, bw), lambda gi, ci: (ci, gi))
    return pl.pallas_call(
        _rwkv_state_kernel,
        out_shape=jax.ShapeDtypeStruct((t, d), BF16),
        grid=(d // bw, t // SCAN_CHUNK),
        in_specs=[blk] * 6 + [pl.BlockSpec((8, bw), lambda gi, ci: (0, gi))],
        out_specs=blk,
        scratch_shapes=[pltpu.VMEM((SCAN_UNITS, SCAN_WIDTH, SCAN_WIDTH), F32)],
        compiler_params=_cparams(("parallel", "arbitrary")),
        name="rwkv_state_scan",
    )(m, c, qp, y0, bonus, g, ln_params)


def _compress_kernel(x_ref, w1_ref, c_ref, w2_ref, o_ref):
    ab = _dot(x_ref[...], w1_ref[...])
    a = ab[:, :NSA_HD]
    bsh = ab[:, NSA_HD:]
    n16 = a.shape[0]
    row = lax.broadcasted_iota(jnp.int32, a.shape, 0)
    b_next = jnp.where(row == n16 - 1, 0.0, pltpu.roll(bsh, n16 - 1, axis=0))
    hid = a + b_next + c_ref[...]
    hid = jax.nn.gelu(hid, approximate=True)
    o_ref[...] = _dot(hid.astype(BF16), w2_ref[...]).astype(o_ref.dtype)


def compress(x16, w1cat, cvec, w2):
    _, g, n16, kd = x16.shape
    d = NSA_HD
    return pl.pallas_call(
        _compress_kernel,
        out_shape=jax.ShapeDtypeStruct((2, g, n16, d), BF16),
        grid=(2, g),
        in_specs=[pl.BlockSpec((None, None, n16, kd), lambda p, gi: (p, gi, 0, 0)),
                  pl.BlockSpec((None, kd, 2 * d), lambda p, gi: (p, 0, 0)),
                  pl.BlockSpec((None, 1, d), lambda p, gi: (p, 0, 0)),
                  pl.BlockSpec((None, d, d), lambda p, gi: (p, 0, 0))],
        out_specs=pl.BlockSpec((None, None, n16, d), lambda p, gi: (p, gi, 0, 0)),
        compiler_params=_cparams(("parallel", "parallel")),
        name="nsa_compress",
    )(x16, w1cat, cvec, w2)


def _softmax_rows(s):
    mx = jnp.max(s, axis=-1, keepdims=True)
    e = jnp.exp(s - mx)
    return e / jnp.sum(e, axis=-1, keepdims=True)


def _nsa_kernel(q_ref, gate_ref, kc_ref, vc_ref, ks_ref, vs_ref, kw_ref, vw_ref,
                gs_ref, ov_ref, bs_ref, bw_ref, o_ref):
    qi = pl.program_id(1)
    R, Q, D = NSA_REP, QBLK, NSA_HD
    q = q_ref[...]
    qs = jnp.concatenate([q[:, r * D:(r + 1) * D] for r in range(R)], axis=0)
    n_cp = kc_ref.shape[0]

    t_rows = qi * Q + (lax.broadcasted_iota(jnp.int32, (R * Q, 1), 0) & (Q - 1))
    c_idx = lax.broadcasted_iota(jnp.int32, (1, n_cp), 1)
    m_off = qi * (Q // CMP_STRIDE) - lax.broadcasted_iota(jnp.int32, (CMP_BIAS_K, n_cp), 1) + CMP_BAND_LO
    k_idx = lax.broadcasted_iota(jnp.int32, (CMP_BIAS_K, n_cp), 0)
    in_band = ((k_idx < CMP_BAND) & (m_off == k_idx)) | ((k_idx == CMP_BAND) & (m_off >= CMP_BAND))
    sel_mat = jnp.where(in_band, 1.0, 0.0).astype(BF16)
    gs = gs_ref[...]
    bias_c = _dot(gs[0], sel_mat) + _dot(gs[1], sel_mat) + _dot(gs[2], sel_mat)
    m_c = (c_idx * CMP_STRIDE + (CMP_LEN - 1)) <= t_rows
    s_c = jnp.where(m_c, _dot_nt(qs, kc_ref[...]) + bias_c, NEG_INF)
    p_c = jnp.where(m_c, _softmax_rows(s_c), 0.0)
    o_c = _dot(p_c.astype(BF16), vc_ref[...])

    p_sum = p_c[0:Q] + p_c[Q:2 * Q] + p_c[2 * Q:3 * Q] + p_c[3 * Q:4 * Q]
    imp = _dot_exact_rhs(p_sum, ov_ref[...])
    n_sb = imp.shape[1]
    t_q = qi * Q + lax.broadcasted_iota(jnp.int32, (Q, 1), 0)
    s_io = lax.broadcasted_iota(jnp.int32, (Q, n_sb), 1)
    forced = (s_io == (t_q >> 6)) | (s_io == 0)
    score = jnp.where(forced, SEL_FORCE, jnp.where(s_io * SEL_BLOCK <= t_q, imp, -SEL_FORCE))
    sel = jnp.zeros((Q, n_sb), F32)
    s_f = s_io.astype(F32)
    for _ in range(SEL_TOP):
        mx = jnp.max(score, axis=-1, keepdims=True)
        first = jnp.min(jnp.where(score == mx, s_f, float(n_sb)), axis=-1, keepdims=True)
        pick = s_f == first
        sel = jnp.where(pick, 1.0, sel)
        score = jnp.where(pick, -3e38, score)
    sel_b = sel.astype(BF16)

    sb_row = lax.broadcasted_iota(jnp.int32, (n_sb, Q), 0)
    half = (lax.broadcasted_iota(jnp.int32, (n_sb, Q), 1) >= SEL_BLOCK).astype(jnp.int32)

    def sel_body(kb, carry):
        m_run, l_run, acc = carry
        start = pl.multiple_of(kb * Q, Q)
        kt = ks_ref[pl.ds(start, Q), :]
        vt = vs_ref[pl.ds(start, Q), :]
        expand = jnp.where(sb_row == 2 * kb + half, 1.0, 0.0).astype(BF16)
        madd = jnp.where(_dot(sel_b, expand) > 0.5, 0.0, NEG_INF)
        madd = jnp.concatenate([madd] * R, axis=0)
        s = _dot_nt(qs, kt) + bs_ref[jnp.minimum(qi - kb, 2)] + madd
        m_new = jnp.maximum(m_run, jnp.max(s, axis=-1, keepdims=True))
        alpha = jnp.exp(m_run - m_new)
        p = jnp.exp(s - m_new)
        l_new = alpha * l_run + jnp.sum(p, axis=-1, keepdims=True)
        acc_new = alpha * acc + _dot(p.astype(BF16), vt)
        return m_new, l_new, acc_new

    init = (jnp.full((R * Q, 1), NEG_INF, F32), jnp.zeros((R * Q, 1), F32), jnp.zeros((R * Q, D), F32))
    _, l_s, acc_s = lax.fori_loop(0, qi + 1, sel_body, init)
    o_s = acc_s / l_s

    w0 = pl.multiple_of(qi * Q, Q)
    kwb = kw_ref[pl.ds(w0, WINDOW + Q), :]
    vwb = vw_ref[pl.ds(w0, WINDOW + Q), :]
    j_io = lax.broadcasted_iota(jnp.int32, (1, WINDOW + Q), 1)
    s_w = jnp.where(j_io >= WINDOW - qi * Q, _dot_nt(qs, kwb) + bw_ref[...], NEG_INF)
    o_w = _dot(_softmax_rows(s_w).astype(BF16), vwb)

    gates = gate_ref[...]
    for r in range(R):
        rows = slice(r * Q, (r + 1) * Q)
        o = (gates[:, 3 * r:3 * r + 1] * o_c[rows] + gates[:, 3 * r + 1:3 * r + 2] * o_s[rows]
             + gates[:, 3 * r + 2:3 * r + 3] * o_w[rows])
        o_ref[:, r * D:(r + 1) * D] = o.astype(o_ref.dtype)


def nsa_attention(q, gates, kc, vc, ks, vs, kw, vw, gs, ov, bs, bw):
    t = q.shape[0]
    g, n_cp, d = kc.shape
    rq = NSA_REP * QBLK
    per_g = lambda shape: pl.BlockSpec((None,) + shape, lambda gi, qi: (gi,) + (0,) * len(shape))
    return pl.pallas_call(
        _nsa_kernel,
        out_shape=jax.ShapeDtypeStruct((t, NSA_HEADS * d), BF16),
        grid=(g, t // QBLK),
        in_specs=[pl.BlockSpec((QBLK, NSA_REP * d), lambda gi, qi: (qi, gi)),
                  pl.BlockSpec((None, QBLK, NSA_REP * N_BRANCH), lambda gi, qi: (gi, qi, 0)),
                  per_g((n_cp, d)), per_g((n_cp, d)),
                  per_g((t, d)), per_g((t, d)),
                  per_g((t + WINDOW, d)), per_g((t + WINDOW, d)),
                  per_g((3, rq, CMP_BIAS_K)),
                  pl.BlockSpec(ov.shape, lambda gi, qi: (0, 0)),
                  per_g((3, rq, QBLK)),
                  per_g((rq, WINDOW + QBLK))],
        out_specs=pl.BlockSpec((QBLK, NSA_REP * d), lambda gi, qi: (qi, gi)),
        compiler_params=_cparams(("parallel", "arbitrary")),
        name="nsa_attention",
    )(q, gates, kc, vc, ks, vs, kw, vw, gs, ov, bs, bw)


def _t5_bucket_np(n):
    n = np.maximum(n, 0)
    max_exact = REL_BUCKETS // 2
    nf = np.maximum(n, 1).astype(np.float64)
    large = max_exact + (np.log(nf / max_exact) / math.log(REL_MAX_DIST / max_exact)
                         * (REL_BUCKETS - max_exact)).astype(np.int64)
    large = np.minimum(large, REL_BUCKETS - 1)
    return np.where(n < max_exact, n, large).astype(np.int32)


def _nsa_bias_tables(rel_bias):
    G, R, Q = NSA_GROUPS, NSA_REP, QBLK
    table = rel_bias.astype(F32).reshape(REL_BUCKETS, G, R).transpose(1, 2, 0)
    qq = np.arange(Q)

    def lookup(dist):
        return table[:, :, _t5_bucket_np(dist)].reshape(G, R * Q, dist.shape[1])

    m = np.arange(CMP_BAND) - CMP_BAND_LO
    gs = lookup(qq[:, None] - (CMP_LEN - 1) + CMP_STRIDE * m[None, :])
    far = jnp.broadcast_to(table[:, :, REL_BUCKETS - 1][:, :, None, None], (G, R, Q, 1)).reshape(G, R * Q, 1)
    gs = jnp.concatenate([gs, far, jnp.zeros((G, R * Q, CMP_BIAS_K - CMP_BAND - 1), F32)], axis=-1)
    h, mid, l = _split3(gs)
    gs3 = jnp.stack([h, mid, l], axis=1)
    jj = np.arange(Q)
    d0 = qq[:, None] - jj[None, :]
    b0 = lookup(d0) + jnp.asarray(np.where(np.tile(d0, (R, 1)) >= 0, 0.0, NEG_INF), F32)[None]
    b1 = lookup(d0 + Q)
    b2 = lookup(d0 + 2 * Q)
    bs = jnp.stack([b0, b1, b2], axis=1)
    dw = qq[:, None] + WINDOW - np.arange(WINDOW + Q)[None, :]
    okw = np.tile((dw >= 0) & (dw < WINDOW), (R, 1))
    bw = lookup(dw) + jnp.asarray(np.where(okw, 0.0, NEG_INF), F32)[None]
    return gs3, bs, bw


def _overlap_table(n_cp, n_sb):
    c = np.arange(n_cp)[:, None] * CMP_STRIDE
    s = np.arange(n_sb)[None, :] * SEL_BLOCK
    return jnp.asarray(((c <= s + SEL_BLOCK - 1) & (c + CMP_LEN - 1 >= s)), BF16)


def _pad_cols(w, n):
    return jnp.pad(w, ((0, 0), (0, n - w.shape[1])))


def _pad_rows(w, n):
    return jnp.pad(w, ((0, n - w.shape[0]), (0, 0)))


def _rwkv_layer(h, norm_g, mu, wr, wk, wv, wo, w0, w1, w2, a0, a1, a2, g1, g2,
                k_k, k_a, r_k, lnx_w, lnx_b, v_first, v0, v1, v2):
    bf = lambda w: w.astype(BF16)
    xr, xw, xk, xv, xa, xg = rwkv_prep(h, norm_g, mu)
    r = matmul(xr, bf(wr))
    k = matmul(xk, bf(wk))
    v = matmul(xv, bf(wv))
    lw_hid = matmul(xw, bf(_pad_cols(w1, 128)), act="tanh", out_dtype=BF16)
    w_lin = matmul(lw_hid, bf(_pad_rows(w2, 128)))
    a_lin = matmul(matmul(xa, bf(a1), out_dtype=BF16), bf(a2))
    g = matmul(matmul(xg, bf(g1), act="sigmoid", out_dtype=BF16), bf(g2))
    zeros = jnp.zeros_like(w0)
    if v_first is None:
        v_lin, v0 = None, zeros
    else:
        v_lin = matmul(matmul(xv, bf(_pad_cols(v1, 128)), out_dtype=BF16), bf(_pad_rows(v2, 128)))
    params = jnp.stack([w0, a0, v0, k_k, k_a, r_k.reshape(-1), zeros, zeros])
    m, c, qp, y0, bonus = rwkv_chunks(r, k, v, w_lin, a_lin, v_lin, v_first, params)
    ln_params = jnp.stack([lnx_w, lnx_b] + [zeros] * 6)
    gated = rwkv_state_scan(m, c, qp, y0, bonus, g, ln_params)
    h = matmul(gated, bf(wo), residual=h)
    return h, (v if v_first is None else v_first)


def _mlp(h, norm_g, w_up, w_down):
    xn = rmsnorm(h, norm_g)
    hid = matmul(xn, w_up.astype(BF16), act="relu2", out_dtype=BF16)
    return matmul(hid, w_down.astype(BF16), residual=h)


def _nsa_shared_kv(h, kv_norm_g, w_kv, cmp_pe, cmp_w1, cmp_b1, cmp_w2):
    t = h.shape[0]
    G, D = NSA_GROUPS, NSA_HD
    kv = matmul(rmsnorm(h, kv_norm_g), w_kv.astype(BF16), out_dtype=BF16)
    kv = kv.reshape(t, N_KV_PARTS, G, D).transpose(1, 2, 0, 3)
    x16 = kv[0:2].reshape(2, G, t // CMP_STRIDE, CMP_STRIDE * D)
    half = CMP_STRIDE * D
    w1cat = jnp.concatenate([cmp_w1[:, :half], cmp_w1[:, half:]], axis=-1).astype(BF16)
    cvec = jnp.einsum("plk,plkd->pd", cmp_pe, cmp_w1.reshape(2, CMP_LEN, D, D),
                      precision=lax.Precision.HIGHEST) + cmp_b1
    kvc = compress(x16, w1cat, cvec.reshape(2, 1, D), cmp_w2.astype(BF16))
    pad = ((0, 0), (WINDOW, 0), (0, 0))
    return kvc[0], kvc[1], kv[2], kv[3], jnp.pad(kv[4], pad), jnp.pad(kv[5], pad)


def _nsa_layer(h, norm_g, wq, wo, kvs, tables):
    t = h.shape[0]
    hn = rmsnorm(h, norm_g)
    nq = NSA_HEADS * NSA_HD
    q = matmul(hn, wq[:, :nq].astype(BF16), scale=NSA_HD ** -0.5, out_dtype=BF16)
    gates = matmul(hn, _pad_cols(wq[:, nq:], 128).astype(BF16), act="sigmoid")
    gates = gates[:, :NSA_HEADS * N_BRANCH].reshape(t, NSA_GROUPS, NSA_REP * N_BRANCH).transpose(1, 0, 2)
    o = nsa_attention(q, gates, *kvs, *tables)
    return matmul(o, wo.astype(BF16), residual=h)


def kernel(x, a_norm_g, rw_mu, rw_wr, rw_wk, rw_wv, rw_wo, rw_w0, rw_w1, rw_w2, rw_a0, rw_a1, rw_a2,
           rw_v0, rw_v1, rw_v2, rw_g1, rw_g2, rw_kk, rw_ka, rw_rk, rw_lnx_w, rw_lnx_b, kv_norm_g, w_kv,
           cmp_pe, cmp_w1, cmp_b1, cmp_w2, b_norm_g, nsa_wq, nsa_wo, rel_bias, m_norm_g, mlp_up,
           mlp_down, final_norm_g):
    b, t, d = x.shape
    assert b == 1 and d == D_MODEL and t % QBLK == 0 and t // SEL_BLOCK <= 128
    h = x.reshape(t, d)
    v_first = None
    kvs = tables = None
    for layer in range(DEPTH):
        if layer < N_A_LAYERS:
            i = layer
            vp = (None, None, None) if i == 0 else (rw_v0[i - 1], rw_v1[i - 1], rw_v2[i - 1])
            h, v_first = _rwkv_layer(
                h, a_norm_g[i], rw_mu[i], rw_wr[i], rw_wk[i], rw_wv[i], rw_wo[i], rw_w0[i], rw_w1[i],
                rw_w2[i], rw_a0[i], rw_a1[i], rw_a2[i], rw_g1[i], rw_g2[i], rw_kk[i], rw_ka[i],
                rw_rk[i].reshape(1, RW_HEADS, RW_HEAD), rw_lnx_w[i], rw_lnx_b[i], v_first, *vp)
        else:
            j = layer - N_A_LAYERS
            if j == 0:
                kvs = _nsa_shared_kv(h, kv_norm_g, w_kv, cmp_pe, cmp_w1, cmp_b1, cmp_w2)
                gs3, bs, bw = _nsa_bias_tables(rel_bias)
                tables = (gs3, _overlap_table(t // CMP_STRIDE, 128), bs, bw)
            h = _nsa_layer(h, b_norm_g[j], nsa_wq[j], nsa_wo[j], kvs, tables)
        h = _mlp(h, m_norm_g[layer], mlp_up[layer], mlp_down[layer])
    return rmsnorm(h, final_norm_g, out_dtype=x.dtype).reshape(b, t, d)
```

```python
import functools
import math

import numpy as np
import jax
import jax.numpy as jnp
from jax import lax
from jax.experimental import pallas as pl
from jax.experimental.pallas import tpu as pltpu

F32 = jnp.float32
BF16 = jnp.bfloat16

D_MODEL = 2048
DEPTH = 4
N_A_LAYERS = 2
RW_HEAD = 64
RW_HEADS = D_MODEL // RW_HEAD
N_SHIFT_MIX = 6
GN_EPS = 64e-5
NSA_HEADS = 16
NSA_GROUPS = 4
NSA_REP = NSA_HEADS // NSA_GROUPS
NSA_HD = D_MODEL // NSA_HEADS
N_KV_PARTS = 6
CMP_STRIDE = 16
CMP_LEN = 32
SEL_BLOCK = 64
SEL_TOP = 16
WINDOW = 512
QBLK = 128
N_BRANCH = 3
SEL_FORCE = 1e4
NEG_INF = -1e30
REL_BUCKETS = 32
REL_MAX_DIST = 128
NORM_EPS = 1e-6

VMEM_LIMIT_BYTES = 56 * 1024 * 1024
SCAN_CHUNK = 64
SCAN_HEADS = 4
SCAN_WIDTH = SCAN_HEADS * RW_HEAD
SCAN_UNITS = 2
SEL_TILE = 512
CMP_BAND = 16
CMP_BAND_LO = 6
CMP_BIAS_K = 32


def _cparams(sem):
    return pltpu.CompilerParams(dimension_semantics=sem, vmem_limit_bytes=VMEM_LIMIT_BYTES)


def _split3(x):
    h = x.astype(BF16)
    r = x - h.astype(F32)
    m = r.astype(BF16)
    l = (r - m.astype(F32)).astype(BF16)
    return h, m, l


def _dot(a, b):
    return jnp.dot(a, b, preferred_element_type=F32)


def _dot_nt(a, b):
    return lax.dot_general(a, b, (((1,), (1,)), ((), ())), preferred_element_type=F32)


def _dot_tn(a, b):
    return lax.dot_general(a, b, (((0,), (0,)), ((), ())), preferred_element_type=F32)


def _dot_exact_lhs(a_bf16, x):
    h, m, l = _split3(x)
    return _dot(a_bf16, h) + _dot(a_bf16, m) + _dot(a_bf16, l)


def _dot_exact_rhs(x, b_bf16):
    h, m, l = _split3(x)
    return _dot(h, b_bf16) + _dot(m, b_bf16) + _dot(l, b_bf16)


def _dot_hi(a, b):
    ah = a.astype(BF16)
    al = (a - ah.astype(F32)).astype(BF16)
    bh = b.astype(BF16)
    bl = (b - bh.astype(F32)).astype(BF16)
    return _dot(ah, bh) + _dot(ah, bl) + _dot(al, bh)


def _matmul_kernel(x_ref, w_ref, *rest, act, scale, has_res, nk):
    if has_res:
        res_ref, o_ref, *scratch = rest
    else:
        o_ref, *scratch = rest

    def epilogue(acc):
        if act == "relu2":
            acc = jnp.square(jnp.maximum(acc, 0.0))
        elif act == "tanh":
            acc = jnp.tanh(acc)
        elif act == "sigmoid":
            acc = jax.nn.sigmoid(acc)
        if scale is not None:
            acc = acc * scale
        if has_res:
            acc = acc + res_ref[...]
        o_ref[...] = acc.astype(o_ref.dtype)

    if nk == 1:
        epilogue(_dot(x_ref[...], w_ref[...]))
    else:
        acc_ref, = scratch
        k = pl.program_id(2)

        @pl.when(k == 0)
        def _():
            acc_ref[...] = jnp.zeros_like(acc_ref)

        acc_ref[...] += _dot(x_ref[...], w_ref[...])

        @pl.when(k == nk - 1)
        def _():
            epilogue(acc_ref[...])


def _pick_tile(n, pref):
    t = min(n, pref)
    while n % t:
        t //= 2
    return t


def matmul(x, w, *, act=None, scale=None, residual=None, out_dtype=F32, tm=1024, tn=512, tk=2048):
    m, kdim = x.shape
    _, n = w.shape
    tm, tn, tk = _pick_tile(m, tm), _pick_tile(n, tn), _pick_tile(kdim, tk)
    nk = kdim // tk
    in_specs = [pl.BlockSpec((tm, tk), lambda j, i, k: (i, k)),
                pl.BlockSpec((tk, tn), lambda j, i, k: (k, j))]
    args = [x, w]
    if residual is not None:
        in_specs.append(pl.BlockSpec((tm, tn), lambda j, i, k: (i, j)))
        args.append(residual)
    return pl.pallas_call(
        functools.partial(_matmul_kernel, act=act, scale=scale, has_res=residual is not None, nk=nk),
        out_shape=jax.ShapeDtypeStruct((m, n), out_dtype),
        grid=(n // tn, m // tm, nk),
        in_specs=in_specs,
        out_specs=pl.BlockSpec((tm, tn), lambda j, i, k: (i, j)),
        scratch_shapes=[pltpu.VMEM((tm, tn), F32)] if nk > 1 else [],
        compiler_params=_cparams(("parallel", "parallel", "arbitrary")),
        name="matmul",
    )(*args)


def _rms(x, g):
    return x * lax.rsqrt(jnp.mean(x * x, axis=-1, keepdims=True) + NORM_EPS) * g


def _rmsnorm_kernel(h_ref, g_ref, o_ref):
    o_ref[...] = _rms(h_ref[...], g_ref[...]).astype(o_ref.dtype)


def rmsnorm(h, g, out_dtype=BF16, tm=512):
    t, d = h.shape
    tm = _pick_tile(t, tm)
    return pl.pallas_call(
        _rmsnorm_kernel,
        out_shape=jax.ShapeDtypeStruct((t, d), out_dtype),
        grid=(t // tm,),
        in_specs=[pl.BlockSpec((tm, d), lambda i: (i, 0)), pl.BlockSpec((1, d), lambda i: (0, 0))],
        out_specs=pl.BlockSpec((tm, d), lambda i: (i, 0)),
        compiler_params=_cparams(("parallel",)),
        name="rmsnorm",
    )(h, g.reshape(1, d))


def _rwkv_prep_kernel(h_ref, prev_ref, g_ref, mu_ref, *o_refs):
    g = g_ref[...]
    xn = _rms(h_ref[...], g)
    prev = _rms(prev_ref[...], g)[7:8, :] * (pl.program_id(0) > 0).astype(F32)
    row = lax.broadcasted_iota(jnp.int32, xn.shape, 0)
    shifted = jnp.where(row == 0, prev, pltpu.roll(xn, 1, axis=0))
    xx = shifted - xn
    for i, o_ref in enumerate(o_refs):
        o_ref[...] = (xn + xx * mu_ref[i:i + 1, :]).astype(o_ref.dtype)


def rwkv_prep(h, g, mu, tm=256):
    t, d = h.shape
    tm = _pick_tile(t, tm)
    blk = pl.BlockSpec((tm, d), lambda i: (i, 0))
    return pl.pallas_call(
        _rwkv_prep_kernel,
        out_shape=[jax.ShapeDtypeStruct((t, d), BF16)] * N_SHIFT_MIX,
        grid=(t // tm,),
        in_specs=[blk,
                  pl.BlockSpec((8, d), lambda i: (jnp.maximum(i * (tm // 8) - 1, 0), 0)),
                  pl.BlockSpec((1, d), lambda i: (0, 0)),
                  pl.BlockSpec((N_SHIFT_MIX, d), lambda i: (0, 0))],
        out_specs=[blk] * N_SHIFT_MIX,
        compiler_params=_cparams(("parallel",)),
        name="rwkv_prep",
    )(h, h, g.reshape(1, d), mu)


def _head_mask():
    row = lax.broadcasted_iota(jnp.int32, (SCAN_WIDTH, SCAN_WIDTH), 0)
    col = lax.broadcasted_iota(jnp.int32, (SCAN_WIDTH, SCAN_WIDTH), 1)
    return (row >> 6) == (col >> 6)


def _block_diag(x, head_mask):
    return jnp.where(head_mask, jnp.concatenate([x] * SCAN_HEADS, axis=0), jnp.zeros((), x.dtype))


def _fold_rows(x_bd):
    L = SCAN_CHUNK
    return x_bd[0:L] + x_bd[L:2 * L] + x_bd[2 * L:3 * L] + x_bd[3 * L:4 * L]


def _tri_inverse_bd(a_bd):
    n = a_bd.shape[0]
    row = lax.broadcasted_iota(jnp.int32, (n, n), 0)
    col = lax.broadcasted_iota(jnp.int32, (n, n), 1)
    eye = jnp.where(row == col, 1.0, 0.0)
    d = jnp.where((row >> 4) == (col >> 4), a_bd, 0.0)
    e = a_bd - d
    p = eye + d
    x = d
    for _ in range(3):
        x = _dot_hi(x, x)
        p = p + _dot_hi(p, x)
    n1 = _dot_hi(p, e)
    n2 = _dot_hi(n1, n1)
    n3 = _dot_hi(n1, n2)
    return _dot_hi(eye + n1 + n2 + n3, p)


def _softplus(x):
    return jnp.maximum(x, 0.0) + jnp.log(1.0 + jnp.exp(-jnp.abs(x)))


def _rwkv_chunk_kernel(*refs, mix_v):
    if mix_v:
        r_ref, k_ref, v_ref, wl_ref, al_ref, vl_ref, vf_ref, par_ref, m_ref, c_ref, qp_ref, y0_ref, bo_ref = refs
    else:
        r_ref, k_ref, v_ref, wl_ref, al_ref, par_ref, m_ref, c_ref, qp_ref, y0_ref, bo_ref = refs
    L, W = SCAN_CHUNK, SCAN_WIDTH
    head_mask = _head_mask()
    ones_bd = jnp.where(head_mask, 1.0, 0.0).astype(BF16)
    tri_incl = jnp.where(lax.broadcasted_iota(jnp.int32, (L, L), 1)
                         <= lax.broadcasted_iota(jnp.int32, (L, L), 0), 1.0, 0.0).astype(BF16)
    j_row = lax.broadcasted_iota(jnp.int32, (L, W), 0)
    m_col = lax.broadcasted_iota(jnp.int32, (L, W), 1) & (L - 1)
    strict = m_col < j_row
    incl = m_col <= j_row
    for u in range(SCAN_UNITS):
        cols = slice(u * W, (u + 1) * W)
        par = par_ref[:, cols]
        w0, a0, v0, k_k, k_a, r_k = (par[i:i + 1] for i in range(6))
        r, k, v = r_ref[:, cols], k_ref[:, cols], v_ref[:, cols]
        lw = -jnp.exp(-_softplus(-(w0 + wl_ref[:, cols])) - 0.5)
        a_s = jax.nn.sigmoid(a0 + al_ref[:, cols])
        if mix_v:
            v = v + (vf_ref[:, cols] - v) * jax.nn.sigmoid(v0 + vl_ref[:, cols])
        kk = k * k_k
        kk = kk * lax.rsqrt(jnp.maximum(_dot_exact_rhs(kk * kk, ones_bd), 1e-24))
        k = k * (1.0 + (a_s - 1.0) * k_a)
        bo_ref[:, cols] = _dot_exact_rhs(r * k * r_k, ones_bd) * v
        c = _dot_exact_lhs(tri_incl, lw)
        c_last = c[L - 1:L, :]
        p_inv = jnp.exp(-c)
        p_rest = jnp.exp(c_last - c)
        rt = r * jnp.exp(c)
        at = (-kk * jnp.exp(c - lw)).astype(BF16)
        b = kk * a_s
        ar = jnp.concatenate([at, rt.astype(BF16)], axis=0)
        g_b = _dot_nt(ar, _block_diag((b * p_inv).astype(BF16), head_mask))
        g_k = _dot_nt(ar, _block_diag((k * p_inv).astype(BF16), head_mask))
        a_ab = jnp.where(strict, g_b[:L], 0.0)
        a_ak = jnp.where(strict, g_k[:L], 0.0).astype(BF16)
        a_rb = jnp.where(incl, g_b[L:], 0.0).astype(BF16)
        a_rk = jnp.where(incl, g_k[L:], 0.0).astype(BF16)
        t_inv = _tri_inverse_bd(_block_diag(a_ab, head_mask))
        t_hi = t_inv.astype(BF16)
        t_lo = (t_inv - t_hi.astype(F32)).astype(BF16)
        at_bd = _block_diag(at, head_mask)
        v_bd = _block_diag(v.astype(BF16), head_mask)
        w_bd = (_dot(t_hi, at_bd) + _dot(t_lo, at_bd)).astype(BF16)
        u0_bd = _dot_hi(t_inv, _dot(_block_diag(a_ak, head_mask), v_bd)).astype(BF16)
        qp_ref[:, cols] = (rt + _dot(a_rb, w_bd)).astype(qp_ref.dtype)
        y0_ref[:, cols] = _dot(a_rb, u0_bd) + _dot(a_rk, v_bd)
        bh_bd = _block_diag((b * p_rest).astype(BF16), head_mask)
        kh_bd = _block_diag((k * p_rest).astype(BF16), head_mask)
        decay_diag = jnp.where(m_col == j_row, jnp.exp(c_last), 0.0)
        m_ref[:, cols] = _fold_rows(_dot_tn(bh_bd, w_bd)) + decay_diag
        c_ref[:, cols] = _fold_rows(_dot_tn(bh_bd, u0_bd) + _dot_tn(kh_bd, v_bd))


def rwkv_chunks(r, k, v, w_lin, a_lin, v_lin, v_first, params):
    t, d = r.shape
    bw = SCAN_UNITS * SCAN_WIDTH
    blk = pl.BlockSpec((SCAN_CHUNK, bw), lambda ci, gi: (ci, gi))
    mix_v = v_lin is not None
    args = [r, k, v, w_lin, a_lin] + ([v_lin, v_first] if mix_v else []) + [params]
    f32 = jax.ShapeDtypeStruct((t, d), F32)
    return pl.pallas_call(
        functools.partial(_rwkv_chunk_kernel, mix_v=mix_v),
        out_shape=[f32, f32, jax.ShapeDtypeStruct((t, d), BF16), f32, f32],
        grid=(t // SCAN_CHUNK, d // bw),
        in_specs=[blk] * (len(args) - 1) + [pl.BlockSpec((8, bw), lambda ci, gi: (0, gi))],
        out_specs=[blk] * 5,
        compiler_params=_cparams(("parallel", "parallel")),
        name="rwkv_chunks",
    )(*args)


def _rwkv_state_kernel(m_ref, c_ref, qp_ref, y0_ref, bo_ref, g_ref, ln_ref, o_ref, st_ref):
    @pl.when(pl.program_id(1) == 0)
    def _():
        st_ref[...] = jnp.zeros_like(st_ref)

    W = SCAN_WIDTH
    head_mask = _head_mask()
    ones_bd = jnp.where(head_mask, 1.0, 0.0).astype(BF16)
    for u in range(SCAN_UNITS):
        cols = slice(u * W, (u + 1) * W)
        st = st_ref[u]
        y = _dot(qp_ref[:, cols], st.astype(BF16)) + y0_ref[:, cols]
        mean = _dot_exact_rhs(y, ones_bd) * (1.0 / RW_HEAD)
        dev = y - mean
        var = _dot_exact_rhs(dev * dev, ones_bd) * (1.0 / RW_HEAD)
        ln = ln_ref[:, cols]
        out = (dev * lax.rsqrt(var + GN_EPS) * ln[0:1] + ln[1:2] + bo_ref[:, cols]) * g_ref[:, cols]
        o_ref[:, cols] = out.astype(o_ref.dtype)
        st_ref[u] = _dot_hi(_block_diag(m_ref[:, cols], head_mask), st) + _block_diag(c_ref[:, cols], head_mask)


def rwkv_state_scan(m, c, qp, y0, bonus, g, ln_params):
    t, d = m.shape
    bw = SCAN_UNITS * SCAN_WIDTH
    blk = pl.BlockSpec((SCAN_CHUNK, bw), lambda gi, ci: (ci, gi))
    return pl.pallas_call(
        _rwkv_state_kernel,
        out_shape=jax.ShapeDtypeStruct((t, d), BF16),
        grid=(d // bw, t // SCAN_CHUNK),
        in_specs=[blk] * 6 + [pl.BlockSpec((8, bw), lambda gi, ci: (0, gi))],
        out_specs=blk,
        scratch_shapes=[pltpu.VMEM((SCAN_UNITS, SCAN_WIDTH, SCAN_WIDTH), F32)],
        compiler_params=_cparams(("parallel", "arbitrary")),
        name="rwkv_state_scan",
    )(m, c, qp, y0, bonus, g, ln_params)


def _compress_kernel(x_ref, w1_ref, c_ref, w2_ref, o_ref):
    ab = _dot(x_ref[...], w1_ref[...])
    a = ab[:, :NSA_HD]
    bsh = ab[:, NSA_HD:]
    n16 = a.shape[0]
    row = lax.broadcasted_iota(jnp.int32, a.shape, 0)
    b_next = jnp.where(row == n16 - 1, 0.0, pltpu.roll(bsh, n16 - 1, axis=0))
    hid = a + b_next + c_ref[...]
    hid = jax.nn.gelu(hid, approximate=True)
    o_ref[...] = _dot(hid.astype(BF16), w2_ref[...]).astype(o_ref.dtype)


def compress(x16, w1cat, cvec, w2):
    _, g, n16, kd = x16.shape
    d = NSA_HD
    return pl.pallas_call(
        _compress_kernel,
        out_shape=jax.ShapeDtypeStruct((2, g, n16, d), BF16),
        grid=(2, g),
        in_specs=[pl.BlockSpec((None, None, n16, kd), lambda p, gi: (p, gi, 0, 0)),
                  pl.BlockSpec((None, kd, 2 * d), lambda p, gi: (p, 0, 0)),
                  pl.BlockSpec((None, 1, d), lambda p, gi: (p, 0, 0)),
                  pl.BlockSpec((None, d, d), lambda p, gi: (p, 0, 0))],
        out_specs=pl.BlockSpec((None, None, n16, d), lambda p, gi: (p, gi, 0, 0)),
        compiler_params=_cparams(("parallel", "parallel")),
        name="nsa_compress",
    )(x16, w1cat, cvec, w2)


def _softmax_rows(s):
    mx = jnp.max(s, axis=-1, keepdims=True)
    e = jnp.exp(s - mx)
    return e / jnp.sum(e, axis=-1, keepdims=True)


def _nsa_kernel(q_ref, gate_ref, kc_ref, vc_ref, ks_ref, vs_ref, kw_ref, vw_ref,
                gs_ref, ov_ref, bs_ref, bw_ref, o_ref):
    qi = pl.program_id(1)
    R, Q, D = NSA_REP, QBLK, NSA_HD
    q = q_ref[...]
    qs = jnp.concatenate([q[:, r * D:(r + 1) * D] for r in range(R)], axis=0)
    n_cp = kc_ref.shape[0]

    t_rows = qi * Q + (lax.broadcasted_iota(jnp.int32, (R * Q, 1), 0) & (Q - 1))
    c_idx = lax.broadcasted_iota(jnp.int32, (1, n_cp), 1)
    m_off = qi * (Q // CMP_STRIDE) - lax.broadcasted_iota(jnp.int32, (CMP_BIAS_K, n_cp), 1) + CMP_BAND_LO
    k_idx = lax.broadcasted_iota(jnp.int32, (CMP_BIAS_K, n_cp), 0)
    in_band = ((k_idx < CMP_BAND) & (m_off == k_idx)) | ((k_idx == CMP_BAND) & (m_off >= CMP_BAND))
    sel_mat = jnp.where(in_band, 1.0, 0.0).astype(BF16)
    gs = gs_ref[...]
    bias_c = _dot(gs[0], sel_mat) + _dot(gs[1], sel_mat) + _dot(gs[2], sel_mat)
    m_c = (c_idx * CMP_STRIDE + (CMP_LEN - 1)) <= t_rows
    s_c = jnp.where(m_c, _dot_nt(qs, kc_ref[...]) + bias_c, NEG_INF)
    p_c = jnp.where(m_c, _softmax_rows(s_c), 0.0)
    o_c = _dot(p_c.astype(BF16), vc_ref[...])

    w0 = pl.multiple_of(qi * Q, Q)
    kwb = kw_ref[pl.ds(w0, WINDOW + Q), :]
    vwb = vw_ref[pl.ds(w0, WINDOW + Q), :]
    j_io = lax.broadcasted_iota(jnp.int32, (1, WINDOW + Q), 1)
    s_w = jnp.where(j_io >= WINDOW - qi * Q, _dot_nt(qs, kwb) + bw_ref[...], NEG_INF)
    o_w = _dot(_softmax_rows(s_w).astype(BF16), vwb)

    p_sum = p_c[0:Q] + p_c[Q:2 * Q] + p_c[2 * Q:3 * Q] + p_c[3 * Q:4 * Q]
    imp_t = _dot_exact_rhs(p_sum, ov_ref[...]).T
    n_sb = imp_t.shape[0]
    s_row = lax.broadcasted_iota(jnp.int32, (n_sb, Q), 0)
    t_col = qi * Q + lax.broadcasted_iota(jnp.int32, (n_sb, Q), 1)
    forced = (s_row == (t_col >> 6)) | (s_row == 0)
    score = jnp.where(forced, SEL_FORCE, jnp.where(s_row * SEL_BLOCK <= t_col, imp_t, -SEL_FORCE))
    sel_t = jnp.zeros((n_sb, Q), F32)
    s_f = s_row.astype(F32)
    for _ in range(SEL_TOP):
        mx = jnp.max(score, axis=0, keepdims=True)
        first = jnp.min(jnp.where(score == mx, s_f, float(n_sb)), axis=0, keepdims=True)
        pick = s_f == first
        sel_t = jnp.where(pick, 1.0, sel_t)
        score = jnp.where(pick, -3e38, score)
    sel_tb = sel_t.astype(BF16)

    def sel_mask(first_sb, n_keys, limit_sb):
        sb = first_sb + (lax.broadcasted_iota(jnp.int32, (n_sb, n_keys), 1) >> 6)
        hit = (lax.broadcasted_iota(jnp.int32, (n_sb, n_keys), 0) == sb) & (sb < limit_sb)
        return jnp.where(_dot_tn(sel_tb, jnp.where(hit, 1.0, 0.0).astype(BF16)) > 0.5, 0.0, NEG_INF)

    def add_mask(s, madd):
        return jnp.concatenate([s[r * Q:(r + 1) * Q] + madd for r in range(R)], axis=0)

    nb0 = jnp.maximum(qi - 1, 0)
    n0 = pl.multiple_of(nb0 * Q, Q)
    s = _dot_nt(qs, ks_ref[pl.ds(n0, 2 * Q), :]) + bs_ref[jnp.where(qi == 0, 1, 0)]
    s = add_mask(s, sel_mask(2 * nb0, 2 * Q, n_sb))
    m_0 = jnp.max(s, axis=-1, keepdims=True)
    p = jnp.exp(s - m_0)
    l_0 = jnp.sum(p, axis=-1, keepdims=True)
    acc_0 = _dot(p.astype(BF16), vs_ref[pl.ds(n0, 2 * Q), :])

    n_far = jnp.maximum(qi - 1, 0)
    blocks_per_tile = SEL_TILE // Q

    def far_body(it, carry):
        m_run, l_run, acc = carry
        start = pl.multiple_of(it * SEL_TILE, SEL_TILE)
        madd = sel_mask(it * (SEL_TILE // SEL_BLOCK), SEL_TILE, 2 * n_far)
        s = add_mask(_dot_nt(qs, ks_ref[pl.ds(start, SEL_TILE), :]), madd)
        m_new = jnp.maximum(m_run, jnp.max(s, axis=-1, keepdims=True))
        alpha = jnp.exp(m_run - m_new)
        p = jnp.exp(s - m_new)
        l_new = alpha * l_run + jnp.sum(p, axis=-1, keepdims=True)
        acc_new = alpha * acc + _dot(p.astype(BF16), vs_ref[pl.ds(start, SEL_TILE), :])
        return m_new, l_new, acc_new

    n_tiles = (n_far + blocks_per_tile - 1) // blocks_per_tile
    _, l_s, acc_s = lax.fori_loop(0, n_tiles, far_body, (m_0, l_0, acc_0))
    o_s = acc_s / l_s

    gates = gate_ref[...]
    for r in range(R):
        rows = slice(r * Q, (r + 1) * Q)
        o = (gates[:, 3 * r:3 * r + 1] * o_c[rows] + gates[:, 3 * r + 1:3 * r + 2] * o_s[rows]
             + gates[:, 3 * r + 2:3 * r + 3] * o_w[rows])
        o_ref[:, r * D:(r + 1) * D] = o.astype(o_ref.dtype)


def nsa_attention(q, gates, kc, vc, ks, vs, kw, vw, gs, ov, bs, bw):
    t = q.shape[0]
    g, n_cp, d = kc.shape
    rq = NSA_REP * QBLK
    per_g = lambda shape: pl.BlockSpec((None,) + shape, lambda gi, qi: (gi,) + (0,) * len(shape))
    return pl.pallas_call(
        _nsa_kernel,
        out_shape=jax.ShapeDtypeStruct((t, NSA_HEADS * d), BF16),
        grid=(g, t // QBLK),
        in_specs=[pl.BlockSpec((QBLK, NSA_REP * d), lambda gi, qi: (qi, gi)),
                  pl.BlockSpec((None, QBLK, NSA_REP * N_BRANCH), lambda gi, qi: (gi, qi, 0)),
                  per_g((n_cp, d)), per_g((n_cp, d)),
                  per_g((t, d)), per_g((t, d)),
                  per_g((t + WINDOW, d)), per_g((t + WINDOW, d)),
                  per_g((3, rq, CMP_BIAS_K)),
                  pl.BlockSpec(ov.shape, lambda gi, qi: (0, 0)),
                  per_g((2, rq, 2 * QBLK)),
                  per_g((rq, WINDOW + QBLK))],
        out_specs=pl.BlockSpec((QBLK, NSA_REP * d), lambda gi, qi: (qi, gi)),
        compiler_params=_cparams(("parallel", "arbitrary")),
        name="nsa_attention",
    )(q, gates, kc, vc, ks, vs, kw, vw, gs, ov, bs, bw)


def _t5_bucket_np(n):
    n = np.maximum(n, 0)
    max_exact = REL_BUCKETS // 2
    nf = np.maximum(n, 1).astype(np.float64)
    large = max_exact + (np.log(nf / max_exact) / math.log(REL_MAX_DIST / max_exact)
                         * (REL_BUCKETS - max_exact)).astype(np.int64)
    large = np.minimum(large, REL_BUCKETS - 1)
    return np.where(n < max_exact, n, large).astype(np.int32)


def _nsa_bias_tables(rel_bias):
    G, R, Q = NSA_GROUPS, NSA_REP, QBLK
    table = rel_bias.astype(F32).reshape(REL_BUCKETS, G, R).transpose(1, 2, 0)
    qq = np.arange(Q)

    def lookup(dist):
        return table[:, :, _t5_bucket_np(dist)].reshape(G, R * Q, dist.shape[1])

    m = np.arange(CMP_BAND) - CMP_BAND_LO
    gs = lookup(qq[:, None] - (CMP_LEN - 1) + CMP_STRIDE * m[None, :])
    far = jnp.broadcast_to(table[:, :, REL_BUCKETS - 1][:, :, None, None], (G, R, Q, 1)).reshape(G, R * Q, 1)
    gs = jnp.concatenate([gs, far, jnp.zeros((G, R * Q, CMP_BIAS_K - CMP_BAND - 1), F32)], axis=-1)
    h, mid, l = _split3(gs)
    gs3 = jnp.stack([h, mid, l], axis=1)
    d0 = np.tile(qq[:, None] - qq[None, :], (R, 1))
    causal = jnp.asarray(np.where(d0 >= 0, 0.0, NEG_INF), F32)[None]
    b0 = lookup(qq[:, None] - qq[None, :]) + causal
    b1 = lookup(qq[:, None] - qq[None, :] + Q)
    far_c = jnp.broadcast_to(far, (G, R * Q, Q))
    neg = jnp.full((G, R * Q, Q), NEG_INF, F32)
    bs = jnp.stack([jnp.concatenate([b1 - far_c, b0 - far_c], -1),
                    jnp.concatenate([b0 - far_c, neg], -1)], axis=1)
    oldest = far_c + jnp.asarray(np.where(d0 < 0, 0.0, NEG_INF), F32)[None]
    bw = jnp.concatenate([oldest, far_c, far_c, b1, b0], -1)
    return gs3, bs, bw


def _overlap_table(n_cp, n_sb):
    c = np.arange(n_cp)[:, None] * CMP_STRIDE
    s = np.arange(n_sb)[None, :] * SEL_BLOCK
    return jnp.asarray(((c <= s + SEL_BLOCK - 1) & (c + CMP_LEN - 1 >= s)), BF16)


def _pad_cols(w, n):
    return jnp.pad(w, ((0, 0), (0, n - w.shape[1])))


def _pad_rows(w, n):
    return jnp.pad(w, ((0, n - w.shape[0]), (0, 0)))


def _rwkv_layer(h, norm_g, mu, wr, wk, wv, wo, w0, w1, w2, a0, a1, a2, g1, g2,
                k_k, k_a, r_k, lnx_w, lnx_b, v_first, v0, v1, v2):
    bf = lambda w: w.astype(BF16)
    xr, xw, xk, xv, xa, xg = rwkv_prep(h, norm_g, mu)
    r = matmul(xr, bf(wr))
    k = matmul(xk, bf(wk))
    v = matmul(xv, bf(wv))
    lw_hid = matmul(xw, bf(_pad_cols(w1, 128)), act="tanh", out_dtype=BF16)
    w_lin = matmul(lw_hid, bf(_pad_rows(w2, 128)))
    a_lin = matmul(matmul(xa, bf(a1), out_dtype=BF16), bf(a2))
    g = matmul(matmul(xg, bf(g1), act="sigmoid", out_dtype=BF16), bf(g2))
    zeros = jnp.zeros_like(w0)
    if v_first is None:
        v_lin, v0 = None, zeros
    else:
        v_lin = matmul(matmul(xv, bf(_pad_cols(v1, 128)), out_dtype=BF16), bf(_pad_rows(v2, 128)))
    params = jnp.stack([w0, a0, v0, k_k, k_a, r_k.reshape(-1), zeros, zeros])
    m, c, qp, y0, bonus = rwkv_chunks(r, k, v, w_lin, a_lin, v_lin, v_first, params)
    ln_params = jnp.stack([lnx_w, lnx_b] + [zeros] * 6)
    gated = rwkv_state_scan(m, c, qp, y0, bonus, g, ln_params)
    h = matmul(gated, bf(wo), residual=h)
    return h, (v if v_first is None else v_first)


def _mlp(h, norm_g, w_up, w_down):
    xn = rmsnorm(h, norm_g)
    hid = matmul(xn, w_up.astype(BF16), act="relu2", out_dtype=BF16)
    return matmul(hid, w_down.astype(BF16), residual=h)


def _nsa_shared_kv(h, kv_norm_g, w_kv, cmp_pe, cmp_w1, cmp_b1, cmp_w2):
    t = h.shape[0]
    G, D = NSA_GROUPS, NSA_HD
    kv = matmul(rmsnorm(h, kv_norm_g), w_kv.astype(BF16), out_dtype=BF16)
    kv = kv.reshape(t, N_KV_PARTS, G, D).transpose(1, 2, 0, 3)
    x16 = kv[0:2].reshape(2, G, t // CMP_STRIDE, CMP_STRIDE * D)
    half = CMP_STRIDE * D
    w1cat = jnp.concatenate([cmp_w1[:, :half], cmp_w1[:, half:]], axis=-1).astype(BF16)
    cvec = jnp.einsum("plk,plkd->pd", cmp_pe, cmp_w1.reshape(2, CMP_LEN, D, D),
                      precision=lax.Precision.HIGHEST) + cmp_b1
    kvc = compress(x16, w1cat, cvec.reshape(2, 1, D), cmp_w2.astype(BF16))
    pad = ((0, 0), (WINDOW, 0), (0, 0))
    return kvc[0], kvc[1], kv[2], kv[3], jnp.pad(kv[4], pad), jnp.pad(kv[5], pad)


def _nsa_layer(h, norm_g, wq, wo, kvs, tables):
    t = h.shape[0]
    hn = rmsnorm(h, norm_g)
    nq = NSA_HEADS * NSA_HD
    q = matmul(hn, wq[:, :nq].astype(BF16), scale=NSA_HD ** -0.5, out_dtype=BF16)
    gates = matmul(hn, _pad_cols(wq[:, nq:], 128).astype(BF16), act="sigmoid")
    gates = gates[:, :NSA_HEADS * N_BRANCH].reshape(t, NSA_GROUPS, NSA_REP * N_BRANCH).transpose(1, 0, 2)
    o = nsa_attention(q, gates, *kvs, *tables)
    return matmul(o, wo.astype(BF16), residual=h)


def kernel(x, a_norm_g, rw_mu, rw_wr, rw_wk, rw_wv, rw_wo, rw_w0, rw_w1, rw_w2, rw_a0, rw_a1, rw_a2,
           rw_v0, rw_v1, rw_v2, rw_g1, rw_g2, rw_kk, rw_ka, rw_rk, rw_lnx_w, rw_lnx_b, kv_norm_g, w_kv,
           cmp_pe, cmp_w1, cmp_b1, cmp_w2, b_norm_g, nsa_wq, nsa_wo, rel_bias, m_norm_g, mlp_up,
           mlp_down, final_norm_g):
    b, t, d = x.shape
    assert b == 1 and d == D_MODEL and t % SEL_TILE == 0 and t // SEL_BLOCK <= 128 and WINDOW == 4 * QBLK
    h = x.reshape(t, d)
    v_first = None
    kvs = tables = None
    for layer in range(DEPTH):
        if layer < N_A_LAYERS:
            i = layer
            vp = (None, None, None) if i == 0 else (rw_v0[i - 1], rw_v1[i - 1], rw_v2[i - 1])
            h, v_first = _rwkv_layer(
                h, a_norm_g[i], rw_mu[i], rw_wr[i], rw_wk[i], rw_wv[i], rw_wo[i], rw_w0[i], rw_w1[i],
                rw_w2[i], rw_a0[i], rw_a1[i], rw_a2[i], rw_g1[i], rw_g2[i], rw_kk[i], rw_ka[i],
                rw_rk[i].reshape(1, RW_HEADS, RW_HEAD), rw_lnx_w[i], rw_lnx_b[i], v_first, *vp)
        else:
            j = layer - N_A_LAYERS
            if j == 0:
                kvs = _nsa_shared_kv(h, kv_norm_g, w_kv, cmp_pe, cmp_w1, cmp_b1, cmp_w2)
                gs3, bs, bw = _nsa_bias_tables(rel_bias)
                tables = (gs3, _overlap_table(t // CMP_STRIDE, 128), bs, bw)
            h = _nsa_layer(h, b_norm_g[j], nsa_wq[j], nsa_wo[j], kvs, tables)
        h = _mlp(h, m_norm_g[layer], mlp_up[layer], mlp_down[layer])
    return rmsnorm(h, final_norm_g, out_dtype=x.dtype).reshape(b, t, d)
```

```python
import functools
import math

import numpy as np
import jax
import jax.numpy as jnp
from jax import lax
from jax.experimental import pallas as pl
from jax.experimental.pallas import tpu as pltpu

F32 = jnp.float32
BF16 = jnp.bfloat16

D_MODEL = 2048
DEPTH = 4
N_A_LAYERS = 2
RW_HEAD = 64
RW_HEADS = D_MODEL // RW_HEAD
N_SHIFT_MIX = 6
GN_EPS = 64e-5
NSA_HEADS = 16
NSA_GROUPS = 4
NSA_REP = NSA_HEADS // NSA_GROUPS
NSA_HD = D_MODEL // NSA_HEADS
N_KV_PARTS = 6
CMP_STRIDE = 16
CMP_LEN = 32
SEL_BLOCK = 64
SEL_TOP = 16
WINDOW = 512
QBLK = 128
N_BRANCH = 3
SEL_FORCE = 1e4
NEG_INF = -1e30
REL_BUCKETS = 32
REL_MAX_DIST = 128
NORM_EPS = 1e-6

VMEM_LIMIT_BYTES = 56 * 1024 * 1024
SCAN_CHUNK = 64
SCAN_HEADS = 4
SCAN_WIDTH = SCAN_HEADS * RW_HEAD
SCAN_UNITS = 4
SEL_TILE = 512
CMP_BAND = 16
CMP_BAND_LO = 6
CMP_BIAS_K = 32


def _cparams(sem):
    return pltpu.CompilerParams(dimension_semantics=sem, vmem_limit_bytes=VMEM_LIMIT_BYTES)


def _split3(x):
    h = x.astype(BF16)
    r = x - h.astype(F32)
    m = r.astype(BF16)
    l = (r - m.astype(F32)).astype(BF16)
    return h, m, l


def _dot(a, b):
    return jnp.dot(a, b, preferred_element_type=F32)


def _dot_nt(a, b):
    return lax.dot_general(a, b, (((1,), (1,)), ((), ())), preferred_element_type=F32)


def _dot_tn(a, b):
    return lax.dot_general(a, b, (((0,), (0,)), ((), ())), preferred_element_type=F32)


def _dot_exact_lhs(a_bf16, x):
    h, m, l = _split3(x)
    return _dot(a_bf16, h) + _dot(a_bf16, m) + _dot(a_bf16, l)


def _dot_exact_rhs(x, b_bf16):
    h, m, l = _split3(x)
    return _dot(h, b_bf16) + _dot(m, b_bf16) + _dot(l, b_bf16)


def _matmul_kernel(x_ref, w_ref, *rest, act, scale, has_res, nk):
    if has_res:
        res_ref, o_ref, *scratch = rest
    else:
        o_ref, *scratch = rest

    def epilogue(acc):
        if act == "relu2":
            acc = jnp.square(jnp.maximum(acc, 0.0))
        elif act == "tanh":
            acc = jnp.tanh(acc)
        elif act == "sigmoid":
            acc = jax.nn.sigmoid(acc)
        if scale is not None:
            acc = acc * scale
        if has_res:
            acc = acc + res_ref[...]
        o_ref[...] = acc.astype(o_ref.dtype)

    if nk == 1:
        epilogue(_dot(x_ref[...], w_ref[...]))
    else:
        acc_ref, = scratch
        k = pl.program_id(2)

        @pl.when(k == 0)
        def _():
            acc_ref[...] = jnp.zeros_like(acc_ref)

        acc_ref[...] += _dot(x_ref[...], w_ref[...])

        @pl.when(k == nk - 1)
        def _():
            epilogue(acc_ref[...])


def _pick_tile(n, pref):
    t = min(n, pref)
    while n % t:
        t //= 2
    return t


def matmul(x, w, *, act=None, scale=None, residual=None, out_dtype=F32, tm=1024, tn=512, tk=2048):
    m, kdim = x.shape
    _, n = w.shape
    tm, tn, tk = _pick_tile(m, tm), _pick_tile(n, tn), _pick_tile(kdim, tk)
    nk = kdim // tk
    in_specs = [pl.BlockSpec((tm, tk), lambda j, i, k: (i, k)),
                pl.BlockSpec((tk, tn), lambda j, i, k: (k, j))]
    args = [x, w]
    if residual is not None:
        in_specs.append(pl.BlockSpec((tm, tn), lambda j, i, k: (i, j)))
        args.append(residual)
    return pl.pallas_call(
        functools.partial(_matmul_kernel, act=act, scale=scale, has_res=residual is not None, nk=nk),
        out_shape=jax.ShapeDtypeStruct((m, n), out_dtype),
        grid=(n // tn, m // tm, nk),
        in_specs=in_specs,
        out_specs=pl.BlockSpec((tm, tn), lambda j, i, k: (i, j)),
        scratch_shapes=[pltpu.VMEM((tm, tn), F32)] if nk > 1 else [],
        compiler_params=_cparams(("parallel", "parallel", "arbitrary")),
        name="matmul",
    )(*args)


def _rms(x, g):
    return x * lax.rsqrt(jnp.mean(x * x, axis=-1, keepdims=True) + NORM_EPS) * g


def _rmsnorm_kernel(h_ref, g_ref, o_ref):
    o_ref[...] = _rms(h_ref[...], g_ref[...]).astype(o_ref.dtype)


def rmsnorm(h, g, out_dtype=BF16, tm=512):
    t, d = h.shape
    tm = _pick_tile(t, tm)
    return pl.pallas_call(
        _rmsnorm_kernel,
        out_shape=jax.ShapeDtypeStruct((t, d), out_dtype),
        grid=(t // tm,),
        in_specs=[pl.BlockSpec((tm, d), lambda i: (i, 0)), pl.BlockSpec((1, d), lambda i: (0, 0))],
        out_specs=pl.BlockSpec((tm, d), lambda i: (i, 0)),
        compiler_params=_cparams(("parallel",)),
        name="rmsnorm",
    )(h, g.reshape(1, d))


def _rwkv_prep_kernel(h_ref, prev_ref, g_ref, mu_ref, *o_refs):
    g = g_ref[...]
    xn = _rms(h_ref[...], g)
    prev = _rms(prev_ref[...], g)[7:8, :] * (pl.program_id(0) > 0).astype(F32)
    row = lax.broadcasted_iota(jnp.int32, xn.shape, 0)
    shifted = jnp.where(row == 0, prev, pltpu.roll(xn, 1, axis=0))
    xx = shifted - xn
    for i, o_ref in enumerate(o_refs):
        o_ref[...] = (xn + xx * mu_ref[i:i + 1, :]).astype(o_ref.dtype)


def rwkv_prep(h, g, mu, tm=256):
    t, d = h.shape
    tm = _pick_tile(t, tm)
    blk = pl.BlockSpec((tm, d), lambda i: (i, 0))
    return pl.pallas_call(
        _rwkv_prep_kernel,
        out_shape=[jax.ShapeDtypeStruct((t, d), BF16)] * N_SHIFT_MIX,
        grid=(t // tm,),
        in_specs=[blk,
                  pl.BlockSpec((8, d), lambda i: (jnp.maximum(i * (tm // 8) - 1, 0), 0)),
                  pl.BlockSpec((1, d), lambda i: (0, 0)),
                  pl.BlockSpec((N_SHIFT_MIX, d), lambda i: (0, 0))],
        out_specs=[blk] * N_SHIFT_MIX,
        compiler_params=_cparams(("parallel",)),
        name="rwkv_prep",
    )(h, h, g.reshape(1, d), mu)


def _head_mask(dtype):
    row = lax.broadcasted_iota(jnp.int32, (SCAN_WIDTH, SCAN_WIDTH), 0)
    col = lax.broadcasted_iota(jnp.int32, (SCAN_WIDTH, SCAN_WIDTH), 1)
    return jnp.where((row >> 6) == (col >> 6), 1.0, 0.0).astype(dtype)


def _block_diag(x, mask):
    return jnp.concatenate([x] * SCAN_HEADS, axis=0) * mask


def _fold_rows(x_full, mask_f32):
    x = x_full * mask_f32
    L = SCAN_CHUNK
    return x[0:L] + x[L:2 * L] + x[2 * L:3 * L] + x[3 * L:4 * L]


def _split2(x):
    hi = x.astype(BF16)
    return hi, (x - hi.astype(F32)).astype(BF16)


def _bd_parts(y, mask_b):
    hi, lo = _split2(y)
    return _block_diag(hi, mask_b), _block_diag(lo, mask_b)


def _mm_heads(x, y_parts):
    y_hi, y_lo = y_parts
    n = x.shape[0]
    x_hi, x_lo = _split2(x)
    both = _dot(jnp.concatenate([x_hi, x_lo], axis=0), y_hi)
    return both[:n] + both[n:] + _dot(x_hi, y_lo)


def _run_interleaved(stage_generators):
    live = list(stage_generators)
    while live:
        still = []
        for gen in live:
            try:
                next(gen)
                still.append(gen)
            except StopIteration:
                pass
        live = still


def _tri_inverse_heads(a, j_row, m_col, mask_b):
    L = SCAN_CHUNK
    eye = jnp.where(m_col == j_row, 1.0, 0.0)
    d = jnp.where((j_row >> 4) == (m_col >> 4), a, 0.0)
    e = a - d
    p = eye + d
    x = _mm_heads(d, _bd_parts(d, mask_b))
    yield
    for _ in range(2):
        both = _mm_heads(jnp.concatenate([x, p], axis=0), _bd_parts(x, mask_b))
        x, p = both[:L], p + both[L:]
        yield
    p = p + _mm_heads(p, _bd_parts(x, mask_b))
    yield
    n1 = _mm_heads(p, _bd_parts(e, mask_b))
    yield
    n2 = _mm_heads(n1, _bd_parts(n1, mask_b))
    yield
    q = eye + n1
    q = q + _mm_heads(q, _bd_parts(n2, mask_b))
    yield
    return _mm_heads(q, _bd_parts(p, mask_b))


def _softplus(x):
    return jnp.maximum(x, 0.0) + jnp.log(1.0 + jnp.exp(-jnp.abs(x)))


def _rwkv_chunk_kernel(*refs, mix_v):
    if mix_v:
        r_ref, k_ref, v_ref, wl_ref, al_ref, vl_ref, vf_ref, par_ref, m_ref, c_ref, qp_ref, y0_ref, bo_ref = refs
    else:
        r_ref, k_ref, v_ref, wl_ref, al_ref, par_ref, m_ref, c_ref, qp_ref, y0_ref, bo_ref = refs
    L, W = SCAN_CHUNK, SCAN_WIDTH
    mask_b = _head_mask(BF16)
    mask_f = _head_mask(F32)
    tri_incl = jnp.where(lax.broadcasted_iota(jnp.int32, (L, L), 1)
                         <= lax.broadcasted_iota(jnp.int32, (L, L), 0), 1.0, 0.0).astype(BF16)
    j_row = lax.broadcasted_iota(jnp.int32, (L, W), 0)
    m_col = lax.broadcasted_iota(jnp.int32, (L, W), 1) & (L - 1)
    strict = m_col < j_row
    incl = m_col <= j_row

    def unit(u):
        cols = slice(u * W, (u + 1) * W)
        par = par_ref[:, cols]
        w0, a0, v0, k_k, k_a, r_k = (par[i:i + 1] for i in range(6))
        r, k, v = r_ref[:, cols], k_ref[:, cols], v_ref[:, cols]
        lw = -jnp.exp(-_softplus(-(w0 + wl_ref[:, cols])) - 0.5)
        a_s = jax.nn.sigmoid(a0 + al_ref[:, cols])
        if mix_v:
            v = v + (vf_ref[:, cols] - v) * jax.nn.sigmoid(v0 + vl_ref[:, cols])
        kk = k * k_k
        k = k * (1.0 + (a_s - 1.0) * k_a)
        kk_sq = _dot_exact_rhs(kk * kk, mask_b)
        rk_sum = _dot_exact_rhs(r * k * r_k, mask_b)
        c = _dot_exact_lhs(tri_incl, lw)
        yield
        kk = kk * lax.rsqrt(jnp.maximum(kk_sq, 1e-24))
        bo_ref[:, cols] = rk_sum * v
        c_last = c[L - 1:L, :]
        p_inv = jnp.exp(-c)
        p_rest = jnp.exp(c_last - c)
        rt = r * jnp.exp(c)
        at = (-kk * jnp.exp(c - lw)).astype(BF16)
        b = kk * a_s
        vb = v.astype(BF16)
        ar = jnp.concatenate([at, rt.astype(BF16)], axis=0)
        g_b = _dot_nt(ar, _block_diag((b * p_inv).astype(BF16), mask_b))
        g_k = _dot_nt(ar, _block_diag((k * p_inv).astype(BF16), mask_b))
        yield
        a_ab = jnp.where(strict, g_b[:L], 0.0)
        a_ak = jnp.where(strict, g_k[:L], 0.0).astype(BF16)
        a_rb = jnp.where(incl, g_b[L:], 0.0).astype(BF16)
        a_rk = jnp.where(incl, g_k[L:], 0.0).astype(BF16)
        v_bd = _block_diag(vb, mask_b)
        akv = _dot(a_ak, v_bd)
        t_inv = yield from _tri_inverse_heads(a_ab, j_row, m_col, mask_b)
        yield
        t_hi, t_lo = _split2(t_inv)
        w_both = _dot(jnp.concatenate([t_hi, t_lo], axis=0), _block_diag(at, mask_b))
        u0 = _mm_heads(t_inv, _bd_parts(akv, mask_b)).astype(BF16)
        yield
        w = (w_both[:L] + w_both[L:]).astype(BF16)
        qp_ref[:, cols] = (rt + _dot(a_rb, _block_diag(w, mask_b))).astype(qp_ref.dtype)
        y0_ref[:, cols] = _dot(a_rb, _block_diag(u0, mask_b)) + _dot(a_rk, v_bd)
        bh = (b * p_rest).astype(BF16)
        kh = (k * p_rest).astype(BF16)
        decay_diag = jnp.where(m_col == j_row, jnp.exp(c_last), 0.0)
        m_ref[:, cols] = _fold_rows(_dot_tn(bh, w), mask_f) + decay_diag
        c_ref[:, cols] = _fold_rows(_dot_tn(bh, u0) + _dot_tn(kh, vb), mask_f)

    _run_interleaved(unit(u) for u in range(SCAN_UNITS))


def rwkv_chunks(r, k, v, w_lin, a_lin, v_lin, v_first, params):
    t, d = r.shape
    bw = SCAN_UNITS * SCAN_WIDTH
    blk = pl.BlockSpec((SCAN_CHUNK, bw), lambda ci, gi: (ci, gi))
    mix_v = v_lin is not None
    args = [r, k, v, w_lin, a_lin] + ([v_lin, v_first] if mix_v else []) + [params]
    f32 = jax.ShapeDtypeStruct((t, d), F32)
    return pl.pallas_call(
        functools.partial(_rwkv_chunk_kernel, mix_v=mix_v),
        out_shape=[f32, f32, jax.ShapeDtypeStruct((t, d), BF16), f32, f32],
        grid=(t // SCAN_CHUNK, d // bw),
        in_specs=[blk] * (len(args) - 1) + [pl.BlockSpec((8, bw), lambda ci, gi: (0, gi))],
        out_specs=[blk] * 5,
        compiler_params=_cparams(("parallel", "parallel")),
        name="rwkv_chunks",
    )(*args)


def _rwkv_state_kernel(m_ref, c_ref, qp_ref, y0_ref, bo_ref, g_ref, ln_ref, o_ref, st_ref):
    @pl.when(pl.program_id(1) == 0)
    def _():
        st_ref[...] = jnp.zeros_like(st_ref)

    W = SCAN_WIDTH
    mask_b = _head_mask(BF16)
    def unit(u):
        cols = slice(u * W, (u + 1) * W)
        st_parts = _bd_parts(st_ref[u], mask_b)
        y = _dot(qp_ref[:, cols], st_parts[0]) + y0_ref[:, cols]
        st_ref[u] = _mm_heads(m_ref[:, cols], st_parts) + c_ref[:, cols]
        yield
        mean = _dot_exact_rhs(y, mask_b) * (1.0 / RW_HEAD)
        yield
        dev = y - mean
        var = _dot_exact_rhs(dev * dev, mask_b) * (1.0 / RW_HEAD)
        yield
        ln = ln_ref[:, cols]
        out = (dev * lax.rsqrt(var + GN_EPS) * ln[0:1] + ln[1:2] + bo_ref[:, cols]) * g_ref[:, cols]
        o_ref[:, cols] = out.astype(o_ref.dtype)

    _run_interleaved(unit(u) for u in range(SCAN_UNITS))


def rwkv_state_scan(m, c, qp, y0, bonus, g, ln_params):
    t, d = m.shape
    bw = SCAN_UNITS * SCAN_WIDTH
    blk = pl.BlockSpec((SCAN_CHUNK, bw), lambda gi, ci: (ci, gi))
    return pl.pallas_call(
        _rwkv_state_kernel,
        out_shape=jax.ShapeDtypeStruct((t, d), BF16),
        grid=(d // bw, t // SCAN_CHUNK),
        in_specs=[blk] * 6 + [pl.BlockSpec((8, bw), lambda gi, ci: (0, gi))],
        out_specs=blk,
        scratch_shapes=[pltpu.VMEM((SCAN_UNITS, RW_HEAD, SCAN_WIDTH), F32)],
        compiler_params=_cparams(("parallel", "arbitrary")),
        name="rwkv_state_scan",
    )(m, c, qp, y0, bonus, g, ln_params)


def _compress_kernel(x_ref, w1_ref, c_ref, w2_ref, o_ref):
    ab = _dot(x_ref[...], w1_ref[...])
    a = ab[:, :NSA_HD]
    bsh = ab[:, NSA_HD:]
    n16 = a.shape[0]
    row = lax.broadcasted_iota(jnp.int32, a.shape, 0)
    b_next = jnp.where(row == n16 - 1, 0.0, pltpu.roll(bsh, n16 - 1, axis=0))
    hid = a + b_next + c_ref[...]
    hid = jax.nn.gelu(hid, approximate=True)
    o_ref[...] = _dot(hid.astype(BF16), w2_ref[...]).astype(o_ref.dtype)


def compress(x16, w1cat, cvec, w2):
    _, g, n16, kd = x16.shape
    d = NSA_HD
    return pl.pallas_call(
        _compress_kernel,
        out_shape=jax.ShapeDtypeStruct((2, g, n16, d), BF16),
        grid=(2, g),
        in_specs=[pl.BlockSpec((None, None, n16, kd), lambda p, gi: (p, gi, 0, 0)),
                  pl.BlockSpec((None, kd, 2 * d), lambda p, gi: (p, 0, 0)),
                  pl.BlockSpec((None, 1, d), lambda p, gi: (p, 0, 0)),
                  pl.BlockSpec((None, d, d), lambda p, gi: (p, 0, 0))],
        out_specs=pl.BlockSpec((None, None, n16, d), lambda p, gi: (p, gi, 0, 0)),
        compiler_params=_cparams(("parallel", "parallel")),
        name="nsa_compress",
    )(x16, w1cat, cvec, w2)


def _softmax_rows(s):
    mx = jnp.max(s, axis=-1, keepdims=True)
    e = jnp.exp(s - mx)
    return e / jnp.sum(e, axis=-1, keepdims=True)


def _nsa_kernel(q_ref, gate_ref, kc_ref, vc_ref, ks_ref, vs_ref, kw_ref, vw_ref,
                gs_ref, ov_ref, bs_ref, bw_ref, o_ref):
    qi = pl.program_id(1)
    R, Q, D = NSA_REP, QBLK, NSA_HD
    q = q_ref[...]
    qs = jnp.concatenate([q[:, r * D:(r + 1) * D] for r in range(R)], axis=0)
    n_cp = kc_ref.shape[0]

    t_rows = qi * Q + (lax.broadcasted_iota(jnp.int32, (R * Q, 1), 0) & (Q - 1))
    c_idx = lax.broadcasted_iota(jnp.int32, (1, n_cp), 1)
    m_off = qi * (Q // CMP_STRIDE) - lax.broadcasted_iota(jnp.int32, (CMP_BIAS_K, n_cp), 1) + CMP_BAND_LO
    k_idx = lax.broadcasted_iota(jnp.int32, (CMP_BIAS_K, n_cp), 0)
    in_band = ((k_idx < CMP_BAND) & (m_off == k_idx)) | ((k_idx == CMP_BAND) & (m_off >= CMP_BAND))
    sel_mat = jnp.where(in_band, 1.0, 0.0).astype(BF16)
    gs = gs_ref[...]
    bias_c = _dot(gs[0], sel_mat) + _dot(gs[1], sel_mat) + _dot(gs[2], sel_mat)
    m_c = (c_idx * CMP_STRIDE + (CMP_LEN - 1)) <= t_rows
    s_c = jnp.where(m_c, _dot_nt(qs, kc_ref[...]) + bias_c, NEG_INF)
    p_c = jnp.where(m_c, _softmax_rows(s_c), 0.0)
    o_c = _dot(p_c.astype(BF16), vc_ref[...])

    w0 = pl.multiple_of(qi * Q, Q)
    kwb = kw_ref[pl.ds(w0, WINDOW + Q), :]
    vwb = vw_ref[pl.ds(w0, WINDOW + Q), :]
    j_io = lax.broadcasted_iota(jnp.int32, (1, WINDOW + Q), 1)
    s_w = jnp.where(j_io >= WINDOW - qi * Q, _dot_nt(qs, kwb) + bw_ref[...], NEG_INF)
    o_w = _dot(_softmax_rows(s_w).astype(BF16), vwb)

    p_sum = p_c[0:Q] + p_c[Q:2 * Q] + p_c[2 * Q:3 * Q] + p_c[3 * Q:4 * Q]
    imp_t = _dot_exact_rhs(p_sum, ov_ref[...]).T
    n_sb = imp_t.shape[0]
    s_row = lax.broadcasted_iota(jnp.int32, (n_sb, Q), 0)
    t_col = qi * Q + lax.broadcasted_iota(jnp.int32, (n_sb, Q), 1)
    forced = (s_row == (t_col >> 6)) | (s_row == 0)
    score = jnp.where(forced, SEL_FORCE, jnp.where(s_row * SEL_BLOCK <= t_col, imp_t, -SEL_FORCE))
    sel_t = jnp.zeros((n_sb, Q), F32)
    s_f = s_row.astype(F32)
    for _ in range(SEL_TOP):
        mx = jnp.max(score, axis=0, keepdims=True)
        first = jnp.min(jnp.where(score == mx, s_f, float(n_sb)), axis=0, keepdims=True)
        pick = s_f == first
        sel_t = jnp.where(pick, 1.0, sel_t)
        score = jnp.where(pick, -3e38, score)
    sel_tb = sel_t.astype(BF16)

    def sel_mask(first_sb, n_keys, limit_sb):
        sb = first_sb + (lax.broadcasted_iota(jnp.int32, (n_sb, n_keys), 1) >> 6)
        hit = (lax.broadcasted_iota(jnp.int32, (n_sb, n_keys), 0) == sb) & (sb < limit_sb)
        return jnp.where(_dot_tn(sel_tb, jnp.where(hit, 1.0, 0.0).astype(BF16)) > 0.5, 0.0, NEG_INF)

    def add_mask(s, madd):
        return jnp.concatenate([s[r * Q:(r + 1) * Q] + madd for r in range(R)], axis=0)

    nb0 = jnp.maximum(qi - 1, 0)
    n0 = pl.multiple_of(nb0 * Q, Q)
    s = _dot_nt(qs, ks_ref[pl.ds(n0, 2 * Q), :]) + bs_ref[jnp.where(qi == 0, 1, 0)]
    s = add_mask(s, sel_mask(2 * nb0, 2 * Q, n_sb))
    m_0 = jnp.max(s, axis=-1, keepdims=True)
    p = jnp.exp(s - m_0)
    l_0 = jnp.sum(p, axis=-1, keepdims=True)
    acc_0 = _dot(p.astype(BF16), vs_ref[pl.ds(n0, 2 * Q), :])

    n_far = jnp.maximum(qi - 1, 0)
    blocks_per_tile = SEL_TILE // Q

    def far_body(it, carry):
        m_run, l_run, acc = carry
        start = pl.multiple_of(it * SEL_TILE, SEL_TILE)
        madd = sel_mask(it * (SEL_TILE // SEL_BLOCK), SEL_TILE, 2 * n_far)
        s = add_mask(_dot_nt(qs, ks_ref[pl.ds(start, SEL_TILE), :]), madd)
        m_new = jnp.maximum(m_run, jnp.max(s, axis=-1, keepdims=True))
        alpha = jnp.exp(m_run - m_new)
        p = jnp.exp(s - m_new)
        l_new = alpha * l_run + jnp.sum(p, axis=-1, keepdims=True)
        acc_new = alpha * acc + _dot(p.astype(BF16), vs_ref[pl.ds(start, SEL_TILE), :])
        return m_new, l_new, acc_new

    n_tiles = (n_far + blocks_per_tile - 1) // blocks_per_tile
    _, l_s, acc_s = lax.fori_loop(0, n_tiles, far_body, (m_0, l_0, acc_0))
    o_s = acc_s / l_s

    gates = gate_ref[...]
    for r in range(R):
        rows = slice(r * Q, (r + 1) * Q)
        o = (gates[:, 3 * r:3 * r + 1] * o_c[rows] + gates[:, 3 * r + 1:3 * r + 2] * o_s[rows]
             + gates[:, 3 * r + 2:3 * r + 3] * o_w[rows])
        o_ref[:, r * D:(r + 1) * D] = o.astype(o_ref.dtype)


def nsa_attention(q, gates, kc, vc, ks, vs, kw, vw, gs, ov, bs, bw):
    t = q.shape[0]
    g, n_cp, d = kc.shape
    rq = NSA_REP * QBLK
    per_g = lambda shape: pl.BlockSpec((None,) + shape, lambda gi, qi: (gi,) + (0,) * len(shape))
    return pl.pallas_call(
        _nsa_kernel,
        out_shape=jax.ShapeDtypeStruct((t, NSA_HEADS * d), BF16),
        grid=(g, t // QBLK),
        in_specs=[pl.BlockSpec((QBLK, NSA_REP * d), lambda gi, qi: (qi, gi)),
                  pl.BlockSpec((None, QBLK, NSA_REP * N_BRANCH), lambda gi, qi: (gi, qi, 0)),
                  per_g((n_cp, d)), per_g((n_cp, d)),
                  per_g((t, d)), per_g((t, d)),
                  per_g((t + WINDOW, d)), per_g((t + WINDOW, d)),
                  per_g((3, rq, CMP_BIAS_K)),
                  pl.BlockSpec(ov.shape, lambda gi, qi: (0, 0)),
                  per_g((2, rq, 2 * QBLK)),
                  per_g((rq, WINDOW + QBLK))],
        out_specs=pl.BlockSpec((QBLK, NSA_REP * d), lambda gi, qi: (qi, gi)),
        compiler_params=_cparams(("parallel", "arbitrary")),
        name="nsa_attention",
    )(q, gates, kc, vc, ks, vs, kw, vw, gs, ov, bs, bw)


def _t5_bucket_np(n):
    n = np.maximum(n, 0)
    max_exact = REL_BUCKETS // 2
    nf = np.maximum(n, 1).astype(np.float64)
    large = max_exact + (np.log(nf / max_exact) / math.log(REL_MAX_DIST / max_exact)
                         * (REL_BUCKETS - max_exact)).astype(np.int64)
    large = np.minimum(large, REL_BUCKETS - 1)
    return np.where(n < max_exact, n, large).astype(np.int32)


def _nsa_bias_tables(rel_bias):
    G, R, Q = NSA_GROUPS, NSA_REP, QBLK
    table = rel_bias.astype(F32).reshape(REL_BUCKETS, G, R).transpose(1, 2, 0)
    qq = np.arange(Q)

    def lookup(dist):
        return table[:, :, _t5_bucket_np(dist)].reshape(G, R * Q, dist.shape[1])

    m = np.arange(CMP_BAND) - CMP_BAND_LO
    gs = lookup(qq[:, None] - (CMP_LEN - 1) + CMP_STRIDE * m[None, :])
    far = jnp.broadcast_to(table[:, :, REL_BUCKETS - 1][:, :, None, None], (G, R, Q, 1)).reshape(G, R * Q, 1)
    gs = jnp.concatenate([gs, far, jnp.zeros((G, R * Q, CMP_BIAS_K - CMP_BAND - 1), F32)], axis=-1)
    h, mid, l = _split3(gs)
    gs3 = jnp.stack([h, mid, l], axis=1)
    d0 = np.tile(qq[:, None] - qq[None, :], (R, 1))
    causal = jnp.asarray(np.where(d0 >= 0, 0.0, NEG_INF), F32)[None]
    b0 = lookup(qq[:, None] - qq[None, :]) + causal
    b1 = lookup(qq[:, None] - qq[None, :] + Q)
    far_c = jnp.broadcast_to(far, (G, R * Q, Q))
    neg = jnp.full((G, R * Q, Q), NEG_INF, F32)
    bs = jnp.stack([jnp.concatenate([b1 - far_c, b0 - far_c], -1),
                    jnp.concatenate([b0 - far_c, neg], -1)], axis=1)
    oldest = far_c + jnp.asarray(np.where(d0 < 0, 0.0, NEG_INF), F32)[None]
    bw = jnp.concatenate([oldest, far_c, far_c, b1, b0], -1)
    return gs3, bs, bw


def _overlap_table(n_cp, n_sb):
    c = np.arange(n_cp)[:, None] * CMP_STRIDE
    s = np.arange(n_sb)[None, :] * SEL_BLOCK
    return jnp.asarray(((c <= s + SEL_BLOCK - 1) & (c + CMP_LEN - 1 >= s)), BF16)


def _pad_cols(w, n):
    return jnp.pad(w, ((0, 0), (0, n - w.shape[1])))


def _pad_rows(w, n):
    return jnp.pad(w, ((0, n - w.shape[0]), (0, 0)))


def _rwkv_layer(h, norm_g, mu, wr, wk, wv, wo, w0, w1, w2, a0, a1, a2, g1, g2,
                k_k, k_a, r_k, lnx_w, lnx_b, v_first, v0, v1, v2):
    bf = lambda w: w.astype(BF16)
    xr, xw, xk, xv, xa, xg = rwkv_prep(h, norm_g, mu)
    r = matmul(xr, bf(wr))
    k = matmul(xk, bf(wk))
    v = matmul(xv, bf(wv))
    lw_hid = matmul(xw, bf(_pad_cols(w1, 128)), act="tanh", out_dtype=BF16)
    w_lin = matmul(lw_hid, bf(_pad_rows(w2, 128)))
    a_lin = matmul(matmul(xa, bf(a1), out_dtype=BF16), bf(a2))
    g = matmul(matmul(xg, bf(g1), act="sigmoid", out_dtype=BF16), bf(g2))
    zeros = jnp.zeros_like(w0)
    if v_first is None:
        v_lin, v0 = None, zeros
    else:
        v_lin = matmul(matmul(xv, bf(_pad_cols(v1, 128)), out_dtype=BF16), bf(_pad_rows(v2, 128)))
    params = jnp.stack([w0, a0, v0, k_k, k_a, r_k.reshape(-1), zeros, zeros])
    m, c, qp, y0, bonus = rwkv_chunks(r, k, v, w_lin, a_lin, v_lin, v_first, params)
    ln_params = jnp.stack([lnx_w, lnx_b] + [zeros] * 6)
    gated = rwkv_state_scan(m, c, qp, y0, bonus, g, ln_params)
    h = matmul(gated, bf(wo), residual=h)
    return h, (v if v_first is None else v_first)


def _mlp(h, norm_g, w_up, w_down):
    xn = rmsnorm(h, norm_g)
    hid = matmul(xn, w_up.astype(BF16), act="relu2", out_dtype=BF16)
    return matmul(hid, w_down.astype(BF16), residual=h)


def _nsa_shared_kv(h, kv_norm_g, w_kv, cmp_pe, cmp_w1, cmp_b1, cmp_w2):
    t = h.shape[0]
    G, D = NSA_GROUPS, NSA_HD
    kv = matmul(rmsnorm(h, kv_norm_g), w_kv.astype(BF16), out_dtype=BF16)
    kv = kv.reshape(t, N_KV_PARTS, G, D).transpose(1, 2, 0, 3)
    x16 = kv[0:2].reshape(2, G, t // CMP_STRIDE, CMP_STRIDE * D)
    half = CMP_STRIDE * D
    w1cat = jnp.concatenate([cmp_w1[:, :half], cmp_w1[:, half:]], axis=-1).astype(BF16)
    cvec = jnp.einsum("plk,plkd->pd", cmp_pe, cmp_w1.reshape(2, CMP_LEN, D, D),
                      precision=lax.Precision.HIGHEST) + cmp_b1
    kvc = compress(x16, w1cat, cvec.reshape(2, 1, D), cmp_w2.astype(BF16))
    pad = ((0, 0), (WINDOW, 0), (0, 0))
    return kvc[0], kvc[1], kv[2], kv[3], jnp.pad(kv[4], pad), jnp.pad(kv[5], pad)


def _nsa_layer(h, norm_g, wq, wo, kvs, tables):
    t = h.shape[0]
    hn = rmsnorm(h, norm_g)
    nq = NSA_HEADS * NSA_HD
    q = matmul(hn, wq[:, :nq].astype(BF16), scale=NSA_HD ** -0.5, out_dtype=BF16)
    gates = matmul(hn, _pad_cols(wq[:, nq:], 128).astype(BF16), act="sigmoid")
    gates = gates[:, :NSA_HEADS * N_BRANCH].reshape(t, NSA_GROUPS, NSA_REP * N_BRANCH).transpose(1, 0, 2)
    o = nsa_attention(q, gates, *kvs, *tables)
    return matmul(o, wo.astype(BF16), residual=h)


def kernel(x, a_norm_g, rw_mu, rw_wr, rw_wk, rw_wv, rw_wo, rw_w0, rw_w1, rw_w2, rw_a0, rw_a1, rw_a2,
           rw_v0, rw_v1, rw_v2, rw_g1, rw_g2, rw_kk, rw_ka, rw_rk, rw_lnx_w, rw_lnx_b, kv_norm_g, w_kv,
           cmp_pe, cmp_w1, cmp_b1, cmp_w2, b_norm_g, nsa_wq, nsa_wo, rel_bias, m_norm_g, mlp_up,
           mlp_down, final_norm_g):
    b, t, d = x.shape
    assert b == 1 and d == D_MODEL and t % SEL_TILE == 0 and t // SEL_BLOCK <= 128 and WINDOW == 4 * QBLK
    h = x.reshape(t, d)
    v_first = None
    kvs = tables = None
    for layer in range(DEPTH):
        if layer < N_A_LAYERS:
            i = layer
            vp = (None, None, None) if i == 0 else (rw_v0[i - 1], rw_v1[i - 1], rw_v2[i - 1])
            h, v_first = _rwkv_layer(
                h, a_norm_g[i], rw_mu[i], rw_wr[i], rw_wk[i], rw_wv[i], rw_wo[i], rw_w0[i], rw_w1[i],
                rw_w2[i], rw_a0[i], rw_a1[i], rw_a2[i], rw_g1[i], rw_g2[i], rw_kk[i], rw_ka[i],
                rw_rk[i].reshape(1, RW_HEADS, RW_HEAD), rw_lnx_w[i], rw_lnx_b[i], v_first, *vp)
        else:
            j = layer - N_A_LAYERS
            if j == 0:
                kvs = _nsa_shared_kv(h, kv_norm_g, w_kv, cmp_pe, cmp_w1, cmp_b1, cmp_w2)
                gs3, bs, bw = _nsa_bias_tables(rel_bias)
                tables = (gs3, _overlap_table(t // CMP_STRIDE, 128), bs, bw)
            h = _nsa_layer(h, b_norm_g[j], nsa_wq[j], nsa_wo[j], kvs, tables)
        h = _mlp(h, m_norm_g[layer], mlp_up[layer], mlp_down[layer])
    return rmsnorm(h, final_norm_g, out_dtype=x.dtype).reshape(b, t, d)
```

```python
import functools
import math

import numpy as np
import jax
import jax.numpy as jnp
from jax import lax
from jax.experimental import pallas as pl
from jax.experimental.pallas import tpu as pltpu

F32 = jnp.float32
BF16 = jnp.bfloat16

D_MODEL = 2048
DEPTH = 4
N_A_LAYERS = 2
RW_HEAD = 64
RW_HEADS = D_MODEL // RW_HEAD
N_SHIFT_MIX = 6
GN_EPS = 64e-5
NSA_HEADS = 16
NSA_GROUPS = 4
NSA_REP = NSA_HEADS // NSA_GROUPS
NSA_HD = D_MODEL // NSA_HEADS
N_KV_PARTS = 6
CMP_STRIDE = 16
CMP_LEN = 32
SEL_BLOCK = 64
SEL_TOP = 16
WINDOW = 512
QBLK = 128
N_BRANCH = 3
SEL_FORCE = 1e4
NEG_INF = -1e30
REL_BUCKETS = 32
REL_MAX_DIST = 128
NORM_EPS = 1e-6

VMEM_LIMIT_BYTES = 56 * 1024 * 1024
SCAN_CHUNK = 64
SCAN_HEADS = 4
SCAN_WIDTH = SCAN_HEADS * RW_HEAD
SCAN_UNITS = 4
SEL_TILE = 512
CMP_BAND = 16
CMP_BAND_LO = 6
CMP_BIAS_K = 32
assert CMP_BAND < CMP_BIAS_K and 4 * CMP_BIAS_K == NSA_HD and NSA_HD == 128


def _cparams(sem):
    return pltpu.CompilerParams(dimension_semantics=sem, vmem_limit_bytes=VMEM_LIMIT_BYTES)


def _split3(x):
    h = x.astype(BF16)
    r = x - h.astype(F32)
    m = r.astype(BF16)
    l = (r - m.astype(F32)).astype(BF16)
    return h, m, l


def _dot(a, b):
    return jnp.dot(a, b, preferred_element_type=F32)


def _dot_nt(a, b):
    return lax.dot_general(a, b, (((1,), (1,)), ((), ())), preferred_element_type=F32)


def _dot_tn(a, b):
    return lax.dot_general(a, b, (((0,), (0,)), ((), ())), preferred_element_type=F32)


def _dot_exact_lhs(a_bf16, x):
    h, m, l = _split3(x)
    return _dot(a_bf16, h) + _dot(a_bf16, m) + _dot(a_bf16, l)


def _dot_exact_rhs(x, b_bf16):
    h, m, l = _split3(x)
    return _dot(h, b_bf16) + _dot(m, b_bf16) + _dot(l, b_bf16)


def _matmul_kernel(x_ref, w_ref, *rest, act, scale, has_res, nk):
    if has_res:
        res_ref, o_ref, *scratch = rest
    else:
        o_ref, *scratch = rest

    def epilogue(acc):
        if act == "relu2":
            acc = jnp.square(jnp.maximum(acc, 0.0))
        elif act == "tanh":
            acc = jnp.tanh(acc)
        elif act == "sigmoid":
            acc = jax.nn.sigmoid(acc)
        if scale is not None:
            acc = acc * scale
        if has_res:
            acc = acc + res_ref[...]
        o_ref[...] = acc.astype(o_ref.dtype)

    if nk == 1:
        epilogue(_dot(x_ref[...], w_ref[...]))
    else:
        acc_ref, = scratch
        k = pl.program_id(2)

        @pl.when(k == 0)
        def _():
            acc_ref[...] = jnp.zeros_like(acc_ref)

        acc_ref[...] += _dot(x_ref[...], w_ref[...])

        @pl.when(k == nk - 1)
        def _():
            epilogue(acc_ref[...])


def _pick_tile(n, pref):
    t = min(n, pref)
    while n % t:
        t //= 2
    return t


def matmul(x, w, *, act=None, scale=None, residual=None, out_dtype=F32, tm=1024, tn=512, tk=2048):
    m, kdim = x.shape
    _, n = w.shape
    tm, tn, tk = _pick_tile(m, tm), _pick_tile(n, tn), _pick_tile(kdim, tk)
    nk = kdim // tk
    in_specs = [pl.BlockSpec((tm, tk), lambda j, i, k: (i, k)),
                pl.BlockSpec((tk, tn), lambda j, i, k: (k, j))]
    args = [x, w]
    if residual is not None:
        in_specs.append(pl.BlockSpec((tm, tn), lambda j, i, k: (i, j)))
        args.append(residual)
    return pl.pallas_call(
        functools.partial(_matmul_kernel, act=act, scale=scale, has_res=residual is not None, nk=nk),
        out_shape=jax.ShapeDtypeStruct((m, n), out_dtype),
        grid=(n // tn, m // tm, nk),
        in_specs=in_specs,
        out_specs=pl.BlockSpec((tm, tn), lambda j, i, k: (i, j)),
        scratch_shapes=[pltpu.VMEM((tm, tn), F32)] if nk > 1 else [],
        compiler_params=_cparams(("parallel", "parallel", "arbitrary")),
        name="matmul",
    )(*args)


def _rms(x, g):
    return x * lax.rsqrt(jnp.mean(x * x, axis=-1, keepdims=True) + NORM_EPS) * g


def _rmsnorm_kernel(h_ref, g_ref, o_ref):
    o_ref[...] = _rms(h_ref[...], g_ref[...]).astype(o_ref.dtype)


def rmsnorm(h, g, out_dtype=BF16, tm=512):
    t, d = h.shape
    tm = _pick_tile(t, tm)
    return pl.pallas_call(
        _rmsnorm_kernel,
        out_shape=jax.ShapeDtypeStruct((t, d), out_dtype),
        grid=(t // tm,),
        in_specs=[pl.BlockSpec((tm, d), lambda i: (i, 0)), pl.BlockSpec((1, d), lambda i: (0, 0))],
        out_specs=pl.BlockSpec((tm, d), lambda i: (i, 0)),
        compiler_params=_cparams(("parallel",)),
        name="rmsnorm",
    )(h, g.reshape(1, d))


def _rwkv_prep_kernel(h_ref, prev_ref, g_ref, mu_ref, *o_refs):
    g = g_ref[...]
    xn = _rms(h_ref[...], g)
    prev = _rms(prev_ref[...], g)[7:8, :] * (pl.program_id(0) > 0).astype(F32)
    row = lax.broadcasted_iota(jnp.int32, xn.shape, 0)
    shifted = jnp.where(row == 0, prev, pltpu.roll(xn, 1, axis=0))
    xx = shifted - xn
    for i, o_ref in enumerate(o_refs):
        o_ref[...] = (xn + xx * mu_ref[i:i + 1, :]).astype(o_ref.dtype)


def rwkv_prep(h, g, mu, tm=256):
    t, d = h.shape
    tm = _pick_tile(t, tm)
    blk = pl.BlockSpec((tm, d), lambda i: (i, 0))
    return pl.pallas_call(
        _rwkv_prep_kernel,
        out_shape=[jax.ShapeDtypeStruct((t, d), BF16)] * N_SHIFT_MIX,
        grid=(t // tm,),
        in_specs=[blk,
                  pl.BlockSpec((8, d), lambda i: (jnp.maximum(i * (tm // 8) - 1, 0), 0)),
                  pl.BlockSpec((1, d), lambda i: (0, 0)),
                  pl.BlockSpec((N_SHIFT_MIX, d), lambda i: (0, 0))],
        out_specs=[blk] * N_SHIFT_MIX,
        compiler_params=_cparams(("parallel",)),
        name="rwkv_prep",
    )(h, h, g.reshape(1, d), mu)


def _head_mask(dtype):
    row = lax.broadcasted_iota(jnp.int32, (SCAN_WIDTH, SCAN_WIDTH), 0)
    col = lax.broadcasted_iota(jnp.int32, (SCAN_WIDTH, SCAN_WIDTH), 1)
    return jnp.where((row >> 6) == (col >> 6), 1.0, 0.0).astype(dtype)


def _block_diag(x, mask):
    return jnp.concatenate([x] * SCAN_HEADS, axis=0) * mask


def _fold_rows(x_full, mask_f32):
    x = x_full * mask_f32
    L = SCAN_CHUNK
    return x[0:L] + x[L:2 * L] + x[2 * L:3 * L] + x[3 * L:4 * L]


def _split2(x):
    hi = x.astype(BF16)
    return hi, (x - hi.astype(F32)).astype(BF16)


def _bd_parts(y, mask_b):
    hi, lo = _split2(y)
    return _block_diag(hi, mask_b), _block_diag(lo, mask_b)


def _mm_heads(x, y_parts):
    y_hi, y_lo = y_parts
    n = x.shape[0]
    x_hi, x_lo = _split2(x)
    both = _dot(jnp.concatenate([x_hi, x_lo], axis=0), y_hi)
    return both[:n] + both[n:] + _dot(x_hi, y_lo)


def _run_interleaved(stage_generators):
    live = list(stage_generators)
    while live:
        still = []
        for gen in live:
            try:
                next(gen)
                still.append(gen)
            except StopIteration:
                pass
        live = still


def _tri_inverse_heads(a, j_row, m_col, mask_b):
    L = SCAN_CHUNK
    eye = jnp.where(m_col == j_row, 1.0, 0.0)
    d = jnp.where((j_row >> 4) == (m_col >> 4), a, 0.0)
    e = a - d
    p = eye + d
    x = _mm_heads(d, _bd_parts(d, mask_b))
    yield
    for _ in range(2):
        both = _mm_heads(jnp.concatenate([x, p], axis=0), _bd_parts(x, mask_b))
        x, p = both[:L], p + both[L:]
        yield
    p = p + _mm_heads(p, _bd_parts(x, mask_b))
    yield
    n1 = _mm_heads(p, _bd_parts(e, mask_b))
    yield
    n2 = _mm_heads(n1, _bd_parts(n1, mask_b))
    yield
    q = eye + n1
    q = q + _mm_heads(q, _bd_parts(n2, mask_b))
    yield
    return _mm_heads(q, _bd_parts(p, mask_b))


def _softplus(x):
    return jnp.maximum(x, 0.0) + jnp.log(1.0 + jnp.exp(-jnp.abs(x)))


def _rwkv_chunk_kernel(*refs, mix_v):
    if mix_v:
        r_ref, k_ref, v_ref, wl_ref, al_ref, vl_ref, vf_ref, par_ref, m_ref, c_ref, qp_ref, y0_ref, bo_ref = refs
    else:
        r_ref, k_ref, v_ref, wl_ref, al_ref, par_ref, m_ref, c_ref, qp_ref, y0_ref, bo_ref = refs
    L, W = SCAN_CHUNK, SCAN_WIDTH
    mask_b = _head_mask(BF16)
    mask_f = _head_mask(F32)
    tri_incl = jnp.where(lax.broadcasted_iota(jnp.int32, (L, L), 1)
                         <= lax.broadcasted_iota(jnp.int32, (L, L), 0), 1.0, 0.0).astype(BF16)
    j_row = lax.broadcasted_iota(jnp.int32, (L, W), 0)
    m_col = lax.broadcasted_iota(jnp.int32, (L, W), 1) & (L - 1)
    strict = m_col < j_row
    incl = m_col <= j_row

    def unit(u):
        cols = slice(u * W, (u + 1) * W)
        par = par_ref[:, cols]
        w0, a0, v0, k_k, k_a, r_k = (par[i:i + 1] for i in range(6))
        r, k, v = r_ref[:, cols], k_ref[:, cols], v_ref[:, cols]
        lw = -jnp.exp(-_softplus(-(w0 + wl_ref[:, cols])) - 0.5)
        a_s = jax.nn.sigmoid(a0 + al_ref[:, cols])
        if mix_v:
            v = v + (vf_ref[:, cols] - v) * jax.nn.sigmoid(v0 + vl_ref[:, cols])
        kk = k * k_k
        k = k * (1.0 + (a_s - 1.0) * k_a)
        kk_sq = _dot_exact_rhs(kk * kk, mask_b)
        rk_sum = _dot_exact_rhs(r * k * r_k, mask_b)
        c = _dot_exact_lhs(tri_incl, lw)
        yield
        kk = kk * lax.rsqrt(jnp.maximum(kk_sq, 1e-24))
        bo_ref[:, cols] = rk_sum * v
        c_last = c[L - 1:L, :]
        p_inv = jnp.exp(-c)
        p_rest = jnp.exp(c_last - c)
        rt = r * jnp.exp(c)
        at = (-kk * jnp.exp(c - lw)).astype(BF16)
        b = kk * a_s
        vb = v.astype(BF16)
        ar = jnp.concatenate([at, rt.astype(BF16)], axis=0)
        g_b = _dot_nt(ar, _block_diag((b * p_inv).astype(BF16), mask_b))
        g_k = _dot_nt(ar, _block_diag((k * p_inv).astype(BF16), mask_b))
        yield
        a_ab = jnp.where(strict, g_b[:L], 0.0)
        a_ak = jnp.where(strict, g_k[:L], 0.0).astype(BF16)
        a_rb = jnp.where(incl, g_b[L:], 0.0).astype(BF16)
        a_rk = jnp.where(incl, g_k[L:], 0.0).astype(BF16)
        v_bd = _block_diag(vb, mask_b)
        akv = _dot(a_ak, v_bd)
        t_inv = yield from _tri_inverse_heads(a_ab, j_row, m_col, mask_b)
        yield
        t_hi, t_lo = _split2(t_inv)
        w_both = _dot(jnp.concatenate([t_hi, t_lo], axis=0), _block_diag(at, mask_b))
        u0 = _mm_heads(t_inv, _bd_parts(akv, mask_b)).astype(BF16)
        yield
        w = (w_both[:L] + w_both[L:]).astype(BF16)
        qp_ref[:, cols] = (rt + _dot(a_rb, _block_diag(w, mask_b))).astype(qp_ref.dtype)
        y0_ref[:, cols] = _dot(a_rb, _block_diag(u0, mask_b)) + _dot(a_rk, v_bd)
        bh = (b * p_rest).astype(BF16)
        kh = (k * p_rest).astype(BF16)
        decay_diag = jnp.where(m_col == j_row, jnp.exp(c_last), 0.0)
        m_ref[:, cols] = _fold_rows(_dot_tn(bh, w), mask_f) + decay_diag
        c_ref[:, cols] = _fold_rows(_dot_tn(bh, u0) + _dot_tn(kh, vb), mask_f)

    _run_interleaved(unit(u) for u in range(SCAN_UNITS))


def rwkv_chunks(r, k, v, w_lin, a_lin, v_lin, v_first, params):
    t, d = r.shape
    bw = SCAN_UNITS * SCAN_WIDTH
    blk = pl.BlockSpec((SCAN_CHUNK, bw), lambda ci, gi: (ci, gi))
    mix_v = v_lin is not None
    args = [r, k, v, w_lin, a_lin] + ([v_lin, v_first] if mix_v else []) + [params]
    f32 = jax.ShapeDtypeStruct((t, d), F32)
    return pl.pallas_call(
        functools.partial(_rwkv_chunk_kernel, mix_v=mix_v),
        out_shape=[f32, f32, jax.ShapeDtypeStruct((t, d), BF16), f32, f32],
        grid=(t // SCAN_CHUNK, d // bw),
        in_specs=[blk] * (len(args) - 1) + [pl.BlockSpec((8, bw), lambda ci, gi: (0, gi))],
        out_specs=[blk] * 5,
        compiler_params=_cparams(("parallel", "parallel")),
        name="rwkv_chunks",
    )(*args)


def _rwkv_state_kernel(m_ref, c_ref, qp_ref, y0_ref, bo_ref, g_ref, ln_ref, o_ref, st_ref):
    @pl.when(pl.program_id(1) == 0)
    def _():
        st_ref[...] = jnp.zeros_like(st_ref)

    W = SCAN_WIDTH
    mask_b = _head_mask(BF16)
    def unit(u):
        cols = slice(u * W, (u + 1) * W)
        st_parts = _bd_parts(st_ref[u], mask_b)
        y = _dot(qp_ref[:, cols], st_parts[0]) + y0_ref[:, cols]
        st_ref[u] = _mm_heads(m_ref[:, cols], st_parts) + c_ref[:, cols]
        yield
        mean = _dot_exact_rhs(y, mask_b) * (1.0 / RW_HEAD)
        yield
        dev = y - mean
        var = _dot_exact_rhs(dev * dev, mask_b) * (1.0 / RW_HEAD)
        yield
        ln = ln_ref[:, cols]
        out = (dev * lax.rsqrt(var + GN_EPS) * ln[0:1] + ln[1:2] + bo_ref[:, cols]) * g_ref[:, cols]
        o_ref[:, cols] = out.astype(o_ref.dtype)

    _run_interleaved(unit(u) for u in range(SCAN_UNITS))


def rwkv_state_scan(m, c, qp, y0, bonus, g, ln_params):
    t, d = m.shape
    bw = SCAN_UNITS * SCAN_WIDTH
    blk = pl.BlockSpec((SCAN_CHUNK, bw), lambda gi, ci: (ci, gi))
    return pl.pallas_call(
        _rwkv_state_kernel,
        out_shape=jax.ShapeDtypeStruct((t, d), BF16),
        grid=(d // bw, t // SCAN_CHUNK),
        in_specs=[blk] * 6 + [pl.BlockSpec((8, bw), lambda gi, ci: (0, gi))],
        out_specs=blk,
        scratch_shapes=[pltpu.VMEM((SCAN_UNITS, RW_HEAD, SCAN_WIDTH), F32)],
        compiler_params=_cparams(("parallel", "arbitrary")),
        name="rwkv_state_scan",
    )(m, c, qp, y0, bonus, g, ln_params)


def _compress_kernel(x_ref, w1_ref, c_ref, w2_ref, o_ref):
    ab = _dot(x_ref[...], w1_ref[...])
    a = ab[:, :NSA_HD]
    bsh = ab[:, NSA_HD:]
    n16 = a.shape[0]
    row = lax.broadcasted_iota(jnp.int32, a.shape, 0)
    b_next = jnp.where(row == n16 - 1, 0.0, pltpu.roll(bsh, n16 - 1, axis=0))
    hid = a + b_next + c_ref[...]
    hid = jax.nn.gelu(hid, approximate=True)
    o_ref[...] = _dot(hid.astype(BF16), w2_ref[...]).astype(o_ref.dtype)


def compress(x16, w1cat, cvec, w2):
    _, g, n16, kd = x16.shape
    d = NSA_HD
    return pl.pallas_call(
        _compress_kernel,
        out_shape=jax.ShapeDtypeStruct((2, g, n16, d), BF16),
        grid=(2, g),
        in_specs=[pl.BlockSpec((None, None, n16, kd), lambda p, gi: (p, gi, 0, 0)),
                  pl.BlockSpec((None, kd, 2 * d), lambda p, gi: (p, 0, 0)),
                  pl.BlockSpec((None, 1, d), lambda p, gi: (p, 0, 0)),
                  pl.BlockSpec((None, d, d), lambda p, gi: (p, 0, 0))],
        out_specs=pl.BlockSpec((None, None, n16, d), lambda p, gi: (p, gi, 0, 0)),
        compiler_params=_cparams(("parallel", "parallel")),
        name="nsa_compress",
    )(x16, w1cat, cvec, w2)


def _softmax_rows(s):
    mx = jnp.max(s, axis=-1, keepdims=True)
    e = jnp.exp(s - mx)
    return e / jnp.sum(e, axis=-1, keepdims=True)


def _nsa_kernel(q_ref, gate_ref, kc_ref, vc_ref, ks_ref, vs_ref, kw_ref, vw_ref, oh_ref,
                gs_ref, ov_ref, bs_ref, bw_ref, o_ref, s_buf, p_buf):
    qi = pl.program_id(1)
    R, Q, D = NSA_REP, QBLK, NSA_HD
    q = q_ref[...]
    qs = jnp.concatenate([q[:, r * D:(r + 1) * D] for r in range(R)], axis=0)
    n_cp = kc_ref.shape[0]
    n_sb = ov_ref.shape[1]
    out = {}

    def compressed_and_topk():
        t_rows = qi * Q + (lax.broadcasted_iota(jnp.int32, (R * Q, 1), 0) & (Q - 1))
        c_idx = lax.broadcasted_iota(jnp.int32, (1, n_cp), 1)
        m_off = qi * (Q // CMP_STRIDE) - lax.broadcasted_iota(jnp.int32, (n_cp, D), 0) + CMP_BAND_LO
        k_idx = lax.broadcasted_iota(jnp.int32, (n_cp, D), 1) & (CMP_BIAS_K - 1)
        in_band = ((k_idx < CMP_BAND) & (m_off == k_idx)) | ((k_idx == CMP_BAND) & (m_off >= CMP_BAND))
        k_c = jnp.concatenate([kc_ref[...], jnp.where(in_band, 1.0, 0.0).astype(BF16)], axis=1)
        qk_c = _dot_nt(jnp.concatenate([qs, gs_ref[...]], axis=1), k_c)
        yield
        m_c = (c_idx * CMP_STRIDE + (CMP_LEN - 1)) <= t_rows
        s_c = jnp.where(m_c, qk_c, NEG_INF)
        p_c = jnp.where(m_c, _softmax_rows(s_c), 0.0)
        out["o_c"] = _dot(p_c.astype(BF16), vc_ref[...])
        p_sum = p_c[0:Q] + p_c[Q:2 * Q] + p_c[2 * Q:3 * Q] + p_c[3 * Q:4 * Q]
        imp = _dot_exact_rhs(p_sum, ov_ref[...])
        yield
        imp_t = imp.T
        s_row = lax.broadcasted_iota(jnp.int32, (n_sb, Q), 0)
        t_col = qi * Q + lax.broadcasted_iota(jnp.int32, (n_sb, Q), 1)
        forced = (s_row == (t_col >> 6)) | (s_row == 0)
        score = jnp.where(forced, SEL_FORCE, jnp.where(s_row * SEL_BLOCK <= t_col, imp_t, -SEL_FORCE))
        sel_t = jnp.zeros((n_sb, Q), F32)
        s_f = s_row.astype(F32)
        for _ in range(SEL_TOP):
            mx = jnp.max(score, axis=0, keepdims=True)
            first = jnp.min(jnp.where(score == mx, s_f, float(n_sb)), axis=0, keepdims=True)
            pick = s_f == first
            sel_t = jnp.where(pick, 1.0, sel_t)
            score = jnp.where(pick, -3e38, score)
        out["sel"] = sel_t.T

    def window():
        w0 = pl.multiple_of(qi * Q, Q)
        qk_w = _dot_nt(qs, kw_ref[pl.ds(w0, WINDOW + Q), :])
        yield
        j_io = lax.broadcasted_iota(jnp.int32, (1, WINDOW + Q), 1)
        s_w = jnp.where(j_io >= WINDOW - qi * Q, qk_w + bw_ref[...], NEG_INF)
        out["o_w"] = _dot(_softmax_rows(s_w).astype(BF16), vw_ref[pl.ds(w0, WINDOW + Q), :])

    _run_interleaved([compressed_and_topk(), window()])
    o_c, o_w, sel = out["o_c"], out["o_w"], out["sel"]

    n_far = jnp.maximum(qi - 1, 0)
    blk = lax.broadcasted_iota(jnp.int32, (Q, n_sb), 1)
    neg_near = jnp.where(sel > 0.5, 0.0, NEG_INF).astype(BF16)
    neg_far = jnp.where((sel > 0.5) & (blk < 2 * n_far), 0.0, NEG_INF).astype(BF16)
    q_near = jnp.concatenate([qs, jnp.concatenate([neg_near] * R, axis=0)], axis=1)
    q_far = jnp.concatenate([qs, jnp.concatenate([neg_far] * R, axis=0)], axis=1)

    nb0 = jnp.maximum(qi - 1, 0)
    n0 = pl.multiple_of(nb0 * Q, Q)

    def sel_keys(start, n):
        return jnp.concatenate([ks_ref[pl.ds(WINDOW + start, n), :], oh_ref[pl.ds(start, n), :]], axis=1)

    s = _dot_nt(q_near, sel_keys(n0, 2 * Q)) + bs_ref[jnp.where(qi == 0, 1, 0)]
    m_0 = jnp.max(s, axis=-1, keepdims=True)
    p = jnp.exp(s - m_0)
    l_0 = jnp.sum(p, axis=-1, keepdims=True)
    acc_0 = _dot(p.astype(BF16), vs_ref[pl.ds(WINDOW + n0, 2 * Q), :])

    n_tiles = (n_far + (SEL_TILE // Q - 1)) // (SEL_TILE // Q)
    n_pairs = (n_tiles + 1) >> 1
    max_tile = oh_ref.shape[0] // SEL_TILE - 1

    def key_tile(i):
        return sel_keys(pl.multiple_of(i * SEL_TILE, SEL_TILE), SEL_TILE)

    def value_tile(i):
        return vs_ref[pl.ds(pl.multiple_of(WINDOW + i * SEL_TILE, SEL_TILE), SEL_TILE), :]

    s_buf[0] = _dot_nt(q_far, key_tile(0))
    p_buf[1] = jnp.zeros(p_buf.shape[1:], p_buf.dtype)

    def half_step(i, slot, carry):
        m_run, l_run, acc, alpha_prev = carry
        s_buf[1 - slot] = _dot_nt(q_far, key_tile(jnp.minimum(i + 1, max_tile)))
        acc = alpha_prev * acc + _dot(p_buf[1 - slot], value_tile(jnp.maximum(i - 1, 0)))
        s_i = s_buf[slot]
        m_new = jnp.maximum(m_run, jnp.max(s_i, axis=-1, keepdims=True))
        alpha = jnp.exp(m_run - m_new)
        p_i = jnp.exp(s_i - m_new)
        p_buf[slot] = p_i.astype(p_buf.dtype)
        return m_new, alpha * l_run + jnp.sum(p_i, axis=-1, keepdims=True), acc, alpha

    def pair_body(j, carry):
        return half_step(2 * j + 1, 1, half_step(2 * j, 0, carry))

    _, l_s, acc_s, alpha_last = lax.fori_loop(0, n_pairs, pair_body, (m_0, l_0, acc_0, jnp.ones_like(l_0)))
    acc_s = alpha_last * acc_s + _dot(p_buf[1], value_tile(jnp.maximum(2 * n_pairs - 1, 0)))
    o_s = acc_s / l_s

    gates = gate_ref[...]
    for r in range(R):
        rows = slice(r * Q, (r + 1) * Q)
        o = (gates[:, 3 * r:3 * r + 1] * o_c[rows] + gates[:, 3 * r + 1:3 * r + 2] * o_s[rows]
             + gates[:, 3 * r + 2:3 * r + 3] * o_w[rows])
        o_ref[:, r * D:(r + 1) * D] = o.astype(o_ref.dtype)


def nsa_attention(q, gates, kc, vc, kv_padded, block_onehot, gs, ov, bs, bw):
    t = q.shape[0]
    g, n_cp, d = kc.shape
    rq = NSA_REP * QBLK
    per_g = lambda shape: pl.BlockSpec((None,) + shape, lambda gi, qi: (gi,) + (0,) * len(shape))
    kv_part = lambda part: pl.BlockSpec((t + WINDOW, d), lambda gi, qi: (0, part * g + gi))
    return pl.pallas_call(
        _nsa_kernel,
        out_shape=jax.ShapeDtypeStruct((t, NSA_HEADS * d), BF16),
        grid=(g, t // QBLK),
        in_specs=[pl.BlockSpec((QBLK, NSA_REP * d), lambda gi, qi: (qi, gi)),
                  pl.BlockSpec((None, QBLK, NSA_REP * N_BRANCH), lambda gi, qi: (gi, qi, 0)),
                  per_g((n_cp, d)), per_g((n_cp, d)),
                  kv_part(2), kv_part(3), kv_part(4), kv_part(5),
                  pl.BlockSpec((t, d), lambda gi, qi: (0, 0)),
                  per_g((rq, d)),
                  pl.BlockSpec(ov.shape, lambda gi, qi: (0, 0)),
                  per_g((2, rq, 2 * QBLK)),
                  per_g((rq, WINDOW + QBLK))],
        out_specs=pl.BlockSpec((QBLK, NSA_REP * d), lambda gi, qi: (qi, gi)),
        scratch_shapes=[pltpu.VMEM((2, rq, SEL_TILE), F32), pltpu.VMEM((2, rq, SEL_TILE), BF16)],
        compiler_params=_cparams(("parallel", "arbitrary")),
        name="nsa_attention",
    )(q, gates, kc, vc, kv_padded, kv_padded, kv_padded, kv_padded, block_onehot, gs, ov, bs, bw)


def _t5_bucket_np(n):
    n = np.maximum(n, 0)
    max_exact = REL_BUCKETS // 2
    nf = np.maximum(n, 1).astype(np.float64)
    large = max_exact + (np.log(nf / max_exact) / math.log(REL_MAX_DIST / max_exact)
                         * (REL_BUCKETS - max_exact)).astype(np.int64)
    large = np.minimum(large, REL_BUCKETS - 1)
    return np.where(n < max_exact, n, large).astype(np.int32)


def _nsa_bias_tables(rel_bias):
    G, R, Q = NSA_GROUPS, NSA_REP, QBLK
    table = rel_bias.astype(F32).reshape(REL_BUCKETS, G, R).transpose(1, 2, 0)
    qq = np.arange(Q)

    def lookup(dist):
        return table[:, :, _t5_bucket_np(dist)].reshape(G, R * Q, dist.shape[1])

    m = np.arange(CMP_BAND) - CMP_BAND_LO
    gs = lookup(qq[:, None] - (CMP_LEN - 1) + CMP_STRIDE * m[None, :])
    far = jnp.broadcast_to(table[:, :, REL_BUCKETS - 1][:, :, None, None], (G, R, Q, 1)).reshape(G, R * Q, 1)
    gs = jnp.concatenate([gs, far, jnp.zeros((G, R * Q, CMP_BIAS_K - CMP_BAND - 1), F32)], axis=-1)
    h, mid, l = _split3(gs)
    gs3 = jnp.concatenate([h, mid, l, jnp.zeros_like(h)], axis=-1)
    d0 = np.tile(qq[:, None] - qq[None, :], (R, 1))
    causal = jnp.asarray(np.where(d0 >= 0, 0.0, NEG_INF), F32)[None]
    b0 = lookup(qq[:, None] - qq[None, :]) + causal
    b1 = lookup(qq[:, None] - qq[None, :] + Q)
    far_c = jnp.broadcast_to(far, (G, R * Q, Q))
    neg = jnp.full((G, R * Q, Q), NEG_INF, F32)
    bs = jnp.stack([jnp.concatenate([b1 - far_c, b0 - far_c], -1),
                    jnp.concatenate([b0 - far_c, neg], -1)], axis=1)
    oldest = far_c + jnp.asarray(np.where(d0 < 0, 0.0, NEG_INF), F32)[None]
    bw = jnp.concatenate([oldest, far_c, far_c, b1, b0], -1)
    return gs3, bs, bw


def _overlap_table(n_cp, n_sb):
    c = np.arange(n_cp)[:, None] * CMP_STRIDE
    s = np.arange(n_sb)[None, :] * SEL_BLOCK
    return jnp.asarray(((c <= s + SEL_BLOCK - 1) & (c + CMP_LEN - 1 >= s)), BF16)


def _pad_cols(w, n):
    return jnp.pad(w, ((0, 0), (0, n - w.shape[1])))


def _pad_rows(w, n):
    return jnp.pad(w, ((0, n - w.shape[0]), (0, 0)))


def _rwkv_layer(h, norm_g, mu, wr, wk, wv, wo, w0, w1, w2, a0, a1, a2, g1, g2,
                k_k, k_a, r_k, lnx_w, lnx_b, v_first, v0, v1, v2):
    bf = lambda w: w.astype(BF16)
    xr, xw, xk, xv, xa, xg = rwkv_prep(h, norm_g, mu)
    r = matmul(xr, bf(wr))
    k = matmul(xk, bf(wk))
    v = matmul(xv, bf(wv))
    lw_hid = matmul(xw, bf(_pad_cols(w1, 128)), act="tanh", out_dtype=BF16)
    w_lin = matmul(lw_hid, bf(_pad_rows(w2, 128)))
    a_lin = matmul(matmul(xa, bf(a1), out_dtype=BF16), bf(a2))
    g = matmul(matmul(xg, bf(g1), act="sigmoid", out_dtype=BF16), bf(g2))
    zeros = jnp.zeros_like(w0)
    if v_first is None:
        v_lin, v0 = None, zeros
    else:
        v_lin = matmul(matmul(xv, bf(_pad_cols(v1, 128)), out_dtype=BF16), bf(_pad_rows(v2, 128)))
    params = jnp.stack([w0, a0, v0, k_k, k_a, r_k.reshape(-1), zeros, zeros])
    m, c, qp, y0, bonus = rwkv_chunks(r, k, v, w_lin, a_lin, v_lin, v_first, params)
    ln_params = jnp.stack([lnx_w, lnx_b] + [zeros] * 6)
    gated = rwkv_state_scan(m, c, qp, y0, bonus, g, ln_params)
    h = matmul(gated, bf(wo), residual=h)
    return h, (v if v_first is None else v_first)


def _mlp(h, norm_g, w_up, w_down):
    xn = rmsnorm(h, norm_g)
    hid = matmul(xn, w_up.astype(BF16), act="relu2", out_dtype=BF16)
    return matmul(hid, w_down.astype(BF16), residual=h)


def _nsa_shared_kv(h, kv_norm_g, w_kv, cmp_pe, cmp_w1, cmp_b1, cmp_w2):
    t = h.shape[0]
    G, D = NSA_GROUPS, NSA_HD
    kv = matmul(rmsnorm(h, kv_norm_g), w_kv.astype(BF16), out_dtype=BF16)
    x16 = kv[:, :2 * G * D].reshape(t, 2, G, D).transpose(1, 2, 0, 3).reshape(2, G, t // CMP_STRIDE, CMP_STRIDE * D)
    half = CMP_STRIDE * D
    w1cat = jnp.concatenate([cmp_w1[:, :half], cmp_w1[:, half:]], axis=-1).astype(BF16)
    cvec = jnp.einsum("plk,plkd->pd", cmp_pe, cmp_w1.reshape(2, CMP_LEN, D, D),
                      precision=lax.Precision.HIGHEST) + cmp_b1
    kvc = compress(x16, w1cat, cvec.reshape(2, 1, D), cmp_w2.astype(BF16))
    block_onehot = jnp.asarray(np.arange(t)[:, None] // SEL_BLOCK == np.arange(D)[None, :], BF16)
    return kvc[0], kvc[1], jnp.pad(kv, ((WINDOW, 0), (0, 0))), block_onehot


def _nsa_layer(h, norm_g, wq, wo, kvs, tables):
    t = h.shape[0]
    hn = rmsnorm(h, norm_g)
    nq = NSA_HEADS * NSA_HD
    q = matmul(hn, wq[:, :nq].astype(BF16), scale=NSA_HD ** -0.5, out_dtype=BF16)
    gates = matmul(hn, _pad_cols(wq[:, nq:], 128).astype(BF16), act="sigmoid")
    gates = gates[:, :NSA_HEADS * N_BRANCH].reshape(t, NSA_GROUPS, NSA_REP * N_BRANCH).transpose(1, 0, 2)
    o = nsa_attention(q, gates, *kvs, *tables)
    return matmul(o, wo.astype(BF16), residual=h)


def kernel(x, a_norm_g, rw_mu, rw_wr, rw_wk, rw_wv, rw_wo, rw_w0, rw_w1, rw_w2, rw_a0, rw_a1, rw_a2,
           rw_v0, rw_v1, rw_v2, rw_g1, rw_g2, rw_kk, rw_ka, rw_rk, rw_lnx_w, rw_lnx_b, kv_norm_g, w_kv,
           cmp_pe, cmp_w1, cmp_b1, cmp_w2, b_norm_g, nsa_wq, nsa_wo, rel_bias, m_norm_g, mlp_up,
           mlp_down, final_norm_g):
    b, t, d = x.shape
    assert b == 1 and d == D_MODEL and t % SEL_TILE == 0 and t // SEL_BLOCK <= 128 and WINDOW == 4 * QBLK
    h = x.reshape(t, d)
    v_first = None
    kvs = tables = None
    for layer in range(DEPTH):
        if layer < N_A_LAYERS:
            i = layer
            vp = (None, None, None) if i == 0 else (rw_v0[i - 1], rw_v1[i - 1], rw_v2[i - 1])
            h, v_first = _rwkv_layer(
                h, a_norm_g[i], rw_mu[i], rw_wr[i], rw_wk[i], rw_wv[i], rw_wo[i], rw_w0[i], rw_w1[i],
                rw_w2[i], rw_a0[i], rw_a1[i], rw_a2[i], rw_g1[i], rw_g2[i], rw_kk[i], rw_ka[i],
                rw_rk[i].reshape(1, RW_HEADS, RW_HEAD), rw_lnx_w[i], rw_lnx_b[i], v_first, *vp)
        else:
            j = layer - N_A_LAYERS
            if j == 0:
                kvs = _nsa_shared_kv(h, kv_norm_g, w_kv, cmp_pe, cmp_w1, cmp_b1, cmp_w2)
                gs3, bs, bw = _nsa_bias_tables(rel_bias)
                tables = (gs3, _overlap_table(t // CMP_STRIDE, 128), bs, bw)
            h = _nsa_layer(h, b_norm_g[j], nsa_wq[j], nsa_wo[j], kvs, tables)
        h = _mlp(h, m_norm_g[layer], mlp_up[layer], mlp_down[layer])
    return rmsnorm(h, final_norm_g, out_dtype=x.dtype).reshape(b, t, d)
```

```python
import functools
import math

import numpy as np
import jax
import jax.numpy as jnp
from jax import lax
from jax.experimental import pallas as pl
from jax.experimental.pallas import tpu as pltpu

F32 = jnp.float32
BF16 = jnp.bfloat16

D_MODEL = 2048
DEPTH = 4
N_A_LAYERS = 2
RW_HEAD = 64
RW_HEADS = D_MODEL // RW_HEAD
N_SHIFT_MIX = 6
GN_EPS = 64e-5
NSA_HEADS = 16
NSA_GROUPS = 4
NSA_REP = NSA_HEADS // NSA_GROUPS
NSA_HD = D_MODEL // NSA_HEADS
N_KV_PARTS = 6
CMP_STRIDE = 16
CMP_LEN = 32
SEL_BLOCK = 64
SEL_TOP = 16
WINDOW = 512
QBLK = 128
N_BRANCH = 3
SEL_FORCE = 1e4
NEG_INF = -1e30
REL_BUCKETS = 32
REL_MAX_DIST = 128
NORM_EPS = 1e-6

VMEM_LIMIT_BYTES = 56 * 1024 * 1024
SCAN_CHUNK = 64
SCAN_HEADS = 4
SCAN_WIDTH = SCAN_HEADS * RW_HEAD
SCAN_UNITS = 4
SEL_TILE = 512
CMP_BAND = 16
CMP_BAND_LO = 6
CMP_BIAS_K = 32
assert CMP_BAND < CMP_BIAS_K and 4 * CMP_BIAS_K == NSA_HD and NSA_HD == 128


def _cparams(sem):
    return pltpu.CompilerParams(dimension_semantics=sem, vmem_limit_bytes=VMEM_LIMIT_BYTES)


def _split3(x):
    h = x.astype(BF16)
    r = x - h.astype(F32)
    m = r.astype(BF16)
    l = (r - m.astype(F32)).astype(BF16)
    return h, m, l


def _dot(a, b):
    return jnp.dot(a, b, preferred_element_type=F32)


def _dot_nt(a, b):
    return lax.dot_general(a, b, (((1,), (1,)), ((), ())), preferred_element_type=F32)


def _dot_tn(a, b):
    return lax.dot_general(a, b, (((0,), (0,)), ((), ())), preferred_element_type=F32)


def _dot_exact_lhs(a_bf16, x):
    h, m, l = _split3(x)
    return _dot(a_bf16, h) + _dot(a_bf16, m) + _dot(a_bf16, l)


def _dot_exact_rhs(x, b_bf16):
    h, m, l = _split3(x)
    return _dot(h, b_bf16) + _dot(m, b_bf16) + _dot(l, b_bf16)


def _activate(acc, act, scale):
    if act == "relu2":
        acc = jnp.square(jnp.maximum(acc, 0.0))
    elif act == "tanh":
        acc = jnp.tanh(acc)
    elif act == "sigmoid":
        acc = jax.nn.sigmoid(acc)
    return acc if scale is None else acc * scale


def _matmul_kernel(x_ref, w_ref, *rest, act, scale, has_res, nk):
    if has_res:
        res_ref, o_ref, *scratch = rest
    else:
        o_ref, *scratch = rest

    def epilogue(acc):
        acc = _activate(acc, act, scale)
        if has_res:
            acc = acc + res_ref[...]
        o_ref[...] = acc.astype(o_ref.dtype)

    if nk == 1:
        epilogue(_dot(x_ref[...], w_ref[...]))
    else:
        acc_ref, = scratch
        k = pl.program_id(2)

        @pl.when(k == 0)
        def _():
            acc_ref[...] = jnp.zeros_like(acc_ref)

        acc_ref[...] += _dot(x_ref[...], w_ref[...])

        @pl.when(k == nk - 1)
        def _():
            epilogue(acc_ref[...])


def _pick_tile(n, pref):
    t = min(n, pref)
    while n % t:
        t //= 2
    return t


def matmul(x, w, *, act=None, scale=None, residual=None, out_dtype=F32, n=None, tm=1024, tn=512, tk=2048):
    m, kdim = x.shape
    n = w.shape[1] if n is None else n
    tm, tn, tk = _pick_tile(m, tm), _pick_tile(n, tn), _pick_tile(kdim, tk)
    nk = kdim // tk
    in_specs = [pl.BlockSpec((tm, tk), lambda j, i, k: (i, k)),
                pl.BlockSpec((tk, tn), lambda j, i, k: (k, j))]
    args = [x, w]
    if residual is not None:
        in_specs.append(pl.BlockSpec((tm, tn), lambda j, i, k: (i, j)))
        args.append(residual)
    return pl.pallas_call(
        functools.partial(_matmul_kernel, act=act, scale=scale, has_res=residual is not None, nk=nk),
        out_shape=jax.ShapeDtypeStruct((m, n), out_dtype),
        grid=(n // tn, m // tm, nk),
        in_specs=in_specs,
        out_specs=pl.BlockSpec((tm, tn), lambda j, i, k: (i, j)),
        scratch_shapes=[pltpu.VMEM((tm, tn), F32)] if nk > 1 else [],
        compiler_params=_cparams(("parallel", "parallel", "arbitrary")),
        name="matmul",
    )(*args)


def _rms(x, g):
    return x * lax.rsqrt(jnp.mean(x * x, axis=-1, keepdims=True) + NORM_EPS) * g


def _rmsnorm_kernel(h_ref, g_ref, o_ref):
    o_ref[...] = _rms(h_ref[...], g_ref[...]).astype(o_ref.dtype)


def rmsnorm(h, g, out_dtype=BF16, tm=512):
    t, d = h.shape
    tm = _pick_tile(t, tm)
    return pl.pallas_call(
        _rmsnorm_kernel,
        out_shape=jax.ShapeDtypeStruct((t, d), out_dtype),
        grid=(t // tm,),
        in_specs=[pl.BlockSpec((tm, d), lambda i: (i, 0)), pl.BlockSpec((1, d), lambda i: (0, 0))],
        out_specs=pl.BlockSpec((tm, d), lambda i: (i, 0)),
        compiler_params=_cparams(("parallel",)),
        name="rmsnorm",
    )(h, g.reshape(1, d))


def _norm_matmul_kernel(h_ref, g_ref, w_ref, o_ref, xn_ref, *, act, scale):
    @pl.when(pl.program_id(1) == 0)
    def _():
        xn_ref[...] = _rms(h_ref[...], g_ref[...]).astype(xn_ref.dtype)

    o_ref[...] = _activate(_dot(xn_ref[...], w_ref[...]), act, scale).astype(o_ref.dtype)


def norm_matmul(h, g, w, *, act=None, scale=None, out_dtype=F32, n=None, tm=1024, tn=512):
    m, kdim = h.shape
    n = w.shape[1] if n is None else n
    tm, tn = _pick_tile(m, tm), _pick_tile(n, tn)
    return pl.pallas_call(
        functools.partial(_norm_matmul_kernel, act=act, scale=scale),
        out_shape=jax.ShapeDtypeStruct((m, n), out_dtype),
        grid=(m // tm, n // tn),
        in_specs=[pl.BlockSpec((tm, kdim), lambda i, j: (i, 0)),
                  pl.BlockSpec((1, kdim), lambda i, j: (0, 0)),
                  pl.BlockSpec((kdim, tn), lambda i, j: (0, j))],
        out_specs=pl.BlockSpec((tm, tn), lambda i, j: (i, j)),
        scratch_shapes=[pltpu.VMEM((tm, kdim), BF16)],
        compiler_params=_cparams(("parallel", "arbitrary")),
        name="norm_matmul",
    )(h, g.reshape(1, kdim), w)


def _rwkv_prep_kernel(h_ref, prev_ref, g_ref, mu_ref, *o_refs):
    g = g_ref[...]
    xn = _rms(h_ref[...], g)
    prev = _rms(prev_ref[...], g)[7:8, :] * (pl.program_id(0) > 0).astype(F32)
    row = lax.broadcasted_iota(jnp.int32, xn.shape, 0)
    shifted = jnp.where(row == 0, prev, pltpu.roll(xn, 1, axis=0))
    xx = shifted - xn
    for i, o_ref in enumerate(o_refs):
        o_ref[...] = (xn + xx * mu_ref[i:i + 1, :]).astype(o_ref.dtype)


def rwkv_prep(h, g, mu, tm=256):
    t, d = h.shape
    tm = _pick_tile(t, tm)
    blk = pl.BlockSpec((tm, d), lambda i: (i, 0))
    return pl.pallas_call(
        _rwkv_prep_kernel,
        out_shape=[jax.ShapeDtypeStruct((t, d), BF16)] * N_SHIFT_MIX,
        grid=(t // tm,),
        in_specs=[blk,
                  pl.BlockSpec((8, d), lambda i: (jnp.maximum(i * (tm // 8) - 1, 0), 0)),
                  pl.BlockSpec((1, d), lambda i: (0, 0)),
                  pl.BlockSpec((N_SHIFT_MIX, d), lambda i: (0, 0))],
        out_specs=[blk] * N_SHIFT_MIX,
        compiler_params=_cparams(("parallel",)),
        name="rwkv_prep",
    )(h, h, g.reshape(1, d), mu)


def _head_mask(dtype):
    row = lax.broadcasted_iota(jnp.int32, (SCAN_WIDTH, SCAN_WIDTH), 0)
    col = lax.broadcasted_iota(jnp.int32, (SCAN_WIDTH, SCAN_WIDTH), 1)
    return jnp.where((row >> 6) == (col >> 6), 1.0, 0.0).astype(dtype)


def _block_diag(x, mask):
    return jnp.concatenate([x] * SCAN_HEADS, axis=0) * mask


def _fold_rows(x_full, mask_f32):
    x = x_full * mask_f32
    L = SCAN_CHUNK
    return x[0:L] + x[L:2 * L] + x[2 * L:3 * L] + x[3 * L:4 * L]


def _split2(x):
    hi = x.astype(BF16)
    return hi, (x - hi.astype(F32)).astype(BF16)


def _bd_parts(y, mask_b):
    hi, lo = _split2(y)
    return _block_diag(hi, mask_b), _block_diag(lo, mask_b)


def _mm_heads(x, y_parts):
    y_hi, y_lo = y_parts
    n = x.shape[0]
    x_hi, x_lo = _split2(x)
    both = _dot(jnp.concatenate([x_hi, x_lo], axis=0), y_hi)
    return both[:n] + both[n:] + _dot(x_hi, y_lo)


def _run_interleaved(stage_generators):
    live = list(stage_generators)
    while live:
        still = []
        for gen in live:
            try:
                next(gen)
                still.append(gen)
            except StopIteration:
                pass
        live = still


def _tri_inverse_heads(a, j_row, m_col, mask_b):
    L = SCAN_CHUNK
    eye = jnp.where(m_col == j_row, 1.0, 0.0)
    d = jnp.where((j_row >> 4) == (m_col >> 4), a, 0.0)
    e = a - d
    p = eye + d
    x = _mm_heads(d, _bd_parts(d, mask_b))
    yield
    for _ in range(2):
        both = _mm_heads(jnp.concatenate([x, p], axis=0), _bd_parts(x, mask_b))
        x, p = both[:L], p + both[L:]
        yield
    p = p + _mm_heads(p, _bd_parts(x, mask_b))
    yield
    n1 = _mm_heads(p, _bd_parts(e, mask_b))
    yield
    n2 = _mm_heads(n1, _bd_parts(n1, mask_b))
    yield
    q = eye + n1
    q = q + _mm_heads(q, _bd_parts(n2, mask_b))
    yield
    return _mm_heads(q, _bd_parts(p, mask_b))


def _softplus(x):
    return jnp.maximum(x, 0.0) + jnp.log(1.0 + jnp.exp(-jnp.abs(x)))


def _rwkv_chunk_kernel(*refs, mix_v):
    if mix_v:
        r_ref, k_ref, v_ref, wl_ref, al_ref, vl_ref, vf_ref, par_ref, m_ref, c_ref, qp_ref, y0_ref, bo_ref = refs
    else:
        r_ref, k_ref, v_ref, wl_ref, al_ref, par_ref, m_ref, c_ref, qp_ref, y0_ref, bo_ref = refs
    L, W = SCAN_CHUNK, SCAN_WIDTH
    mask_b = _head_mask(BF16)
    mask_f = _head_mask(F32)
    tri_incl = jnp.where(lax.broadcasted_iota(jnp.int32, (L, L), 1)
                         <= lax.broadcasted_iota(jnp.int32, (L, L), 0), 1.0, 0.0).astype(BF16)
    j_row = lax.broadcasted_iota(jnp.int32, (L, W), 0)
    m_col = lax.broadcasted_iota(jnp.int32, (L, W), 1) & (L - 1)
    strict = m_col < j_row
    incl = m_col <= j_row

    def unit(u):
        cols = slice(u * W, (u + 1) * W)
        par = par_ref[:, cols]
        w0, a0, v0, k_k, k_a, r_k = (par[i:i + 1] for i in range(6))
        r, k, v = r_ref[:, cols], k_ref[:, cols], v_ref[:, cols]
        lw = -jnp.exp(-_softplus(-(w0 + wl_ref[:, cols])) - 0.5)
        a_s = jax.nn.sigmoid(a0 + al_ref[:, cols])
        if mix_v:
            v = v + (vf_ref[:, cols] - v) * jax.nn.sigmoid(v0 + vl_ref[:, cols])
        kk = k * k_k
        k = k * (1.0 + (a_s - 1.0) * k_a)
        kk_sq = _dot_exact_rhs(kk * kk, mask_b)
        rk_sum = _dot_exact_rhs(r * k * r_k, mask_b)
        c = _dot_exact_lhs(tri_incl, lw)
        yield
        kk = kk * lax.rsqrt(jnp.maximum(kk_sq, 1e-24))
        bo_ref[:, cols] = rk_sum * v
        c_last = c[L - 1:L, :]
        p_inv = jnp.exp(-c)
        p_rest = jnp.exp(c_last - c)
        rt = r * jnp.exp(c)
        at = (-kk * jnp.exp(c - lw)).astype(BF16)
        b = kk * a_s
        vb = v.astype(BF16)
        ar = jnp.concatenate([at, rt.astype(BF16)], axis=0)
        g_b = _dot_nt(ar, _block_diag((b * p_inv).astype(BF16), mask_b))
        g_k = _dot_nt(ar, _block_diag((k * p_inv).astype(BF16), mask_b))
        yield
        a_ab = jnp.where(strict, g_b[:L], 0.0)
        a_ak = jnp.where(strict, g_k[:L], 0.0).astype(BF16)
        a_rb = jnp.where(incl, g_b[L:], 0.0).astype(BF16)
        a_rk = jnp.where(incl, g_k[L:], 0.0).astype(BF16)
        v_bd = _block_diag(vb, mask_b)
        akv = _dot(a_ak, v_bd)
        t_inv = yield from _tri_inverse_heads(a_ab, j_row, m_col, mask_b)
        yield
        t_hi, t_lo = _split2(t_inv)
        w_both = _dot(jnp.concatenate([t_hi, t_lo], axis=0), _block_diag(at, mask_b))
        u0 = _mm_heads(t_inv, _bd_parts(akv, mask_b)).astype(BF16)
        yield
        w = (w_both[:L] + w_both[L:]).astype(BF16)
        qp_ref[:, cols] = (rt + _dot(a_rb, _block_diag(w, mask_b))).astype(qp_ref.dtype)
        y0_ref[:, cols] = _dot(a_rb, _block_diag(u0, mask_b)) + _dot(a_rk, v_bd)
        bh = (b * p_rest).astype(BF16)
        kh = (k * p_rest).astype(BF16)
        decay_diag = jnp.where(m_col == j_row, jnp.exp(c_last), 0.0)
        m_ref[:, cols] = _fold_rows(_dot_tn(bh, w), mask_f) + decay_diag
        c_ref[:, cols] = _fold_rows(_dot_tn(bh, u0) + _dot_tn(kh, vb), mask_f)

    _run_interleaved(unit(u) for u in range(SCAN_UNITS))


def rwkv_chunks(r, k, v, w_lin, a_lin, v_lin, v_first, params):
    t, d = r.shape
    bw = SCAN_UNITS * SCAN_WIDTH
    blk = pl.BlockSpec((SCAN_CHUNK, bw), lambda ci, gi: (ci, gi))
    mix_v = v_lin is not None
    args = [r, k, v, w_lin, a_lin] + ([v_lin, v_first] if mix_v else []) + [params]
    f32 = jax.ShapeDtypeStruct((t, d), F32)
    return pl.pallas_call(
        functools.partial(_rwkv_chunk_kernel, mix_v=mix_v),
        out_shape=[f32, f32, jax.ShapeDtypeStruct((t, d), BF16), f32, f32],
        grid=(t // SCAN_CHUNK, d // bw),
        in_specs=[blk] * (len(args) - 1) + [pl.BlockSpec((8, bw), lambda ci, gi: (0, gi))],
        out_specs=[blk] * 5,
        compiler_params=_cparams(("parallel", "parallel")),
        name="rwkv_chunks",
    )(*args)


def _rwkv_state_kernel(m_ref, c_ref, qp_ref, y0_ref, bo_ref, g_ref, ln_ref, o_ref, st_ref):
    @pl.when(pl.program_id(1) == 0)
    def _():
        st_ref[...] = jnp.zeros_like(st_ref)

    W = SCAN_WIDTH
    mask_b = _head_mask(BF16)
    def unit(u):
        cols = slice(u * W, (u + 1) * W)
        st_parts = _bd_parts(st_ref[u], mask_b)
        y = _dot(qp_ref[:, cols], st_parts[0]) + y0_ref[:, cols]
        st_ref[u] = _mm_heads(m_ref[:, cols], st_parts) + c_ref[:, cols]
        yield
        mean = _dot_exact_rhs(y, mask_b) * (1.0 / RW_HEAD)
        yield
        dev = y - mean
        var = _dot_exact_rhs(dev * dev, mask_b) * (1.0 / RW_HEAD)
        yield
        ln = ln_ref[:, cols]
        out = (dev * lax.rsqrt(var + GN_EPS) * ln[0:1] + ln[1:2] + bo_ref[:, cols]) * g_ref[:, cols]
        o_ref[:, cols] = out.astype(o_ref.dtype)

    _run_interleaved(unit(u) for u in range(SCAN_UNITS))


def rwkv_state_scan(m, c, qp, y0, bonus, g, ln_params):
    t, d = m.shape
    bw = SCAN_UNITS * SCAN_WIDTH
    blk = pl.BlockSpec((SCAN_CHUNK, bw), lambda gi, ci: (ci, gi))
    return pl.pallas_call(
        _rwkv_state_kernel,
        out_shape=jax.ShapeDtypeStruct((t, d), BF16),
        grid=(d // bw, t // SCAN_CHUNK),
        in_specs=[blk] * 6 + [pl.BlockSpec((8, bw), lambda gi, ci: (0, gi))],
        out_specs=blk,
        scratch_shapes=[pltpu.VMEM((SCAN_UNITS, RW_HEAD, SCAN_WIDTH), F32)],
        compiler_params=_cparams(("parallel", "arbitrary")),
        name="rwkv_state_scan",
    )(m, c, qp, y0, bonus, g, ln_params)


def _compress_kernel(x_ref, w1_ref, c_ref, w2_ref, o_ref):
    ab = _dot(x_ref[...], w1_ref[...])
    a = ab[:, :NSA_HD]
    bsh = ab[:, NSA_HD:]
    n16 = a.shape[0]
    row = lax.broadcasted_iota(jnp.int32, a.shape, 0)
    b_next = jnp.where(row == n16 - 1, 0.0, pltpu.roll(bsh, n16 - 1, axis=0))
    hid = a + b_next + c_ref[...]
    hid = jax.nn.gelu(hid, approximate=True)
    o_ref[...] = _dot(hid.astype(BF16), w2_ref[...]).astype(o_ref.dtype)


def compress(x16, w1cat, cvec, w2):
    _, g, n16, kd = x16.shape
    d = NSA_HD
    return pl.pallas_call(
        _compress_kernel,
        out_shape=jax.ShapeDtypeStruct((2, g, n16, d), BF16),
        grid=(2, g),
        in_specs=[pl.BlockSpec((None, None, n16, kd), lambda p, gi: (p, gi, 0, 0)),
                  pl.BlockSpec((None, kd, 2 * d), lambda p, gi: (p, 0, 0)),
                  pl.BlockSpec((None, 1, d), lambda p, gi: (p, 0, 0)),
                  pl.BlockSpec((None, d, d), lambda p, gi: (p, 0, 0))],
        out_specs=pl.BlockSpec((None, None, n16, d), lambda p, gi: (p, gi, 0, 0)),
        compiler_params=_cparams(("parallel", "parallel")),
        name="nsa_compress",
    )(x16, w1cat, cvec, w2)


def _softmax_rows(s):
    mx = jnp.max(s, axis=-1, keepdims=True)
    e = jnp.exp(s - mx)
    return e / jnp.sum(e, axis=-1, keepdims=True)


def _nsa_kernel(q_ref, gate_ref, kc_ref, vc_ref, ks_ref, vs_ref, kw_ref, vw_ref, oh_ref,
                gs_ref, ov_ref, bs_ref, bw_ref, o_ref, s_buf, p_buf):
    qi = pl.program_id(1)
    R, Q, D = NSA_REP, QBLK, NSA_HD
    q = q_ref[...]
    qs = jnp.concatenate([q[:, r * D:(r + 1) * D] for r in range(R)], axis=0)
    n_cp = kc_ref.shape[0]
    n_sb = ov_ref.shape[1]
    out = {}

    def compressed_and_topk():
        t_rows = qi * Q + (lax.broadcasted_iota(jnp.int32, (R * Q, 1), 0) & (Q - 1))
        c_idx = lax.broadcasted_iota(jnp.int32, (1, n_cp), 1)
        m_off = qi * (Q // CMP_STRIDE) - lax.broadcasted_iota(jnp.int32, (n_cp, D), 0) + CMP_BAND_LO
        k_idx = lax.broadcasted_iota(jnp.int32, (n_cp, D), 1) & (CMP_BIAS_K - 1)
        in_band = ((k_idx < CMP_BAND) & (m_off == k_idx)) | ((k_idx == CMP_BAND) & (m_off >= CMP_BAND))
        k_c = jnp.concatenate([kc_ref[...], jnp.where(in_band, 1.0, 0.0).astype(BF16)], axis=1)
        qk_c = _dot_nt(jnp.concatenate([qs, gs_ref[...]], axis=1), k_c)
        yield
        m_c = (c_idx * CMP_STRIDE + (CMP_LEN - 1)) <= t_rows
        s_c = jnp.where(m_c, qk_c, NEG_INF)
        p_c = jnp.where(m_c, _softmax_rows(s_c), 0.0)
        out["o_c"] = _dot(p_c.astype(BF16), vc_ref[...])
        p_sum = p_c[0:Q] + p_c[Q:2 * Q] + p_c[2 * Q:3 * Q] + p_c[3 * Q:4 * Q]
        imp = _dot_exact_rhs(p_sum, ov_ref[...])
        yield
        imp_t = imp.T
        s_row = lax.broadcasted_iota(jnp.int32, (n_sb, Q), 0)
        t_col = qi * Q + lax.broadcasted_iota(jnp.int32, (n_sb, Q), 1)
        forced = (s_row == (t_col >> 6)) | (s_row == 0)
        score = jnp.where(forced, SEL_FORCE, jnp.where(s_row * SEL_BLOCK <= t_col, imp_t, -SEL_FORCE))
        sel_t = jnp.zeros((n_sb, Q), F32)
        s_f = s_row.astype(F32)
        for _ in range(SEL_TOP):
            mx = jnp.max(score, axis=0, keepdims=True)
            first = jnp.min(jnp.where(score == mx, s_f, float(n_sb)), axis=0, keepdims=True)
            pick = s_f == first
            sel_t = jnp.where(pick, 1.0, sel_t)
            score = jnp.where(pick, -3e38, score)
        out["sel"] = sel_t.T

    def window():
        w0 = pl.multiple_of(qi * Q, Q)
        qk_w = _dot_nt(qs, kw_ref[pl.ds(w0, WINDOW + Q), :])
        yield
        j_io = lax.broadcasted_iota(jnp.int32, (1, WINDOW + Q), 1)
        s_w = jnp.where(j_io >= WINDOW - qi * Q, qk_w + bw_ref[...], NEG_INF)
        out["o_w"] = _dot(_softmax_rows(s_w).astype(BF16), vw_ref[pl.ds(w0, WINDOW + Q), :])

    _run_interleaved([compressed_and_topk(), window()])
    o_c, o_w, sel = out["o_c"], out["o_w"], out["sel"]

    n_far = jnp.maximum(qi - 1, 0)
    blk = lax.broadcasted_iota(jnp.int32, (Q, n_sb), 1)
    neg_near = jnp.where(sel > 0.5, 0.0, NEG_INF).astype(BF16)
    neg_far = jnp.where((sel > 0.5) & (blk < 2 * n_far), 0.0, NEG_INF).astype(BF16)
    q_near = jnp.concatenate([qs, jnp.concatenate([neg_near] * R, axis=0)], axis=1)
    q_far = jnp.concatenate([qs, jnp.concatenate([neg_far] * R, axis=0)], axis=1)

    nb0 = jnp.maximum(qi - 1, 0)
    n0 = pl.multiple_of(nb0 * Q, Q)

    def sel_keys(start, n):
        return jnp.concatenate([ks_ref[pl.ds(WINDOW + start, n), :], oh_ref[pl.ds(start, n), :]], axis=1)

    s = _dot_nt(q_near, sel_keys(n0, 2 * Q)) + bs_ref[jnp.where(qi == 0, 1, 0)]
    m_0 = jnp.max(s, axis=-1, keepdims=True)
    p = jnp.exp(s - m_0)
    l_0 = jnp.sum(p, axis=-1, keepdims=True)
    acc_0 = _dot(p.astype(BF16), vs_ref[pl.ds(WINDOW + n0, 2 * Q), :])

    n_tiles = (n_far + (SEL_TILE // Q - 1)) // (SEL_TILE // Q)
    n_pairs = (n_tiles + 1) >> 1
    max_tile = oh_ref.shape[0] // SEL_TILE - 1

    def key_tile(i):
        return sel_keys(pl.multiple_of(i * SEL_TILE, SEL_TILE), SEL_TILE)

    def value_tile(i):
        return vs_ref[pl.ds(pl.multiple_of(WINDOW + i * SEL_TILE, SEL_TILE), SEL_TILE), :]

    s_buf[0] = _dot_nt(q_far, key_tile(0))
    p_buf[1] = jnp.zeros(p_buf.shape[1:], p_buf.dtype)

    def half_step(i, slot, carry):
        m_run, l_run, acc, alpha_prev = carry
        s_buf[1 - slot] = _dot_nt(q_far, key_tile(jnp.minimum(i + 1, max_tile)))
        acc = alpha_prev * acc + _dot(p_buf[1 - slot], value_tile(jnp.maximum(i - 1, 0)))
        s_i = s_buf[slot]
        m_new = jnp.maximum(m_run, jnp.max(s_i, axis=-1, keepdims=True))
        alpha = jnp.exp(m_run - m_new)
        p_i = jnp.exp(s_i - m_new)
        p_buf[slot] = p_i.astype(p_buf.dtype)
        return m_new, alpha * l_run + jnp.sum(p_i, axis=-1, keepdims=True), acc, alpha

    def pair_body(j, carry):
        return half_step(2 * j + 1, 1, half_step(2 * j, 0, carry))

    _, l_s, acc_s, alpha_last = lax.fori_loop(0, n_pairs, pair_body, (m_0, l_0, acc_0, jnp.ones_like(l_0)))
    acc_s = alpha_last * acc_s + _dot(p_buf[1], value_tile(jnp.maximum(2 * n_pairs - 1, 0)))
    o_s = acc_s / l_s

    gates = gate_ref[...]
    for r in range(R):
        rows = slice(r * Q, (r + 1) * Q)
        o = (gates[:, 3 * r:3 * r + 1] * o_c[rows] + gates[:, 3 * r + 1:3 * r + 2] * o_s[rows]
             + gates[:, 3 * r + 2:3 * r + 3] * o_w[rows])
        o_ref[:, r * D:(r + 1) * D] = o.astype(o_ref.dtype)


def nsa_attention(q, gates, kc, vc, kv_padded, block_onehot, gs, ov, bs, bw):
    t = q.shape[0]
    g, n_cp, d = kc.shape
    rq = NSA_REP * QBLK
    per_g = lambda shape: pl.BlockSpec((None,) + shape, lambda gi, qi: (gi,) + (0,) * len(shape))
    kv_part = lambda part: pl.BlockSpec((t + WINDOW, d), lambda gi, qi: (0, part * g + gi))
    return pl.pallas_call(
        _nsa_kernel,
        out_shape=jax.ShapeDtypeStruct((t, NSA_HEADS * d), BF16),
        grid=(g, t // QBLK),
        in_specs=[pl.BlockSpec((QBLK, NSA_REP * d), lambda gi, qi: (qi, gi)),
                  pl.BlockSpec((None, QBLK, NSA_REP * N_BRANCH), lambda gi, qi: (gi, qi, 0)),
                  per_g((n_cp, d)), per_g((n_cp, d)),
                  kv_part(2), kv_part(3), kv_part(4), kv_part(5),
                  pl.BlockSpec((t, d), lambda gi, qi: (0, 0)),
                  per_g((rq, d)),
                  pl.BlockSpec(ov.shape, lambda gi, qi: (0, 0)),
                  per_g((2, rq, 2 * QBLK)),
                  per_g((rq, WINDOW + QBLK))],
        out_specs=pl.BlockSpec((QBLK, NSA_REP * d), lambda gi, qi: (qi, gi)),
        scratch_shapes=[pltpu.VMEM((2, rq, SEL_TILE), F32), pltpu.VMEM((2, rq, SEL_TILE), BF16)],
        compiler_params=_cparams(("parallel", "arbitrary")),
        name="nsa_attention",
    )(q, gates, kc, vc, kv_padded, kv_padded, kv_padded, kv_padded, block_onehot, gs, ov, bs, bw)


def _t5_bucket_np(n):
    n = np.maximum(n, 0)
    max_exact = REL_BUCKETS // 2
    nf = np.maximum(n, 1).astype(np.float64)
    large = max_exact + (np.log(nf / max_exact) / math.log(REL_MAX_DIST / max_exact)
                         * (REL_BUCKETS - max_exact)).astype(np.int64)
    large = np.minimum(large, REL_BUCKETS - 1)
    return np.where(n < max_exact, n, large).astype(np.int32)


def _nsa_bias_tables(rel_bias):
    G, R, Q = NSA_GROUPS, NSA_REP, QBLK
    table = rel_bias.astype(F32).reshape(REL_BUCKETS, G, R).transpose(1, 2, 0)
    qq = np.arange(Q)

    def lookup(dist):
        return table[:, :, _t5_bucket_np(dist)].reshape(G, R * Q, dist.shape[1])

    m = np.arange(CMP_BAND) - CMP_BAND_LO
    gs = lookup(qq[:, None] - (CMP_LEN - 1) + CMP_STRIDE * m[None, :])
    far = jnp.broadcast_to(table[:, :, REL_BUCKETS - 1][:, :, None, None], (G, R, Q, 1)).reshape(G, R * Q, 1)
    gs = jnp.concatenate([gs, far, jnp.zeros((G, R * Q, CMP_BIAS_K - CMP_BAND - 1), F32)], axis=-1)
    h, mid, l = _split3(gs)
    gs3 = jnp.concatenate([h, mid, l, jnp.zeros_like(h)], axis=-1)
    def toeplitz(c):
        mm = np.arange(2 * Q)
        v = table[:, :, _t5_bucket_np(np.where(mm < Q, c - mm, c + 2 * Q - mm))]
        rows = jnp.tile(v, (1, 1, Q))[:, :, :Q * (2 * Q - 1)].reshape(G, R, Q, 2 * Q - 1)
        return rows[:, :, :, :Q].reshape(G, R * Q, Q)

    d0 = np.tile(qq[:, None] - qq[None, :], (R, 1))
    causal = jnp.asarray(np.where(d0 >= 0, 0.0, NEG_INF), F32)[None]
    b0 = toeplitz(0) + causal
    b1 = toeplitz(Q)
    far_c = jnp.broadcast_to(far, (G, R * Q, Q))
    neg = jnp.full((G, R * Q, Q), NEG_INF, F32)
    bs = jnp.stack([jnp.concatenate([b1 - far_c, b0 - far_c], -1),
                    jnp.concatenate([b0 - far_c, neg], -1)], axis=1)
    oldest = far_c + jnp.asarray(np.where(d0 < 0, 0.0, NEG_INF), F32)[None]
    bw = jnp.concatenate([oldest, far_c, far_c, b1, b0], -1)
    return gs3, bs, bw


def _overlap_table(n_cp, n_sb):
    c = np.arange(n_cp)[:, None] * CMP_STRIDE
    s = np.arange(n_sb)[None, :] * SEL_BLOCK
    return jnp.asarray(((c <= s + SEL_BLOCK - 1) & (c + CMP_LEN - 1 >= s)), BF16)


def _pad_cols(w, n):
    return jnp.pad(w, ((0, 0), (0, n - w.shape[1])))


def _pad_rows(w, n):
    return jnp.pad(w, ((0, n - w.shape[0]), (0, 0)))


def _rwkv_layer(h, norm_g, mu, wr, wk, wv, wo, w0, w1, w2, a0, a1, a2, g1, g2,
                k_k, k_a, r_k, lnx_w, lnx_b, v_first, v0, v1, v2):
    bf = lambda w: w.astype(BF16)
    xr, xw, xk, xv, xa, xg = rwkv_prep(h, norm_g, mu)
    r = matmul(xr, bf(wr))
    k = matmul(xk, bf(wk))
    v = matmul(xv, bf(wv))
    lw_hid = matmul(xw, bf(_pad_cols(w1, 128)), act="tanh", out_dtype=BF16)
    w_lin = matmul(lw_hid, bf(_pad_rows(w2, 128)))
    a_lin = matmul(matmul(xa, bf(a1), out_dtype=BF16), bf(a2))
    g = matmul(matmul(xg, bf(g1), act="sigmoid", out_dtype=BF16), bf(g2))
    zeros = jnp.zeros_like(w0)
    if v_first is None:
        v_lin, v0 = None, zeros
    else:
        v_lin = matmul(matmul(xv, bf(_pad_cols(v1, 128)), out_dtype=BF16), bf(_pad_rows(v2, 128)))
    params = jnp.stack([w0, a0, v0, k_k, k_a, r_k.reshape(-1), zeros, zeros])
    m, c, qp, y0, bonus = rwkv_chunks(r, k, v, w_lin, a_lin, v_lin, v_first, params)
    ln_params = jnp.stack([lnx_w, lnx_b] + [zeros] * 6)
    gated = rwkv_state_scan(m, c, qp, y0, bonus, g, ln_params)
    h = matmul(gated, bf(wo), residual=h)
    return h, (v if v_first is None else v_first)


def _mlp(h, norm_g, w_up, w_down):
    hid = norm_matmul(h, norm_g, w_up.astype(BF16), act="relu2", out_dtype=BF16)
    return matmul(hid, w_down.astype(BF16), residual=h)


def _nsa_shared_kv(h, kv_norm_g, w_kv, cmp_pe, cmp_w1, cmp_b1, cmp_w2):
    t = h.shape[0]
    G, D = NSA_GROUPS, NSA_HD
    kv = norm_matmul(h, kv_norm_g, w_kv.astype(BF16), out_dtype=BF16)
    x16 = kv[:, :2 * G * D].reshape(t, 2, G, D).transpose(1, 2, 0, 3).reshape(2, G, t // CMP_STRIDE, CMP_STRIDE * D)
    half = CMP_STRIDE * D
    w1cat = jnp.concatenate([cmp_w1[:, :half], cmp_w1[:, half:]], axis=-1).astype(BF16)
    cvec = jnp.einsum("plk,plkd->pd", cmp_pe, cmp_w1.reshape(2, CMP_LEN, D, D),
                      precision=lax.Precision.HIGHEST) + cmp_b1
    kvc = compress(x16, w1cat, cvec.reshape(2, 1, D), cmp_w2.astype(BF16))
    block_onehot = jnp.asarray(np.arange(t)[:, None] // SEL_BLOCK == np.arange(D)[None, :], BF16)
    return kvc[0], kvc[1], jnp.pad(kv, ((WINDOW, 0), (0, 0))), block_onehot


def _nsa_layer(h, norm_g, wq, wo, kvs, tables):
    t = h.shape[0]
    nq = NSA_HEADS * NSA_HD
    q = norm_matmul(h, norm_g, wq.astype(BF16), n=nq, scale=NSA_HD ** -0.5, out_dtype=BF16)
    gates = norm_matmul(h, norm_g, _pad_cols(wq[:, nq:], 128).astype(BF16), act="sigmoid")
    gates = gates[:, :NSA_HEADS * N_BRANCH].reshape(t, NSA_GROUPS, NSA_REP * N_BRANCH).transpose(1, 0, 2)
    o = nsa_attention(q, gates, *kvs, *tables)
    return matmul(o, wo.astype(BF16), residual=h)


def kernel(x, a_norm_g, rw_mu, rw_wr, rw_wk, rw_wv, rw_wo, rw_w0, rw_w1, rw_w2, rw_a0, rw_a1, rw_a2,
           rw_v0, rw_v1, rw_v2, rw_g1, rw_g2, rw_kk, rw_ka, rw_rk, rw_lnx_w, rw_lnx_b, kv_norm_g, w_kv,
           cmp_pe, cmp_w1, cmp_b1, cmp_w2, b_norm_g, nsa_wq, nsa_wo, rel_bias, m_norm_g, mlp_up,
           mlp_down, final_norm_g):
    b, t, d = x.shape
    assert b == 1 and d == D_MODEL and t % SEL_TILE == 0 and t // SEL_BLOCK <= 128 and WINDOW == 4 * QBLK
    h = x.reshape(t, d)
    v_first = None
    kvs = tables = None
    for layer in range(DEPTH):
        if layer < N_A_LAYERS:
            i = layer
            vp = (None, None, None) if i == 0 else (rw_v0[i - 1], rw_v1[i - 1], rw_v2[i - 1])
            h, v_first = _rwkv_layer(
                h, a_norm_g[i], rw_mu[i], rw_wr[i], rw_wk[i], rw_wv[i], rw_wo[i], rw_w0[i], rw_w1[i],
                rw_w2[i], rw_a0[i], rw_a1[i], rw_a2[i], rw_g1[i], rw_g2[i], rw_kk[i], rw_ka[i],
                rw_rk[i].reshape(1, RW_HEADS, RW_HEAD), rw_lnx_w[i], rw_lnx_b[i], v_first, *vp)
        else:
            j = layer - N_A_LAYERS
            if j == 0:
                kvs = _nsa_shared_kv(h, kv_norm_g, w_kv, cmp_pe, cmp_w1, cmp_b1, cmp_w2)
                gs3, bs, bw = _nsa_bias_tables(rel_bias)
                tables = (gs3, _overlap_table(t // CMP_STRIDE, 128), bs, bw)
            h = _nsa_layer(h, b_norm_g[j], nsa_wq[j], nsa_wo[j], kvs, tables)
        h = _mlp(h, m_norm_g[layer], mlp_up[layer], mlp_down[layer])
    return rmsnorm(h, final_norm_g, out_dtype=x.dtype).reshape(b, t, d)
```

```python
import functools
import math

import numpy as np
import jax
import jax.numpy as jnp
from jax import lax
from jax.experimental import pallas as pl
from jax.experimental.pallas import tpu as pltpu

F32 = jnp.float32
BF16 = jnp.bfloat16

D_MODEL = 2048
DEPTH = 4
N_A_LAYERS = 2
RW_HEAD = 64
RW_HEADS = D_MODEL // RW_HEAD
N_SHIFT_MIX = 6
GN_EPS = 64e-5
NSA_HEADS = 16
NSA_GROUPS = 4
NSA_REP = NSA_HEADS // NSA_GROUPS
NSA_HD = D_MODEL // NSA_HEADS
N_KV_PARTS = 6
CMP_STRIDE = 16
CMP_LEN = 32
SEL_BLOCK = 64
SEL_TOP = 16
WINDOW = 512
QBLK = 128
N_BRANCH = 3
SEL_FORCE = 1e4
NEG_INF = -1e30
REL_BUCKETS = 32
REL_MAX_DIST = 128
NORM_EPS = 1e-6

VMEM_LIMIT_BYTES = 56 * 1024 * 1024
SCAN_CHUNK = 64
SCAN_HEADS = 4
SCAN_WIDTH = SCAN_HEADS * RW_HEAD
SCAN_UNITS = 4
SEL_TILE = 512
CMP_BAND = 16
CMP_BAND_LO = 6
CMP_BIAS_K = 32
assert CMP_BAND < CMP_BIAS_K and 4 * CMP_BIAS_K == NSA_HD and NSA_HD == 128


def _cparams(sem):
    return pltpu.CompilerParams(dimension_semantics=sem, vmem_limit_bytes=VMEM_LIMIT_BYTES)


def _split3(x):
    h = x.astype(BF16)
    r = x - h.astype(F32)
    m = r.astype(BF16)
    l = (r - m.astype(F32)).astype(BF16)
    return h, m, l


def _dot(a, b):
    return jnp.dot(a, b, preferred_element_type=F32)


def _dot_nt(a, b):
    return lax.dot_general(a, b, (((1,), (1,)), ((), ())), preferred_element_type=F32)


def _dot_tn(a, b):
    return lax.dot_general(a, b, (((0,), (0,)), ((), ())), preferred_element_type=F32)


def _dot_exact_lhs(a_bf16, x):
    h, m, l = _split3(x)
    return _dot(a_bf16, h) + _dot(a_bf16, m) + _dot(a_bf16, l)


def _dot_exact_rhs(x, b_bf16):
    h, m, l = _split3(x)
    return _dot(h, b_bf16) + _dot(m, b_bf16) + _dot(l, b_bf16)


def _activate(acc, act, scale):
    if act == "relu2":
        acc = jnp.square(jnp.maximum(acc, 0.0))
    elif act == "tanh":
        acc = jnp.tanh(acc)
    elif act == "sigmoid":
        acc = jax.nn.sigmoid(acc)
    return acc if scale is None else acc * scale


def _matmul_kernel(x_ref, w_ref, *rest, act, scale, has_res, nk):
    if has_res:
        res_ref, o_ref, *scratch = rest
    else:
        o_ref, *scratch = rest

    def epilogue(acc):
        acc = _activate(acc, act, scale)
        if has_res:
            acc = acc + res_ref[...]
        o_ref[...] = acc.astype(o_ref.dtype)

    if nk == 1:
        epilogue(_dot(x_ref[...], w_ref[...]))
    else:
        acc_ref, = scratch
        k = pl.program_id(2)

        @pl.when(k == 0)
        def _():
            acc_ref[...] = jnp.zeros_like(acc_ref)

        acc_ref[...] += _dot(x_ref[...], w_ref[...])

        @pl.when(k == nk - 1)
        def _():
            epilogue(acc_ref[...])


def _pick_tile(n, pref):
    t = min(n, pref)
    while n % t:
        t //= 2
    return t


def _weight_spec(w, layer, rows, cols, index):
    if w.ndim == 2:
        return pl.BlockSpec((rows, cols), index)
    return pl.BlockSpec((None, rows, cols), lambda *g: (layer,) + tuple(index(*g)))


def matmul(x, w, *, layer=None, act=None, scale=None, residual=None, out_dtype=F32, n=None,
           tm=1024, tn=512, tk=2048):
    m, kdim = x.shape
    n = w.shape[-1] if n is None else n
    tm, tn, tk = _pick_tile(m, tm), _pick_tile(n, tn), _pick_tile(kdim, tk)
    nk = kdim // tk
    in_specs = [pl.BlockSpec((tm, tk), lambda j, i, k: (i, k)),
                _weight_spec(w, layer, tk, tn, lambda j, i, k: (k, j))]
    args = [x, w]
    if residual is not None:
        in_specs.append(pl.BlockSpec((tm, tn), lambda j, i, k: (i, j)))
        args.append(residual)
    return pl.pallas_call(
        functools.partial(_matmul_kernel, act=act, scale=scale, has_res=residual is not None, nk=nk),
        out_shape=jax.ShapeDtypeStruct((m, n), out_dtype),
        grid=(n // tn, m // tm, nk),
        in_specs=in_specs,
        out_specs=pl.BlockSpec((tm, tn), lambda j, i, k: (i, j)),
        scratch_shapes=[pltpu.VMEM((tm, tn), F32)] if nk > 1 else [],
        compiler_params=_cparams(("parallel", "parallel", "arbitrary")),
        name="matmul",
    )(*args)


def _rms(x, g):
    return x * lax.rsqrt(jnp.mean(x * x, axis=-1, keepdims=True) + NORM_EPS) * g


def _rmsnorm_kernel(h_ref, g_ref, o_ref):
    o_ref[...] = _rms(h_ref[...], g_ref[...]).astype(o_ref.dtype)


def rmsnorm(h, g, out_dtype=BF16, tm=512):
    t, d = h.shape
    tm = _pick_tile(t, tm)
    return pl.pallas_call(
        _rmsnorm_kernel,
        out_shape=jax.ShapeDtypeStruct((t, d), out_dtype),
        grid=(t // tm,),
        in_specs=[pl.BlockSpec((tm, d), lambda i: (i, 0)), pl.BlockSpec((1, d), lambda i: (0, 0))],
        out_specs=pl.BlockSpec((tm, d), lambda i: (i, 0)),
        compiler_params=_cparams(("parallel",)),
        name="rmsnorm",
    )(h, g.reshape(1, d))


def _norm_matmul_kernel(h_ref, g_ref, w_ref, o_ref, xn_ref, *, act, scale):
    @pl.when(pl.program_id(1) == 0)
    def _():
        xn_ref[...] = _rms(h_ref[...], g_ref[...]).astype(xn_ref.dtype)

    o_ref[...] = _activate(_dot(xn_ref[...], w_ref[...].astype(BF16)), act, scale).astype(o_ref.dtype)


def norm_matmul(h, g, w, *, layer=None, act=None, scale=None, out_dtype=F32, n=None, tm=1024, tn=512):
    m, kdim = h.shape
    n = w.shape[-1] if n is None else n
    tm, tn = _pick_tile(m, tm), _pick_tile(n, tn)
    return pl.pallas_call(
        functools.partial(_norm_matmul_kernel, act=act, scale=scale),
        out_shape=jax.ShapeDtypeStruct((m, n), out_dtype),
        grid=(m // tm, n // tn),
        in_specs=[pl.BlockSpec((tm, kdim), lambda i, j: (i, 0)),
                  pl.BlockSpec((1, kdim), lambda i, j: (0, 0)),
                  _weight_spec(w, layer, kdim, tn, lambda i, j: (0, j))],
        out_specs=pl.BlockSpec((tm, tn), lambda i, j: (i, j)),
        scratch_shapes=[pltpu.VMEM((tm, kdim), BF16)],
        compiler_params=_cparams(("parallel", "arbitrary")),
        name="norm_matmul",
    )(h, g.reshape(1, kdim), w)


def _rwkv_prep_kernel(h_ref, prev_ref, g_ref, mu_ref, *o_refs):
    g = g_ref[...]
    xn = _rms(h_ref[...], g)
    prev = _rms(prev_ref[...], g)[7:8, :] * (pl.program_id(0) > 0).astype(F32)
    row = lax.broadcasted_iota(jnp.int32, xn.shape, 0)
    shifted = jnp.where(row == 0, prev, pltpu.roll(xn, 1, axis=0))
    xx = shifted - xn
    for i, o_ref in enumerate(o_refs):
        o_ref[...] = (xn + xx * mu_ref[i:i + 1, :]).astype(o_ref.dtype)


def rwkv_prep(h, g, mu, tm=256):
    t, d = h.shape
    tm = _pick_tile(t, tm)
    blk = pl.BlockSpec((tm, d), lambda i: (i, 0))
    return pl.pallas_call(
        _rwkv_prep_kernel,
        out_shape=[jax.ShapeDtypeStruct((t, d), BF16)] * N_SHIFT_MIX,
        grid=(t // tm,),
        in_specs=[blk,
                  pl.BlockSpec((8, d), lambda i: (jnp.maximum(i * (tm // 8) - 1, 0), 0)),
                  pl.BlockSpec((1, d), lambda i: (0, 0)),
                  pl.BlockSpec((N_SHIFT_MIX, d), lambda i: (0, 0))],
        out_specs=[blk] * N_SHIFT_MIX,
        compiler_params=_cparams(("parallel",)),
        name="rwkv_prep",
    )(h, h, g.reshape(1, d), mu)


def _head_mask(dtype):
    row = lax.broadcasted_iota(jnp.int32, (SCAN_WIDTH, SCAN_WIDTH), 0)
    col = lax.broadcasted_iota(jnp.int32, (SCAN_WIDTH, SCAN_WIDTH), 1)
    return jnp.where((row >> 6) == (col >> 6), 1.0, 0.0).astype(dtype)


def _block_diag(x, mask):
    return jnp.concatenate([x] * SCAN_HEADS, axis=0) * mask


def _fold_rows(x_full, mask_f32):
    x = x_full * mask_f32
    L = SCAN_CHUNK
    return x[0:L] + x[L:2 * L] + x[2 * L:3 * L] + x[3 * L:4 * L]


def _split2(x):
    hi = x.astype(BF16)
    return hi, (x - hi.astype(F32)).astype(BF16)


def _bd_parts(y, mask_b):
    hi, lo = _split2(y)
    return _block_diag(hi, mask_b), _block_diag(lo, mask_b)


def _mm_heads(x, y_parts):
    y_hi, y_lo = y_parts
    n = x.shape[0]
    x_hi, x_lo = _split2(x)
    both = _dot(jnp.concatenate([x_hi, x_lo], axis=0), y_hi)
    return both[:n] + both[n:] + _dot(x_hi, y_lo)


def _run_interleaved(stage_generators):
    live = list(stage_generators)
    while live:
        still = []
        for gen in live:
            try:
                next(gen)
                still.append(gen)
            except StopIteration:
                pass
        live = still


def _tri_inverse_heads(a, j_row, m_col, mask_b):
    L = SCAN_CHUNK
    eye = jnp.where(m_col == j_row, 1.0, 0.0)
    d = jnp.where((j_row >> 4) == (m_col >> 4), a, 0.0)
    e = a - d
    p = eye + d
    x = _mm_heads(d, _bd_parts(d, mask_b))
    yield
    for _ in range(2):
        both = _mm_heads(jnp.concatenate([x, p], axis=0), _bd_parts(x, mask_b))
        x, p = both[:L], p + both[L:]
        yield
    p = p + _mm_heads(p, _bd_parts(x, mask_b))
    yield
    n1 = _mm_heads(p, _bd_parts(e, mask_b))
    yield
    n2 = _mm_heads(n1, _bd_parts(n1, mask_b))
    yield
    q = eye + n1
    q = q + _mm_heads(q, _bd_parts(n2, mask_b))
    yield
    return _mm_heads(q, _bd_parts(p, mask_b))


def _softplus(x):
    return jnp.maximum(x, 0.0) + jnp.log(1.0 + jnp.exp(-jnp.abs(x)))


def _rwkv_chunk_kernel(*refs, mix_v):
    if mix_v:
        r_ref, k_ref, v_ref, wl_ref, al_ref, vl_ref, vf_ref, par_ref, m_ref, c_ref, qp_ref, y0_ref, bo_ref = refs
    else:
        r_ref, k_ref, v_ref, wl_ref, al_ref, par_ref, m_ref, c_ref, qp_ref, y0_ref, bo_ref = refs
    L, W = SCAN_CHUNK, SCAN_WIDTH
    mask_b = _head_mask(BF16)
    mask_f = _head_mask(F32)
    tri_incl = jnp.where(lax.broadcasted_iota(jnp.int32, (L, L), 1)
                         <= lax.broadcasted_iota(jnp.int32, (L, L), 0), 1.0, 0.0).astype(BF16)
    j_row = lax.broadcasted_iota(jnp.int32, (L, W), 0)
    m_col = lax.broadcasted_iota(jnp.int32, (L, W), 1) & (L - 1)
    strict = m_col < j_row
    incl = m_col <= j_row

    def unit(u):
        cols = slice(u * W, (u + 1) * W)
        par = par_ref[:, cols]
        w0, a0, v0, k_k, k_a, r_k = (par[i:i + 1] for i in range(6))
        r, k, v = r_ref[:, cols], k_ref[:, cols], v_ref[:, cols]
        lw = -jnp.exp(-_softplus(-(w0 + wl_ref[:, cols])) - 0.5)
        a_s = jax.nn.sigmoid(a0 + al_ref[:, cols])
        if mix_v:
            v = v + (vf_ref[:, cols] - v) * jax.nn.sigmoid(v0 + vl_ref[:, cols])
        kk = k * k_k
        k = k * (1.0 + (a_s - 1.0) * k_a)
        kk_sq = _dot_exact_rhs(kk * kk, mask_b)
        rk_sum = _dot_exact_rhs(r * k * r_k, mask_b)
        c = _dot_exact_lhs(tri_incl, lw)
        yield
        kk = kk * lax.rsqrt(jnp.maximum(kk_sq, 1e-24))
        bo_ref[:, cols] = rk_sum * v
        c_last = c[L - 1:L, :]
        p_inv = jnp.exp(-c)
        p_rest = jnp.exp(c_last - c)
        rt = r * jnp.exp(c)
        at = (-kk * jnp.exp(c - lw)).astype(BF16)
        b = kk * a_s
        vb = v.astype(BF16)
        ar = jnp.concatenate([at, rt.astype(BF16)], axis=0)
        g_b = _dot_nt(ar, _block_diag((b * p_inv).astype(BF16), mask_b))
        g_k = _dot_nt(ar, _block_diag((k * p_inv).astype(BF16), mask_b))
        yield
        a_ab = jnp.where(strict, g_b[:L], 0.0)
        a_ak = jnp.where(strict, g_k[:L], 0.0).astype(BF16)
        a_rb = jnp.where(incl, g_b[L:], 0.0).astype(BF16)
        a_rk = jnp.where(incl, g_k[L:], 0.0).astype(BF16)
        v_bd = _block_diag(vb, mask_b)
        akv = _dot(a_ak, v_bd)
        t_inv = yield from _tri_inverse_heads(a_ab, j_row, m_col, mask_b)
        yield
        t_hi, t_lo = _split2(t_inv)
        w_both = _dot(jnp.concatenate([t_hi, t_lo], axis=0), _block_diag(at, mask_b))
        u0 = _mm_heads(t_inv, _bd_parts(akv, mask_b)).astype(BF16)
        yield
        w = (w_both[:L] + w_both[L:]).astype(BF16)
        qp_ref[:, cols] = (rt + _dot(a_rb, _block_diag(w, mask_b))).astype(qp_ref.dtype)
        y0_ref[:, cols] = _dot(a_rb, _block_diag(u0, mask_b)) + _dot(a_rk, v_bd)
        bh = (b * p_rest).astype(BF16)
        kh = (k * p_rest).astype(BF16)
        decay_diag = jnp.where(m_col == j_row, jnp.exp(c_last), 0.0)
        m_ref[:, cols] = _fold_rows(_dot_tn(bh, w), mask_f) + decay_diag
        c_ref[:, cols] = _fold_rows(_dot_tn(bh, u0) + _dot_tn(kh, vb), mask_f)

    _run_interleaved(unit(u) for u in range(SCAN_UNITS))


def rwkv_chunks(r, k, v, w_lin, a_lin, v_lin, v_first, params):
    t, d = r.shape
    bw = SCAN_UNITS * SCAN_WIDTH
    blk = pl.BlockSpec((SCAN_CHUNK, bw), lambda ci, gi: (ci, gi))
    mix_v = v_lin is not None
    args = [r, k, v, w_lin, a_lin] + ([v_lin, v_first] if mix_v else []) + [params]
    f32 = jax.ShapeDtypeStruct((t, d), F32)
    return pl.pallas_call(
        functools.partial(_rwkv_chunk_kernel, mix_v=mix_v),
        out_shape=[f32, f32, jax.ShapeDtypeStruct((t, d), BF16), f32, f32],
        grid=(t // SCAN_CHUNK, d // bw),
        in_specs=[blk] * (len(args) - 1) + [pl.BlockSpec((8, bw), lambda ci, gi: (0, gi))],
        out_specs=[blk] * 5,
        compiler_params=_cparams(("parallel", "parallel")),
        name="rwkv_chunks",
    )(*args)


def _rwkv_state_kernel(m_ref, c_ref, qp_ref, y0_ref, bo_ref, g_ref, ln_ref, o_ref, st_ref):
    @pl.when(pl.program_id(1) == 0)
    def _():
        st_ref[...] = jnp.zeros_like(st_ref)

    W = SCAN_WIDTH
    mask_b = _head_mask(BF16)
    def unit(u):
        cols = slice(u * W, (u + 1) * W)
        st_parts = _bd_parts(st_ref[u], mask_b)
        y = _dot(qp_ref[:, cols], st_parts[0]) + y0_ref[:, cols]
        st_ref[u] = _mm_heads(m_ref[:, cols], st_parts) + c_ref[:, cols]
        yield
        mean = _dot_exact_rhs(y, mask_b) * (1.0 / RW_HEAD)
        yield
        dev = y - mean
        var = _dot_exact_rhs(dev * dev, mask_b) * (1.0 / RW_HEAD)
        yield
        ln = ln_ref[:, cols]
        out = (dev * lax.rsqrt(var + GN_EPS) * ln[0:1] + ln[1:2] + bo_ref[:, cols]) * g_ref[:, cols]
        o_ref[:, cols] = out.astype(o_ref.dtype)

    _run_interleaved(unit(u) for u in range(SCAN_UNITS))


def rwkv_state_scan(m, c, qp, y0, bonus, g, ln_params):
    t, d = m.shape
    bw = SCAN_UNITS * SCAN_WIDTH
    blk = pl.BlockSpec((SCAN_CHUNK, bw), lambda gi, ci: (ci, gi))
    return pl.pallas_call(
        _rwkv_state_kernel,
        out_shape=jax.ShapeDtypeStruct((t, d), BF16),
        grid=(d // bw, t // SCAN_CHUNK),
        in_specs=[blk] * 6 + [pl.BlockSpec((8, bw), lambda gi, ci: (0, gi))],
        out_specs=blk,
        scratch_shapes=[pltpu.VMEM((SCAN_UNITS, RW_HEAD, SCAN_WIDTH), F32)],
        compiler_params=_cparams(("parallel", "arbitrary")),
        name="rwkv_state_scan",
    )(m, c, qp, y0, bonus, g, ln_params)


def _compress_kernel(x_ref, w1_ref, c_ref, w2_ref, o_ref):
    ab = _dot(x_ref[...], w1_ref[...])
    a = ab[:, :NSA_HD]
    bsh = ab[:, NSA_HD:]
    n16 = a.shape[0]
    row = lax.broadcasted_iota(jnp.int32, a.shape, 0)
    b_next = jnp.where(row == n16 - 1, 0.0, pltpu.roll(bsh, n16 - 1, axis=0))
    hid = a + b_next + c_ref[...]
    hid = jax.nn.gelu(hid, approximate=True)
    o_ref[...] = _dot(hid.astype(BF16), w2_ref[...]).astype(o_ref.dtype)


def compress(x16, w1cat, cvec, w2):
    _, g, n16, kd = x16.shape
    d = NSA_HD
    return pl.pallas_call(
        _compress_kernel,
        out_shape=jax.ShapeDtypeStruct((2, g, n16, d), BF16),
        grid=(2, g),
        in_specs=[pl.BlockSpec((None, None, n16, kd), lambda p, gi: (p, gi, 0, 0)),
                  pl.BlockSpec((None, kd, 2 * d), lambda p, gi: (p, 0, 0)),
                  pl.BlockSpec((None, 1, d), lambda p, gi: (p, 0, 0)),
                  pl.BlockSpec((None, d, d), lambda p, gi: (p, 0, 0))],
        out_specs=pl.BlockSpec((None, None, n16, d), lambda p, gi: (p, gi, 0, 0)),
        compiler_params=_cparams(("parallel", "parallel")),
        name="nsa_compress",
    )(x16, w1cat, cvec, w2)


def _softmax_rows(s):
    mx = jnp.max(s, axis=-1, keepdims=True)
    e = jnp.exp(s - mx)
    return e / jnp.sum(e, axis=-1, keepdims=True)


def _nsa_kernel(q_ref, gate_ref, kc_ref, vc_ref, ks_ref, vs_ref, kw_ref, vw_ref, oh_ref,
                gs_ref, ov_ref, bs_ref, bw_ref, o_ref, s_buf, p_buf):
    qi = pl.program_id(1)
    R, Q, D = NSA_REP, QBLK, NSA_HD
    q = q_ref[...]
    qs = jnp.concatenate([q[:, r * D:(r + 1) * D] for r in range(R)], axis=0)
    n_cp = kc_ref.shape[0]
    n_sb = ov_ref.shape[1]
    out = {}

    def compressed_and_topk():
        t_rows = qi * Q + (lax.broadcasted_iota(jnp.int32, (R * Q, 1), 0) & (Q - 1))
        c_idx = lax.broadcasted_iota(jnp.int32, (1, n_cp), 1)
        m_off = qi * (Q // CMP_STRIDE) - lax.broadcasted_iota(jnp.int32, (n_cp, D), 0) + CMP_BAND_LO
        k_idx = lax.broadcasted_iota(jnp.int32, (n_cp, D), 1) & (CMP_BIAS_K - 1)
        in_band = ((k_idx < CMP_BAND) & (m_off == k_idx)) | ((k_idx == CMP_BAND) & (m_off >= CMP_BAND))
        k_c = jnp.concatenate([kc_ref[...], jnp.where(in_band, 1.0, 0.0).astype(BF16)], axis=1)
        qk_c = _dot_nt(jnp.concatenate([qs, gs_ref[...]], axis=1), k_c)
        yield
        m_c = (c_idx * CMP_STRIDE + (CMP_LEN - 1)) <= t_rows
        s_c = jnp.where(m_c, qk_c, NEG_INF)
        p_c = jnp.where(m_c, _softmax_rows(s_c), 0.0)
        out["o_c"] = _dot(p_c.astype(BF16), vc_ref[...])
        p_sum = p_c[0:Q] + p_c[Q:2 * Q] + p_c[2 * Q:3 * Q] + p_c[3 * Q:4 * Q]
        imp = _dot_exact_rhs(p_sum, ov_ref[...])
        yield
        imp_t = imp.T
        s_row = lax.broadcasted_iota(jnp.int32, (n_sb, Q), 0)
        t_col = qi * Q + lax.broadcasted_iota(jnp.int32, (n_sb, Q), 1)
        forced = (s_row == (t_col >> 6)) | (s_row == 0)
        score = jnp.where(forced, SEL_FORCE, jnp.where(s_row * SEL_BLOCK <= t_col, imp_t, -SEL_FORCE))
        sel_t = jnp.zeros((n_sb, Q), F32)
        s_f = s_row.astype(F32)
        for _ in range(SEL_TOP):
            mx = jnp.max(score, axis=0, keepdims=True)
            first = jnp.min(jnp.where(score == mx, s_f, float(n_sb)), axis=0, keepdims=True)
            pick = s_f == first
            sel_t = jnp.where(pick, 1.0, sel_t)
            score = jnp.where(pick, -3e38, score)
        out["sel"] = sel_t.T

    def window():
        w0 = pl.multiple_of(qi * Q, Q)
        qk_w = _dot_nt(qs, kw_ref[pl.ds(w0, WINDOW + Q), :])
        yield
        j_io = lax.broadcasted_iota(jnp.int32, (1, WINDOW + Q), 1)
        s_w = jnp.where(j_io >= WINDOW - qi * Q, qk_w + bw_ref[...], NEG_INF)
        out["o_w"] = _dot(_softmax_rows(s_w).astype(BF16), vw_ref[pl.ds(w0, WINDOW + Q), :])

    _run_interleaved([compressed_and_topk(), window()])
    o_c, o_w, sel = out["o_c"], out["o_w"], out["sel"]

    n_far = jnp.maximum(qi - 1, 0)
    blk = lax.broadcasted_iota(jnp.int32, (Q, n_sb), 1)
    neg_near = jnp.where(sel > 0.5, 0.0, NEG_INF).astype(BF16)
    neg_far = jnp.where((sel > 0.5) & (blk < 2 * n_far), 0.0, NEG_INF).astype(BF16)
    q_near = jnp.concatenate([qs, jnp.concatenate([neg_near] * R, axis=0)], axis=1)
    q_far = jnp.concatenate([qs, jnp.concatenate([neg_far] * R, axis=0)], axis=1)

    nb0 = jnp.maximum(qi - 1, 0)
    n0 = pl.multiple_of(nb0 * Q, Q)

    def sel_keys(start, n):
        return jnp.concatenate([ks_ref[pl.ds(WINDOW + start, n), :], oh_ref[pl.ds(start, n), :]], axis=1)

    s = _dot_nt(q_near, sel_keys(n0, 2 * Q)) + bs_ref[jnp.where(qi == 0, 1, 0)]
    m_0 = jnp.max(s, axis=-1, keepdims=True)
    p = jnp.exp(s - m_0)
    l_0 = jnp.sum(p, axis=-1, keepdims=True)
    acc_0 = _dot(p.astype(BF16), vs_ref[pl.ds(WINDOW + n0, 2 * Q), :])

    n_tiles = (n_far + (SEL_TILE // Q - 1)) // (SEL_TILE // Q)
    n_pairs = (n_tiles + 1) >> 1
    max_tile = oh_ref.shape[0] // SEL_TILE - 1

    def key_tile(i):
        return sel_keys(pl.multiple_of(i * SEL_TILE, SEL_TILE), SEL_TILE)

    def value_tile(i):
        return vs_ref[pl.ds(pl.multiple_of(WINDOW + i * SEL_TILE, SEL_TILE), SEL_TILE), :]

    s_buf[0] = _dot_nt(q_far, key_tile(0))
    p_buf[1] = jnp.zeros(p_buf.shape[1:], p_buf.dtype)

    def half_step(i, slot, carry):
        m_run, l_run, acc, alpha_prev = carry
        s_buf[1 - slot] = _dot_nt(q_far, key_tile(jnp.minimum(i + 1, max_tile)))
        acc = alpha_prev * acc + _dot(p_buf[1 - slot], value_tile(jnp.maximum(i - 1, 0)))
        s_i = s_buf[slot]
        m_new = jnp.maximum(m_run, jnp.max(s_i, axis=-1, keepdims=True))
        alpha = jnp.exp(m_run - m_new)
        p_i = jnp.exp(s_i - m_new)
        p_buf[slot] = p_i.astype(p_buf.dtype)
        return m_new, alpha * l_run + jnp.sum(p_i, axis=-1, keepdims=True), acc, alpha

    def pair_body(j, carry):
        return half_step(2 * j + 1, 1, half_step(2 * j, 0, carry))

    _, l_s, acc_s, alpha_last = lax.fori_loop(0, n_pairs, pair_body, (m_0, l_0, acc_0, jnp.ones_like(l_0)))
    acc_s = alpha_last * acc_s + _dot(p_buf[1], value_tile(jnp.maximum(2 * n_pairs - 1, 0)))
    o_s = acc_s / l_s

    gates = gate_ref[...]
    for r in range(R):
        rows = slice(r * Q, (r + 1) * Q)
        o = (gates[:, 3 * r:3 * r + 1] * o_c[rows] + gates[:, 3 * r + 1:3 * r + 2] * o_s[rows]
             + gates[:, 3 * r + 2:3 * r + 3] * o_w[rows])
        o_ref[:, r * D:(r + 1) * D] = o.astype(o_ref.dtype)


def nsa_attention(q, gates, kc, vc, kv_padded, block_onehot, gs, ov, bs, bw):
    t = q.shape[0]
    g, n_cp, d = kc.shape
    rq = NSA_REP * QBLK
    per_g = lambda shape: pl.BlockSpec((None,) + shape, lambda gi, qi: (gi,) + (0,) * len(shape))
    kv_part = lambda part: pl.BlockSpec((t + WINDOW, d), lambda gi, qi: (0, part * g + gi))
    return pl.pallas_call(
        _nsa_kernel,
        out_shape=jax.ShapeDtypeStruct((t, NSA_HEADS * d), BF16),
        grid=(g, t // QBLK),
        in_specs=[pl.BlockSpec((QBLK, NSA_REP * d), lambda gi, qi: (qi, gi)),
                  pl.BlockSpec((None, QBLK, NSA_REP * N_BRANCH), lambda gi, qi: (gi, qi, 0)),
                  per_g((n_cp, d)), per_g((n_cp, d)),
                  kv_part(2), kv_part(3), kv_part(4), kv_part(5),
                  pl.BlockSpec((t, d), lambda gi, qi: (0, 0)),
                  per_g((rq, d)),
                  pl.BlockSpec(ov.shape, lambda gi, qi: (0, 0)),
                  per_g((2, rq, 2 * QBLK)),
                  per_g((rq, WINDOW + QBLK))],
        out_specs=pl.BlockSpec((QBLK, NSA_REP * d), lambda gi, qi: (qi, gi)),
        scratch_shapes=[pltpu.VMEM((2, rq, SEL_TILE), F32), pltpu.VMEM((2, rq, SEL_TILE), BF16)],
        compiler_params=_cparams(("parallel", "arbitrary")),
        name="nsa_attention",
    )(q, gates, kc, vc, kv_padded, kv_padded, kv_padded, kv_padded, block_onehot, gs, ov, bs, bw)


def _t5_bucket_np(n):
    n = np.maximum(n, 0)
    max_exact = REL_BUCKETS // 2
    nf = np.maximum(n, 1).astype(np.float64)
    large = max_exact + (np.log(nf / max_exact) / math.log(REL_MAX_DIST / max_exact)
                         * (REL_BUCKETS - max_exact)).astype(np.int64)
    large = np.minimum(large, REL_BUCKETS - 1)
    return np.where(n < max_exact, n, large).astype(np.int32)


def _nsa_bias_tables(rel_bias):
    G, R, Q = NSA_GROUPS, NSA_REP, QBLK
    table = rel_bias.astype(F32).reshape(REL_BUCKETS, G, R).transpose(1, 2, 0)
    qq = np.arange(Q)

    def lookup(dist):
        return table[:, :, _t5_bucket_np(dist)].reshape(G, R * Q, dist.shape[1])

    m = np.arange(CMP_BAND) - CMP_BAND_LO
    gs = lookup(qq[:, None] - (CMP_LEN - 1) + CMP_STRIDE * m[None, :])
    far = jnp.broadcast_to(table[:, :, REL_BUCKETS - 1][:, :, None, None], (G, R, Q, 1)).reshape(G, R * Q, 1)
    gs = jnp.concatenate([gs, far, jnp.zeros((G, R * Q, CMP_BIAS_K - CMP_BAND - 1), F32)], axis=-1)
    h, mid, l = _split3(gs)
    gs3 = jnp.concatenate([h, mid, l, jnp.zeros_like(h)], axis=-1)
    def toeplitz(c):
        mm = np.arange(2 * Q)
        v = table[:, :, _t5_bucket_np(np.where(mm < Q, c - mm, c + 2 * Q - mm))]
        rows = jnp.tile(v, (1, 1, Q))[:, :, :Q * (2 * Q - 1)].reshape(G, R, Q, 2 * Q - 1)
        return rows[:, :, :, :Q].reshape(G, R * Q, Q)

    d0 = np.tile(qq[:, None] - qq[None, :], (R, 1))
    causal = jnp.asarray(np.where(d0 >= 0, 0.0, NEG_INF), F32)[None]
    b0 = toeplitz(0) + causal
    b1 = toeplitz(Q)
    far_c = jnp.broadcast_to(far, (G, R * Q, Q))
    neg = jnp.full((G, R * Q, Q), NEG_INF, F32)
    bs = jnp.stack([jnp.concatenate([b1 - far_c, b0 - far_c], -1),
                    jnp.concatenate([b0 - far_c, neg], -1)], axis=1)
    oldest = far_c + jnp.asarray(np.where(d0 < 0, 0.0, NEG_INF), F32)[None]
    bw = jnp.concatenate([oldest, far_c, far_c, b1, b0], -1)
    return gs3, bs, bw


def _overlap_table(n_cp, n_sb):
    c = np.arange(n_cp)[:, None] * CMP_STRIDE
    s = np.arange(n_sb)[None, :] * SEL_BLOCK
    return jnp.asarray(((c <= s + SEL_BLOCK - 1) & (c + CMP_LEN - 1 >= s)), BF16)


def _pad_cols(w, n):
    return jnp.pad(w, ((0, 0), (0, n - w.shape[1])))


def _pad_rows(w, n):
    return jnp.pad(w, ((0, n - w.shape[0]), (0, 0)))


def _rwkv_layer(h, layer, norm_g, mu, wr, wk, wv, wo, w0, w1, w2, a0, a1, a2, g1, g2,
                k_k, k_a, r_k, lnx_w, lnx_b, v_first, v0, v1, v2):
    bf = lambda w: w.astype(BF16)
    xr, xw, xk, xv, xa, xg = rwkv_prep(h, norm_g, mu)
    r = matmul(xr, wr, layer=layer)
    k = matmul(xk, wk, layer=layer)
    v = matmul(xv, wv, layer=layer)
    lw_hid = matmul(xw, bf(_pad_cols(w1, 128)), act="tanh", out_dtype=BF16)
    w_lin = matmul(lw_hid, bf(_pad_rows(w2, 128)))
    a_lin = matmul(matmul(xa, bf(a1), out_dtype=BF16), bf(a2))
    g = matmul(matmul(xg, bf(g1), act="sigmoid", out_dtype=BF16), bf(g2))
    zeros = jnp.zeros_like(w0)
    if v_first is None:
        v_lin, v0 = None, zeros
    else:
        v_lin = matmul(matmul(xv, bf(_pad_cols(v1, 128)), out_dtype=BF16), bf(_pad_rows(v2, 128)))
    params = jnp.stack([w0, a0, v0, k_k, k_a, r_k.reshape(-1), zeros, zeros])
    m, c, qp, y0, bonus = rwkv_chunks(r, k, v, w_lin, a_lin, v_lin, v_first, params)
    ln_params = jnp.stack([lnx_w, lnx_b] + [zeros] * 6)
    gated = rwkv_state_scan(m, c, qp, y0, bonus, g, ln_params)
    h = matmul(gated, wo, layer=layer, residual=h)
    return h, (v if v_first is None else v_first)


def _mlp(h, layer, norm_g, w_up, w_down):
    hid = norm_matmul(h, norm_g, w_up, layer=layer, act="relu2", out_dtype=BF16)
    return matmul(hid, w_down, layer=layer, residual=h)


def _nsa_shared_kv(h, kv_norm_g, w_kv, cmp_pe, cmp_w1, cmp_b1, cmp_w2):
    t = h.shape[0]
    G, D = NSA_GROUPS, NSA_HD
    kv = norm_matmul(h, kv_norm_g, w_kv.astype(BF16), out_dtype=BF16)
    x16 = kv[:, :2 * G * D].reshape(t, 2, G, D).transpose(1, 2, 0, 3).reshape(2, G, t // CMP_STRIDE, CMP_STRIDE * D)
    half = CMP_STRIDE * D
    w1cat = jnp.concatenate([cmp_w1[:, :half], cmp_w1[:, half:]], axis=-1).astype(BF16)
    cvec = jnp.einsum("plk,plkd->pd", cmp_pe, cmp_w1.reshape(2, CMP_LEN, D, D),
                      precision=lax.Precision.HIGHEST) + cmp_b1
    kvc = compress(x16, w1cat, cvec.reshape(2, 1, D), cmp_w2.astype(BF16))
    block_onehot = jnp.asarray(np.arange(t)[:, None] // SEL_BLOCK == np.arange(D)[None, :], BF16)
    return kvc[0], kvc[1], jnp.pad(kv, ((WINDOW, 0), (0, 0))), block_onehot


def _nsa_layer(h, layer, norm_g, wq, wo, kvs, tables):
    t = h.shape[0]
    nq = NSA_HEADS * NSA_HD
    q = norm_matmul(h, norm_g, wq, layer=layer, n=nq, scale=NSA_HD ** -0.5, out_dtype=BF16)
    gates = norm_matmul(h, norm_g, _pad_cols(wq[layer, :, nq:], 128), act="sigmoid")
    gates = gates[:, :NSA_HEADS * N_BRANCH].reshape(t, NSA_GROUPS, NSA_REP * N_BRANCH).transpose(1, 0, 2)
    o = nsa_attention(q, gates, *kvs, *tables)
    return matmul(o, wo, layer=layer, residual=h)


def kernel(x, a_norm_g, rw_mu, rw_wr, rw_wk, rw_wv, rw_wo, rw_w0, rw_w1, rw_w2, rw_a0, rw_a1, rw_a2,
           rw_v0, rw_v1, rw_v2, rw_g1, rw_g2, rw_kk, rw_ka, rw_rk, rw_lnx_w, rw_lnx_b, kv_norm_g, w_kv,
           cmp_pe, cmp_w1, cmp_b1, cmp_w2, b_norm_g, nsa_wq, nsa_wo, rel_bias, m_norm_g, mlp_up,
           mlp_down, final_norm_g):
    b, t, d = x.shape
    assert b == 1 and d == D_MODEL and t % SEL_TILE == 0 and t // SEL_BLOCK <= 128 and WINDOW == 4 * QBLK
    h = x.reshape(t, d)
    v_first = None
    kvs = tables = None
    rw_w = [w.astype(BF16) for w in (rw_wr, rw_wk, rw_wv, rw_wo)]
    nsa_wq_b, nsa_wo_b, mlp_down_b = nsa_wq.astype(BF16), nsa_wo.astype(BF16), mlp_down.astype(BF16)
    for layer in range(DEPTH):
        if layer < N_A_LAYERS:
            i = layer
            vp = (None, None, None) if i == 0 else (rw_v0[i - 1], rw_v1[i - 1], rw_v2[i - 1])
            h, v_first = _rwkv_layer(
                h, i, a_norm_g[i], rw_mu[i], *rw_w, rw_w0[i], rw_w1[i],
                rw_w2[i], rw_a0[i], rw_a1[i], rw_a2[i], rw_g1[i], rw_g2[i], rw_kk[i], rw_ka[i],
                rw_rk[i].reshape(1, RW_HEADS, RW_HEAD), rw_lnx_w[i], rw_lnx_b[i], v_first, *vp)
        else:
            j = layer - N_A_LAYERS
            if j == 0:
                kvs = _nsa_shared_kv(h, kv_norm_g, w_kv, cmp_pe, cmp_w1, cmp_b1, cmp_w2)
                gs3, bs, bw = _nsa_bias_tables(rel_bias)
                tables = (gs3, _overlap_table(t // CMP_STRIDE, 128), bs, bw)
            h = _nsa_layer(h, j, b_norm_g[j], nsa_wq_b, nsa_wo_b, kvs, tables)
        h = _mlp(h, layer, m_norm_g[layer], mlp_up, mlp_down_b)
    return rmsnorm(h, final_norm_g, out_dtype=x.dtype).reshape(b, t, d)
```

```python
import functools
import math

import numpy as np
import jax
import jax.numpy as jnp
from jax import lax
from jax.experimental import pallas as pl
from jax.experimental.pallas import tpu as pltpu

F32 = jnp.float32
BF16 = jnp.bfloat16

D_MODEL = 2048
DEPTH = 4
N_A_LAYERS = 2
RW_HEAD = 64
RW_HEADS = D_MODEL // RW_HEAD
N_SHIFT_MIX = 6
GN_EPS = 64e-5
NSA_HEADS = 16
NSA_GROUPS = 4
NSA_REP = NSA_HEADS // NSA_GROUPS
NSA_HD = D_MODEL // NSA_HEADS
N_KV_PARTS = 6
CMP_STRIDE = 16
CMP_LEN = 32
SEL_BLOCK = 64
SEL_TOP = 16
WINDOW = 512
QBLK = 128
N_BRANCH = 3
SEL_FORCE = 1e4
NEG_INF = -1e30
REL_BUCKETS = 32
REL_MAX_DIST = 128
NORM_EPS = 1e-6
LOG2E = math.log2(math.e)

VMEM_LIMIT_BYTES = 56 * 1024 * 1024
SCAN_CHUNK = 64
SCAN_HEADS = 4
SCAN_WIDTH = SCAN_HEADS * RW_HEAD
SCAN_UNITS = 4
SEL_TILE = 512
CMP_BAND = 16
CMP_BAND_LO = 6
CMP_BIAS_K = 32
assert CMP_BAND < CMP_BIAS_K and 4 * CMP_BIAS_K == NSA_HD and NSA_HD == 128


def _cparams(sem):
    return pltpu.CompilerParams(dimension_semantics=sem, vmem_limit_bytes=VMEM_LIMIT_BYTES)


def _split3(x):
    h = x.astype(BF16)
    r = x - h.astype(F32)
    m = r.astype(BF16)
    l = (r - m.astype(F32)).astype(BF16)
    return h, m, l


def _dot(a, b):
    return jnp.dot(a, b, preferred_element_type=F32)


def _dot_nt(a, b):
    return lax.dot_general(a, b, (((1,), (1,)), ((), ())), preferred_element_type=F32)


def _dot_tn(a, b):
    return lax.dot_general(a, b, (((0,), (0,)), ((), ())), preferred_element_type=F32)


def _dot_exact_lhs(a_bf16, x):
    h, m, l = _split3(x)
    return _dot(a_bf16, h) + _dot(a_bf16, m) + _dot(a_bf16, l)


def _dot_exact_rhs(x, b_bf16):
    h, m, l = _split3(x)
    return _dot(h, b_bf16) + _dot(m, b_bf16) + _dot(l, b_bf16)


def _activate(acc, act, scale):
    if act == "relu2":
        acc = jnp.square(jnp.maximum(acc, 0.0))
    elif act == "tanh":
        acc = jnp.tanh(acc)
    elif act == "sigmoid":
        acc = jax.nn.sigmoid(acc)
    return acc if scale is None else acc * scale


def _matmul_kernel(x_ref, w_ref, *rest, act, scale, has_res, nk):
    if has_res:
        res_ref, o_ref, *scratch = rest
    else:
        o_ref, *scratch = rest

    def epilogue(acc):
        acc = _activate(acc, act, scale)
        if has_res:
            acc = acc + res_ref[...]
        o_ref[...] = acc.astype(o_ref.dtype)

    if nk == 1:
        epilogue(_dot(x_ref[...], w_ref[...]))
    else:
        acc_ref, = scratch
        k = pl.program_id(2)

        @pl.when(k == 0)
        def _():
            acc_ref[...] = jnp.zeros_like(acc_ref)

        acc_ref[...] += _dot(x_ref[...], w_ref[...])

        @pl.when(k == nk - 1)
        def _():
            epilogue(acc_ref[...])


def _pick_tile(n, pref):
    t = min(n, pref)
    while n % t:
        t //= 2
    return t


def _weight_spec(w, layer, rows, cols, index):
    if w.ndim == 2:
        return pl.BlockSpec((rows, cols), index)
    return pl.BlockSpec((None, rows, cols), lambda *g: (layer,) + tuple(index(*g)))


def matmul(x, w, *, layer=None, act=None, scale=None, residual=None, out_dtype=F32, n=None,
           tm=1024, tn=512, tk=2048):
    m, kdim = x.shape
    n = w.shape[-1] if n is None else n
    tm, tn, tk = _pick_tile(m, tm), _pick_tile(n, tn), _pick_tile(kdim, tk)
    nk = kdim // tk
    in_specs = [pl.BlockSpec((tm, tk), lambda j, i, k: (i, k)),
                _weight_spec(w, layer, tk, tn, lambda j, i, k: (k, j))]
    args = [x, w]
    if residual is not None:
        in_specs.append(pl.BlockSpec((tm, tn), lambda j, i, k: (i, j)))
        args.append(residual)
    return pl.pallas_call(
        functools.partial(_matmul_kernel, act=act, scale=scale, has_res=residual is not None, nk=nk),
        out_shape=jax.ShapeDtypeStruct((m, n), out_dtype),
        grid=(n // tn, m // tm, nk),
        in_specs=in_specs,
        out_specs=pl.BlockSpec((tm, tn), lambda j, i, k: (i, j)),
        scratch_shapes=[pltpu.VMEM((tm, tn), F32)] if nk > 1 else [],
        compiler_params=_cparams(("parallel", "parallel", "arbitrary")),
        name="matmul",
    )(*args)


def _rms(x, g):
    return x * lax.rsqrt(jnp.mean(x * x, axis=-1, keepdims=True) + NORM_EPS) * g


def _rmsnorm_kernel(h_ref, g_ref, o_ref):
    o_ref[...] = _rms(h_ref[...], g_ref[...]).astype(o_ref.dtype)


def rmsnorm(h, g, out_dtype=BF16, tm=512):
    t, d = h.shape
    tm = _pick_tile(t, tm)
    return pl.pallas_call(
        _rmsnorm_kernel,
        out_shape=jax.ShapeDtypeStruct((t, d), out_dtype),
        grid=(t // tm,),
        in_specs=[pl.BlockSpec((tm, d), lambda i: (i, 0)), pl.BlockSpec((1, d), lambda i: (0, 0))],
        out_specs=pl.BlockSpec((tm, d), lambda i: (i, 0)),
        compiler_params=_cparams(("parallel",)),
        name="rmsnorm",
    )(h, g.reshape(1, d))


def _norm_matmul_kernel(h_ref, g_ref, w_ref, o_ref, xn_ref, *, act, scale):
    @pl.when(pl.program_id(1) == 0)
    def _():
        xn_ref[...] = _rms(h_ref[...], g_ref[...]).astype(xn_ref.dtype)

    o_ref[...] = _activate(_dot(xn_ref[...], w_ref[...].astype(BF16)), act, scale).astype(o_ref.dtype)


def norm_matmul(h, g, w, *, layer=None, act=None, scale=None, out_dtype=F32, n=None, tm=1024, tn=512):
    m, kdim = h.shape
    n = w.shape[-1] if n is None else n
    tm, tn = _pick_tile(m, tm), _pick_tile(n, tn)
    return pl.pallas_call(
        functools.partial(_norm_matmul_kernel, act=act, scale=scale),
        out_shape=jax.ShapeDtypeStruct((m, n), out_dtype),
        grid=(m // tm, n // tn),
        in_specs=[pl.BlockSpec((tm, kdim), lambda i, j: (i, 0)),
                  pl.BlockSpec((1, kdim), lambda i, j: (0, 0)),
                  _weight_spec(w, layer, kdim, tn, lambda i, j: (0, j))],
        out_specs=pl.BlockSpec((tm, tn), lambda i, j: (i, j)),
        scratch_shapes=[pltpu.VMEM((tm, kdim), BF16)],
        compiler_params=_cparams(("parallel", "arbitrary")),
        name="norm_matmul",
    )(h, g.reshape(1, kdim), w)


def _rwkv_prep_kernel(h_ref, prev_ref, g_ref, mu_ref, *o_refs):
    g = g_ref[...]
    xn = _rms(h_ref[...], g)
    prev = _rms(prev_ref[...], g)[7:8, :] * (pl.program_id(0) > 0).astype(F32)
    row = lax.broadcasted_iota(jnp.int32, xn.shape, 0)
    shifted = jnp.where(row == 0, prev, pltpu.roll(xn, 1, axis=0))
    xx = shifted - xn
    for i, o_ref in enumerate(o_refs):
        o_ref[...] = (xn + xx * mu_ref[i:i + 1, :]).astype(o_ref.dtype)


def rwkv_prep(h, g, mu, tm=256):
    t, d = h.shape
    tm = _pick_tile(t, tm)
    blk = pl.BlockSpec((tm, d), lambda i: (i, 0))
    return pl.pallas_call(
        _rwkv_prep_kernel,
        out_shape=[jax.ShapeDtypeStruct((t, d), BF16)] * N_SHIFT_MIX,
        grid=(t // tm,),
        in_specs=[blk,
                  pl.BlockSpec((8, d), lambda i: (jnp.maximum(i * (tm // 8) - 1, 0), 0)),
                  pl.BlockSpec((1, d), lambda i: (0, 0)),
                  pl.BlockSpec((N_SHIFT_MIX, d), lambda i: (0, 0))],
        out_specs=[blk] * N_SHIFT_MIX,
        compiler_params=_cparams(("parallel",)),
        name="rwkv_prep",
    )(h, h, g.reshape(1, d), mu)


def _head_mask(dtype):
    row = lax.broadcasted_iota(jnp.int32, (SCAN_WIDTH, SCAN_WIDTH), 0)
    col = lax.broadcasted_iota(jnp.int32, (SCAN_WIDTH, SCAN_WIDTH), 1)
    return jnp.where((row >> 6) == (col >> 6), 1.0, 0.0).astype(dtype)


def _block_diag(x, mask):
    return jnp.concatenate([x] * SCAN_HEADS, axis=0) * mask


def _fold_rows(x_full, mask_f32):
    x = x_full * mask_f32
    L = SCAN_CHUNK
    return x[0:L] + x[L:2 * L] + x[2 * L:3 * L] + x[3 * L:4 * L]


def _split2(x):
    hi = x.astype(BF16)
    return hi, (x - hi.astype(F32)).astype(BF16)


def _bd_parts(y, mask_b):
    hi, lo = _split2(y)
    return _block_diag(hi, mask_b), _block_diag(lo, mask_b)


def _mm_heads(x, y_parts):
    y_hi, y_lo = y_parts
    n = x.shape[0]
    x_hi, x_lo = _split2(x)
    both = _dot(jnp.concatenate([x_hi, x_lo], axis=0), y_hi)
    return both[:n] + both[n:] + _dot(x_hi, y_lo)


def _run_interleaved(stage_generators):
    live = list(stage_generators)
    while live:
        still = []
        for gen in live:
            try:
                next(gen)
                still.append(gen)
            except StopIteration:
                pass
        live = still


def _tri_inverse_heads(a, j_row, m_col, mask_b):
    L = SCAN_CHUNK
    eye = jnp.where(m_col == j_row, 1.0, 0.0)
    d = jnp.where((j_row >> 4) == (m_col >> 4), a, 0.0)
    e = a - d
    p = eye + d
    x = _mm_heads(d, _bd_parts(d, mask_b))
    yield
    for _ in range(2):
        both = _mm_heads(jnp.concatenate([x, p], axis=0), _bd_parts(x, mask_b))
        x, p = both[:L], p + both[L:]
        yield
    p = p + _mm_heads(p, _bd_parts(x, mask_b))
    yield
    n1 = _mm_heads(p, _bd_parts(e, mask_b))
    yield
    n2 = _mm_heads(n1, _bd_parts(n1, mask_b))
    yield
    q = eye + n1
    q = q + _mm_heads(q, _bd_parts(n2, mask_b))
    yield
    return _mm_heads(q, _bd_parts(p, mask_b))


def _softplus(x):
    return jnp.maximum(x, 0.0) + jnp.log(1.0 + jnp.exp(-jnp.abs(x)))


def _rwkv_chunk_kernel(*refs, mix_v):
    if mix_v:
        r_ref, k_ref, v_ref, wl_ref, al_ref, vl_ref, vf_ref, par_ref, m_ref, c_ref, qp_ref, y0_ref, bo_ref = refs
    else:
        r_ref, k_ref, v_ref, wl_ref, al_ref, par_ref, m_ref, c_ref, qp_ref, y0_ref, bo_ref = refs
    L, W = SCAN_CHUNK, SCAN_WIDTH
    mask_b = _head_mask(BF16)
    mask_f = _head_mask(F32)
    tri_incl = jnp.where(lax.broadcasted_iota(jnp.int32, (L, L), 1)
                         <= lax.broadcasted_iota(jnp.int32, (L, L), 0), 1.0, 0.0).astype(BF16)
    j_row = lax.broadcasted_iota(jnp.int32, (L, W), 0)
    m_col = lax.broadcasted_iota(jnp.int32, (L, W), 1) & (L - 1)
    strict = m_col < j_row
    incl = m_col <= j_row

    def unit(u):
        cols = slice(u * W, (u + 1) * W)
        par = par_ref[:, cols]
        w0, a0, v0, k_k, k_a, r_k = (par[i:i + 1] for i in range(6))
        r, k, v = r_ref[:, cols], k_ref[:, cols], v_ref[:, cols]
        lw = -jnp.exp(-_softplus(-(w0 + wl_ref[:, cols])) - 0.5)
        a_s = jax.nn.sigmoid(a0 + al_ref[:, cols])
        if mix_v:
            v = v + (vf_ref[:, cols] - v) * jax.nn.sigmoid(v0 + vl_ref[:, cols])
        kk = k * k_k
        k = k * (1.0 + (a_s - 1.0) * k_a)
        kk_sq = _dot_exact_rhs(kk * kk, mask_b)
        rk_sum = _dot_exact_rhs(r * k * r_k, mask_b)
        c = _dot_exact_lhs(tri_incl, lw)
        yield
        kk = kk * lax.rsqrt(jnp.maximum(kk_sq, 1e-24))
        bo_ref[:, cols] = rk_sum * v
        c_last = c[L - 1:L, :]
        p_inv = jnp.exp(-c)
        p_rest = jnp.exp(c_last - c)
        rt = r * jnp.exp(c)
        at = (-kk * jnp.exp(c - lw)).astype(BF16)
        b = kk * a_s
        vb = v.astype(BF16)
        ar = jnp.concatenate([at, rt.astype(BF16)], axis=0)
        g_b = _dot_nt(ar, _block_diag((b * p_inv).astype(BF16), mask_b))
        g_k = _dot_nt(ar, _block_diag((k * p_inv).astype(BF16), mask_b))
        yield
        a_ab = jnp.where(strict, g_b[:L], 0.0)
        a_ak = jnp.where(strict, g_k[:L], 0.0).astype(BF16)
        a_rb = jnp.where(incl, g_b[L:], 0.0).astype(BF16)
        a_rk = jnp.where(incl, g_k[L:], 0.0).astype(BF16)
        v_bd = _block_diag(vb, mask_b)
        akv = _dot(a_ak, v_bd)
        t_inv = yield from _tri_inverse_heads(a_ab, j_row, m_col, mask_b)
        yield
        t_hi, t_lo = _split2(t_inv)
        w_both = _dot(jnp.concatenate([t_hi, t_lo], axis=0), _block_diag(at, mask_b))
        u0 = _mm_heads(t_inv, _bd_parts(akv, mask_b)).astype(BF16)
        yield
        w = (w_both[:L] + w_both[L:]).astype(BF16)
        qp_ref[:, cols] = (rt + _dot(a_rb, _block_diag(w, mask_b))).astype(qp_ref.dtype)
        y0_ref[:, cols] = _dot(a_rb, _block_diag(u0, mask_b)) + _dot(a_rk, v_bd)
        bh = (b * p_rest).astype(BF16)
        kh = (k * p_rest).astype(BF16)
        decay_diag = jnp.where(m_col == j_row, jnp.exp(c_last), 0.0)
        m_ref[:, cols] = _fold_rows(_dot_tn(bh, w), mask_f) + decay_diag
        c_ref[:, cols] = _fold_rows(_dot_tn(bh, u0) + _dot_tn(kh, vb), mask_f)

    _run_interleaved(unit(u) for u in range(SCAN_UNITS))


def rwkv_chunks(r, k, v, w_lin, a_lin, v_lin, v_first, params):
    t, d = r.shape
    bw = SCAN_UNITS * SCAN_WIDTH
    blk = pl.BlockSpec((SCAN_CHUNK, bw), lambda ci, gi: (ci, gi))
    mix_v = v_lin is not None
    args = [r, k, v, w_lin, a_lin] + ([v_lin, v_first] if mix_v else []) + [params]
    f32 = jax.ShapeDtypeStruct((t, d), F32)
    return pl.pallas_call(
        functools.partial(_rwkv_chunk_kernel, mix_v=mix_v),
        out_shape=[f32, f32, jax.ShapeDtypeStruct((t, d), BF16), f32, f32],
        grid=(t // SCAN_CHUNK, d // bw),
        in_specs=[blk] * (len(args) - 1) + [pl.BlockSpec((8, bw), lambda ci, gi: (0, gi))],
        out_specs=[blk] * 5,
        compiler_params=_cparams(("parallel", "parallel")),
        name="rwkv_chunks",
    )(*args)


def _rwkv_state_kernel(m_ref, c_ref, qp_ref, y0_ref, bo_ref, g_ref, ln_ref, o_ref, st_ref):
    @pl.when(pl.program_id(1) == 0)
    def _():
        st_ref[...] = jnp.zeros_like(st_ref)

    W = SCAN_WIDTH
    mask_b = _head_mask(BF16)
    def unit(u):
        cols = slice(u * W, (u + 1) * W)
        st_parts = _bd_parts(st_ref[u], mask_b)
        y = _dot(qp_ref[:, cols], st_parts[0]) + y0_ref[:, cols]
        st_ref[u] = _mm_heads(m_ref[:, cols], st_parts) + c_ref[:, cols]
        yield
        mean = _dot_exact_rhs(y, mask_b) * (1.0 / RW_HEAD)
        yield
        dev = y - mean
        var = _dot_exact_rhs(dev * dev, mask_b) * (1.0 / RW_HEAD)
        yield
        ln = ln_ref[:, cols]
        out = (dev * lax.rsqrt(var + GN_EPS) * ln[0:1] + ln[1:2] + bo_ref[:, cols]) * g_ref[:, cols]
        o_ref[:, cols] = out.astype(o_ref.dtype)

    _run_interleaved(unit(u) for u in range(SCAN_UNITS))


def rwkv_state_scan(m, c, qp, y0, bonus, g, ln_params):
    t, d = m.shape
    bw = SCAN_UNITS * SCAN_WIDTH
    blk = pl.BlockSpec((SCAN_CHUNK, bw), lambda gi, ci: (ci, gi))
    return pl.pallas_call(
        _rwkv_state_kernel,
        out_shape=jax.ShapeDtypeStruct((t, d), BF16),
        grid=(d // bw, t // SCAN_CHUNK),
        in_specs=[blk] * 6 + [pl.BlockSpec((8, bw), lambda gi, ci: (0, gi))],
        out_specs=blk,
        scratch_shapes=[pltpu.VMEM((SCAN_UNITS, RW_HEAD, SCAN_WIDTH), F32)],
        compiler_params=_cparams(("parallel", "arbitrary")),
        name="rwkv_state_scan",
    )(m, c, qp, y0, bonus, g, ln_params)


def _compress_kernel(x_ref, w1_ref, c_ref, w2_ref, o_ref):
    ab = _dot(x_ref[...], w1_ref[...])
    a = ab[:, :NSA_HD]
    bsh = ab[:, NSA_HD:]
    n16 = a.shape[0]
    row = lax.broadcasted_iota(jnp.int32, a.shape, 0)
    b_next = jnp.where(row == n16 - 1, 0.0, pltpu.roll(bsh, n16 - 1, axis=0))
    hid = a + b_next + c_ref[...]
    hid = jax.nn.gelu(hid, approximate=True)
    o_ref[...] = _dot(hid.astype(BF16), w2_ref[...]).astype(o_ref.dtype)


def compress(x16, w1cat, cvec, w2):
    _, g, n16, kd = x16.shape
    d = NSA_HD
    return pl.pallas_call(
        _compress_kernel,
        out_shape=jax.ShapeDtypeStruct((2, g, n16, d), BF16),
        grid=(2, g),
        in_specs=[pl.BlockSpec((None, None, n16, kd), lambda p, gi: (p, gi, 0, 0)),
                  pl.BlockSpec((None, kd, 2 * d), lambda p, gi: (p, 0, 0)),
                  pl.BlockSpec((None, 1, d), lambda p, gi: (p, 0, 0)),
                  pl.BlockSpec((None, d, d), lambda p, gi: (p, 0, 0))],
        out_specs=pl.BlockSpec((None, None, n16, d), lambda p, gi: (p, gi, 0, 0)),
        compiler_params=_cparams(("parallel", "parallel")),
        name="nsa_compress",
    )(x16, w1cat, cvec, w2)


def _with_ones(v):
    return jnp.concatenate([v, jnp.ones_like(v)], axis=1)


def _nsa_kernel(q_ref, gate_ref, kc_ref, vc_ref, ks_ref, vs_ref, kw_ref, vw_ref, oh_ref,
                gs_ref, ov_ref, bs_ref, bw_ref, o_ref, s_buf, p_buf):
    qi = pl.program_id(1)
    R, Q, D = NSA_REP, QBLK, NSA_HD
    q = q_ref[...]
    qs = jnp.concatenate([q[:, r * D:(r + 1) * D] for r in range(R)], axis=0)
    n_cp = kc_ref.shape[0]
    n_sb = ov_ref.shape[1]
    out = {}

    def compressed_and_topk():
        t_rows = qi * Q + (lax.broadcasted_iota(jnp.int32, (R * Q, 1), 0) & (Q - 1))
        c_idx = lax.broadcasted_iota(jnp.int32, (1, n_cp), 1)
        m_off = qi * (Q // CMP_STRIDE) - lax.broadcasted_iota(jnp.int32, (n_cp, D), 0) + CMP_BAND_LO
        k_idx = lax.broadcasted_iota(jnp.int32, (n_cp, D), 1) & (CMP_BIAS_K - 1)
        in_band = ((k_idx < CMP_BAND) & (m_off == k_idx)) | ((k_idx == CMP_BAND) & (m_off >= CMP_BAND))
        k_c = jnp.concatenate([kc_ref[...], jnp.where(in_band, 1.0, 0.0).astype(BF16)], axis=1)
        qk_c = _dot_nt(jnp.concatenate([qs, gs_ref[...]], axis=1), k_c)
        yield
        m_c = (c_idx * CMP_STRIDE + (CMP_LEN - 1)) <= t_rows
        s_c = jnp.where(m_c, qk_c, NEG_INF)
        e_c = jnp.exp2(s_c - jnp.max(s_c, axis=-1, keepdims=True))
        acc_c = _dot(e_c.astype(BF16), _with_ones(vc_ref[...]))
        inv_l = jnp.where(t_rows >= CMP_LEN - 1, 1.0 / acc_c[:, D:D + 1], 0.0)
        out["o_c"] = acc_c[:, :D] * inv_l
        p_c = e_c * inv_l
        p_sum = p_c[0:Q] + p_c[Q:2 * Q] + p_c[2 * Q:3 * Q] + p_c[3 * Q:4 * Q]
        imp = _dot_exact_rhs(p_sum, ov_ref[...])
        yield
        imp_t = imp.T
        s_row = lax.broadcasted_iota(jnp.int32, (n_sb, Q), 0)
        t_col = qi * Q + lax.broadcasted_iota(jnp.int32, (n_sb, Q), 1)
        forced = (s_row == (t_col >> 6)) | (s_row == 0)
        score = jnp.where(forced, SEL_FORCE, jnp.where(s_row * SEL_BLOCK <= t_col, imp_t, -SEL_FORCE))
        sel_t = jnp.zeros((n_sb, Q), F32)
        s_f = s_row.astype(F32)
        for _ in range(SEL_TOP):
            mx = jnp.max(score, axis=0, keepdims=True)
            first = jnp.min(jnp.where(score == mx, s_f, float(n_sb)), axis=0, keepdims=True)
            pick = s_f == first
            sel_t = jnp.where(pick, 1.0, sel_t)
            score = jnp.where(pick, -3e38, score)
        out["sel"] = sel_t.T

    def window():
        w0 = pl.multiple_of(qi * Q, Q)
        qk_w = _dot_nt(qs, kw_ref[pl.ds(w0, WINDOW + Q), :])
        yield
        j_io = lax.broadcasted_iota(jnp.int32, (1, WINDOW + Q), 1)
        s_w = jnp.where(j_io >= WINDOW - qi * Q, qk_w + bw_ref[...], NEG_INF)
        e_w = jnp.exp2(s_w - jnp.max(s_w, axis=-1, keepdims=True))
        acc_w = _dot(e_w.astype(BF16), _with_ones(vw_ref[pl.ds(w0, WINDOW + Q), :]))
        out["o_w"] = acc_w[:, :D] / acc_w[:, D:]

    _run_interleaved([compressed_and_topk(), window()])
    o_c, o_w, sel = out["o_c"], out["o_w"], out["sel"]

    n_far = jnp.maximum(qi - 1, 0)
    blk = lax.broadcasted_iota(jnp.int32, (Q, n_sb), 1)
    neg_near = jnp.where(sel > 0.5, 0.0, NEG_INF).astype(BF16)
    neg_far = jnp.where((sel > 0.5) & (blk < 2 * n_far), 0.0, NEG_INF).astype(BF16)
    q_near = jnp.concatenate([qs, jnp.concatenate([neg_near] * R, axis=0)], axis=1)
    q_far = jnp.concatenate([qs, jnp.concatenate([neg_far] * R, axis=0)], axis=1)

    nb0 = jnp.maximum(qi - 1, 0)
    n0 = pl.multiple_of(nb0 * Q, Q)

    def sel_keys(start, n):
        return jnp.concatenate([ks_ref[pl.ds(WINDOW + start, n), :], oh_ref[pl.ds(start, n), :]], axis=1)

    s = _dot_nt(q_near, sel_keys(n0, 2 * Q)) + bs_ref[jnp.where(qi == 0, 1, 0)]
    m_0 = jnp.max(s, axis=-1, keepdims=True)
    acc_0 = _dot(jnp.exp2(s - m_0).astype(BF16), _with_ones(vs_ref[pl.ds(WINDOW + n0, 2 * Q), :]))

    n_tiles = (n_far + (SEL_TILE // Q - 1)) // (SEL_TILE // Q)
    n_pairs = (n_tiles + 1) >> 1
    max_tile = oh_ref.shape[0] // SEL_TILE - 1

    def key_tile(i):
        return sel_keys(pl.multiple_of(i * SEL_TILE, SEL_TILE), SEL_TILE)

    def value_tile(i):
        return _with_ones(vs_ref[pl.ds(pl.multiple_of(WINDOW + i * SEL_TILE, SEL_TILE), SEL_TILE), :])

    s_buf[0] = _dot_nt(q_far, key_tile(0))
    p_buf[1] = jnp.zeros(p_buf.shape[1:], p_buf.dtype)

    def half_step(i, slot, carry):
        m_run, acc, alpha_prev = carry
        s_buf[1 - slot] = _dot_nt(q_far, key_tile(jnp.minimum(i + 1, max_tile)))
        acc = alpha_prev * acc + _dot(p_buf[1 - slot], value_tile(jnp.maximum(i - 1, 0)))
        s_i = s_buf[slot]
        m_new = jnp.maximum(m_run, jnp.max(s_i, axis=-1, keepdims=True))
        p_buf[slot] = jnp.exp2(s_i - m_new).astype(p_buf.dtype)
        return m_new, acc, jnp.exp2(m_run - m_new)

    def pair_body(j, carry):
        return half_step(2 * j + 1, 1, half_step(2 * j, 0, carry))

    _, acc_s, alpha_last = lax.fori_loop(0, n_pairs, pair_body, (m_0, acc_0, jnp.ones_like(m_0)))
    acc_s = alpha_last * acc_s + _dot(p_buf[1], value_tile(jnp.maximum(2 * n_pairs - 1, 0)))
    o_s = acc_s[:, :D] / acc_s[:, D:]

    gates = gate_ref[...]
    for r in range(R):
        rows = slice(r * Q, (r + 1) * Q)
        o = (gates[:, 3 * r:3 * r + 1] * o_c[rows] + gates[:, 3 * r + 1:3 * r + 2] * o_s[rows]
             + gates[:, 3 * r + 2:3 * r + 3] * o_w[rows])
        o_ref[:, r * D:(r + 1) * D] = o.astype(o_ref.dtype)


def nsa_attention(q, gates, kc, vc, kv_padded, block_onehot, gs, ov, bs, bw):
    t = q.shape[0]
    g, n_cp, d = kc.shape
    rq = NSA_REP * QBLK
    per_g = lambda shape: pl.BlockSpec((None,) + shape, lambda gi, qi: (gi,) + (0,) * len(shape))
    kv_part = lambda part: pl.BlockSpec((t + WINDOW, d), lambda gi, qi: (0, part * g + gi))
    return pl.pallas_call(
        _nsa_kernel,
        out_shape=jax.ShapeDtypeStruct((t, NSA_HEADS * d), BF16),
        grid=(g, t // QBLK),
        in_specs=[pl.BlockSpec((QBLK, NSA_REP * d), lambda gi, qi: (qi, gi)),
                  pl.BlockSpec((None, QBLK, NSA_REP * N_BRANCH), lambda gi, qi: (gi, qi, 0)),
                  per_g((n_cp, d)), per_g((n_cp, d)),
                  kv_part(2), kv_part(3), kv_part(4), kv_part(5),
                  pl.BlockSpec((t, d), lambda gi, qi: (0, 0)),
                  per_g((rq, d)),
                  pl.BlockSpec(ov.shape, lambda gi, qi: (0, 0)),
                  per_g((2, rq, 2 * QBLK)),
                  per_g((rq, WINDOW + QBLK))],
        out_specs=pl.BlockSpec((QBLK, NSA_REP * d), lambda gi, qi: (qi, gi)),
        scratch_shapes=[pltpu.VMEM((2, rq, SEL_TILE), F32), pltpu.VMEM((2, rq, SEL_TILE), BF16)],
        compiler_params=_cparams(("parallel", "arbitrary")),
        name="nsa_attention",
    )(q, gates, kc, vc, kv_padded, kv_padded, kv_padded, kv_padded, block_onehot, gs, ov, bs, bw)


def _t5_bucket_np(n):
    n = np.maximum(n, 0)
    max_exact = REL_BUCKETS // 2
    nf = np.maximum(n, 1).astype(np.float64)
    large = max_exact + (np.log(nf / max_exact) / math.log(REL_MAX_DIST / max_exact)
                         * (REL_BUCKETS - max_exact)).astype(np.int64)
    large = np.minimum(large, REL_BUCKETS - 1)
    return np.where(n < max_exact, n, large).astype(np.int32)


def _nsa_bias_tables(rel_bias):
    G, R, Q = NSA_GROUPS, NSA_REP, QBLK
    table = (rel_bias.astype(F32) * LOG2E).reshape(REL_BUCKETS, G, R).transpose(1, 2, 0)
    qq = np.arange(Q)

    def lookup(dist):
        return table[:, :, _t5_bucket_np(dist)].reshape(G, R * Q, dist.shape[1])

    m = np.arange(CMP_BAND) - CMP_BAND_LO
    gs = lookup(qq[:, None] - (CMP_LEN - 1) + CMP_STRIDE * m[None, :])
    far = jnp.broadcast_to(table[:, :, REL_BUCKETS - 1][:, :, None, None], (G, R, Q, 1)).reshape(G, R * Q, 1)
    gs = jnp.concatenate([gs, far, jnp.zeros((G, R * Q, CMP_BIAS_K - CMP_BAND - 1), F32)], axis=-1)
    h, mid, l = _split3(gs)
    gs3 = jnp.concatenate([h, mid, l, jnp.zeros_like(h)], axis=-1)
    def toeplitz(c):
        mm = np.arange(2 * Q)
        v = table[:, :, _t5_bucket_np(np.where(mm < Q, c - mm, c + 2 * Q - mm))]
        rows = jnp.tile(v, (1, 1, Q))[:, :, :Q * (2 * Q - 1)].reshape(G, R, Q, 2 * Q - 1)
        return rows[:, :, :, :Q].reshape(G, R * Q, Q)

    d0 = np.tile(qq[:, None] - qq[None, :], (R, 1))
    causal = jnp.asarray(np.where(d0 >= 0, 0.0, NEG_INF), F32)[None]
    b0 = toeplitz(0) + causal
    b1 = toeplitz(Q)
    far_c = jnp.broadcast_to(far, (G, R * Q, Q))
    neg = jnp.full((G, R * Q, Q), NEG_INF, F32)
    bs = jnp.stack([jnp.concatenate([b1 - far_c, b0 - far_c], -1),
                    jnp.concatenate([b0 - far_c, neg], -1)], axis=1)
    oldest = far_c + jnp.asarray(np.where(d0 < 0, 0.0, NEG_INF), F32)[None]
    bw = jnp.concatenate([oldest, far_c, far_c, b1, b0], -1)
    return gs3, bs, bw


def _overlap_table(n_cp, n_sb):
    c = np.arange(n_cp)[:, None] * CMP_STRIDE
    s = np.arange(n_sb)[None, :] * SEL_BLOCK
    return jnp.asarray(((c <= s + SEL_BLOCK - 1) & (c + CMP_LEN - 1 >= s)), BF16)


def _pad_cols(w, n):
    return jnp.pad(w, ((0, 0), (0, n - w.shape[1])))


def _pad_rows(w, n):
    return jnp.pad(w, ((0, n - w.shape[0]), (0, 0)))


def _rwkv_layer(h, layer, norm_g, mu, wr, wk, wv, wo, w0, w1, w2, a0, a1, a2, g1, g2,
                k_k, k_a, r_k, lnx_w, lnx_b, v_first, v0, v1, v2):
    bf = lambda w: w.astype(BF16)
    xr, xw, xk, xv, xa, xg = rwkv_prep(h, norm_g, mu)
    r = matmul(xr, wr, layer=layer)
    k = matmul(xk, wk, layer=layer)
    v = matmul(xv, wv, layer=layer)
    lw_hid = matmul(xw, bf(_pad_cols(w1, 128)), act="tanh", out_dtype=BF16)
    w_lin = matmul(lw_hid, bf(_pad_rows(w2, 128)))
    a_lin = matmul(matmul(xa, bf(a1), out_dtype=BF16), bf(a2))
    g = matmul(matmul(xg, bf(g1), act="sigmoid", out_dtype=BF16), bf(g2))
    zeros = jnp.zeros_like(w0)
    if v_first is None:
        v_lin, v0 = None, zeros
    else:
        v_lin = matmul(matmul(xv, bf(_pad_cols(v1, 128)), out_dtype=BF16), bf(_pad_rows(v2, 128)))
    params = jnp.stack([w0, a0, v0, k_k, k_a, r_k.reshape(-1), zeros, zeros])
    m, c, qp, y0, bonus = rwkv_chunks(r, k, v, w_lin, a_lin, v_lin, v_first, params)
    ln_params = jnp.stack([lnx_w, lnx_b] + [zeros] * 6)
    gated = rwkv_state_scan(m, c, qp, y0, bonus, g, ln_params)
    h = matmul(gated, wo, layer=layer, residual=h)
    return h, (v if v_first is None else v_first)


def _mlp(h, layer, norm_g, w_up, w_down):
    hid = norm_matmul(h, norm_g, w_up, layer=layer, act="relu2", out_dtype=BF16)
    return matmul(hid, w_down, layer=layer, residual=h)


def _nsa_shared_kv(h, kv_norm_g, w_kv, cmp_pe, cmp_w1, cmp_b1, cmp_w2):
    t = h.shape[0]
    G, D = NSA_GROUPS, NSA_HD
    kv = norm_matmul(h, kv_norm_g, w_kv.astype(BF16), out_dtype=BF16)
    x16 = kv[:, :2 * G * D].reshape(t, 2, G, D).transpose(1, 2, 0, 3).reshape(2, G, t // CMP_STRIDE, CMP_STRIDE * D)
    half = CMP_STRIDE * D
    w1cat = jnp.concatenate([cmp_w1[:, :half], cmp_w1[:, half:]], axis=-1).astype(BF16)
    cvec = jnp.einsum("plk,plkd->pd", cmp_pe, cmp_w1.reshape(2, CMP_LEN, D, D),
                      precision=lax.Precision.HIGHEST) + cmp_b1
    kvc = compress(x16, w1cat, cvec.reshape(2, 1, D), cmp_w2.astype(BF16))
    block_onehot = jnp.asarray(np.arange(t)[:, None] // SEL_BLOCK == np.arange(D)[None, :], BF16)
    return kvc[0], kvc[1], jnp.pad(kv, ((WINDOW, 0), (0, 0))), block_onehot


def _nsa_layer(h, layer, norm_g, wq, wo, kvs, tables):
    t = h.shape[0]
    nq = NSA_HEADS * NSA_HD
    q = norm_matmul(h, norm_g, wq, layer=layer, n=nq, scale=NSA_HD ** -0.5 * LOG2E, out_dtype=BF16)
    gates = norm_matmul(h, norm_g, _pad_cols(wq[layer, :, nq:], 128), act="sigmoid")
    gates = gates[:, :NSA_HEADS * N_BRANCH].reshape(t, NSA_GROUPS, NSA_REP * N_BRANCH).transpose(1, 0, 2)
    o = nsa_attention(q, gates, *kvs, *tables)
    return matmul(o, wo, layer=layer, residual=h)


def kernel(x, a_norm_g, rw_mu, rw_wr, rw_wk, rw_wv, rw_wo, rw_w0, rw_w1, rw_w2, rw_a0, rw_a1, rw_a2,
           rw_v0, rw_v1, rw_v2, rw_g1, rw_g2, rw_kk, rw_ka, rw_rk, rw_lnx_w, rw_lnx_b, kv_norm_g, w_kv,
           cmp_pe, cmp_w1, cmp_b1, cmp_w2, b_norm_g, nsa_wq, nsa_wo, rel_bias, m_norm_g, mlp_up,
           mlp_down, final_norm_g):
    b, t, d = x.shape
    assert b == 1 and d == D_MODEL and t % SEL_TILE == 0 and t // SEL_BLOCK <= 128 and WINDOW == 4 * QBLK
    h = x.reshape(t, d)
    v_first = None
    kvs = tables = None
    rw_w = [w.astype(BF16) for w in (rw_wr, rw_wk, rw_wv, rw_wo)]
    nsa_wq_b, nsa_wo_b, mlp_down_b = nsa_wq.astype(BF16), nsa_wo.astype(BF16), mlp_down.astype(BF16)
    for layer in range(DEPTH):
        if layer < N_A_LAYERS:
            i = layer
            vp = (None, None, None) if i == 0 else (rw_v0[i - 1], rw_v1[i - 1], rw_v2[i - 1])
            h, v_first = _rwkv_layer(
                h, i, a_norm_g[i], rw_mu[i], *rw_w, rw_w0[i], rw_w1[i],
                rw_w2[i], rw_a0[i], rw_a1[i], rw_a2[i], rw_g1[i], rw_g2[i], rw_kk[i], rw_ka[i],
                rw_rk[i].reshape(1, RW_HEADS, RW_HEAD), rw_lnx_w[i], rw_lnx_b[i], v_first, *vp)
        else:
            j = layer - N_A_LAYERS
            if j == 0:
                kvs = _nsa_shared_kv(h, kv_norm_g, w_kv, cmp_pe, cmp_w1, cmp_b1, cmp_w2)
                gs3, bs, bw = _nsa_bias_tables(rel_bias)
                tables = (gs3, _overlap_table(t // CMP_STRIDE, 128), bs, bw)
            h = _nsa_layer(h, j, b_norm_g[j], nsa_wq_b, nsa_wo_b, kvs, tables)
        h = _mlp(h, layer, m_norm_g[layer], mlp_up, mlp_down_b)
    return rmsnorm(h, final_norm_g, out_dtype=x.dtype).reshape(b, t, d)
```

```python
import functools
import math

import numpy as np
import jax
import jax.numpy as jnp
from jax import lax
from jax.experimental import pallas as pl
from jax.experimental.pallas import tpu as pltpu

F32 = jnp.float32
BF16 = jnp.bfloat16

D_MODEL = 2048
DEPTH = 4
N_A_LAYERS = 2
RW_HEAD = 64
RW_HEADS = D_MODEL // RW_HEAD
N_SHIFT_MIX = 6
GN_EPS = 64e-5
NSA_HEADS = 16
NSA_GROUPS = 4
NSA_REP = NSA_HEADS // NSA_GROUPS
NSA_HD = D_MODEL // NSA_HEADS
N_KV_PARTS = 6
CMP_STRIDE = 16
CMP_LEN = 32
SEL_BLOCK = 64
SEL_TOP = 16
WINDOW = 512
QBLK = 128
N_BRANCH = 3
SEL_FORCE = 1e4
NEG_INF = -1e30
REL_BUCKETS = 32
REL_MAX_DIST = 128
NORM_EPS = 1e-6
LOG2E = math.log2(math.e)

VMEM_LIMIT_BYTES = 56 * 1024 * 1024
SCAN_CHUNK = 64
SCAN_HEADS = 4
SCAN_WIDTH = SCAN_HEADS * RW_HEAD
SCAN_UNITS = 8
SEL_TILE = 512
CMP_BAND = 16
CMP_BAND_LO = 6
CMP_BIAS_K = 32
assert CMP_BAND < CMP_BIAS_K and 4 * CMP_BIAS_K == NSA_HD and NSA_HD == 128


def _cparams(sem):
    return pltpu.CompilerParams(dimension_semantics=sem, vmem_limit_bytes=VMEM_LIMIT_BYTES)


def _split3(x):
    h = x.astype(BF16)
    r = x - h.astype(F32)
    m = r.astype(BF16)
    l = (r - m.astype(F32)).astype(BF16)
    return h, m, l


def _dot(a, b):
    return jnp.dot(a, b, preferred_element_type=F32)


def _dot_nt(a, b):
    return lax.dot_general(a, b, (((1,), (1,)), ((), ())), preferred_element_type=F32)


def _dot_tn(a, b):
    return lax.dot_general(a, b, (((0,), (0,)), ((), ())), preferred_element_type=F32)


def _dot_exact_lhs(a_bf16, x):
    h, m, l = _split3(x)
    return _dot(a_bf16, h) + _dot(a_bf16, m) + _dot(a_bf16, l)


def _dot_exact_rhs(x, b_bf16):
    h, m, l = _split3(x)
    return _dot(h, b_bf16) + _dot(m, b_bf16) + _dot(l, b_bf16)


def _activate(acc, act, scale):
    if act == "relu2":
        acc = jnp.square(jnp.maximum(acc, 0.0))
    elif act == "tanh":
        acc = jnp.tanh(acc)
    elif act == "sigmoid":
        acc = jax.nn.sigmoid(acc)
    return acc if scale is None else acc * scale


def _matmul_kernel(x_ref, w_ref, *rest, act, scale, has_res, nk):
    if has_res:
        res_ref, o_ref, *scratch = rest
    else:
        o_ref, *scratch = rest

    def epilogue(acc):
        acc = _activate(acc, act, scale)
        if has_res:
            acc = acc + res_ref[...]
        o_ref[...] = acc.astype(o_ref.dtype)

    if nk == 1:
        epilogue(_dot(x_ref[...], w_ref[...]))
    else:
        acc_ref, = scratch
        k = pl.program_id(2)

        @pl.when(k == 0)
        def _():
            acc_ref[...] = jnp.zeros_like(acc_ref)

        acc_ref[...] += _dot(x_ref[...], w_ref[...])

        @pl.when(k == nk - 1)
        def _():
            epilogue(acc_ref[...])


def _pick_tile(n, pref):
    t = min(n, pref)
    while n % t:
        t //= 2
    return t


def _weight_spec(w, layer, rows, cols, index):
    if w.ndim == 2:
        return pl.BlockSpec((rows, cols), index)
    return pl.BlockSpec((None, rows, cols), lambda *g: (layer,) + tuple(index(*g)))


def matmul(x, w, *, layer=None, act=None, scale=None, residual=None, out_dtype=F32, n=None,
           tm=1024, tn=512, tk=2048):
    m, kdim = x.shape
    n = w.shape[-1] if n is None else n
    tm, tn, tk = _pick_tile(m, tm), _pick_tile(n, tn), _pick_tile(kdim, tk)
    nk = kdim // tk
    in_specs = [pl.BlockSpec((tm, tk), lambda j, i, k: (i, k)),
                _weight_spec(w, layer, tk, tn, lambda j, i, k: (k, j))]
    args = [x, w]
    if residual is not None:
        in_specs.append(pl.BlockSpec((tm, tn), lambda j, i, k: (i, j)))
        args.append(residual)
    return pl.pallas_call(
        functools.partial(_matmul_kernel, act=act, scale=scale, has_res=residual is not None, nk=nk),
        out_shape=jax.ShapeDtypeStruct((m, n), out_dtype),
        grid=(n // tn, m // tm, nk),
        in_specs=in_specs,
        out_specs=pl.BlockSpec((tm, tn), lambda j, i, k: (i, j)),
        scratch_shapes=[pltpu.VMEM((tm, tn), F32)] if nk > 1 else [],
        compiler_params=_cparams(("parallel", "parallel", "arbitrary")),
        name="matmul",
    )(*args)


def _lora_kernel(x_ref, w1_ref, w2_ref, o_ref, *, act):
    hidden = _activate(_dot(x_ref[...], w1_ref[...]), act, None).astype(BF16)
    o_ref[...] = _dot(hidden, w2_ref[...])


def lora(x, w1, w2, *, act=None, tm=1024):
    m, kdim = x.shape
    r, n = w2.shape
    tm = _pick_tile(m, tm)
    return pl.pallas_call(
        functools.partial(_lora_kernel, act=act),
        out_shape=jax.ShapeDtypeStruct((m, n), F32),
        grid=(m // tm,),
        in_specs=[pl.BlockSpec((tm, kdim), lambda i: (i, 0)),
                  pl.BlockSpec((kdim, r), lambda i: (0, 0)),
                  pl.BlockSpec((r, n), lambda i: (0, 0))],
        out_specs=pl.BlockSpec((tm, n), lambda i: (i, 0)),
        compiler_params=_cparams(("parallel",)),
        name="lora",
    )(x, w1, w2)


def _rms(x, g):
    return x * lax.rsqrt(jnp.mean(x * x, axis=-1, keepdims=True) + NORM_EPS) * g


def _rmsnorm_kernel(h_ref, g_ref, o_ref):
    o_ref[...] = _rms(h_ref[...], g_ref[...]).astype(o_ref.dtype)


def rmsnorm(h, g, out_dtype=BF16, tm=512):
    t, d = h.shape
    tm = _pick_tile(t, tm)
    return pl.pallas_call(
        _rmsnorm_kernel,
        out_shape=jax.ShapeDtypeStruct((t, d), out_dtype),
        grid=(t // tm,),
        in_specs=[pl.BlockSpec((tm, d), lambda i: (i, 0)), pl.BlockSpec((1, d), lambda i: (0, 0))],
        out_specs=pl.BlockSpec((tm, d), lambda i: (i, 0)),
        compiler_params=_cparams(("parallel",)),
        name="rmsnorm",
    )(h, g.reshape(1, d))


def _norm_matmul_kernel(h_ref, g_ref, w_ref, o_ref, xn_ref, *, act, scale):
    @pl.when(pl.program_id(1) == 0)
    def _():
        xn_ref[...] = _rms(h_ref[...], g_ref[...]).astype(xn_ref.dtype)

    o_ref[...] = _activate(_dot(xn_ref[...], w_ref[...].astype(BF16)), act, scale).astype(o_ref.dtype)


def norm_matmul(h, g, w, *, layer=None, act=None, scale=None, out_dtype=F32, n=None, tm=1024, tn=512):
    m, kdim = h.shape
    n = w.shape[-1] if n is None else n
    tm, tn = _pick_tile(m, tm), _pick_tile(n, tn)
    return pl.pallas_call(
        functools.partial(_norm_matmul_kernel, act=act, scale=scale),
        out_shape=jax.ShapeDtypeStruct((m, n), out_dtype),
        grid=(m // tm, n // tn),
        in_specs=[pl.BlockSpec((tm, kdim), lambda i, j: (i, 0)),
                  pl.BlockSpec((1, kdim), lambda i, j: (0, 0)),
                  _weight_spec(w, layer, kdim, tn, lambda i, j: (0, j))],
        out_specs=pl.BlockSpec((tm, tn), lambda i, j: (i, j)),
        scratch_shapes=[pltpu.VMEM((tm, kdim), BF16)],
        compiler_params=_cparams(("parallel", "arbitrary")),
        name="norm_matmul",
    )(h, g.reshape(1, kdim), w)


def _rwkv_prep_kernel(h_ref, prev_ref, g_ref, mu_ref, *o_refs):
    g = g_ref[...]
    xn = _rms(h_ref[...], g)
    prev = _rms(prev_ref[...], g)[7:8, :] * (pl.program_id(0) > 0).astype(F32)
    row = lax.broadcasted_iota(jnp.int32, xn.shape, 0)
    shifted = jnp.where(row == 0, prev, pltpu.roll(xn, 1, axis=0))
    xx = shifted - xn
    for i, o_ref in enumerate(o_refs):
        o_ref[...] = (xn + xx * mu_ref[i:i + 1, :]).astype(o_ref.dtype)


def rwkv_prep(h, g, mu, tm=256):
    t, d = h.shape
    tm = _pick_tile(t, tm)
    blk = pl.BlockSpec((tm, d), lambda i: (i, 0))
    return pl.pallas_call(
        _rwkv_prep_kernel,
        out_shape=[jax.ShapeDtypeStruct((t, d), BF16)] * N_SHIFT_MIX,
        grid=(t // tm,),
        in_specs=[blk,
                  pl.BlockSpec((8, d), lambda i: (jnp.maximum(i * (tm // 8) - 1, 0), 0)),
                  pl.BlockSpec((1, d), lambda i: (0, 0)),
                  pl.BlockSpec((N_SHIFT_MIX, d), lambda i: (0, 0))],
        out_specs=[blk] * N_SHIFT_MIX,
        compiler_params=_cparams(("parallel",)),
        name="rwkv_prep",
    )(h, h, g.reshape(1, d), mu)


def _head_mask(dtype):
    row = lax.broadcasted_iota(jnp.int32, (SCAN_WIDTH, SCAN_WIDTH), 0)
    col = lax.broadcasted_iota(jnp.int32, (SCAN_WIDTH, SCAN_WIDTH), 1)
    return jnp.where((row >> 6) == (col >> 6), 1.0, 0.0).astype(dtype)


def _block_diag(x, mask):
    return jnp.concatenate([x] * SCAN_HEADS, axis=0) * mask


def _fold_rows(x_full, mask_f32):
    x = x_full * mask_f32
    L = SCAN_CHUNK
    return x[0:L] + x[L:2 * L] + x[2 * L:3 * L] + x[3 * L:4 * L]


def _split2(x):
    hi = x.astype(BF16)
    return hi, (x - hi.astype(F32)).astype(BF16)


def _head_sums(xs, mask_b):
    n = xs[0].shape[0]
    r = _dot(jnp.concatenate([piece for x in xs for piece in _split3(x)], axis=0), mask_b)
    return [r[3 * i * n:(3 * i + 1) * n] + r[(3 * i + 1) * n:(3 * i + 2) * n] + r[(3 * i + 2) * n:(3 * i + 3) * n]
            for i in range(len(xs))]


def _bd_parts(y_split, mask_b):
    hi, lo = y_split
    return _block_diag(hi, mask_b), _block_diag(lo, mask_b)


def _mm_heads(x_split, y_parts):
    x_hi, x_lo = x_split
    y_hi, y_lo = y_parts
    n = x_hi.shape[0]
    both = _dot(jnp.concatenate([x_hi, x_lo], axis=0), y_hi)
    return both[:n] + both[n:] + _dot(x_hi, y_lo)


def _run_interleaved(stage_generators):
    live = list(stage_generators)
    while live:
        still = []
        for gen in live:
            try:
                next(gen)
                still.append(gen)
            except StopIteration:
                pass
        live = still


def _tri_inverse_heads(a, j_row, m_col, mask_b):
    L = SCAN_CHUNK
    eye = jnp.where(m_col == j_row, 1.0, 0.0)
    d = jnp.where((j_row >> 4) == (m_col >> 4), a, 0.0)
    e = a - d
    p = eye + d
    d_split = _split2(d)
    x = _mm_heads(d_split, _bd_parts(d_split, mask_b))
    yield
    for _ in range(2):
        x_split, p_split = _split2(x), _split2(p)
        stacked = tuple(jnp.concatenate([xs, ps], axis=0) for xs, ps in zip(x_split, p_split))
        both = _mm_heads(stacked, _bd_parts(x_split, mask_b))
        x, p = both[:L], p + both[L:]
        yield
    p = p + _mm_heads(_split2(p), _bd_parts(_split2(x), mask_b))
    yield
    p_split = _split2(p)
    n1 = _mm_heads(p_split, _bd_parts(_split2(e), mask_b))
    yield
    n1_split = _split2(n1)
    n2 = _mm_heads(n1_split, _bd_parts(n1_split, mask_b))
    yield
    q = eye + n1
    q = q + _mm_heads(_split2(q), _bd_parts(_split2(n2), mask_b))
    yield
    return _mm_heads(_split2(q), _bd_parts(p_split, mask_b))


def _softplus(x):
    return jnp.maximum(x, 0.0) + jnp.log(1.0 + jnp.exp(-jnp.abs(x)))


def _rwkv_chunk_kernel(*refs, mix_v):
    if mix_v:
        r_ref, k_ref, v_ref, wl_ref, al_ref, vl_ref, vf_ref, par_ref, m_ref, c_ref, qp_ref, y0_ref, bo_ref = refs
    else:
        r_ref, k_ref, v_ref, wl_ref, al_ref, par_ref, m_ref, c_ref, qp_ref, y0_ref, bo_ref = refs
    L, W = SCAN_CHUNK, SCAN_WIDTH
    mask_b = _head_mask(BF16)
    mask_f = _head_mask(F32)
    tri_incl = jnp.where(lax.broadcasted_iota(jnp.int32, (L, L), 1)
                         <= lax.broadcasted_iota(jnp.int32, (L, L), 0), 1.0, 0.0).astype(BF16)
    j_row = lax.broadcasted_iota(jnp.int32, (L, W), 0)
    m_col = lax.broadcasted_iota(jnp.int32, (L, W), 1) & (L - 1)
    strict = m_col < j_row
    incl = m_col <= j_row

    def unit(u):
        cols = slice(u * W, (u + 1) * W)
        par = par_ref[:, cols]
        w0, a0, v0, k_k, k_a, r_k = (par[i:i + 1] for i in range(6))
        r, k, v = r_ref[:, cols], k_ref[:, cols], v_ref[:, cols]
        lw = -jnp.exp(-_softplus(-(w0 + wl_ref[:, cols])) - 0.5)
        a_s = jax.nn.sigmoid(a0 + al_ref[:, cols])
        if mix_v:
            v = v + (vf_ref[:, cols] - v) * jax.nn.sigmoid(v0 + vl_ref[:, cols])
        kk = k * k_k
        k = k * (1.0 + (a_s - 1.0) * k_a)
        kk_sq, rk_sum = _head_sums([kk * kk, r * k * r_k], mask_b)
        c = _dot_exact_lhs(tri_incl, lw)
        yield
        kk = kk * lax.rsqrt(jnp.maximum(kk_sq, 1e-24))
        bo_ref[:, cols] = rk_sum * v
        c_last = c[L - 1:L, :]
        p_inv = jnp.exp(-c)
        p_rest = jnp.exp(c_last - c)
        rt = r * jnp.exp(c)
        at = (-kk * jnp.exp(c - lw)).astype(BF16)
        b = kk * a_s
        vb = v.astype(BF16)
        ar = jnp.concatenate([at, rt.astype(BF16)], axis=0)
        g_b = _dot_nt(ar, _block_diag((b * p_inv).astype(BF16), mask_b))
        g_k = _dot_nt(ar, _block_diag((k * p_inv).astype(BF16), mask_b))
        yield
        a_ab = jnp.where(strict, g_b[:L], 0.0)
        a_ak = jnp.where(strict, g_k[:L], 0.0).astype(BF16)
        a_rb = jnp.where(incl, g_b[L:], 0.0).astype(BF16)
        a_rk = jnp.where(incl, g_k[L:], 0.0).astype(BF16)
        v_bd = _block_diag(vb, mask_b)
        akv = _dot(a_ak, v_bd)
        t_inv = yield from _tri_inverse_heads(a_ab, j_row, m_col, mask_b)
        yield
        t_split = _split2(t_inv)
        w_both = _dot(jnp.concatenate(t_split, axis=0), _block_diag(at, mask_b))
        u0 = _mm_heads(t_split, _bd_parts(_split2(akv), mask_b)).astype(BF16)
        yield
        w = (w_both[:L] + w_both[L:]).astype(BF16)
        qp_ref[:, cols] = (rt + _dot(a_rb, _block_diag(w, mask_b))).astype(qp_ref.dtype)
        y0_ref[:, cols] = _dot(a_rb, _block_diag(u0, mask_b)) + _dot(a_rk, v_bd)
        bh = (b * p_rest).astype(BF16)
        kh = (k * p_rest).astype(BF16)
        decay_diag = jnp.where(m_col == j_row, jnp.exp(c_last), 0.0)
        m_ref[:, cols] = _fold_rows(_dot_tn(bh, w), mask_f) + decay_diag
        c_ref[:, cols] = _fold_rows(_dot_tn(bh, u0) + _dot_tn(kh, vb), mask_f)

    _run_interleaved(unit(u) for u in range(SCAN_UNITS))


def rwkv_chunks(r, k, v, w_lin, a_lin, v_lin, v_first, params):
    t, d = r.shape
    bw = SCAN_UNITS * SCAN_WIDTH
    blk = pl.BlockSpec((SCAN_CHUNK, bw), lambda ci, gi: (ci, gi))
    mix_v = v_lin is not None
    args = [r, k, v, w_lin, a_lin] + ([v_lin, v_first] if mix_v else []) + [params]
    f32 = jax.ShapeDtypeStruct((t, d), F32)
    return pl.pallas_call(
        functools.partial(_rwkv_chunk_kernel, mix_v=mix_v),
        out_shape=[f32, f32, jax.ShapeDtypeStruct((t, d), BF16), f32, f32],
        grid=(t // SCAN_CHUNK, d // bw),
        in_specs=[blk] * (len(args) - 1) + [pl.BlockSpec((8, bw), lambda ci, gi: (0, gi))],
        out_specs=[blk] * 5,
        compiler_params=_cparams(("parallel", "parallel")),
        name="rwkv_chunks",
    )(*args)


def _rwkv_state_kernel(m_ref, c_ref, qp_ref, y0_ref, bo_ref, g_ref, ln_ref, o_ref, st_ref):
    @pl.when(pl.program_id(1) == 0)
    def _():
        st_ref[...] = jnp.zeros_like(st_ref)

    W = SCAN_WIDTH
    mask_b = _head_mask(BF16)
    def unit(u):
        cols = slice(u * W, (u + 1) * W)
        st_parts = _bd_parts(_split2(st_ref[u]), mask_b)
        y = _dot(qp_ref[:, cols], st_parts[0]) + y0_ref[:, cols]
        st_ref[u] = _mm_heads(_split2(m_ref[:, cols]), st_parts) + c_ref[:, cols]
        yield
        mean = _head_sums([y], mask_b)[0] * (1.0 / RW_HEAD)
        yield
        dev = y - mean
        var = _head_sums([dev * dev], mask_b)[0] * (1.0 / RW_HEAD)
        yield
        ln = ln_ref[:, cols]
        out = (dev * lax.rsqrt(var + GN_EPS) * ln[0:1] + ln[1:2] + bo_ref[:, cols]) * g_ref[:, cols]
        o_ref[:, cols] = out.astype(o_ref.dtype)

    _run_interleaved(unit(u) for u in range(SCAN_UNITS))


def rwkv_state_scan(m, c, qp, y0, bonus, g, ln_params):
    t, d = m.shape
    bw = SCAN_UNITS * SCAN_WIDTH
    blk = pl.BlockSpec((SCAN_CHUNK, bw), lambda gi, ci: (ci, gi))
    return pl.pallas_call(
        _rwkv_state_kernel,
        out_shape=jax.ShapeDtypeStruct((t, d), BF16),
        grid=(d // bw, t // SCAN_CHUNK),
        in_specs=[blk] * 6 + [pl.BlockSpec((8, bw), lambda gi, ci: (0, gi))],
        out_specs=blk,
        scratch_shapes=[pltpu.VMEM((SCAN_UNITS, RW_HEAD, SCAN_WIDTH), F32)],
        compiler_params=_cparams(("parallel", "arbitrary")),
        name="rwkv_state_scan",
    )(m, c, qp, y0, bonus, g, ln_params)


def _compress_kernel(x_ref, w1_ref, c_ref, w2_ref, o_ref):
    ab = _dot(x_ref[...], w1_ref[...])
    a = ab[:, :NSA_HD]
    bsh = ab[:, NSA_HD:]
    n16 = a.shape[0]
    row = lax.broadcasted_iota(jnp.int32, a.shape, 0)
    b_next = jnp.where(row == n16 - 1, 0.0, pltpu.roll(bsh, n16 - 1, axis=0))
    hid = a + b_next + c_ref[...]
    hid = jax.nn.gelu(hid, approximate=True)
    o_ref[...] = _dot(hid.astype(BF16), w2_ref[...]).astype(o_ref.dtype)


def compress(x16, w1cat, cvec, w2):
    _, g, n16, kd = x16.shape
    d = NSA_HD
    return pl.pallas_call(
        _compress_kernel,
        out_shape=jax.ShapeDtypeStruct((2, g, n16, d), BF16),
        grid=(2, g),
        in_specs=[pl.BlockSpec((None, None, n16, kd), lambda p, gi: (p, gi, 0, 0)),
                  pl.BlockSpec((None, kd, 2 * d), lambda p, gi: (p, 0, 0)),
                  pl.BlockSpec((None, 1, d), lambda p, gi: (p, 0, 0)),
                  pl.BlockSpec((None, d, d), lambda p, gi: (p, 0, 0))],
        out_specs=pl.BlockSpec((None, None, n16, d), lambda p, gi: (p, gi, 0, 0)),
        compiler_params=_cparams(("parallel", "parallel")),
        name="nsa_compress",
    )(x16, w1cat, cvec, w2)


def _with_ones(v):
    return jnp.concatenate([v, jnp.ones_like(v)], axis=1)


def _nsa_kernel(q_ref, gate_ref, kc_ref, vc_ref, ks_ref, vs_ref, kw_ref, vw_ref, oh_ref,
                gs_ref, ov_ref, bs_ref, bw_ref, o_ref, s_buf, p_buf):
    qi = pl.program_id(1)
    R, Q, D = NSA_REP, QBLK, NSA_HD
    q = q_ref[...]
    qs = jnp.concatenate([q[:, r * D:(r + 1) * D] for r in range(R)], axis=0)
    n_cp = kc_ref.shape[0]
    n_sb = ov_ref.shape[1]
    out = {}

    def compressed_and_topk():
        t_rows = qi * Q + (lax.broadcasted_iota(jnp.int32, (R * Q, 1), 0) & (Q - 1))
        c_idx = lax.broadcasted_iota(jnp.int32, (1, n_cp), 1)
        m_off = qi * (Q // CMP_STRIDE) - lax.broadcasted_iota(jnp.int32, (n_cp, D), 0) + CMP_BAND_LO
        k_idx = lax.broadcasted_iota(jnp.int32, (n_cp, D), 1) & (CMP_BIAS_K - 1)
        in_band = ((k_idx < CMP_BAND) & (m_off == k_idx)) | ((k_idx == CMP_BAND) & (m_off >= CMP_BAND))
        k_c = jnp.concatenate([kc_ref[...], jnp.where(in_band, 1.0, 0.0).astype(BF16)], axis=1)
        qk_c = _dot_nt(jnp.concatenate([qs, gs_ref[...]], axis=1), k_c)
        yield
        m_c = (c_idx * CMP_STRIDE + (CMP_LEN - 1)) <= t_rows
        s_c = jnp.where(m_c, qk_c, NEG_INF)
        e_c = jnp.exp2(s_c - jnp.max(s_c, axis=-1, keepdims=True))
        acc_c = _dot(e_c.astype(BF16), _with_ones(vc_ref[...]))
        inv_l = jnp.where(t_rows >= CMP_LEN - 1, 1.0 / acc_c[:, D:D + 1], 0.0)
        out["o_c"] = acc_c[:, :D] * inv_l
        p_c = e_c * inv_l
        p_sum = p_c[0:Q] + p_c[Q:2 * Q] + p_c[2 * Q:3 * Q] + p_c[3 * Q:4 * Q]
        imp = _dot_exact_rhs(p_sum, ov_ref[...])
        yield
        imp_t = imp.T
        s_row = lax.broadcasted_iota(jnp.int32, (n_sb, Q), 0)
        t_col = qi * Q + lax.broadcasted_iota(jnp.int32, (n_sb, Q), 1)
        forced = (s_row == (t_col >> 6)) | (s_row == 0)
        score = jnp.where(forced, SEL_FORCE, jnp.where(s_row * SEL_BLOCK <= t_col, imp_t, -SEL_FORCE))
        sel_t = jnp.zeros((n_sb, Q), F32)
        s_f = s_row.astype(F32)
        for _ in range(SEL_TOP):
            mx = jnp.max(score, axis=0, keepdims=True)
            first = jnp.min(jnp.where(score == mx, s_f, float(n_sb)), axis=0, keepdims=True)
            pick = s_f == first
            sel_t = jnp.where(pick, 1.0, sel_t)
            score = jnp.where(pick, -3e38, score)
        out["sel"] = sel_t.T

    def window():
        w0 = pl.multiple_of(qi * Q, Q)
        qk_w = _dot_nt(qs, kw_ref[pl.ds(w0, WINDOW + Q), :])
        yield
        j_io = lax.broadcasted_iota(jnp.int32, (1, WINDOW + Q), 1)
        s_w = jnp.where(j_io >= WINDOW - qi * Q, qk_w + bw_ref[...], NEG_INF)
        e_w = jnp.exp2(s_w - jnp.max(s_w, axis=-1, keepdims=True))
        acc_w = _dot(e_w.astype(BF16), _with_ones(vw_ref[pl.ds(w0, WINDOW + Q), :]))
        out["o_w"] = acc_w[:, :D] / acc_w[:, D:]

    _run_interleaved([compressed_and_topk(), window()])
    o_c, o_w, sel = out["o_c"], out["o_w"], out["sel"]

    n_far = jnp.maximum(qi - 1, 0)
    blk = lax.broadcasted_iota(jnp.int32, (Q, n_sb), 1)
    neg_near = jnp.where(sel > 0.5, 0.0, NEG_INF).astype(BF16)
    neg_far = jnp.where((sel > 0.5) & (blk < 2 * n_far), 0.0, NEG_INF).astype(BF16)
    q_near = jnp.concatenate([qs, jnp.concatenate([neg_near] * R, axis=0)], axis=1)
    q_far = jnp.concatenate([qs, jnp.concatenate([neg_far] * R, axis=0)], axis=1)

    nb0 = jnp.maximum(qi - 1, 0)
    n0 = pl.multiple_of(nb0 * Q, Q)

    def sel_keys(start, n):
        return jnp.concatenate([ks_ref[pl.ds(WINDOW + start, n), :], oh_ref[pl.ds(start, n), :]], axis=1)

    s = _dot_nt(q_near, sel_keys(n0, 2 * Q)) + bs_ref[jnp.where(qi == 0, 1, 0)]
    m_0 = jnp.max(s, axis=-1, keepdims=True)
    acc_0 = _dot(jnp.exp2(s - m_0).astype(BF16), _with_ones(vs_ref[pl.ds(WINDOW + n0, 2 * Q), :]))

    n_tiles = (n_far + (SEL_TILE // Q - 1)) // (SEL_TILE // Q)
    n_pairs = (n_tiles + 1) >> 1
    max_tile = oh_ref.shape[0] // SEL_TILE - 1

    def key_tile(i):
        return sel_keys(pl.multiple_of(i * SEL_TILE, SEL_TILE), SEL_TILE)

    def value_tile(i):
        return _with_ones(vs_ref[pl.ds(pl.multiple_of(WINDOW + i * SEL_TILE, SEL_TILE), SEL_TILE), :])

    s_buf[0] = _dot_nt(q_far, key_tile(0))
    p_buf[1] = jnp.zeros(p_buf.shape[1:], p_buf.dtype)

    def half_step(i, slot, carry):
        m_run, acc, alpha_prev = carry
        s_buf[1 - slot] = _dot_nt(q_far, key_tile(jnp.minimum(i + 1, max_tile)))
        acc = alpha_prev * acc + _dot(p_buf[1 - slot], value_tile(jnp.maximum(i - 1, 0)))
        s_i = s_buf[slot]
        m_new = jnp.maximum(m_run, jnp.max(s_i, axis=-1, keepdims=True))
        p_buf[slot] = jnp.exp2(s_i - m_new).astype(p_buf.dtype)
        return m_new, acc, jnp.exp2(m_run - m_new)

    def pair_body(j, carry):
        return half_step(2 * j + 1, 1, half_step(2 * j, 0, carry))

    _, acc_s, alpha_last = lax.fori_loop(0, n_pairs, pair_body, (m_0, acc_0, jnp.ones_like(m_0)))
    acc_s = alpha_last * acc_s + _dot(p_buf[1], value_tile(jnp.maximum(2 * n_pairs - 1, 0)))
    o_s = acc_s[:, :D] / acc_s[:, D:]

    gates = gate_ref[...]
    for r in range(R):
        rows = slice(r * Q, (r + 1) * Q)
        o = (gates[:, 3 * r:3 * r + 1] * o_c[rows] + gates[:, 3 * r + 1:3 * r + 2] * o_s[rows]
             + gates[:, 3 * r + 2:3 * r + 3] * o_w[rows])
        o_ref[:, r * D:(r + 1) * D] = o.astype(o_ref.dtype)


def nsa_attention(q, gates, kc, vc, kv_padded, block_onehot, gs, ov, bs, bw):
    t = q.shape[0]
    g, n_cp, d = kc.shape
    rq = NSA_REP * QBLK
    per_g = lambda shape: pl.BlockSpec((None,) + shape, lambda gi, qi: (gi,) + (0,) * len(shape))
    kv_part = lambda part: pl.BlockSpec((t + WINDOW, d), lambda gi, qi: (0, part * g + gi))
    return pl.pallas_call(
        _nsa_kernel,
        out_shape=jax.ShapeDtypeStruct((t, NSA_HEADS * d), BF16),
        grid=(g, t // QBLK),
        in_specs=[pl.BlockSpec((QBLK, NSA_REP * d), lambda gi, qi: (qi, gi)),
                  pl.BlockSpec((None, QBLK, NSA_REP * N_BRANCH), lambda gi, qi: (gi, qi, 0)),
                  per_g((n_cp, d)), per_g((n_cp, d)),
                  kv_part(2), kv_part(3), kv_part(4), kv_part(5),
                  pl.BlockSpec((t, d), lambda gi, qi: (0, 0)),
                  per_g((rq, d)),
                  pl.BlockSpec(ov.shape, lambda gi, qi: (0, 0)),
                  per_g((2, rq, 2 * QBLK)),
                  per_g((rq, WINDOW + QBLK))],
        out_specs=pl.BlockSpec((QBLK, NSA_REP * d), lambda gi, qi: (qi, gi)),
        scratch_shapes=[pltpu.VMEM((2, rq, SEL_TILE), F32), pltpu.VMEM((2, rq, SEL_TILE), BF16)],
        compiler_params=_cparams(("parallel", "arbitrary")),
        name="nsa_attention",
    )(q, gates, kc, vc, kv_padded, kv_padded, kv_padded, kv_padded, block_onehot, gs, ov, bs, bw)


def _t5_bucket_np(n):
    n = np.maximum(n, 0)
    max_exact = REL_BUCKETS // 2
    nf = np.maximum(n, 1).astype(np.float64)
    large = max_exact + (np.log(nf / max_exact) / math.log(REL_MAX_DIST / max_exact)
                         * (REL_BUCKETS - max_exact)).astype(np.int64)
    large = np.minimum(large, REL_BUCKETS - 1)
    return np.where(n < max_exact, n, large).astype(np.int32)


def _nsa_bias_tables(rel_bias):
    G, R, Q = NSA_GROUPS, NSA_REP, QBLK
    table = (rel_bias.astype(F32) * LOG2E).reshape(REL_BUCKETS, G, R).transpose(1, 2, 0)
    qq = np.arange(Q)

    def lookup(dist):
        return table[:, :, _t5_bucket_np(dist)].reshape(G, R * Q, dist.shape[1])

    m = np.arange(CMP_BAND) - CMP_BAND_LO
    gs = lookup(qq[:, None] - (CMP_LEN - 1) + CMP_STRIDE * m[None, :])
    far = jnp.broadcast_to(table[:, :, REL_BUCKETS - 1][:, :, None, None], (G, R, Q, 1)).reshape(G, R * Q, 1)
    gs = jnp.concatenate([gs, far, jnp.zeros((G, R * Q, CMP_BIAS_K - CMP_BAND - 1), F32)], axis=-1)
    h, mid, l = _split3(gs)
    gs3 = jnp.concatenate([h, mid, l, jnp.zeros_like(h)], axis=-1)
    def toeplitz(c):
        mm = np.arange(2 * Q)
        v = table[:, :, _t5_bucket_np(np.where(mm < Q, c - mm, c + 2 * Q - mm))]
        rows = jnp.tile(v, (1, 1, Q))[:, :, :Q * (2 * Q - 1)].reshape(G, R, Q, 2 * Q - 1)
        return rows[:, :, :, :Q].reshape(G, R * Q, Q)

    d0 = np.tile(qq[:, None] - qq[None, :], (R, 1))
    causal = jnp.asarray(np.where(d0 >= 0, 0.0, NEG_INF), F32)[None]
    b0 = toeplitz(0) + causal
    b1 = toeplitz(Q)
    far_c = jnp.broadcast_to(far, (G, R * Q, Q))
    neg = jnp.full((G, R * Q, Q), NEG_INF, F32)
    bs = jnp.stack([jnp.concatenate([b1 - far_c, b0 - far_c], -1),
                    jnp.concatenate([b0 - far_c, neg], -1)], axis=1)
    oldest = far_c + jnp.asarray(np.where(d0 < 0, 0.0, NEG_INF), F32)[None]
    bw = jnp.concatenate([oldest, far_c, far_c, b1, b0], -1)
    return gs3, bs, bw


def _overlap_table(n_cp, n_sb):
    c = np.arange(n_cp)[:, None] * CMP_STRIDE
    s = np.arange(n_sb)[None, :] * SEL_BLOCK
    return jnp.asarray(((c <= s + SEL_BLOCK - 1) & (c + CMP_LEN - 1 >= s)), BF16)


def _pad_cols(w, n):
    return jnp.pad(w, ((0, 0), (0, n - w.shape[1])))


def _pad_rows(w, n):
    return jnp.pad(w, ((0, n - w.shape[0]), (0, 0)))


def _rwkv_layer(h, layer, norm_g, mu, wr, wk, wv, wo, w0, w1, w2, a0, a1, a2, g1, g2,
                k_k, k_a, r_k, lnx_w, lnx_b, v_first, v0, v1, v2):
    bf = lambda w: w.astype(BF16)
    xr, xw, xk, xv, xa, xg = rwkv_prep(h, norm_g, mu)
    r = matmul(xr, wr, layer=layer)
    k = matmul(xk, wk, layer=layer)
    v = matmul(xv, wv, layer=layer)
    w_lin = lora(xw, bf(_pad_cols(w1, 128)), bf(_pad_rows(w2, 128)), act="tanh")
    a_lin = lora(xa, bf(a1), bf(a2))
    g = lora(xg, bf(g1), bf(g2), act="sigmoid")
    zeros = jnp.zeros_like(w0)
    if v_first is None:
        v_lin, v0 = None, zeros
    else:
        v_lin = lora(xv, bf(_pad_cols(v1, 128)), bf(_pad_rows(v2, 128)))
    params = jnp.stack([w0, a0, v0, k_k, k_a, r_k.reshape(-1), zeros, zeros])
    m, c, qp, y0, bonus = rwkv_chunks(r, k, v, w_lin, a_lin, v_lin, v_first, params)
    ln_params = jnp.stack([lnx_w, lnx_b] + [zeros] * 6)
    gated = rwkv_state_scan(m, c, qp, y0, bonus, g, ln_params)
    h = matmul(gated, wo, layer=layer, residual=h)
    return h, (v if v_first is None else v_first)


def _mlp(h, layer, norm_g, w_up, w_down):
    hid = norm_matmul(h, norm_g, w_up, layer=layer, act="relu2", out_dtype=BF16)
    return matmul(hid, w_down, layer=layer, residual=h)


def _nsa_shared_kv(h, kv_norm_g, w_kv, cmp_pe, cmp_w1, cmp_b1, cmp_w2):
    t = h.shape[0]
    G, D = NSA_GROUPS, NSA_HD
    kv = norm_matmul(h, kv_norm_g, w_kv.astype(BF16), out_dtype=BF16)
    x16 = kv[:, :2 * G * D].reshape(t, 2, G, D).transpose(1, 2, 0, 3).reshape(2, G, t // CMP_STRIDE, CMP_STRIDE * D)
    half = CMP_STRIDE * D
    w1cat = jnp.concatenate([cmp_w1[:, :half], cmp_w1[:, half:]], axis=-1).astype(BF16)
    cvec = jnp.einsum("plk,plkd->pd", cmp_pe, cmp_w1.reshape(2, CMP_LEN, D, D),
                      precision=lax.Precision.HIGHEST) + cmp_b1
    kvc = compress(x16, w1cat, cvec.reshape(2, 1, D), cmp_w2.astype(BF16))
    block_onehot = jnp.asarray(np.arange(t)[:, None] // SEL_BLOCK == np.arange(D)[None, :], BF16)
    return kvc[0], kvc[1], jnp.pad(kv, ((WINDOW, 0), (0, 0))), block_onehot


def _nsa_layer(h, layer, norm_g, wq, wo, kvs, tables):
    t = h.shape[0]
    nq = NSA_HEADS * NSA_HD
    q = norm_matmul(h, norm_g, wq, layer=layer, n=nq, scale=NSA_HD ** -0.5 * LOG2E, out_dtype=BF16)
    gates = norm_matmul(h, norm_g, _pad_cols(wq[layer, :, nq:], 128), act="sigmoid")
    gates = gates[:, :NSA_HEADS * N_BRANCH].reshape(t, NSA_GROUPS, NSA_REP * N_BRANCH).transpose(1, 0, 2)
    o = nsa_attention(q, gates, *kvs, *tables)
    return matmul(o, wo, layer=layer, residual=h)


def kernel(x, a_norm_g, rw_mu, rw_wr, rw_wk, rw_wv, rw_wo, rw_w0, rw_w1, rw_w2, rw_a0, rw_a1, rw_a2,
           rw_v0, rw_v1, rw_v2, rw_g1, rw_g2, rw_kk, rw_ka, rw_rk, rw_lnx_w, rw_lnx_b, kv_norm_g, w_kv,
           cmp_pe, cmp_w1, cmp_b1, cmp_w2, b_norm_g, nsa_wq, nsa_wo, rel_bias, m_norm_g, mlp_up,
           mlp_down, final_norm_g):
    b, t, d = x.shape
    assert b == 1 and d == D_MODEL and t % SEL_TILE == 0 and t // SEL_BLOCK <= 128 and WINDOW == 4 * QBLK
    h = x.reshape(t, d)
    v_first = None
    kvs = tables = None
    rw_w = [w.astype(BF16) for w in (rw_wr, rw_wk, rw_wv, rw_wo)]
    nsa_wq_b, nsa_wo_b, mlp_down_b = nsa_wq.astype(BF16), nsa_wo.astype(BF16), mlp_down.astype(BF16)
    for layer in range(DEPTH):
        if layer < N_A_LAYERS:
            i = layer
            vp = (None, None, None) if i == 0 else (rw_v0[i - 1], rw_v1[i - 1], rw_v2[i - 1])
            h, v_first = _rwkv_layer(
                h, i, a_norm_g[i], rw_mu[i], *rw_w, rw_w0[i], rw_w1[i],
                rw_w2[i], rw_a0[i], rw_a1[i], rw_a2[i], rw_g1[i], rw_g2[i], rw_kk[i], rw_ka[i],
                rw_rk[i].reshape(1, RW_HEADS, RW_HEAD), rw_lnx_w[i], rw_lnx_b[i], v_first, *vp)
        else:
            j = layer - N_A_LAYERS
            if j == 0:
                kvs = _nsa_shared_kv(h, kv_norm_g, w_kv, cmp_pe, cmp_w1, cmp_b1, cmp_w2)
                gs3, bs, bw = _nsa_bias_tables(rel_bias)
                tables = (gs3, _overlap_table(t // CMP_STRIDE, 128), bs, bw)
            h = _nsa_layer(h, j, b_norm_g[j], nsa_wq_b, nsa_wo_b, kvs, tables)
        h = _mlp(h, layer, m_norm_g[layer], mlp_up, mlp_down_b)
    return rmsnorm(h, final_norm_g, out_dtype=x.dtype).reshape(b, t, d)
```

```python
import functools
import math

import numpy as np
import jax
import jax.numpy as jnp
from jax import lax
from jax.experimental import pallas as pl
from jax.experimental.pallas import tpu as pltpu

F32 = jnp.float32
BF16 = jnp.bfloat16

D_MODEL = 2048
DEPTH = 4
N_A_LAYERS = 2
RW_HEAD = 64
RW_HEADS = D_MODEL // RW_HEAD
N_SHIFT_MIX = 6
GN_EPS = 64e-5
NSA_HEADS = 16
NSA_GROUPS = 4
NSA_REP = NSA_HEADS // NSA_GROUPS
NSA_HD = D_MODEL // NSA_HEADS
N_KV_PARTS = 6
CMP_STRIDE = 16
CMP_LEN = 32
SEL_BLOCK = 64
SEL_TOP = 16
WINDOW = 512
QBLK = 128
N_BRANCH = 3
SEL_FORCE = 1e4
NEG_INF = -1e30
REL_BUCKETS = 32
REL_MAX_DIST = 128
NORM_EPS = 1e-6
LOG2E = math.log2(math.e)

V7X_VMEM_BYTES = 64 * 1024 * 1024
VMEM_LIMIT_BYTES = V7X_VMEM_BYTES - 8 * 1024 * 1024
NSA_VMEM_LIMIT_BYTES = V7X_VMEM_BYTES - 4 * 1024 * 1024
SCAN_CHUNK = 64
SCAN_HEADS = 4
SCAN_WIDTH = SCAN_HEADS * RW_HEAD
SCAN_UNITS = 8
SEL_TILE = 512
NSA_STEP_GROUPS = 2
CMP_BAND = 16
CMP_BAND_LO = 6
CMP_BIAS_K = 32
assert CMP_BAND < CMP_BIAS_K and 4 * CMP_BIAS_K == NSA_HD and NSA_HD == 128


def _cparams(sem, vmem_limit_bytes=VMEM_LIMIT_BYTES):
    return pltpu.CompilerParams(dimension_semantics=sem, vmem_limit_bytes=vmem_limit_bytes)


def _split3(x):
    h = x.astype(BF16)
    r = x - h.astype(F32)
    m = r.astype(BF16)
    l = (r - m.astype(F32)).astype(BF16)
    return h, m, l


def _dot(a, b):
    return jnp.dot(a, b, preferred_element_type=F32)


def _dot_nt(a, b):
    return lax.dot_general(a, b, (((1,), (1,)), ((), ())), preferred_element_type=F32)


def _dot_tn(a, b):
    return lax.dot_general(a, b, (((0,), (0,)), ((), ())), preferred_element_type=F32)


def _dot_exact_lhs(a_bf16, x):
    h, m, l = _split3(x)
    return _dot(a_bf16, h) + _dot(a_bf16, m) + _dot(a_bf16, l)


def _dot_exact_rhs(x, b_bf16):
    h, m, l = _split3(x)
    return _dot(h, b_bf16) + _dot(m, b_bf16) + _dot(l, b_bf16)


def _activate(acc, act, scale):
    if act == "relu2":
        acc = jnp.square(jnp.maximum(acc, 0.0))
    elif act == "tanh":
        acc = jnp.tanh(acc)
    elif act == "sigmoid":
        acc = jax.nn.sigmoid(acc)
    return acc if scale is None else acc * scale


def _matmul_kernel(x_ref, w_ref, *rest, act, scale, has_res, nk):
    if has_res:
        res_ref, o_ref, *scratch = rest
    else:
        o_ref, *scratch = rest

    def epilogue(acc):
        acc = _activate(acc, act, scale)
        if has_res:
            acc = acc + res_ref[...]
        o_ref[...] = acc.astype(o_ref.dtype)

    if nk == 1:
        epilogue(_dot(x_ref[...], w_ref[...]))
    else:
        acc_ref, = scratch
        k = pl.program_id(2)

        @pl.when(k == 0)
        def _():
            acc_ref[...] = jnp.zeros_like(acc_ref)

        acc_ref[...] += _dot(x_ref[...], w_ref[...])

        @pl.when(k == nk - 1)
        def _():
            epilogue(acc_ref[...])


def _pick_tile(n, pref):
    t = min(n, pref)
    while n % t:
        t //= 2
    return t


def _weight_spec(w, layer, rows, cols, index):
    if w.ndim == 2:
        return pl.BlockSpec((rows, cols), index)
    return pl.BlockSpec((None, rows, cols), lambda *g: (layer,) + tuple(index(*g)))


def matmul(x, w, *, layer=None, act=None, scale=None, residual=None, out_dtype=F32, n=None,
           tm=1024, tn=512, tk=2048):
    m, kdim = x.shape
    n = w.shape[-1] if n is None else n
    tm, tn, tk = _pick_tile(m, tm), _pick_tile(n, tn), _pick_tile(kdim, tk)
    nk = kdim // tk
    in_specs = [pl.BlockSpec((tm, tk), lambda j, i, k: (i, k)),
                _weight_spec(w, layer, tk, tn, lambda j, i, k: (k, j))]
    args = [x, w]
    if residual is not None:
        in_specs.append(pl.BlockSpec((tm, tn), lambda j, i, k: (i, j)))
        args.append(residual)
    return pl.pallas_call(
        functools.partial(_matmul_kernel, act=act, scale=scale, has_res=residual is not None, nk=nk),
        out_shape=jax.ShapeDtypeStruct((m, n), out_dtype),
        grid=(n // tn, m // tm, nk),
        in_specs=in_specs,
        out_specs=pl.BlockSpec((tm, tn), lambda j, i, k: (i, j)),
        scratch_shapes=[pltpu.VMEM((tm, tn), F32)] if nk > 1 else [],
        compiler_params=_cparams(("parallel", "parallel", "arbitrary")),
        name="matmul",
    )(*args)


def _lora_kernel(x_ref, w1_ref, w2_ref, o_ref, *, act):
    hidden = _activate(_dot(x_ref[...], w1_ref[...]), act, None).astype(BF16)
    o_ref[...] = _dot(hidden, w2_ref[...])


def lora(x, w1, w2, *, act=None, tm=1024):
    m, kdim = x.shape
    r, n = w2.shape
    tm = _pick_tile(m, tm)
    return pl.pallas_call(
        functools.partial(_lora_kernel, act=act),
        out_shape=jax.ShapeDtypeStruct((m, n), F32),
        grid=(m // tm,),
        in_specs=[pl.BlockSpec((tm, kdim), lambda i: (i, 0)),
                  pl.BlockSpec((kdim, r), lambda i: (0, 0)),
                  pl.BlockSpec((r, n), lambda i: (0, 0))],
        out_specs=pl.BlockSpec((tm, n), lambda i: (i, 0)),
        compiler_params=_cparams(("parallel",)),
        name="lora",
    )(x, w1, w2)


def _rms(x, g):
    return x * lax.rsqrt(jnp.mean(x * x, axis=-1, keepdims=True) + NORM_EPS) * g


def _rmsnorm_kernel(h_ref, g_ref, o_ref):
    o_ref[...] = _rms(h_ref[...], g_ref[...]).astype(o_ref.dtype)


def rmsnorm(h, g, out_dtype=BF16, tm=512):
    t, d = h.shape
    tm = _pick_tile(t, tm)
    return pl.pallas_call(
        _rmsnorm_kernel,
        out_shape=jax.ShapeDtypeStruct((t, d), out_dtype),
        grid=(t // tm,),
        in_specs=[pl.BlockSpec((tm, d), lambda i: (i, 0)), pl.BlockSpec((1, d), lambda i: (0, 0))],
        out_specs=pl.BlockSpec((tm, d), lambda i: (i, 0)),
        compiler_params=_cparams(("parallel",)),
        name="rmsnorm",
    )(h, g.reshape(1, d))


def _norm_matmul_kernel(h_ref, g_ref, w_ref, o_ref, xn_ref, *, act, scale):
    @pl.when(pl.program_id(1) == 0)
    def _():
        xn_ref[...] = _rms(h_ref[...], g_ref[...]).astype(xn_ref.dtype)

    o_ref[...] = _activate(_dot(xn_ref[...], w_ref[...].astype(BF16)), act, scale).astype(o_ref.dtype)


def norm_matmul(h, g, w, *, layer=None, act=None, scale=None, out_dtype=F32, n=None, tm=1024, tn=512):
    m, kdim = h.shape
    n = w.shape[-1] if n is None else n
    tm, tn = _pick_tile(m, tm), _pick_tile(n, tn)
    return pl.pallas_call(
        functools.partial(_norm_matmul_kernel, act=act, scale=scale),
        out_shape=jax.ShapeDtypeStruct((m, n), out_dtype),
        grid=(m // tm, n // tn),
        in_specs=[pl.BlockSpec((tm, kdim), lambda i, j: (i, 0)),
                  pl.BlockSpec((1, kdim), lambda i, j: (0, 0)),
                  _weight_spec(w, layer, kdim, tn, lambda i, j: (0, j))],
        out_specs=pl.BlockSpec((tm, tn), lambda i, j: (i, j)),
        scratch_shapes=[pltpu.VMEM((tm, kdim), BF16)],
        compiler_params=_cparams(("parallel", "arbitrary")),
        name="norm_matmul",
    )(h, g.reshape(1, kdim), w)


def _rwkv_prep_kernel(h_ref, prev_ref, g_ref, mu_ref, *o_refs):
    g = g_ref[...]
    xn = _rms(h_ref[...], g)
    prev = _rms(prev_ref[...], g)[7:8, :] * (pl.program_id(0) > 0).astype(F32)
    row = lax.broadcasted_iota(jnp.int32, xn.shape, 0)
    shifted = jnp.where(row == 0, prev, pltpu.roll(xn, 1, axis=0))
    xx = shifted - xn
    for i, o_ref in enumerate(o_refs):
        o_ref[...] = (xn + xx * mu_ref[i:i + 1, :]).astype(o_ref.dtype)


def rwkv_prep(h, g, mu, tm=256):
    t, d = h.shape
    tm = _pick_tile(t, tm)
    blk = pl.BlockSpec((tm, d), lambda i: (i, 0))
    return pl.pallas_call(
        _rwkv_prep_kernel,
        out_shape=[jax.ShapeDtypeStruct((t, d), BF16)] * N_SHIFT_MIX,
        grid=(t // tm,),
        in_specs=[blk,
                  pl.BlockSpec((8, d), lambda i: (jnp.maximum(i * (tm // 8) - 1, 0), 0)),
                  pl.BlockSpec((1, d), lambda i: (0, 0)),
                  pl.BlockSpec((N_SHIFT_MIX, d), lambda i: (0, 0))],
        out_specs=[blk] * N_SHIFT_MIX,
        compiler_params=_cparams(("parallel",)),
        name="rwkv_prep",
    )(h, h, g.reshape(1, d), mu)


def _head_mask(dtype):
    row = lax.broadcasted_iota(jnp.int32, (SCAN_WIDTH, SCAN_WIDTH), 0)
    col = lax.broadcasted_iota(jnp.int32, (SCAN_WIDTH, SCAN_WIDTH), 1)
    return jnp.where((row >> 6) == (col >> 6), 1.0, 0.0).astype(dtype)


def _block_diag(x, mask):
    return jnp.concatenate([x] * SCAN_HEADS, axis=0) * mask


def _fold_rows(x_full, mask_f32):
    x = x_full * mask_f32
    L = SCAN_CHUNK
    return x[0:L] + x[L:2 * L] + x[2 * L:3 * L] + x[3 * L:4 * L]


def _split2(x):
    hi = x.astype(BF16)
    return hi, (x - hi.astype(F32)).astype(BF16)


def _head_sums(xs, mask_b):
    n = xs[0].shape[0]
    r = _dot(jnp.concatenate([piece for x in xs for piece in _split3(x)], axis=0), mask_b)
    return [r[3 * i * n:(3 * i + 1) * n] + r[(3 * i + 1) * n:(3 * i + 2) * n] + r[(3 * i + 2) * n:(3 * i + 3) * n]
            for i in range(len(xs))]


def _bd_parts(y_split, mask_b):
    hi, lo = y_split
    return _block_diag(hi, mask_b), _block_diag(lo, mask_b)


def _mm_heads(x_split, y_parts):
    x_hi, x_lo = x_split
    y_hi, y_lo = y_parts
    n = x_hi.shape[0]
    both = _dot(jnp.concatenate([x_hi, x_lo], axis=0), y_hi)
    return both[:n] + both[n:] + _dot(x_hi, y_lo)


def _run_interleaved(stage_generators):
    live = list(stage_generators)
    while live:
        still = []
        for gen in live:
            try:
                next(gen)
                still.append(gen)
            except StopIteration:
                pass
        live = still


def _tri_inverse_heads(a, j_row, m_col, mask_b):
    L = SCAN_CHUNK
    eye = jnp.where(m_col == j_row, 1.0, 0.0)
    d = jnp.where((j_row >> 4) == (m_col >> 4), a, 0.0)
    e = a - d
    p = eye + d
    d_split = _split2(d)
    x = _mm_heads(d_split, _bd_parts(d_split, mask_b))
    yield
    for _ in range(2):
        x_split, p_split = _split2(x), _split2(p)
        stacked = tuple(jnp.concatenate([xs, ps], axis=0) for xs, ps in zip(x_split, p_split))
        both = _mm_heads(stacked, _bd_parts(x_split, mask_b))
        x, p = both[:L], p + both[L:]
        yield
    p = p + _mm_heads(_split2(p), _bd_parts(_split2(x), mask_b))
    yield
    p_split = _split2(p)
    n1 = _mm_heads(p_split, _bd_parts(_split2(e), mask_b))
    yield
    n1_split = _split2(n1)
    n2 = _mm_heads(n1_split, _bd_parts(n1_split, mask_b))
    yield
    q = eye + n1
    q = q + _mm_heads(_split2(q), _bd_parts(_split2(n2), mask_b))
    yield
    return _mm_heads(_split2(q), _bd_parts(p_split, mask_b))


def _softplus(x):
    return jnp.maximum(x, 0.0) + jnp.log(1.0 + jnp.exp(-jnp.abs(x)))


def _rwkv_chunk_kernel(*refs, mix_v):
    if mix_v:
        r_ref, k_ref, v_ref, wl_ref, al_ref, vl_ref, vf_ref, par_ref, m_ref, c_ref, qp_ref, y0_ref, bo_ref = refs
    else:
        r_ref, k_ref, v_ref, wl_ref, al_ref, par_ref, m_ref, c_ref, qp_ref, y0_ref, bo_ref = refs
    L, W = SCAN_CHUNK, SCAN_WIDTH
    mask_b = _head_mask(BF16)
    mask_f = _head_mask(F32)
    tri_incl = jnp.where(lax.broadcasted_iota(jnp.int32, (L, L), 1)
                         <= lax.broadcasted_iota(jnp.int32, (L, L), 0), 1.0, 0.0).astype(BF16)
    j_row = lax.broadcasted_iota(jnp.int32, (L, W), 0)
    m_col = lax.broadcasted_iota(jnp.int32, (L, W), 1) & (L - 1)
    strict = m_col < j_row
    incl = m_col <= j_row

    def unit(u):
        cols = slice(u * W, (u + 1) * W)
        par = par_ref[:, cols]
        w0, a0, v0, k_k, k_a, r_k = (par[i:i + 1] for i in range(6))
        r, k, v = r_ref[:, cols], k_ref[:, cols], v_ref[:, cols]
        lw = -jnp.exp(-_softplus(-(w0 + wl_ref[:, cols])) - 0.5)
        a_s = jax.nn.sigmoid(a0 + al_ref[:, cols])
        if mix_v:
            v = v + (vf_ref[:, cols] - v) * jax.nn.sigmoid(v0 + vl_ref[:, cols])
        kk = k * k_k
        k = k * (1.0 + (a_s - 1.0) * k_a)
        kk_sq, rk_sum = _head_sums([kk * kk, r * k * r_k], mask_b)
        c = _dot_exact_lhs(tri_incl, lw)
        yield
        kk = kk * lax.rsqrt(jnp.maximum(kk_sq, 1e-24))
        bo_ref[:, cols] = rk_sum * v
        c_last = c[L - 1:L, :]
        p_inv = jnp.exp(-c)
        p_rest = jnp.exp(c_last - c)
        rt = r * jnp.exp(c)
        at = (-kk * jnp.exp(c - lw)).astype(BF16)
        b = kk * a_s
        vb = v.astype(BF16)
        ar = jnp.concatenate([at, rt.astype(BF16)], axis=0)
        g_b = _dot_nt(ar, _block_diag((b * p_inv).astype(BF16), mask_b))
        g_k = _dot_nt(ar, _block_diag((k * p_inv).astype(BF16), mask_b))
        yield
        a_ab = jnp.where(strict, g_b[:L], 0.0)
        a_ak = jnp.where(strict, g_k[:L], 0.0).astype(BF16)
        a_rb = jnp.where(incl, g_b[L:], 0.0).astype(BF16)
        a_rk = jnp.where(incl, g_k[L:], 0.0).astype(BF16)
        v_bd = _block_diag(vb, mask_b)
        akv = _dot(a_ak, v_bd)
        t_inv = yield from _tri_inverse_heads(a_ab, j_row, m_col, mask_b)
        yield
        t_split = _split2(t_inv)
        w_both = _dot(jnp.concatenate(t_split, axis=0), _block_diag(at, mask_b))
        u0 = _mm_heads(t_split, _bd_parts(_split2(akv), mask_b)).astype(BF16)
        yield
        w = (w_both[:L] + w_both[L:]).astype(BF16)
        qp_ref[:, cols] = (rt + _dot(a_rb, _block_diag(w, mask_b))).astype(qp_ref.dtype)
        y0_ref[:, cols] = _dot(a_rb, _block_diag(u0, mask_b)) + _dot(a_rk, v_bd)
        bh = (b * p_rest).astype(BF16)
        kh = (k * p_rest).astype(BF16)
        decay_diag = jnp.where(m_col == j_row, jnp.exp(c_last), 0.0)
        m_ref[:, cols] = _fold_rows(_dot_tn(bh, w), mask_f) + decay_diag
        c_ref[:, cols] = _fold_rows(_dot_tn(bh, u0) + _dot_tn(kh, vb), mask_f)

    _run_interleaved(unit(u) for u in range(SCAN_UNITS))


def rwkv_chunks(r, k, v, w_lin, a_lin, v_lin, v_first, params):
    t, d = r.shape
    bw = SCAN_UNITS * SCAN_WIDTH
    blk = pl.BlockSpec((SCAN_CHUNK, bw), lambda ci, gi: (ci, gi))
    mix_v = v_lin is not None
    args = [r, k, v, w_lin, a_lin] + ([v_lin, v_first] if mix_v else []) + [params]
    f32 = jax.ShapeDtypeStruct((t, d), F32)
    return pl.pallas_call(
        functools.partial(_rwkv_chunk_kernel, mix_v=mix_v),
        out_shape=[f32, f32, jax.ShapeDtypeStruct((t, d), BF16), f32, f32],
        grid=(t // SCAN_CHUNK, d // bw),
        in_specs=[blk] * (len(args) - 1) + [pl.BlockSpec((8, bw), lambda ci, gi: (0, gi))],
        out_specs=[blk] * 5,
        compiler_params=_cparams(("parallel", "parallel")),
        name="rwkv_chunks",
    )(*args)


def _rwkv_state_kernel(m_ref, c_ref, qp_ref, y0_ref, bo_ref, g_ref, ln_ref, o_ref, st_ref):
    @pl.when(pl.program_id(1) == 0)
    def _():
        st_ref[...] = jnp.zeros_like(st_ref)

    W = SCAN_WIDTH
    mask_b = _head_mask(BF16)
    def unit(u):
        cols = slice(u * W, (u + 1) * W)
        st_parts = _bd_parts(_split2(st_ref[u]), mask_b)
        y = _dot(qp_ref[:, cols], st_parts[0]) + y0_ref[:, cols]
        st_ref[u] = _mm_heads(_split2(m_ref[:, cols]), st_parts) + c_ref[:, cols]
        yield
        mean = _head_sums([y], mask_b)[0] * (1.0 / RW_HEAD)
        yield
        dev = y - mean
        var = _head_sums([dev * dev], mask_b)[0] * (1.0 / RW_HEAD)
        yield
        ln = ln_ref[:, cols]
        out = (dev * lax.rsqrt(var + GN_EPS) * ln[0:1] + ln[1:2] + bo_ref[:, cols]) * g_ref[:, cols]
        o_ref[:, cols] = out.astype(o_ref.dtype)

    _run_interleaved(unit(u) for u in range(SCAN_UNITS))


def rwkv_state_scan(m, c, qp, y0, bonus, g, ln_params):
    t, d = m.shape
    bw = SCAN_UNITS * SCAN_WIDTH
    blk = pl.BlockSpec((SCAN_CHUNK, bw), lambda gi, ci: (ci, gi))
    return pl.pallas_call(
        _rwkv_state_kernel,
        out_shape=jax.ShapeDtypeStruct((t, d), BF16),
        grid=(d // bw, t // SCAN_CHUNK),
        in_specs=[blk] * 6 + [pl.BlockSpec((8, bw), lambda gi, ci: (0, gi))],
        out_specs=blk,
        scratch_shapes=[pltpu.VMEM((SCAN_UNITS, RW_HEAD, SCAN_WIDTH), F32)],
        compiler_params=_cparams(("parallel", "arbitrary")),
        name="rwkv_state_scan",
    )(m, c, qp, y0, bonus, g, ln_params)


def _compress_kernel(x_ref, w1_ref, c_ref, w2_ref, o_ref):
    ab = _dot(x_ref[...], w1_ref[...])
    a = ab[:, :NSA_HD]
    bsh = ab[:, NSA_HD:]
    n16 = a.shape[0]
    row = lax.broadcasted_iota(jnp.int32, a.shape, 0)
    b_next = jnp.where(row == n16 - 1, 0.0, pltpu.roll(bsh, n16 - 1, axis=0))
    hid = a + b_next + c_ref[...]
    hid = jax.nn.gelu(hid, approximate=True)
    o_ref[...] = _dot(hid.astype(BF16), w2_ref[...]).astype(o_ref.dtype)


def compress(x16, w1cat, cvec, w2):
    _, g, n16, kd = x16.shape
    d = NSA_HD
    return pl.pallas_call(
        _compress_kernel,
        out_shape=jax.ShapeDtypeStruct((2, g, n16, d), BF16),
        grid=(2, g),
        in_specs=[pl.BlockSpec((None, None, n16, kd), lambda p, gi: (p, gi, 0, 0)),
                  pl.BlockSpec((None, kd, 2 * d), lambda p, gi: (p, 0, 0)),
                  pl.BlockSpec((None, 1, d), lambda p, gi: (p, 0, 0)),
                  pl.BlockSpec((None, d, d), lambda p, gi: (p, 0, 0))],
        out_specs=pl.BlockSpec((None, None, n16, d), lambda p, gi: (p, gi, 0, 0)),
        compiler_params=_cparams(("parallel", "parallel")),
        name="nsa_compress",
    )(x16, w1cat, cvec, w2)


def _with_ones(v):
    return jnp.concatenate([v, jnp.ones_like(v)], axis=1)


def _nsa_kernel(q_ref, gate_ref, kc_ref, vc_ref, ks_ref, vs_ref, kw_ref, vw_ref, oh_ref,
                gs_ref, ov_ref, bs_ref, o_ref, s_buf, p_buf):
    qi = pl.program_id(1)
    R, Q, D, NG = NSA_REP, QBLK, NSA_HD, NSA_STEP_GROUPS
    n_cp = kc_ref.shape[1]
    n_sb = ov_ref.shape[1]
    n_far = jnp.maximum(qi - 1, 0)
    n0 = pl.multiple_of(jnp.maximum(qi - 1, 0) * Q, Q)
    n_tiles = (n_far + (SEL_TILE // Q - 1)) // (SEL_TILE // Q)
    n_pairs = (n_tiles + 1) >> 1
    max_tile = oh_ref.shape[0] // SEL_TILE - 1
    state = [{} for _ in range(NG)]

    def slab(ref, gl, start, n):
        return ref[pl.ds(start, n), gl * D:(gl + 1) * D]

    def sel_keys(gl, start, n):
        return jnp.concatenate([slab(ks_ref, gl, WINDOW + start, n), oh_ref[pl.ds(start, n), :]], axis=1)

    def key_tile(gl, i):
        return sel_keys(gl, pl.multiple_of(i * SEL_TILE, SEL_TILE), SEL_TILE)

    def value_tile(gl, i):
        return _with_ones(slab(vs_ref, gl, pl.multiple_of(WINDOW + i * SEL_TILE, SEL_TILE), SEL_TILE))

    def before_loop(gl):
        st = state[gl]
        q = q_ref[:, gl * R * D:(gl + 1) * R * D]
        qs = jnp.concatenate([q[:, r * D:(r + 1) * D] for r in range(R)], axis=0)
        t_rows = qi * Q + (lax.broadcasted_iota(jnp.int32, (R * Q, 1), 0) & (Q - 1))
        c_idx = lax.broadcasted_iota(jnp.int32, (1, n_cp), 1)
        m_off = qi * (Q // CMP_STRIDE) - lax.broadcasted_iota(jnp.int32, (n_cp, D), 0) + CMP_BAND_LO
        k_idx = lax.broadcasted_iota(jnp.int32, (n_cp, D), 1) & (CMP_BIAS_K - 1)
        in_band = ((k_idx < CMP_BAND) & (m_off == k_idx)) | ((k_idx == CMP_BAND) & (m_off >= CMP_BAND))
        k_c = jnp.concatenate([kc_ref[gl], jnp.where(in_band, 1.0, 0.0).astype(BF16)], axis=1)
        qk_c = _dot_nt(jnp.concatenate([qs, gs_ref[gl]], axis=1), k_c)
        w0 = pl.multiple_of(qi * Q, Q)
        qk_w = _dot_nt(qs, slab(kw_ref, gl, w0, WINDOW + Q))
        yield
        m_c = (c_idx * CMP_STRIDE + (CMP_LEN - 1)) <= t_rows
        s_c = jnp.where(m_c, qk_c, NEG_INF)
        e_c = jnp.exp2(s_c - jnp.max(s_c, axis=-1, keepdims=True))
        acc_c = _dot(e_c.astype(BF16), _with_ones(vc_ref[gl]))
        yield
        inv_l = jnp.where(t_rows >= CMP_LEN - 1, 1.0 / acc_c[:, D:D + 1], 0.0)
        st["o_c"] = acc_c[:, :D] * inv_l
        p_c = e_c * inv_l
        p_sum = p_c[0:Q] + p_c[Q:2 * Q] + p_c[2 * Q:3 * Q] + p_c[3 * Q:4 * Q]
        imp = _dot_exact_rhs(p_sum, ov_ref[...])
        j_io = lax.broadcasted_iota(jnp.int32, (1, WINDOW + Q), 1)
        oldest = jnp.where(lax.broadcasted_iota(jnp.int32, (R * Q, Q), 1) > (t_rows & (Q - 1)), 0.0, NEG_INF)
        bias_w = jnp.concatenate([oldest, jnp.zeros((R * Q, WINDOW - 2 * Q), F32), bs_ref[gl, 0]], axis=1)
        s_w = jnp.where(j_io >= WINDOW - qi * Q, qk_w + bias_w, NEG_INF)
        e_w = jnp.exp2(s_w - jnp.max(s_w, axis=-1, keepdims=True))
        acc_w = _dot(e_w.astype(BF16), _with_ones(slab(vw_ref, gl, w0, WINDOW + Q)))
        yield
        st["o_w"] = acc_w[:, :D] / acc_w[:, D:]
        imp_t = imp.T
        s_row = lax.broadcasted_iota(jnp.int32, (n_sb, Q), 0)
        t_col = qi * Q + lax.broadcasted_iota(jnp.int32, (n_sb, Q), 1)
        forced = (s_row == (t_col >> 6)) | (s_row == 0)
        score = jnp.where(forced, SEL_FORCE, jnp.where(s_row * SEL_BLOCK <= t_col, imp_t, -SEL_FORCE))
        sel_t = jnp.zeros((n_sb, Q), F32)
        s_f = s_row.astype(F32)
        for _ in range(SEL_TOP):
            mx = jnp.max(score, axis=0, keepdims=True)
            first = jnp.min(jnp.where(score == mx, s_f, float(n_sb)), axis=0, keepdims=True)
            pick = s_f == first
            sel_t = jnp.where(pick, 1.0, sel_t)
            score = jnp.where(pick, -3e38, score)
        sel = sel_t.T
        blk = lax.broadcasted_iota(jnp.int32, (Q, n_sb), 1)
        neg_near = jnp.where(sel > 0.5, 0.0, NEG_INF).astype(BF16)
        neg_far = jnp.where((sel > 0.5) & (blk < 2 * n_far), 0.0, NEG_INF).astype(BF16)
        q_near = jnp.concatenate([qs, jnp.concatenate([neg_near] * R, axis=0)], axis=1)
        st["q_far"] = jnp.concatenate([qs, jnp.concatenate([neg_far] * R, axis=0)], axis=1)
        s = _dot_nt(q_near, sel_keys(gl, n0, 2 * Q)) + bs_ref[gl, jnp.where(qi == 0, 1, 0)]
        s_buf[gl, 0] = _dot_nt(st["q_far"], key_tile(gl, 0))
        p_buf[gl, 1] = jnp.zeros(p_buf.shape[2:], p_buf.dtype)
        yield
        st["m"] = jnp.max(s, axis=-1, keepdims=True)
        st["acc"] = _dot(jnp.exp2(s - st["m"]).astype(BF16), _with_ones(slab(vs_ref, gl, WINDOW + n0, 2 * Q)))

    _run_interleaved([before_loop(gl) for gl in range(NG)])

    def half_step(gl, i, slot, carry):
        m_run, acc, alpha_prev = carry
        s_buf[gl, 1 - slot] = _dot_nt(state[gl]["q_far"], key_tile(gl, jnp.minimum(i + 1, max_tile)))
        acc = alpha_prev * acc + _dot(p_buf[gl, 1 - slot], value_tile(gl, jnp.maximum(i - 1, 0)))
        s_i = s_buf[gl, slot]
        m_new = jnp.maximum(m_run, jnp.max(s_i, axis=-1, keepdims=True))
        p_buf[gl, slot] = jnp.exp2(s_i - m_new).astype(p_buf.dtype)
        return m_new, acc, jnp.exp2(m_run - m_new)

    def far_loop(gl):
        st = state[gl]
        pair_body = lambda j, carry: half_step(gl, 2 * j + 1, 1, half_step(gl, 2 * j, 0, carry))
        return lax.fori_loop(0, n_pairs, pair_body, (st["m"], st["acc"], jnp.ones_like(st["m"])))

    carries = [far_loop(gl) for gl in range(NG)]
    last_tile = jnp.maximum(2 * n_pairs - 1, 0)
    for gl, (_, acc_s, alpha_last) in enumerate(carries):
        acc_s = alpha_last * acc_s + _dot(p_buf[gl, 1], value_tile(gl, last_tile))
        o_s = acc_s[:, :D] / acc_s[:, D:]
        o_c, o_w = state[gl]["o_c"], state[gl]["o_w"]
        gates = gate_ref[gl]
        for r in range(R):
            rows = slice(r * Q, (r + 1) * Q)
            o = (gates[:, 3 * r:3 * r + 1] * o_c[rows] + gates[:, 3 * r + 1:3 * r + 2] * o_s[rows]
                 + gates[:, 3 * r + 2:3 * r + 3] * o_w[rows])
            o_ref[:, (gl * R + r) * D:(gl * R + r + 1) * D] = o.astype(o_ref.dtype)


def nsa_attention(q, gates, kc, vc, kv_padded, block_onehot, gs, ov, bs):
    t = q.shape[0]
    g, n_cp, d = kc.shape
    rq = NSA_REP * QBLK
    ng = NSA_STEP_GROUPS
    assert g % ng == 0
    per_g = lambda shape: pl.BlockSpec((ng,) + shape, lambda gi, qi: (gi,) + (0,) * len(shape))
    kv_part = lambda part: pl.BlockSpec((t + WINDOW, ng * d), lambda gi, qi: (0, part * g // ng + gi))
    return pl.pallas_call(
        _nsa_kernel,
        out_shape=jax.ShapeDtypeStruct((t, NSA_HEADS * d), BF16),
        grid=(g // ng, t // QBLK),
        in_specs=[pl.BlockSpec((QBLK, ng * NSA_REP * d), lambda gi, qi: (qi, gi)),
                  pl.BlockSpec((ng, QBLK, NSA_REP * N_BRANCH), lambda gi, qi: (gi, qi, 0)),
                  per_g((n_cp, d)), per_g((n_cp, d)),
                  kv_part(2), kv_part(3), kv_part(4), kv_part(5),
                  pl.BlockSpec((t, d), lambda gi, qi: (0, 0)),
                  per_g((rq, d)),
                  pl.BlockSpec(ov.shape, lambda gi, qi: (0, 0)),
                  per_g((2, rq, 2 * QBLK))],
        out_specs=pl.BlockSpec((QBLK, ng * NSA_REP * d), lambda gi, qi: (qi, gi)),
        scratch_shapes=[pltpu.VMEM((ng, 2, rq, SEL_TILE), F32), pltpu.VMEM((ng, 2, rq, SEL_TILE), BF16)],
        compiler_params=_cparams(("parallel", "arbitrary"), NSA_VMEM_LIMIT_BYTES),
        name="nsa_attention",
    )(q, gates, kc, vc, kv_padded, kv_padded, kv_padded, kv_padded, block_onehot, gs, ov, bs)


def _t5_bucket_np(n):
    n = np.maximum(n, 0)
    max_exact = REL_BUCKETS // 2
    nf = np.maximum(n, 1).astype(np.float64)
    large = max_exact + (np.log(nf / max_exact) / math.log(REL_MAX_DIST / max_exact)
                         * (REL_BUCKETS - max_exact)).astype(np.int64)
    large = np.minimum(large, REL_BUCKETS - 1)
    return np.where(n < max_exact, n, large).astype(np.int32)


def _nsa_bias_tables(rel_bias):
    G, R, Q = NSA_GROUPS, NSA_REP, QBLK
    table = (rel_bias.astype(F32) * LOG2E).reshape(REL_BUCKETS, G, R).transpose(1, 2, 0)
    qq = np.arange(Q)

    def lookup(dist):
        return table[:, :, _t5_bucket_np(dist)].reshape(G, R * Q, dist.shape[1])

    m = np.arange(CMP_BAND) - CMP_BAND_LO
    gs = lookup(qq[:, None] - (CMP_LEN - 1) + CMP_STRIDE * m[None, :])
    far = jnp.broadcast_to(table[:, :, REL_BUCKETS - 1][:, :, None, None], (G, R, Q, 1)).reshape(G, R * Q, 1)
    gs = jnp.concatenate([gs, far, jnp.zeros((G, R * Q, CMP_BIAS_K - CMP_BAND - 1), F32)], axis=-1)
    h, mid, l = _split3(gs)
    gs3 = jnp.concatenate([h, mid, l, jnp.zeros_like(h)], axis=-1)
    def toeplitz(c):
        mm = np.arange(2 * Q)
        v = table[:, :, _t5_bucket_np(np.where(mm < Q, c - mm, c + 2 * Q - mm))]
        rows = jnp.tile(v, (1, 1, Q))[:, :, :Q * (2 * Q - 1)].reshape(G, R, Q, 2 * Q - 1)
        return rows[:, :, :, :Q].reshape(G, R * Q, Q)

    d0 = np.tile(qq[:, None] - qq[None, :], (R, 1))
    causal = jnp.asarray(np.where(d0 >= 0, 0.0, NEG_INF), F32)[None]
    b0 = toeplitz(0) + causal
    b1 = toeplitz(Q)
    far_c = jnp.broadcast_to(far, (G, R * Q, Q))
    neg = jnp.full((G, R * Q, Q), NEG_INF, F32)
    bs = jnp.stack([jnp.concatenate([b1 - far_c, b0 - far_c], -1),
                    jnp.concatenate([b0 - far_c, neg], -1)], axis=1)
    return gs3, bs


def _overlap_table(n_cp, n_sb):
    c = np.arange(n_cp)[:, None] * CMP_STRIDE
    s = np.arange(n_sb)[None, :] * SEL_BLOCK
    return jnp.asarray(((c <= s + SEL_BLOCK - 1) & (c + CMP_LEN - 1 >= s)), BF16)


def _pad_cols(w, n):
    return jnp.pad(w, ((0, 0), (0, n - w.shape[1])))


def _pad_rows(w, n):
    return jnp.pad(w, ((0, n - w.shape[0]), (0, 0)))


def _rwkv_layer(h, layer, norm_g, mu, wr, wk, wv, wo, w0, w1, w2, a0, a1, a2, g1, g2,
                k_k, k_a, r_k, lnx_w, lnx_b, v_first, v0, v1, v2):
    bf = lambda w: w.astype(BF16)
    xr, xw, xk, xv, xa, xg = rwkv_prep(h, norm_g, mu)
    r = matmul(xr, wr, layer=layer)
    k = matmul(xk, wk, layer=layer)
    v = matmul(xv, wv, layer=layer)
    w_lin = lora(xw, bf(_pad_cols(w1, 128)), bf(_pad_rows(w2, 128)), act="tanh")
    a_lin = lora(xa, bf(a1), bf(a2))
    g = lora(xg, bf(g1), bf(g2), act="sigmoid")
    zeros = jnp.zeros_like(w0)
    if v_first is None:
        v_lin, v0 = None, zeros
    else:
        v_lin = lora(xv, bf(_pad_cols(v1, 128)), bf(_pad_rows(v2, 128)))
    params = jnp.stack([w0, a0, v0, k_k, k_a, r_k.reshape(-1), zeros, zeros])
    m, c, qp, y0, bonus = rwkv_chunks(r, k, v, w_lin, a_lin, v_lin, v_first, params)
    ln_params = jnp.stack([lnx_w, lnx_b] + [zeros] * 6)
    gated = rwkv_state_scan(m, c, qp, y0, bonus, g, ln_params)
    h = matmul(gated, wo, layer=layer, residual=h)
    return h, (v if v_first is None else v_first)


def _mlp(h, layer, norm_g, w_up, w_down):
    hid = norm_matmul(h, norm_g, w_up, layer=layer, act="relu2", out_dtype=BF16)
    return matmul(hid, w_down, layer=layer, residual=h)


def _nsa_shared_kv(h, kv_norm_g, w_kv, cmp_pe, cmp_w1, cmp_b1, cmp_w2):
    t = h.shape[0]
    G, D = NSA_GROUPS, NSA_HD
    kv = norm_matmul(h, kv_norm_g, w_kv.astype(BF16), out_dtype=BF16)
    x16 = kv[:, :2 * G * D].reshape(t, 2, G, D).transpose(1, 2, 0, 3).reshape(2, G, t // CMP_STRIDE, CMP_STRIDE * D)
    half = CMP_STRIDE * D
    w1cat = jnp.concatenate([cmp_w1[:, :half], cmp_w1[:, half:]], axis=-1).astype(BF16)
    cvec = jnp.einsum("plk,plkd->pd", cmp_pe, cmp_w1.reshape(2, CMP_LEN, D, D),
                      precision=lax.Precision.HIGHEST) + cmp_b1
    kvc = compress(x16, w1cat, cvec.reshape(2, 1, D), cmp_w2.astype(BF16))
    block_onehot = jnp.asarray(np.arange(t)[:, None] // SEL_BLOCK == np.arange(D)[None, :], BF16)
    return kvc[0], kvc[1], jnp.pad(kv, ((WINDOW, 0), (0, 0))), block_onehot


def _nsa_layer(h, layer, norm_g, wq, wo, kvs, tables):
    t = h.shape[0]
    nq = NSA_HEADS * NSA_HD
    q = norm_matmul(h, norm_g, wq, layer=layer, n=nq, scale=NSA_HD ** -0.5 * LOG2E, out_dtype=BF16)
    gates = norm_matmul(h, norm_g, _pad_cols(wq[layer, :, nq:], 128), act="sigmoid")
    gates = gates[:, :NSA_HEADS * N_BRANCH].reshape(t, NSA_GROUPS, NSA_REP * N_BRANCH).transpose(1, 0, 2)
    o = nsa_attention(q, gates, *kvs, *tables)
    return matmul(o, wo, layer=layer, residual=h)


def kernel(x, a_norm_g, rw_mu, rw_wr, rw_wk, rw_wv, rw_wo, rw_w0, rw_w1, rw_w2, rw_a0, rw_a1, rw_a2,
           rw_v0, rw_v1, rw_v2, rw_g1, rw_g2, rw_kk, rw_ka, rw_rk, rw_lnx_w, rw_lnx_b, kv_norm_g, w_kv,
           cmp_pe, cmp_w1, cmp_b1, cmp_w2, b_norm_g, nsa_wq, nsa_wo, rel_bias, m_norm_g, mlp_up,
           mlp_down, final_norm_g):
    b, t, d = x.shape
    assert b == 1 and d == D_MODEL and t % SEL_TILE == 0 and t // SEL_BLOCK <= 128 and WINDOW == 4 * QBLK
    h = x.reshape(t, d)
    v_first = None
    kvs = tables = None
    rw_w = [w.astype(BF16) for w in (rw_wr, rw_wk, rw_wv, rw_wo)]
    nsa_wq_b, nsa_wo_b, mlp_down_b = nsa_wq.astype(BF16), nsa_wo.astype(BF16), mlp_down.astype(BF16)
    for layer in range(DEPTH):
        if layer < N_A_LAYERS:
            i = layer
            vp = (None, None, None) if i == 0 else (rw_v0[i - 1], rw_v1[i - 1], rw_v2[i - 1])
            h, v_first = _rwkv_layer(
                h, i, a_norm_g[i], rw_mu[i], *rw_w, rw_w0[i], rw_w1[i],
                rw_w2[i], rw_a0[i], rw_a1[i], rw_a2[i], rw_g1[i], rw_g2[i], rw_kk[i], rw_ka[i],
                rw_rk[i].reshape(1, RW_HEADS, RW_HEAD), rw_lnx_w[i], rw_lnx_b[i], v_first, *vp)
        else:
            j = layer - N_A_LAYERS
            if j == 0:
                kvs = _nsa_shared_kv(h, kv_norm_g, w_kv, cmp_pe, cmp_w1, cmp_b1, cmp_w2)
                gs3, bs = _nsa_bias_tables(rel_bias)
                tables = (gs3, _overlap_table(t // CMP_STRIDE, 128), bs)
            h = _nsa_layer(h, j, b_norm_g[j], nsa_wq_b, nsa_wo_b, kvs, tables)
        h = _mlp(h, layer, m_norm_g[layer], mlp_up, mlp_down_b)
    return rmsnorm(h, final_norm_g, out_dtype=x.dtype).reshape(b, t, d)
```

```python
import functools
import math

import numpy as np
import jax
import jax.numpy as jnp
from jax import lax
from jax.experimental import pallas as pl
from jax.experimental.pallas import tpu as pltpu

F32 = jnp.float32
BF16 = jnp.bfloat16

D_MODEL = 2048
DEPTH = 4
N_A_LAYERS = 2
RW_HEAD = 64
RW_HEADS = D_MODEL // RW_HEAD
N_SHIFT_MIX = 6
GN_EPS = 64e-5
NSA_HEADS = 16
NSA_GROUPS = 4
NSA_REP = NSA_HEADS // NSA_GROUPS
NSA_HD = D_MODEL // NSA_HEADS
CMP_STRIDE = 16
CMP_LEN = 32
SEL_BLOCK = 64
SEL_TOP = 16
WINDOW = 512
QBLK = 128
N_BRANCH = 3
SEL_FORCE = 1e4
NEG_INF = -1e30
REL_BUCKETS = 32
REL_MAX_DIST = 128
NORM_EPS = 1e-6
LOG2E = math.log2(math.e)

V7X_VMEM_BYTES = 64 * 1024 * 1024
VMEM_LIMIT_BYTES = V7X_VMEM_BYTES - 8 * 1024 * 1024
NSA_VMEM_LIMIT_BYTES = V7X_VMEM_BYTES - 4 * 1024 * 1024
SCAN_CHUNK = 64
SCAN_HEADS = 4
SCAN_WIDTH = SCAN_HEADS * RW_HEAD
SCAN_UNITS = 8
SEL_TILE = 512
NSA_STEP_GROUPS = 2
CMP_BAND = 16
CMP_BAND_LO = 6
CMP_BIAS_K = 32
assert CMP_BAND < CMP_BIAS_K and 4 * CMP_BIAS_K == NSA_HD and NSA_HD == 128


def _cparams(sem, vmem_limit_bytes=VMEM_LIMIT_BYTES):
    return pltpu.CompilerParams(dimension_semantics=sem, vmem_limit_bytes=vmem_limit_bytes)


def _split3(x):
    h = x.astype(BF16)
    r = x - h.astype(F32)
    m = r.astype(BF16)
    l = (r - m.astype(F32)).astype(BF16)
    return h, m, l


def _dot(a, b):
    return jnp.dot(a, b, preferred_element_type=F32)


def _dot_nt(a, b):
    return lax.dot_general(a, b, (((1,), (1,)), ((), ())), preferred_element_type=F32)


def _dot_tn(a, b):
    return lax.dot_general(a, b, (((0,), (0,)), ((), ())), preferred_element_type=F32)


def _dot_exact_lhs(a_bf16, x):
    h, m, l = _split3(x)
    return _dot(a_bf16, h) + _dot(a_bf16, m) + _dot(a_bf16, l)


def _dot_exact_rhs(x, b_bf16):
    h, m, l = _split3(x)
    return _dot(h, b_bf16) + _dot(m, b_bf16) + _dot(l, b_bf16)


def _activate(acc, act, scale):
    if act == "relu2":
        acc = jnp.square(jnp.maximum(acc, 0.0))
    elif act == "tanh":
        acc = jnp.tanh(acc)
    elif act == "sigmoid":
        acc = jax.nn.sigmoid(acc)
    return acc if scale is None else acc * scale


def _matmul_kernel(x_ref, w_ref, *rest, act, scale, has_res, nk):
    if has_res:
        res_ref, o_ref, *scratch = rest
    else:
        o_ref, *scratch = rest

    def epilogue(acc):
        acc = _activate(acc, act, scale)
        if has_res:
            acc = acc + res_ref[...]
        o_ref[...] = acc.astype(o_ref.dtype)

    if nk == 1:
        epilogue(_dot(x_ref[...], w_ref[...].astype(BF16)))
    else:
        acc_ref, = scratch
        k = pl.program_id(2)

        @pl.when(k == 0)
        def _():
            acc_ref[...] = jnp.zeros_like(acc_ref)

        acc_ref[...] += _dot(x_ref[...], w_ref[...].astype(BF16))

        @pl.when(k == nk - 1)
        def _():
            epilogue(acc_ref[...])


def _pick_tile(n, pref):
    t = min(n, pref)
    while n % t:
        t //= 2
    return t


def _weight_spec(w, layer, rows, cols, index):
    if w.ndim == 2:
        return pl.BlockSpec((rows, cols), index)
    return pl.BlockSpec((None, rows, cols), lambda *g: (layer,) + tuple(index(*g)))


def matmul(x, w, *, layer=None, act=None, scale=None, residual=None, out_dtype=F32, tm=1024, tn=512, tk=2048):
    m, kdim = x.shape
    n = w.shape[-1]
    tm, tn, tk = _pick_tile(m, tm), _pick_tile(n, tn), _pick_tile(kdim, tk)
    nk = kdim // tk
    in_specs = [pl.BlockSpec((tm, tk), lambda j, i, k: (i, k)),
                _weight_spec(w, layer, tk, tn, lambda j, i, k: (k, j))]
    args = [x, w]
    if residual is not None:
        in_specs.append(pl.BlockSpec((tm, tn), lambda j, i, k: (i, j)))
        args.append(residual)
    return pl.pallas_call(
        functools.partial(_matmul_kernel, act=act, scale=scale, has_res=residual is not None, nk=nk),
        out_shape=jax.ShapeDtypeStruct((m, n), out_dtype),
        grid=(n // tn, m // tm, nk),
        in_specs=in_specs,
        out_specs=pl.BlockSpec((tm, tn), lambda j, i, k: (i, j)),
        scratch_shapes=[pltpu.VMEM((tm, tn), F32)] if nk > 1 else [],
        compiler_params=_cparams(("parallel", "parallel", "arbitrary")),
        name="matmul",
    )(*args)


def _lora_kernel(x_ref, w1_ref, w2_ref, o_ref, *, act):
    hidden = _activate(_dot(x_ref[...], w1_ref[...]), act, None).astype(BF16)
    o_ref[...] = _dot(hidden, w2_ref[...])


def lora(x, w1, w2, *, act=None, tm=1024):
    m, kdim = x.shape
    r, n = w2.shape
    tm = _pick_tile(m, tm)
    return pl.pallas_call(
        functools.partial(_lora_kernel, act=act),
        out_shape=jax.ShapeDtypeStruct((m, n), F32),
        grid=(m // tm,),
        in_specs=[pl.BlockSpec((tm, kdim), lambda i: (i, 0)),
                  pl.BlockSpec((kdim, r), lambda i: (0, 0)),
                  pl.BlockSpec((r, n), lambda i: (0, 0))],
        out_specs=pl.BlockSpec((tm, n), lambda i: (i, 0)),
        compiler_params=_cparams(("parallel",)),
        name="lora",
    )(x, w1, w2)


def _rms(x, g):
    return x * lax.rsqrt(jnp.mean(x * x, axis=-1, keepdims=True) + NORM_EPS) * g


def _rmsnorm_kernel(h_ref, g_ref, o_ref):
    o_ref[...] = _rms(h_ref[...], g_ref[...]).astype(o_ref.dtype)


def rmsnorm(h, g, out_dtype=BF16, tm=512):
    t, d = h.shape
    tm = _pick_tile(t, tm)
    return pl.pallas_call(
        _rmsnorm_kernel,
        out_shape=jax.ShapeDtypeStruct((t, d), out_dtype),
        grid=(t // tm,),
        in_specs=[pl.BlockSpec((tm, d), lambda i: (i, 0)), pl.BlockSpec((1, d), lambda i: (0, 0))],
        out_specs=pl.BlockSpec((tm, d), lambda i: (i, 0)),
        compiler_params=_cparams(("parallel",)),
        name="rmsnorm",
    )(h, g.reshape(1, d))


def _norm_matmul_kernel(h_ref, g_ref, w_ref, o_ref, xn_ref, *, act, scale):
    @pl.when(pl.program_id(1) == 0)
    def _():
        xn_ref[...] = _rms(h_ref[...], g_ref[...]).astype(xn_ref.dtype)

    o_ref[...] = _activate(_dot(xn_ref[...], w_ref[...].astype(BF16)), act, scale).astype(o_ref.dtype)


def norm_matmul(h, g, w, *, layer=None, act=None, scale=None, out_dtype=F32, n=None, tm=1024, tn=512):
    m, kdim = h.shape
    n = w.shape[-1] if n is None else n
    tm, tn = _pick_tile(m, tm), _pick_tile(n, tn)
    return pl.pallas_call(
        functools.partial(_norm_matmul_kernel, act=act, scale=scale),
        out_shape=jax.ShapeDtypeStruct((m, n), out_dtype),
        grid=(m // tm, n // tn),
        in_specs=[pl.BlockSpec((tm, kdim), lambda i, j: (i, 0)),
                  pl.BlockSpec((1, kdim), lambda i, j: (0, 0)),
                  _weight_spec(w, layer, kdim, tn, lambda i, j: (0, j))],
        out_specs=pl.BlockSpec((tm, tn), lambda i, j: (i, j)),
        scratch_shapes=[pltpu.VMEM((tm, kdim), BF16)],
        compiler_params=_cparams(("parallel", "arbitrary")),
        name="norm_matmul",
    )(h, g.reshape(1, kdim), w)


def _rwkv_prep_kernel(h_ref, prev_ref, g_ref, mu_ref, *o_refs):
    g = g_ref[...]
    xn = _rms(h_ref[...], g)
    prev = _rms(prev_ref[...], g)[7:8, :] * (pl.program_id(0) > 0).astype(F32)
    row = lax.broadcasted_iota(jnp.int32, xn.shape, 0)
    shifted = jnp.where(row == 0, prev, pltpu.roll(xn, 1, axis=0))
    xx = shifted - xn
    for i, o_ref in enumerate(o_refs):
        o_ref[...] = (xn + xx * mu_ref[i:i + 1, :]).astype(o_ref.dtype)


def rwkv_prep(h, g, mu, tm=256):
    t, d = h.shape
    tm = _pick_tile(t, tm)
    blk = pl.BlockSpec((tm, d), lambda i: (i, 0))
    return pl.pallas_call(
        _rwkv_prep_kernel,
        out_shape=[jax.ShapeDtypeStruct((t, d), BF16)] * N_SHIFT_MIX,
        grid=(t // tm,),
        in_specs=[blk,
                  pl.BlockSpec((8, d), lambda i: (jnp.maximum(i * (tm // 8) - 1, 0), 0)),
                  pl.BlockSpec((1, d), lambda i: (0, 0)),
                  pl.BlockSpec((N_SHIFT_MIX, d), lambda i: (0, 0))],
        out_specs=[blk] * N_SHIFT_MIX,
        compiler_params=_cparams(("parallel",)),
        name="rwkv_prep",
    )(h, h, g.reshape(1, d), mu)


def _head_mask(dtype):
    row = lax.broadcasted_iota(jnp.int32, (SCAN_WIDTH, SCAN_WIDTH), 0)
    col = lax.broadcasted_iota(jnp.int32, (SCAN_WIDTH, SCAN_WIDTH), 1)
    return jnp.where((row >> 6) == (col >> 6), 1.0, 0.0).astype(dtype)


def _block_diag(x, mask):
    return jnp.concatenate([x] * SCAN_HEADS, axis=0) * mask


def _fold_rows(x_full, mask_f32):
    x = x_full * mask_f32
    L = SCAN_CHUNK
    return x[0:L] + x[L:2 * L] + x[2 * L:3 * L] + x[3 * L:4 * L]


def _split2(x):
    hi = x.astype(BF16)
    return hi, (x - hi.astype(F32)).astype(BF16)


def _head_sums(xs, mask_b):
    n = xs[0].shape[0]
    r = _dot(jnp.concatenate([piece for x in xs for piece in _split3(x)], axis=0), mask_b)
    return [r[3 * i * n:(3 * i + 1) * n] + r[(3 * i + 1) * n:(3 * i + 2) * n] + r[(3 * i + 2) * n:(3 * i + 3) * n]
            for i in range(len(xs))]


def _bd_parts(y_split, mask_b):
    hi, lo = y_split
    return _block_diag(hi, mask_b), _block_diag(lo, mask_b)


def _mm_heads(x_split, y_parts):
    x_hi, x_lo = x_split
    y_hi, y_lo = y_parts
    n = x_hi.shape[0]
    both = _dot(jnp.concatenate([x_hi, x_lo], axis=0), y_hi)
    return both[:n] + both[n:] + _dot(x_hi, y_lo)


def _run_interleaved(stage_generators):
    live = list(stage_generators)
    while live:
        still = []
        for gen in live:
            try:
                next(gen)
                still.append(gen)
            except StopIteration:
                pass
        live = still


def _tri_inverse_heads(a, j_row, m_col, mask_b):
    L = SCAN_CHUNK
    eye = jnp.where(m_col == j_row, 1.0, 0.0)
    d = jnp.where((j_row >> 4) == (m_col >> 4), a, 0.0)
    e = a - d
    p = eye + d
    d_split = _split2(d)
    x = _mm_heads(d_split, _bd_parts(d_split, mask_b))
    yield
    for _ in range(2):
        x_split, p_split = _split2(x), _split2(p)
        stacked = tuple(jnp.concatenate([xs, ps], axis=0) for xs, ps in zip(x_split, p_split))
        both = _mm_heads(stacked, _bd_parts(x_split, mask_b))
        x, p = both[:L], p + both[L:]
        yield
    p = p + _mm_heads(_split2(p), _bd_parts(_split2(x), mask_b))
    yield
    p_split = _split2(p)
    n1 = _mm_heads(p_split, _bd_parts(_split2(e), mask_b))
    yield
    n1_split = _split2(n1)
    n2 = _mm_heads(n1_split, _bd_parts(n1_split, mask_b))
    yield
    q = eye + n1
    q = q + _mm_heads(_split2(q), _bd_parts(_split2(n2), mask_b))
    yield
    return _mm_heads(_split2(q), _bd_parts(p_split, mask_b))


def _softplus(x):
    return jnp.maximum(x, 0.0) + jnp.log(1.0 + jnp.exp(-jnp.abs(x)))


def _rwkv_chunk_kernel(*refs, mix_v):
    if mix_v:
        r_ref, k_ref, v_ref, wl_ref, al_ref, vl_ref, vf_ref, par_ref, m_ref, c_ref, qp_ref, y0_ref, bo_ref = refs
    else:
        r_ref, k_ref, v_ref, wl_ref, al_ref, par_ref, m_ref, c_ref, qp_ref, y0_ref, bo_ref = refs
    L, W = SCAN_CHUNK, SCAN_WIDTH
    mask_b = _head_mask(BF16)
    mask_f = _head_mask(F32)
    tri_incl = jnp.where(lax.broadcasted_iota(jnp.int32, (L, L), 1)
                         <= lax.broadcasted_iota(jnp.int32, (L, L), 0), 1.0, 0.0).astype(BF16)
    j_row = lax.broadcasted_iota(jnp.int32, (L, W), 0)
    m_col = lax.broadcasted_iota(jnp.int32, (L, W), 1) & (L - 1)
    strict = m_col < j_row
    incl = m_col <= j_row

    def unit(u):
        cols = slice(u * W, (u + 1) * W)
        par = par_ref[:, cols]
        w0, a0, v0, k_k, k_a, r_k = (par[i:i + 1] for i in range(6))
        r, k, v = r_ref[:, cols], k_ref[:, cols], v_ref[:, cols]
        lw = -jnp.exp(-_softplus(-(w0 + wl_ref[:, cols])) - 0.5)
        a_s = jax.nn.sigmoid(a0 + al_ref[:, cols])
        if mix_v:
            v = v + (vf_ref[:, cols] - v) * jax.nn.sigmoid(v0 + vl_ref[:, cols])
        kk = k * k_k
        k = k * (1.0 + (a_s - 1.0) * k_a)
        kk_sq, rk_sum = _head_sums([kk * kk, r * k * r_k], mask_b)
        c = _dot_exact_lhs(tri_incl, lw)
        yield
        kk = kk * lax.rsqrt(jnp.maximum(kk_sq, 1e-24))
        bo_ref[:, cols] = rk_sum * v
        c_last = c[L - 1:L, :]
        p_inv = jnp.exp(-c)
        p_rest = jnp.exp(c_last - c)
        rt = r * jnp.exp(c)
        at = (-kk * jnp.exp(c - lw)).astype(BF16)
        b = kk * a_s
        vb = v.astype(BF16)
        ar = jnp.concatenate([at, rt.astype(BF16)], axis=0)
        g_b = _dot_nt(ar, _block_diag((b * p_inv).astype(BF16), mask_b))
        g_k = _dot_nt(ar, _block_diag((k * p_inv).astype(BF16), mask_b))
        yield
        a_ab = jnp.where(strict, g_b[:L], 0.0)
        a_ak = jnp.where(strict, g_k[:L], 0.0).astype(BF16)
        a_rb = jnp.where(incl, g_b[L:], 0.0).astype(BF16)
        a_rk = jnp.where(incl, g_k[L:], 0.0).astype(BF16)
        v_bd = _block_diag(vb, mask_b)
        akv = _dot(a_ak, v_bd)
        t_inv = yield from _tri_inverse_heads(a_ab, j_row, m_col, mask_b)
        yield
        t_split = _split2(t_inv)
        w_both = _dot(jnp.concatenate(t_split, axis=0), _block_diag(at, mask_b))
        u0 = _mm_heads(t_split, _bd_parts(_split2(akv), mask_b)).astype(BF16)
        yield
        w = (w_both[:L] + w_both[L:]).astype(BF16)
        qp_ref[:, cols] = (rt + _dot(a_rb, _block_diag(w, mask_b))).astype(qp_ref.dtype)
        y0_ref[:, cols] = _dot(a_rb, _block_diag(u0, mask_b)) + _dot(a_rk, v_bd)
        bh = (b * p_rest).astype(BF16)
        kh = (k * p_rest).astype(BF16)
        decay_diag = jnp.where(m_col == j_row, jnp.exp(c_last), 0.0)
        m_ref[:, cols] = _fold_rows(_dot_tn(bh, w), mask_f) + decay_diag
        c_ref[:, cols] = _fold_rows(_dot_tn(bh, u0) + _dot_tn(kh, vb), mask_f)

    _run_interleaved(unit(u) for u in range(SCAN_UNITS))


def rwkv_chunks(r, k, v, w_lin, a_lin, v_lin, v_first, params):
    t, d = r.shape
    bw = SCAN_UNITS * SCAN_WIDTH
    blk = pl.BlockSpec((SCAN_CHUNK, bw), lambda ci, gi: (ci, gi))
    mix_v = v_lin is not None
    args = [r, k, v, w_lin, a_lin] + ([v_lin, v_first] if mix_v else []) + [params]
    f32 = jax.ShapeDtypeStruct((t, d), F32)
    return pl.pallas_call(
        functools.partial(_rwkv_chunk_kernel, mix_v=mix_v),
        out_shape=[f32, f32, jax.ShapeDtypeStruct((t, d), BF16), f32, f32],
        grid=(t // SCAN_CHUNK, d // bw),
        in_specs=[blk] * (len(args) - 1) + [pl.BlockSpec((8, bw), lambda ci, gi: (0, gi))],
        out_specs=[blk] * 5,
        compiler_params=_cparams(("parallel", "parallel")),
        name="rwkv_chunks",
    )(*args)


def _rwkv_state_kernel(m_ref, c_ref, qp_ref, y0_ref, bo_ref, g_ref, ln_ref, o_ref, st_ref):
    @pl.when(pl.program_id(1) == 0)
    def _():
        st_ref[...] = jnp.zeros_like(st_ref)

    W = SCAN_WIDTH
    mask_b = _head_mask(BF16)
    def unit(u):
        cols = slice(u * W, (u + 1) * W)
        st_parts = _bd_parts(_split2(st_ref[u]), mask_b)
        y = _dot(qp_ref[:, cols], st_parts[0]) + y0_ref[:, cols]
        st_ref[u] = _mm_heads(_split2(m_ref[:, cols]), st_parts) + c_ref[:, cols]
        yield
        mean = _head_sums([y], mask_b)[0] * (1.0 / RW_HEAD)
        yield
        dev = y - mean
        var = _head_sums([dev * dev], mask_b)[0] * (1.0 / RW_HEAD)
        yield
        ln = ln_ref[:, cols]
        out = (dev * lax.rsqrt(var + GN_EPS) * ln[0:1] + ln[1:2] + bo_ref[:, cols]) * g_ref[:, cols]
        o_ref[:, cols] = out.astype(o_ref.dtype)

    _run_interleaved(unit(u) for u in range(SCAN_UNITS))


def rwkv_state_scan(m, c, qp, y0, bonus, g, ln_params):
    t, d = m.shape
    bw = SCAN_UNITS * SCAN_WIDTH
    blk = pl.BlockSpec((SCAN_CHUNK, bw), lambda gi, ci: (ci, gi))
    return pl.pallas_call(
        _rwkv_state_kernel,
        out_shape=jax.ShapeDtypeStruct((t, d), BF16),
        grid=(d // bw, t // SCAN_CHUNK),
        in_specs=[blk] * 6 + [pl.BlockSpec((8, bw), lambda gi, ci: (0, gi))],
        out_specs=blk,
        scratch_shapes=[pltpu.VMEM((SCAN_UNITS, RW_HEAD, SCAN_WIDTH), F32)],
        compiler_params=_cparams(("parallel", "arbitrary")),
        name="rwkv_state_scan",
    )(m, c, qp, y0, bonus, g, ln_params)


def _compress_kernel(x_ref, w1_ref, c_ref, w2_ref, o_ref):
    ab = _dot(x_ref[...], w1_ref[...])
    a = ab[:, :NSA_HD]
    bsh = ab[:, NSA_HD:]
    n16 = a.shape[0]
    row = lax.broadcasted_iota(jnp.int32, a.shape, 0)
    b_next = jnp.where(row == n16 - 1, 0.0, pltpu.roll(bsh, n16 - 1, axis=0))
    hid = a + b_next + c_ref[...]
    hid = jax.nn.gelu(hid, approximate=True)
    o_ref[...] = _dot(hid.astype(BF16), w2_ref[...]).astype(o_ref.dtype)


def compress(x16, w1cat, cvec, w2):
    _, g, n16, kd = x16.shape
    d = NSA_HD
    return pl.pallas_call(
        _compress_kernel,
        out_shape=jax.ShapeDtypeStruct((2, g, n16, d), BF16),
        grid=(2, g),
        in_specs=[pl.BlockSpec((None, None, n16, kd), lambda p, gi: (p, gi, 0, 0)),
                  pl.BlockSpec((None, kd, 2 * d), lambda p, gi: (p, 0, 0)),
                  pl.BlockSpec((None, 1, d), lambda p, gi: (p, 0, 0)),
                  pl.BlockSpec((None, d, d), lambda p, gi: (p, 0, 0))],
        out_specs=pl.BlockSpec((None, None, n16, d), lambda p, gi: (p, gi, 0, 0)),
        compiler_params=_cparams(("parallel", "parallel")),
        name="nsa_compress",
    )(x16, w1cat, cvec, w2)


def _with_ones(v):
    return jnp.concatenate([v, jnp.ones_like(v)], axis=1)


def _nsa_kernel(q_ref, gate_ref, kc_ref, vc_ref, ks_ref, vs_ref, kw_ref, vw_ref, oh_ref,
                gs_ref, ov_ref, bs_ref, o_ref, s_buf, p_buf):
    qi = pl.program_id(1)
    R, Q, D, NG = NSA_REP, QBLK, NSA_HD, NSA_STEP_GROUPS
    n_cp = kc_ref.shape[1]
    n_sb = ov_ref.shape[1]
    n_far = jnp.maximum(qi - 1, 0)
    n0 = pl.multiple_of(jnp.maximum(qi - 1, 0) * Q, Q)
    n_tiles = (n_far + (SEL_TILE // Q - 1)) // (SEL_TILE // Q)
    max_tile = oh_ref.shape[0] // SEL_TILE - 1
    state = [{} for _ in range(NG)]

    def slab(ref, gl, start, n):
        return ref[pl.ds(start, n), gl * D:(gl + 1) * D]

    def sel_keys(gl, start, n):
        return jnp.concatenate([slab(ks_ref, gl, WINDOW + start, n), oh_ref[pl.ds(start, n), :]], axis=1)

    def key_tile(gl, i):
        return sel_keys(gl, pl.multiple_of(i * SEL_TILE, SEL_TILE), SEL_TILE)

    def value_tile(gl, i):
        return _with_ones(slab(vs_ref, gl, pl.multiple_of(WINDOW + i * SEL_TILE, SEL_TILE), SEL_TILE))

    def before_loop(gl):
        st = state[gl]
        q = q_ref[:, gl * R * D:(gl + 1) * R * D]
        qs = jnp.concatenate([q[:, r * D:(r + 1) * D] for r in range(R)], axis=0)
        t_rows = qi * Q + (lax.broadcasted_iota(jnp.int32, (R * Q, 1), 0) & (Q - 1))
        c_idx = lax.broadcasted_iota(jnp.int32, (1, n_cp), 1)
        m_off = qi * (Q // CMP_STRIDE) - lax.broadcasted_iota(jnp.int32, (n_cp, D), 0) + CMP_BAND_LO
        k_idx = lax.broadcasted_iota(jnp.int32, (n_cp, D), 1) & (CMP_BIAS_K - 1)
        in_band = ((k_idx < CMP_BAND) & (m_off == k_idx)) | ((k_idx == CMP_BAND) & (m_off >= CMP_BAND))
        k_c = jnp.concatenate([kc_ref[gl], jnp.where(in_band, 1.0, 0.0).astype(BF16)], axis=1)
        qk_c = _dot_nt(jnp.concatenate([qs, gs_ref[gl]], axis=1), k_c)
        w0 = pl.multiple_of(qi * Q, Q)
        qk_w = _dot_nt(qs, slab(kw_ref, gl, w0, WINDOW + Q))
        yield
        m_c = (c_idx * CMP_STRIDE + (CMP_LEN - 1)) <= t_rows
        s_c = jnp.where(m_c, qk_c, NEG_INF)
        e_c = jnp.exp2(s_c - jnp.max(s_c, axis=-1, keepdims=True))
        acc_c = _dot(e_c.astype(BF16), _with_ones(vc_ref[gl]))
        yield
        inv_l = jnp.where(t_rows >= CMP_LEN - 1, 1.0 / acc_c[:, D:D + 1], 0.0)
        st["o_c"] = acc_c[:, :D] * inv_l
        p_c = e_c * inv_l
        p_sum = p_c[0:Q] + p_c[Q:2 * Q] + p_c[2 * Q:3 * Q] + p_c[3 * Q:4 * Q]
        imp = _dot_exact_rhs(p_sum, ov_ref[...])
        j_io = lax.broadcasted_iota(jnp.int32, (1, WINDOW + Q), 1)
        oldest = jnp.where(lax.broadcasted_iota(jnp.int32, (R * Q, Q), 1) > (t_rows & (Q - 1)), 0.0, NEG_INF)
        bias_w = jnp.concatenate([oldest, jnp.zeros((R * Q, WINDOW - 2 * Q), F32), bs_ref[gl, 0]], axis=1)
        s_w = jnp.where(j_io >= WINDOW - qi * Q, qk_w + bias_w, NEG_INF)
        e_w = jnp.exp2(s_w - jnp.max(s_w, axis=-1, keepdims=True))
        acc_w = _dot(e_w.astype(BF16), _with_ones(slab(vw_ref, gl, w0, WINDOW + Q)))
        yield
        st["o_w"] = acc_w[:, :D] / acc_w[:, D:]
        imp_t = imp.T
        s_row = lax.broadcasted_iota(jnp.int32, (n_sb, Q), 0)
        t_col = qi * Q + lax.broadcasted_iota(jnp.int32, (n_sb, Q), 1)
        forced = (s_row == (t_col >> 6)) | (s_row == 0)
        score = jnp.where(forced, SEL_FORCE, jnp.where(s_row * SEL_BLOCK <= t_col, imp_t, -SEL_FORCE))
        sel_t = jnp.zeros((n_sb, Q), F32)
        s_f = s_row.astype(F32)
        for _ in range(SEL_TOP):
            mx = jnp.max(score, axis=0, keepdims=True)
            first = jnp.min(jnp.where(score == mx, s_f, float(n_sb)), axis=0, keepdims=True)
            pick = s_f == first
            sel_t = jnp.where(pick, 1.0, sel_t)
            score = jnp.where(pick, -3e38, score)
        sel = sel_t.T
        blk = lax.broadcasted_iota(jnp.int32, (Q, n_sb), 1)
        neg_near = jnp.where(sel > 0.5, 0.0, NEG_INF).astype(BF16)
        neg_far = jnp.where((sel > 0.5) & (blk < 2 * n_far), 0.0, NEG_INF).astype(BF16)
        q_near = jnp.concatenate([qs, jnp.concatenate([neg_near] * R, axis=0)], axis=1)
        st["q_far"] = jnp.concatenate([qs, jnp.concatenate([neg_far] * R, axis=0)], axis=1)
        s = _dot_nt(q_near, sel_keys(gl, n0, 2 * Q)) + bs_ref[gl, jnp.where(qi == 0, 1, 0)]
        s_far = _dot_nt(st["q_far"], key_tile(gl, 0))
        s_buf[gl, 0] = s_far
        p_buf[gl, 1] = jnp.zeros(p_buf.shape[2:], p_buf.dtype)
        yield
        st["row_max"] = jnp.max(s_far, axis=-1, keepdims=True)
        st["m"] = jnp.max(s, axis=-1, keepdims=True)
        st["acc"] = _dot(jnp.exp2(s - st["m"]).astype(BF16), _with_ones(slab(vs_ref, gl, WINDOW + n0, 2 * Q)))

    _run_interleaved([before_loop(gl) for gl in range(NG)])

    def half_step(gl, i, slot, carry):
        m_run, acc, row_max = carry
        s_next = _dot_nt(state[gl]["q_far"], key_tile(gl, jnp.minimum(i + 1, max_tile)))
        s_buf[gl, 1 - slot] = s_next
        pv = _dot(p_buf[gl, 1 - slot], value_tile(gl, jnp.maximum(i - 1, 0)))
        m_new = jnp.maximum(m_run, row_max)
        acc = jnp.exp2(m_run - m_new) * (acc + pv)
        p_buf[gl, slot] = jnp.exp2(s_buf[gl, slot] - m_new).astype(p_buf.dtype)
        return m_new, acc, jnp.max(s_next, axis=-1, keepdims=True)

    def far_loop(gl):
        st = state[gl]
        pair_body = lambda j, carry: half_step(gl, 2 * j + 1, 1, half_step(gl, 2 * j, 0, carry))
        return lax.fori_loop(0, n_pairs, pair_body, (st["m"], st["acc"], st["row_max"]))

    n_pairs = (n_tiles + 1) >> 1
    carries = [far_loop(gl) for gl in range(NG)]
    last_tile = jnp.maximum(2 * n_pairs - 1, 0)
    for gl, (_, acc_s, _) in enumerate(carries):
        acc_s = acc_s + _dot(p_buf[gl, 1], value_tile(gl, last_tile))
        o_s = acc_s[:, :D] / acc_s[:, D:]
        o_c, o_w = state[gl]["o_c"], state[gl]["o_w"]
        gates = gate_ref[gl]
        for r in range(R):
            rows = slice(r * Q, (r + 1) * Q)
            o = (gates[:, 3 * r:3 * r + 1] * o_c[rows] + gates[:, 3 * r + 1:3 * r + 2] * o_s[rows]
                 + gates[:, 3 * r + 2:3 * r + 3] * o_w[rows])
            o_ref[:, (gl * R + r) * D:(gl * R + r + 1) * D] = o.astype(o_ref.dtype)


def nsa_attention(q, gates, kc, vc, kv_padded, block_onehot, gs, ov, bs):
    t = q.shape[0]
    g, n_cp, d = kc.shape
    rq = NSA_REP * QBLK
    ng = NSA_STEP_GROUPS
    assert g % ng == 0
    per_g = lambda shape: pl.BlockSpec((ng,) + shape, lambda gi, qi: (gi,) + (0,) * len(shape))
    kv_part = lambda part: pl.BlockSpec((t + WINDOW, ng * d), lambda gi, qi: (0, part * g // ng + gi))
    return pl.pallas_call(
        _nsa_kernel,
        out_shape=jax.ShapeDtypeStruct((t, NSA_HEADS * d), BF16),
        grid=(g // ng, t // QBLK),
        in_specs=[pl.BlockSpec((QBLK, ng * NSA_REP * d), lambda gi, qi: (qi, gi)),
                  pl.BlockSpec((ng, QBLK, NSA_REP * N_BRANCH), lambda gi, qi: (gi, qi, 0)),
                  per_g((n_cp, d)), per_g((n_cp, d)),
                  kv_part(2), kv_part(3), kv_part(4), kv_part(5),
                  pl.BlockSpec((t, d), lambda gi, qi: (0, 0)),
                  per_g((rq, d)),
                  pl.BlockSpec(ov.shape, lambda gi, qi: (0, 0)),
                  per_g((2, rq, 2 * QBLK))],
        out_specs=pl.BlockSpec((QBLK, ng * NSA_REP * d), lambda gi, qi: (qi, gi)),
        scratch_shapes=[pltpu.VMEM((ng, 2, rq, SEL_TILE), F32), pltpu.VMEM((ng, 2, rq, SEL_TILE), BF16)],
        compiler_params=_cparams(("parallel", "arbitrary"), NSA_VMEM_LIMIT_BYTES),
        name="nsa_attention",
    )(q, gates, kc, vc, kv_padded, kv_padded, kv_padded, kv_padded, block_onehot, gs, ov, bs)


def _t5_bucket_np(n):
    n = np.maximum(n, 0)
    max_exact = REL_BUCKETS // 2
    nf = np.maximum(n, 1).astype(np.float64)
    large = max_exact + (np.log(nf / max_exact) / math.log(REL_MAX_DIST / max_exact)
                         * (REL_BUCKETS - max_exact)).astype(np.int64)
    large = np.minimum(large, REL_BUCKETS - 1)
    return np.where(n < max_exact, n, large).astype(np.int32)


def _nsa_bias_tables(rel_bias):
    G, R, Q = NSA_GROUPS, NSA_REP, QBLK
    table = (rel_bias.astype(F32) * LOG2E).reshape(REL_BUCKETS, G, R).transpose(1, 2, 0)
    qq = np.arange(Q)

    def lookup(dist):
        return table[:, :, _t5_bucket_np(dist)].reshape(G, R * Q, dist.shape[1])

    m = np.arange(CMP_BAND) - CMP_BAND_LO
    gs = lookup(qq[:, None] - (CMP_LEN - 1) + CMP_STRIDE * m[None, :])
    far = jnp.broadcast_to(table[:, :, REL_BUCKETS - 1][:, :, None, None], (G, R, Q, 1)).reshape(G, R * Q, 1)
    gs = jnp.concatenate([gs, far, jnp.zeros((G, R * Q, CMP_BIAS_K - CMP_BAND - 1), F32)], axis=-1)
    h, mid, l = _split3(gs)
    gs3 = jnp.concatenate([h, mid, l, jnp.zeros_like(h)], axis=-1)
    def toeplitz(c):
        mm = np.arange(2 * Q)
        v = table[:, :, _t5_bucket_np(np.where(mm < Q, c - mm, c + 2 * Q - mm))]
        rows = jnp.tile(v, (1, 1, Q))[:, :, :Q * (2 * Q - 1)].reshape(G, R, Q, 2 * Q - 1)
        return rows[:, :, :, :Q].reshape(G, R * Q, Q)

    d0 = np.tile(qq[:, None] - qq[None, :], (R, 1))
    causal = jnp.asarray(np.where(d0 >= 0, 0.0, NEG_INF), F32)[None]
    b0 = toeplitz(0) + causal
    b1 = toeplitz(Q)
    far_c = jnp.broadcast_to(far, (G, R * Q, Q))
    neg = jnp.full((G, R * Q, Q), NEG_INF, F32)
    bs = jnp.stack([jnp.concatenate([b1 - far_c, b0 - far_c], -1),
                    jnp.concatenate([b0 - far_c, neg], -1)], axis=1)
    return gs3, bs


def _overlap_table(n_cp, n_sb):
    c = np.arange(n_cp)[:, None] * CMP_STRIDE
    s = np.arange(n_sb)[None, :] * SEL_BLOCK
    return jnp.asarray(((c <= s + SEL_BLOCK - 1) & (c + CMP_LEN - 1 >= s)), BF16)


def _pad_cols(w, n):
    return jnp.pad(w, ((0, 0), (0, n - w.shape[1])))


def _pad_rows(w, n):
    return jnp.pad(w, ((0, n - w.shape[0]), (0, 0)))


def _rwkv_layer(h, layer, norm_g, mu, wr, wk, wv, wo, w0, w1, w2, a0, a1, a2, g1, g2,
                k_k, k_a, r_k, lnx_w, lnx_b, v_first, v0, v1, v2):
    bf = lambda w: w.astype(BF16)
    xr, xw, xk, xv, xa, xg = rwkv_prep(h, norm_g, mu)
    r = matmul(xr, wr, layer=layer)
    k = matmul(xk, wk, layer=layer)
    v = matmul(xv, wv, layer=layer)
    w_lin = lora(xw, bf(_pad_cols(w1, 128)), bf(_pad_rows(w2, 128)), act="tanh")
    a_lin = lora(xa, bf(a1), bf(a2))
    g = lora(xg, bf(g1), bf(g2), act="sigmoid")
    zeros = jnp.zeros_like(w0)
    if v_first is None:
        v_lin, v0 = None, zeros
    else:
        v_lin = lora(xv, bf(_pad_cols(v1, 128)), bf(_pad_rows(v2, 128)))
    params = jnp.stack([w0, a0, v0, k_k, k_a, r_k.reshape(-1), zeros, zeros])
    m, c, qp, y0, bonus = rwkv_chunks(r, k, v, w_lin, a_lin, v_lin, v_first, params)
    ln_params = jnp.stack([lnx_w, lnx_b] + [zeros] * 6)
    gated = rwkv_state_scan(m, c, qp, y0, bonus, g, ln_params)
    h = matmul(gated, wo, layer=layer, residual=h)
    return h, (v if v_first is None else v_first)


def _mlp(h, layer, norm_g, w_up, w_down):
    hid = norm_matmul(h, norm_g, w_up, layer=layer, act="relu2", out_dtype=BF16)
    return matmul(hid, w_down, layer=layer, residual=h, tm=2048)


def _nsa_shared_kv(h, kv_norm_g, w_kv, cmp_pe, cmp_w1, cmp_b1, cmp_w2):
    t = h.shape[0]
    G, D = NSA_GROUPS, NSA_HD
    kv = norm_matmul(h, kv_norm_g, w_kv, out_dtype=BF16)
    x16 = kv[:, :2 * G * D].reshape(t, 2, G, D).transpose(1, 2, 0, 3).reshape(2, G, t // CMP_STRIDE, CMP_STRIDE * D)
    half = CMP_STRIDE * D
    w1cat = jnp.concatenate([cmp_w1[:, :half], cmp_w1[:, half:]], axis=-1).astype(BF16)
    cvec = jnp.einsum("plk,plkd->pd", cmp_pe, cmp_w1.reshape(2, CMP_LEN, D, D),
                      precision=lax.Precision.HIGHEST) + cmp_b1
    kvc = compress(x16, w1cat, cvec.reshape(2, 1, D), cmp_w2.astype(BF16))
    block_onehot = jnp.asarray(np.arange(t)[:, None] // SEL_BLOCK == np.arange(D)[None, :], BF16)
    return kvc[0], kvc[1], jnp.pad(kv, ((WINDOW, 0), (0, 0))), block_onehot


def _nsa_layer(h, layer, norm_g, wq, wo, kvs, tables):
    t = h.shape[0]
    nq = NSA_HEADS * NSA_HD
    q = norm_matmul(h, norm_g, wq, layer=layer, n=nq, scale=NSA_HD ** -0.5 * LOG2E, out_dtype=BF16)
    gates = norm_matmul(h, norm_g, _pad_cols(wq[layer, :, nq:], 128).astype(BF16), act="sigmoid")
    gates = gates[:, :NSA_HEADS * N_BRANCH].reshape(t, NSA_GROUPS, NSA_REP * N_BRANCH).transpose(1, 0, 2)
    o = nsa_attention(q, gates, *kvs, *tables)
    return matmul(o, wo, layer=layer, residual=h)


def kernel(x, a_norm_g, rw_mu, rw_wr, rw_wk, rw_wv, rw_wo, rw_w0, rw_w1, rw_w2, rw_a0, rw_a1, rw_a2,
           rw_v0, rw_v1, rw_v2, rw_g1, rw_g2, rw_kk, rw_ka, rw_rk, rw_lnx_w, rw_lnx_b, kv_norm_g, w_kv,
           cmp_pe, cmp_w1, cmp_b1, cmp_w2, b_norm_g, nsa_wq, nsa_wo, rel_bias, m_norm_g, mlp_up,
           mlp_down, final_norm_g):
    b, t, d = x.shape
    assert b == 1 and d == D_MODEL and t % SEL_TILE == 0 and t // SEL_BLOCK <= 128 and WINDOW == 4 * QBLK
    h = x.reshape(t, d)
    v_first = None
    kvs = tables = None
    rw_w = (rw_wr, rw_wk, rw_wv, rw_wo)
    for layer in range(DEPTH):
        if layer < N_A_LAYERS:
            i = layer
            vp = (None, None, None) if i == 0 else (rw_v0[i - 1], rw_v1[i - 1], rw_v2[i - 1])
            h, v_first = _rwkv_layer(
                h, i, a_norm_g[i], rw_mu[i], *rw_w, rw_w0[i], rw_w1[i],
                rw_w2[i], rw_a0[i], rw_a1[i], rw_a2[i], rw_g1[i], rw_g2[i], rw_kk[i], rw_ka[i],
                rw_rk[i].reshape(1, RW_HEADS, RW_HEAD), rw_lnx_w[i], rw_lnx_b[i], v_first, *vp)
        else:
            j = layer - N_A_LAYERS
            if j == 0:
                kvs = _nsa_shared_kv(h, kv_norm_g, w_kv, cmp_pe, cmp_w1, cmp_b1, cmp_w2)
                gs3, bs = _nsa_bias_tables(rel_bias)
                tables = (gs3, _overlap_table(t // CMP_STRIDE, 128), bs)
            h = _nsa_layer(h, j, b_norm_g[j], nsa_wq, nsa_wo, kvs, tables)
        h = _mlp(h, layer, m_norm_g[layer], mlp_up, mlp_down)
    return rmsnorm(h, final_norm_g, out_dtype=x.dtype).reshape(b, t, d)
```

```python
import functools
import math

import numpy as np
import jax
import jax.numpy as jnp
from jax import lax
from jax.experimental import pallas as pl
from jax.experimental.pallas import tpu as pltpu

F32 = jnp.float32
BF16 = jnp.bfloat16

D_MODEL = 2048
DEPTH = 4
N_A_LAYERS = 2
RW_HEAD = 64
RW_HEADS = D_MODEL // RW_HEAD
N_SHIFT_MIX = 6
GN_EPS = 64e-5
NSA_HEADS = 16
NSA_GROUPS = 4
NSA_REP = NSA_HEADS // NSA_GROUPS
NSA_HD = D_MODEL // NSA_HEADS
CMP_STRIDE = 16
CMP_LEN = 32
SEL_BLOCK = 64
SEL_TOP = 16
WINDOW = 512
QBLK = 128
N_BRANCH = 3
SEL_FORCE = 1e4
NEG_INF = -1e30
REL_BUCKETS = 32
REL_MAX_DIST = 128
NORM_EPS = 1e-6
LOG2E = math.log2(math.e)

V7X_VMEM_BYTES = 64 * 1024 * 1024
VMEM_LIMIT_BYTES = V7X_VMEM_BYTES - 8 * 1024 * 1024
NSA_VMEM_LIMIT_BYTES = V7X_VMEM_BYTES - 4 * 1024 * 1024
SCAN_CHUNK = 64
SCAN_HEADS = 4
SCAN_WIDTH = SCAN_HEADS * RW_HEAD
SCAN_UNITS = 8
SEL_TILE = 512
NSA_STEP_GROUPS = 2
CMP_BAND = 16
CMP_BAND_LO = 6
CMP_BIAS_K = 32
assert CMP_BAND < CMP_BIAS_K and 4 * CMP_BIAS_K == NSA_HD and NSA_HD == 128


def _cparams(sem, vmem_limit_bytes=VMEM_LIMIT_BYTES):
    return pltpu.CompilerParams(dimension_semantics=sem, vmem_limit_bytes=vmem_limit_bytes)


def _split3(x):
    h = x.astype(BF16)
    r = x - h.astype(F32)
    m = r.astype(BF16)
    l = (r - m.astype(F32)).astype(BF16)
    return h, m, l


def _dot(a, b):
    return jnp.dot(a, b, preferred_element_type=F32)


def _dot_nt(a, b):
    return lax.dot_general(a, b, (((1,), (1,)), ((), ())), preferred_element_type=F32)


def _dot_tn(a, b):
    return lax.dot_general(a, b, (((0,), (0,)), ((), ())), preferred_element_type=F32)


def _dot_exact_lhs(a_bf16, x):
    h, m, l = _split3(x)
    return _dot(a_bf16, h) + _dot(a_bf16, m) + _dot(a_bf16, l)


def _dot_exact_rhs(x, b_bf16):
    h, m, l = _split3(x)
    return _dot(h, b_bf16) + _dot(m, b_bf16) + _dot(l, b_bf16)


def _activate(acc, act, scale):
    if act == "relu2":
        acc = jnp.square(jnp.maximum(acc, 0.0))
    elif act == "tanh":
        acc = jnp.tanh(acc)
    elif act == "sigmoid":
        acc = jax.nn.sigmoid(acc)
    return acc if scale is None else acc * scale


def _matmul_kernel(x_ref, w_ref, *rest, act, scale, has_res, nk):
    if has_res:
        res_ref, o_ref, *scratch = rest
    else:
        o_ref, *scratch = rest

    def epilogue(acc):
        acc = _activate(acc, act, scale)
        if has_res:
            acc = acc + res_ref[...]
        o_ref[...] = acc.astype(o_ref.dtype)

    if nk == 1:
        epilogue(_dot(x_ref[...], w_ref[...].astype(BF16)))
    else:
        acc_ref, = scratch
        k = pl.program_id(2)

        @pl.when(k == 0)
        def _():
            acc_ref[...] = jnp.zeros_like(acc_ref)

        acc_ref[...] += _dot(x_ref[...], w_ref[...].astype(BF16))

        @pl.when(k == nk - 1)
        def _():
            epilogue(acc_ref[...])


def _pick_tile(n, pref):
    t = min(n, pref)
    while n % t:
        t //= 2
    return t


def _weight_spec(w, layer, rows, cols, index):
    if w.ndim == 2:
        return pl.BlockSpec((rows, cols), index)
    return pl.BlockSpec((None, rows, cols), lambda *g: (layer,) + tuple(index(*g)))


def matmul(x, w, *, layer=None, act=None, scale=None, residual=None, out_dtype=F32, tm=1024, tn=512, tk=2048):
    m, kdim = x.shape
    n = w.shape[-1]
    tm, tn, tk = _pick_tile(m, tm), _pick_tile(n, tn), _pick_tile(kdim, tk)
    nk = kdim // tk
    in_specs = [pl.BlockSpec((tm, tk), lambda j, i, k: (i, k)),
                _weight_spec(w, layer, tk, tn, lambda j, i, k: (k, j))]
    args = [x, w]
    if residual is not None:
        in_specs.append(pl.BlockSpec((tm, tn), lambda j, i, k: (i, j)))
        args.append(residual)
    return pl.pallas_call(
        functools.partial(_matmul_kernel, act=act, scale=scale, has_res=residual is not None, nk=nk),
        out_shape=jax.ShapeDtypeStruct((m, n), out_dtype),
        grid=(n // tn, m // tm, nk),
        in_specs=in_specs,
        out_specs=pl.BlockSpec((tm, tn), lambda j, i, k: (i, j)),
        scratch_shapes=[pltpu.VMEM((tm, tn), F32)] if nk > 1 else [],
        compiler_params=_cparams(("parallel", "parallel", "arbitrary")),
        name="matmul",
    )(*args)


def _lora_kernel(x_ref, w1_ref, w2_ref, o_ref, *, act):
    hidden = _activate(_dot(x_ref[...], w1_ref[...]), act, None).astype(BF16)
    o_ref[...] = _dot(hidden, w2_ref[...])


def lora(x, w1, w2, *, act=None, tm=1024):
    m, kdim = x.shape
    r, n = w2.shape
    tm = _pick_tile(m, tm)
    return pl.pallas_call(
        functools.partial(_lora_kernel, act=act),
        out_shape=jax.ShapeDtypeStruct((m, n), F32),
        grid=(m // tm,),
        in_specs=[pl.BlockSpec((tm, kdim), lambda i: (i, 0)),
                  pl.BlockSpec((kdim, r), lambda i: (0, 0)),
                  pl.BlockSpec((r, n), lambda i: (0, 0))],
        out_specs=pl.BlockSpec((tm, n), lambda i: (i, 0)),
        compiler_params=_cparams(("parallel",)),
        name="lora",
    )(x, w1, w2)


def _rms(x, g):
    return x * lax.rsqrt(jnp.mean(x * x, axis=-1, keepdims=True) + NORM_EPS) * g


def _rmsnorm_kernel(h_ref, g_ref, o_ref):
    o_ref[...] = _rms(h_ref[...], g_ref[...]).astype(o_ref.dtype)


def rmsnorm(h, g, out_dtype=BF16, tm=512):
    t, d = h.shape
    tm = _pick_tile(t, tm)
    return pl.pallas_call(
        _rmsnorm_kernel,
        out_shape=jax.ShapeDtypeStruct((t, d), out_dtype),
        grid=(t // tm,),
        in_specs=[pl.BlockSpec((tm, d), lambda i: (i, 0)), pl.BlockSpec((1, d), lambda i: (0, 0))],
        out_specs=pl.BlockSpec((tm, d), lambda i: (i, 0)),
        compiler_params=_cparams(("parallel",)),
        name="rmsnorm",
    )(h, g.reshape(1, d))


def _norm_matmul_kernel(h_ref, g_ref, w_ref, o_ref, xn_ref, *, act, scale):
    @pl.when(pl.program_id(1) == 0)
    def _():
        xn_ref[...] = _rms(h_ref[...], g_ref[...]).astype(xn_ref.dtype)

    o_ref[...] = _activate(_dot(xn_ref[...], w_ref[...].astype(BF16)), act, scale).astype(o_ref.dtype)


def norm_matmul(h, g, w, *, layer=None, act=None, scale=None, out_dtype=F32, n=None, tm=2048, tn=256):
    m, kdim = h.shape
    n = w.shape[-1] if n is None else n
    tm, tn = _pick_tile(m, tm), _pick_tile(n, tn)
    return pl.pallas_call(
        functools.partial(_norm_matmul_kernel, act=act, scale=scale),
        out_shape=jax.ShapeDtypeStruct((m, n), out_dtype),
        grid=(m // tm, n // tn),
        in_specs=[pl.BlockSpec((tm, kdim), lambda i, j: (i, 0)),
                  pl.BlockSpec((1, kdim), lambda i, j: (0, 0)),
                  _weight_spec(w, layer, kdim, tn, lambda i, j: (0, j))],
        out_specs=pl.BlockSpec((tm, tn), lambda i, j: (i, j)),
        scratch_shapes=[pltpu.VMEM((tm, kdim), BF16)],
        compiler_params=_cparams(("parallel", "arbitrary")),
        name="norm_matmul",
    )(h, g.reshape(1, kdim), w)


def _rwkv_prep_kernel(h_ref, prev_ref, g_ref, mu_ref, *o_refs):
    g = g_ref[...]
    xn = _rms(h_ref[...], g)
    prev = _rms(prev_ref[...], g)[7:8, :] * (pl.program_id(0) > 0).astype(F32)
    row = lax.broadcasted_iota(jnp.int32, xn.shape, 0)
    shifted = jnp.where(row == 0, prev, pltpu.roll(xn, 1, axis=0))
    xx = shifted - xn
    for i, o_ref in enumerate(o_refs):
        o_ref[...] = (xn + xx * mu_ref[i:i + 1, :]).astype(o_ref.dtype)


def rwkv_prep(h, g, mu, tm=256):
    t, d = h.shape
    tm = _pick_tile(t, tm)
    blk = pl.BlockSpec((tm, d), lambda i: (i, 0))
    return pl.pallas_call(
        _rwkv_prep_kernel,
        out_shape=[jax.ShapeDtypeStruct((t, d), BF16)] * N_SHIFT_MIX,
        grid=(t // tm,),
        in_specs=[blk,
                  pl.BlockSpec((8, d), lambda i: (jnp.maximum(i * (tm // 8) - 1, 0), 0)),
                  pl.BlockSpec((1, d), lambda i: (0, 0)),
                  pl.BlockSpec((N_SHIFT_MIX, d), lambda i: (0, 0))],
        out_specs=[blk] * N_SHIFT_MIX,
        compiler_params=_cparams(("parallel",)),
        name="rwkv_prep",
    )(h, h, g.reshape(1, d), mu)


def _head_mask(dtype):
    row = lax.broadcasted_iota(jnp.int32, (SCAN_WIDTH, SCAN_WIDTH), 0)
    col = lax.broadcasted_iota(jnp.int32, (SCAN_WIDTH, SCAN_WIDTH), 1)
    return jnp.where((row >> 6) == (col >> 6), 1.0, 0.0).astype(dtype)


def _block_diag(x, mask):
    return jnp.concatenate([x] * SCAN_HEADS, axis=0) * mask


def _fold_rows(x_full, mask_f32):
    x = x_full * mask_f32
    L = SCAN_CHUNK
    return x[0:L] + x[L:2 * L] + x[2 * L:3 * L] + x[3 * L:4 * L]


def _split2(x):
    hi = x.astype(BF16)
    return hi, (x - hi.astype(F32)).astype(BF16)


def _head_sums(xs, mask_b):
    n = xs[0].shape[0]
    r = _dot(jnp.concatenate([piece for x in xs for piece in _split3(x)], axis=0), mask_b)
    return [r[3 * i * n:(3 * i + 1) * n] + r[(3 * i + 1) * n:(3 * i + 2) * n] + r[(3 * i + 2) * n:(3 * i + 3) * n]
            for i in range(len(xs))]


def _bd_parts(y_split, mask_b):
    hi, lo = y_split
    return _block_diag(hi, mask_b), _block_diag(lo, mask_b)


def _mm_heads(x_split, y_parts):
    x_hi, x_lo = x_split
    y_hi, y_lo = y_parts
    n = x_hi.shape[0]
    both = _dot(jnp.concatenate([x_hi, x_lo], axis=0), y_hi)
    return both[:n] + both[n:] + _dot(x_hi, y_lo)


def _run_interleaved(stage_generators):
    live = list(stage_generators)
    while live:
        still = []
        for gen in live:
            try:
                next(gen)
                still.append(gen)
            except StopIteration:
                pass
        live = still


def _tri_inverse_heads(a, j_row, m_col, mask_b):
    L = SCAN_CHUNK
    eye = jnp.where(m_col == j_row, 1.0, 0.0)
    d = jnp.where((j_row >> 4) == (m_col >> 4), a, 0.0)
    e = a - d
    p = eye + d
    d_split = _split2(d)
    x = _mm_heads(d_split, _bd_parts(d_split, mask_b))
    yield
    for _ in range(2):
        x_split, p_split = _split2(x), _split2(p)
        stacked = tuple(jnp.concatenate([xs, ps], axis=0) for xs, ps in zip(x_split, p_split))
        both = _mm_heads(stacked, _bd_parts(x_split, mask_b))
        x, p = both[:L], p + both[L:]
        yield
    p = p + _mm_heads(_split2(p), _bd_parts(_split2(x), mask_b))
    yield
    p_split = _split2(p)
    n1 = _mm_heads(p_split, _bd_parts(_split2(e), mask_b))
    yield
    n1_split = _split2(n1)
    n2 = _mm_heads(n1_split, _bd_parts(n1_split, mask_b))
    yield
    q = eye + n1
    q = q + _mm_heads(_split2(q), _bd_parts(_split2(n2), mask_b))
    yield
    return _mm_heads(_split2(q), _bd_parts(p_split, mask_b))


def _softplus(x):
    return jnp.maximum(x, 0.0) + jnp.log(1.0 + jnp.exp(-jnp.abs(x)))


def _rwkv_chunk_kernel(*refs, mix_v):
    if mix_v:
        r_ref, k_ref, v_ref, wl_ref, al_ref, vl_ref, vf_ref, par_ref, m_ref, c_ref, qp_ref, y0_ref, bo_ref = refs
    else:
        r_ref, k_ref, v_ref, wl_ref, al_ref, par_ref, m_ref, c_ref, qp_ref, y0_ref, bo_ref = refs
    L, W = SCAN_CHUNK, SCAN_WIDTH
    mask_b = _head_mask(BF16)
    mask_f = _head_mask(F32)
    tri_incl = jnp.where(lax.broadcasted_iota(jnp.int32, (L, L), 1)
                         <= lax.broadcasted_iota(jnp.int32, (L, L), 0), 1.0, 0.0).astype(BF16)
    j_row = lax.broadcasted_iota(jnp.int32, (L, W), 0)
    m_col = lax.broadcasted_iota(jnp.int32, (L, W), 1) & (L - 1)
    strict = m_col < j_row
    incl = m_col <= j_row

    def unit(u):
        cols = slice(u * W, (u + 1) * W)
        par = par_ref[:, cols]
        w0, a0, v0, k_k, k_a, r_k = (par[i:i + 1] for i in range(6))
        r, k, v = r_ref[:, cols], k_ref[:, cols], v_ref[:, cols]
        lw = -jnp.exp(-_softplus(-(w0 + wl_ref[:, cols])) - 0.5)
        a_s = jax.nn.sigmoid(a0 + al_ref[:, cols])
        if mix_v:
            v = v + (vf_ref[:, cols] - v) * jax.nn.sigmoid(v0 + vl_ref[:, cols])
        kk = k * k_k
        k = k * (1.0 + (a_s - 1.0) * k_a)
        kk_sq, rk_sum = _head_sums([kk * kk, r * k * r_k], mask_b)
        c = _dot_exact_lhs(tri_incl, lw)
        yield
        kk = kk * lax.rsqrt(jnp.maximum(kk_sq, 1e-24))
        bo_ref[:, cols] = rk_sum * v
        c_last = c[L - 1:L, :]
        p_inv = jnp.exp(-c)
        p_rest = jnp.exp(c_last - c)
        rt = r * jnp.exp(c)
        at = (-kk * jnp.exp(c - lw)).astype(BF16)
        b = kk * a_s
        vb = v.astype(BF16)
        ar = jnp.concatenate([at, rt.astype(BF16)], axis=0)
        g_b = _dot_nt(ar, _block_diag((b * p_inv).astype(BF16), mask_b))
        g_k = _dot_nt(ar, _block_diag((k * p_inv).astype(BF16), mask_b))
        yield
        a_ab = jnp.where(strict, g_b[:L], 0.0)
        a_ak = jnp.where(strict, g_k[:L], 0.0).astype(BF16)
        a_rb = jnp.where(incl, g_b[L:], 0.0).astype(BF16)
        a_rk = jnp.where(incl, g_k[L:], 0.0).astype(BF16)
        v_bd = _block_diag(vb, mask_b)
        akv = _dot(a_ak, v_bd)
        t_inv = yield from _tri_inverse_heads(a_ab, j_row, m_col, mask_b)
        yield
        t_split = _split2(t_inv)
        w_both = _dot(jnp.concatenate(t_split, axis=0), _block_diag(at, mask_b))
        u0 = _mm_heads(t_split, _bd_parts(_split2(akv), mask_b)).astype(BF16)
        yield
        w = (w_both[:L] + w_both[L:]).astype(BF16)
        qp_ref[:, cols] = (rt + _dot(a_rb, _block_diag(w, mask_b))).astype(qp_ref.dtype)
        y0_ref[:, cols] = _dot(a_rb, _block_diag(u0, mask_b)) + _dot(a_rk, v_bd)
        bh = (b * p_rest).astype(BF16)
        kh = (k * p_rest).astype(BF16)
        decay_diag = jnp.where(m_col == j_row, jnp.exp(c_last), 0.0)
        m_ref[:, cols] = _fold_rows(_dot_tn(bh, w), mask_f) + decay_diag
        c_ref[:, cols] = _fold_rows(_dot_tn(bh, u0) + _dot_tn(kh, vb), mask_f)

    _run_interleaved(unit(u) for u in range(SCAN_UNITS))


def rwkv_chunks(r, k, v, w_lin, a_lin, v_lin, v_first, params):
    t, d = r.shape
    bw = SCAN_UNITS * SCAN_WIDTH
    blk = pl.BlockSpec((SCAN_CHUNK, bw), lambda ci, gi: (ci, gi))
    mix_v = v_lin is not None
    args = [r, k, v, w_lin, a_lin] + ([v_lin, v_first] if mix_v else []) + [params]
    f32 = jax.ShapeDtypeStruct((t, d), F32)
    return pl.pallas_call(
        functools.partial(_rwkv_chunk_kernel, mix_v=mix_v),
        out_shape=[f32, f32, jax.ShapeDtypeStruct((t, d), BF16), f32, f32],
        grid=(t // SCAN_CHUNK, d // bw),
        in_specs=[blk] * (len(args) - 1) + [pl.BlockSpec((8, bw), lambda ci, gi: (0, gi))],
        out_specs=[blk] * 5,
        compiler_params=_cparams(("parallel", "parallel")),
        name="rwkv_chunks",
    )(*args)


def _rwkv_state_kernel(m_ref, c_ref, qp_ref, y0_ref, bo_ref, g_ref, ln_ref, o_ref, st_ref):
    @pl.when(pl.program_id(1) == 0)
    def _():
        st_ref[...] = jnp.zeros_like(st_ref)

    W = SCAN_WIDTH
    mask_b = _head_mask(BF16)
    def unit(u):
        cols = slice(u * W, (u + 1) * W)
        st_parts = _bd_parts(_split2(st_ref[u]), mask_b)
        y = _dot(qp_ref[:, cols], st_parts[0]) + y0_ref[:, cols]
        st_ref[u] = _mm_heads(_split2(m_ref[:, cols]), st_parts) + c_ref[:, cols]
        yield
        mean = _head_sums([y], mask_b)[0] * (1.0 / RW_HEAD)
        yield
        dev = y - mean
        var = _head_sums([dev * dev], mask_b)[0] * (1.0 / RW_HEAD)
        yield
        ln = ln_ref[:, cols]
        out = (dev * lax.rsqrt(var + GN_EPS) * ln[0:1] + ln[1:2] + bo_ref[:, cols]) * g_ref[:, cols]
        o_ref[:, cols] = out.astype(o_ref.dtype)

    _run_interleaved(unit(u) for u in range(SCAN_UNITS))


def rwkv_state_scan(m, c, qp, y0, bonus, g, ln_params):
    t, d = m.shape
    bw = SCAN_UNITS * SCAN_WIDTH
    blk = pl.BlockSpec((SCAN_CHUNK, bw), lambda gi, ci: (ci, gi))
    return pl.pallas_call(
        _rwkv_state_kernel,
        out_shape=jax.ShapeDtypeStruct((t, d), BF16),
        grid=(d // bw, t // SCAN_CHUNK),
        in_specs=[blk] * 6 + [pl.BlockSpec((8, bw), lambda gi, ci: (0, gi))],
        out_specs=blk,
        scratch_shapes=[pltpu.VMEM((SCAN_UNITS, RW_HEAD, SCAN_WIDTH), F32)],
        compiler_params=_cparams(("parallel", "arbitrary")),
        name="rwkv_state_scan",
    )(m, c, qp, y0, bonus, g, ln_params)


def _compress_kernel(x_ref, w1_ref, c_ref, w2_ref, o_ref):
    ab = _dot(x_ref[...], w1_ref[...])
    a = ab[:, :NSA_HD]
    bsh = ab[:, NSA_HD:]
    n16 = a.shape[0]
    row = lax.broadcasted_iota(jnp.int32, a.shape, 0)
    b_next = jnp.where(row == n16 - 1, 0.0, pltpu.roll(bsh, n16 - 1, axis=0))
    hid = a + b_next + c_ref[...]
    hid = jax.nn.gelu(hid, approximate=True)
    o_ref[...] = _dot(hid.astype(BF16), w2_ref[...]).astype(o_ref.dtype)


def compress(x16, w1cat, cvec, w2):
    _, g, n16, kd = x16.shape
    d = NSA_HD
    return pl.pallas_call(
        _compress_kernel,
        out_shape=jax.ShapeDtypeStruct((2, g, n16, d), BF16),
        grid=(2, g),
        in_specs=[pl.BlockSpec((None, None, n16, kd), lambda p, gi: (p, gi, 0, 0)),
                  pl.BlockSpec((None, kd, 2 * d), lambda p, gi: (p, 0, 0)),
                  pl.BlockSpec((None, 1, d), lambda p, gi: (p, 0, 0)),
                  pl.BlockSpec((None, d, d), lambda p, gi: (p, 0, 0))],
        out_specs=pl.BlockSpec((None, None, n16, d), lambda p, gi: (p, gi, 0, 0)),
        compiler_params=_cparams(("parallel", "parallel")),
        name="nsa_compress",
    )(x16, w1cat, cvec, w2)


def _with_ones(v):
    return jnp.concatenate([v, jnp.ones_like(v)], axis=1)


def _nsa_kernel(q_ref, gate_ref, kc_ref, vc_ref, ks_ref, vs_ref, kw_ref, vw_ref, oh_ref,
                gs_ref, ov_ref, bs_ref, o_ref, s_buf, p_buf):
    qi = pl.program_id(1)
    R, Q, D, NG = NSA_REP, QBLK, NSA_HD, NSA_STEP_GROUPS
    n_cp = kc_ref.shape[1]
    n_sb = ov_ref.shape[1]
    n_far = jnp.maximum(qi - 1, 0)
    n0 = pl.multiple_of(jnp.maximum(qi - 1, 0) * Q, Q)
    n_tiles = (n_far + (SEL_TILE // Q - 1)) // (SEL_TILE // Q)
    max_tile = oh_ref.shape[0] // SEL_TILE - 1
    state = [{} for _ in range(NG)]

    def slab(ref, gl, start, n):
        return ref[pl.ds(start, n), gl * D:(gl + 1) * D]

    def sel_keys(gl, start, n):
        return jnp.concatenate([slab(ks_ref, gl, WINDOW + start, n), oh_ref[pl.ds(start, n), :]], axis=1)

    def key_tile(gl, i):
        return sel_keys(gl, pl.multiple_of(i * SEL_TILE, SEL_TILE), SEL_TILE)

    def value_tile(gl, i):
        return _with_ones(slab(vs_ref, gl, pl.multiple_of(WINDOW + i * SEL_TILE, SEL_TILE), SEL_TILE))

    def before_loop(gl):
        st = state[gl]
        q = q_ref[:, gl * R * D:(gl + 1) * R * D]
        qs = jnp.concatenate([q[:, r * D:(r + 1) * D] for r in range(R)], axis=0)
        t_rows = qi * Q + (lax.broadcasted_iota(jnp.int32, (R * Q, 1), 0) & (Q - 1))
        c_idx = lax.broadcasted_iota(jnp.int32, (1, n_cp), 1)
        m_off = qi * (Q // CMP_STRIDE) - lax.broadcasted_iota(jnp.int32, (n_cp, D), 0) + CMP_BAND_LO
        k_idx = lax.broadcasted_iota(jnp.int32, (n_cp, D), 1) & (CMP_BIAS_K - 1)
        in_band = ((k_idx < CMP_BAND) & (m_off == k_idx)) | ((k_idx == CMP_BAND) & (m_off >= CMP_BAND))
        k_c = jnp.concatenate([kc_ref[gl], jnp.where(in_band, 1.0, 0.0).astype(BF16)], axis=1)
        qk_c = _dot_nt(jnp.concatenate([qs, gs_ref[gl]], axis=1), k_c)
        w0 = pl.multiple_of(qi * Q, Q)
        qk_w = _dot_nt(qs, slab(kw_ref, gl, w0, WINDOW + Q))
        yield
        m_c = (c_idx * CMP_STRIDE + (CMP_LEN - 1)) <= t_rows
        s_c = jnp.where(m_c, qk_c, NEG_INF)
        e_c = jnp.exp2(s_c - jnp.max(s_c, axis=-1, keepdims=True))
        acc_c = _dot(e_c.astype(BF16), _with_ones(vc_ref[gl]))
        yield
        inv_l = jnp.where(t_rows >= CMP_LEN - 1, 1.0 / acc_c[:, D:D + 1], 0.0)
        st["o_c"] = acc_c[:, :D] * inv_l
        p_c = e_c * inv_l
        p_sum = p_c[0:Q] + p_c[Q:2 * Q] + p_c[2 * Q:3 * Q] + p_c[3 * Q:4 * Q]
        imp = _dot_exact_rhs(p_sum, ov_ref[...])
        j_io = lax.broadcasted_iota(jnp.int32, (1, WINDOW + Q), 1)
        oldest = jnp.where(lax.broadcasted_iota(jnp.int32, (R * Q, Q), 1) > (t_rows & (Q - 1)), 0.0, NEG_INF)
        bias_w = jnp.concatenate([oldest, jnp.zeros((R * Q, WINDOW - 2 * Q), F32), bs_ref[gl, 0]], axis=1)
        s_w = jnp.where(j_io >= WINDOW - qi * Q, qk_w + bias_w, NEG_INF)
        e_w = jnp.exp2(s_w - jnp.max(s_w, axis=-1, keepdims=True))
        acc_w = _dot(e_w.astype(BF16), _with_ones(slab(vw_ref, gl, w0, WINDOW + Q)))
        yield
        st["o_w"] = acc_w[:, :D] / acc_w[:, D:]
        imp_t = imp.T
        s_row = lax.broadcasted_iota(jnp.int32, (n_sb, Q), 0)
        t_col = qi * Q + lax.broadcasted_iota(jnp.int32, (n_sb, Q), 1)
        forced = (s_row == (t_col >> 6)) | (s_row == 0)
        score = jnp.where(forced, SEL_FORCE, jnp.where(s_row * SEL_BLOCK <= t_col, imp_t, -SEL_FORCE))
        sel_t = jnp.zeros((n_sb, Q), F32)
        s_f = s_row.astype(F32)
        for _ in range(SEL_TOP):
            mx = jnp.max(score, axis=0, keepdims=True)
            first = jnp.min(jnp.where(score == mx, s_f, float(n_sb)), axis=0, keepdims=True)
            pick = s_f == first
            sel_t = jnp.where(pick, 1.0, sel_t)
            score = jnp.where(pick, -3e38, score)
        sel = sel_t.T
        blk = lax.broadcasted_iota(jnp.int32, (Q, n_sb), 1)
        neg_near = jnp.where(sel > 0.5, 0.0, NEG_INF).astype(BF16)
        neg_far = jnp.where((sel > 0.5) & (blk < 2 * n_far), 0.0, NEG_INF).astype(BF16)
        q_near = jnp.concatenate([qs, jnp.concatenate([neg_near] * R, axis=0)], axis=1)
        st["q_far"] = jnp.concatenate([qs, jnp.concatenate([neg_far] * R, axis=0)], axis=1)
        s = _dot_nt(q_near, sel_keys(gl, n0, 2 * Q)) + bs_ref[gl, jnp.where(qi == 0, 1, 0)]
        s_far = _dot_nt(st["q_far"], key_tile(gl, 0))
        s_buf[gl, 0] = s_far
        p_buf[gl, 1] = jnp.zeros(p_buf.shape[2:], p_buf.dtype)
        yield
        st["row_max"] = jnp.max(s_far, axis=-1, keepdims=True)
        st["m"] = jnp.max(s, axis=-1, keepdims=True)
        st["acc"] = _dot(jnp.exp2(s - st["m"]).astype(BF16), _with_ones(slab(vs_ref, gl, WINDOW + n0, 2 * Q)))

    _run_interleaved([before_loop(gl) for gl in range(NG)])

    def half_step(gl, i, slot, carry):
        m_run, acc, row_max = carry
        s_next = _dot_nt(state[gl]["q_far"], key_tile(gl, jnp.minimum(i + 1, max_tile)))
        s_buf[gl, 1 - slot] = s_next
        pv = _dot(p_buf[gl, 1 - slot], value_tile(gl, jnp.maximum(i - 1, 0)))
        m_new = jnp.maximum(m_run, row_max)
        acc = jnp.exp2(m_run - m_new) * (acc + pv)
        p_buf[gl, slot] = jnp.exp2(s_buf[gl, slot] - m_new).astype(p_buf.dtype)
        return m_new, acc, jnp.max(s_next, axis=-1, keepdims=True)

    def far_loop(gl):
        st = state[gl]
        pair_body = lambda j, carry: half_step(gl, 2 * j + 1, 1, half_step(gl, 2 * j, 0, carry))
        return lax.fori_loop(0, n_pairs, pair_body, (st["m"], st["acc"], st["row_max"]))

    n_pairs = (n_tiles + 1) >> 1
    carries = [far_loop(gl) for gl in range(NG)]
    last_tile = jnp.maximum(2 * n_pairs - 1, 0)
    for gl, (_, acc_s, _) in enumerate(carries):
        acc_s = acc_s + _dot(p_buf[gl, 1], value_tile(gl, last_tile))
        o_s = acc_s[:, :D] / acc_s[:, D:]
        o_c, o_w = state[gl]["o_c"], state[gl]["o_w"]
        gates = gate_ref[gl]
        for r in range(R):
            rows = slice(r * Q, (r + 1) * Q)
            o = (gates[:, 3 * r:3 * r + 1] * o_c[rows] + gates[:, 3 * r + 1:3 * r + 2] * o_s[rows]
                 + gates[:, 3 * r + 2:3 * r + 3] * o_w[rows])
            o_ref[:, (gl * R + r) * D:(gl * R + r + 1) * D] = o.astype(o_ref.dtype)


def nsa_attention(q, gates, kc, vc, kv_padded, block_onehot, gs, ov, bs):
    t = q.shape[0]
    g, n_cp, d = kc.shape
    rq = NSA_REP * QBLK
    ng = NSA_STEP_GROUPS
    assert g % ng == 0
    per_g = lambda shape: pl.BlockSpec((ng,) + shape, lambda gi, qi: (gi,) + (0,) * len(shape))
    kv_part = lambda part: pl.BlockSpec((t + WINDOW, ng * d), lambda gi, qi: (0, part * g // ng + gi))
    return pl.pallas_call(
        _nsa_kernel,
        out_shape=jax.ShapeDtypeStruct((t, NSA_HEADS * d), BF16),
        grid=(g // ng, t // QBLK),
        in_specs=[pl.BlockSpec((QBLK, ng * NSA_REP * d), lambda gi, qi: (qi, gi)),
                  pl.BlockSpec((ng, QBLK, NSA_REP * N_BRANCH), lambda gi, qi: (gi, qi, 0)),
                  per_g((n_cp, d)), per_g((n_cp, d)),
                  kv_part(2), kv_part(3), kv_part(4), kv_part(5),
                  pl.BlockSpec((t, d), lambda gi, qi: (0, 0)),
                  per_g((rq, d)),
                  pl.BlockSpec(ov.shape, lambda gi, qi: (0, 0)),
                  per_g((2, rq, 2 * QBLK))],
        out_specs=pl.BlockSpec((QBLK, ng * NSA_REP * d), lambda gi, qi: (qi, gi)),
        scratch_shapes=[pltpu.VMEM((ng, 2, rq, SEL_TILE), F32), pltpu.VMEM((ng, 2, rq, SEL_TILE), BF16)],
        compiler_params=_cparams(("parallel", "arbitrary"), NSA_VMEM_LIMIT_BYTES),
        name="nsa_attention",
    )(q, gates, kc, vc, kv_padded, kv_padded, kv_padded, kv_padded, block_onehot, gs, ov, bs)


def _t5_bucket_np(n):
    n = np.maximum(n, 0)
    max_exact = REL_BUCKETS // 2
    nf = np.maximum(n, 1).astype(np.float64)
    large = max_exact + (np.log(nf / max_exact) / math.log(REL_MAX_DIST / max_exact)
                         * (REL_BUCKETS - max_exact)).astype(np.int64)
    large = np.minimum(large, REL_BUCKETS - 1)
    return np.where(n < max_exact, n, large).astype(np.int32)


def _nsa_bias_tables(rel_bias):
    G, R, Q = NSA_GROUPS, NSA_REP, QBLK
    table = (rel_bias.astype(F32) * LOG2E).reshape(REL_BUCKETS, G, R).transpose(1, 2, 0)
    qq = np.arange(Q)

    def lookup(dist):
        return table[:, :, _t5_bucket_np(dist)].reshape(G, R * Q, dist.shape[1])

    m = np.arange(CMP_BAND) - CMP_BAND_LO
    gs = lookup(qq[:, None] - (CMP_LEN - 1) + CMP_STRIDE * m[None, :])
    far = jnp.broadcast_to(table[:, :, REL_BUCKETS - 1][:, :, None, None], (G, R, Q, 1)).reshape(G, R * Q, 1)
    gs = jnp.concatenate([gs, far, jnp.zeros((G, R * Q, CMP_BIAS_K - CMP_BAND - 1), F32)], axis=-1)
    h, mid, l = _split3(gs)
    gs3 = jnp.concatenate([h, mid, l, jnp.zeros_like(h)], axis=-1)
    def toeplitz(c):
        mm = np.arange(2 * Q)
        v = table[:, :, _t5_bucket_np(np.where(mm < Q, c - mm, c + 2 * Q - mm))]
        rows = jnp.tile(v, (1, 1, Q))[:, :, :Q * (2 * Q - 1)].reshape(G, R, Q, 2 * Q - 1)
        return rows[:, :, :, :Q].reshape(G, R * Q, Q)

    d0 = np.tile(qq[:, None] - qq[None, :], (R, 1))
    causal = jnp.asarray(np.where(d0 >= 0, 0.0, NEG_INF), F32)[None]
    b0 = toeplitz(0) + causal
    b1 = toeplitz(Q)
    far_c = jnp.broadcast_to(far, (G, R * Q, Q))
    neg = jnp.full((G, R * Q, Q), NEG_INF, F32)
    bs = jnp.stack([jnp.concatenate([b1 - far_c, b0 - far_c], -1),
                    jnp.concatenate([b0 - far_c, neg], -1)], axis=1)
    return gs3, bs


def _overlap_table(n_cp, n_sb):
    c = np.arange(n_cp)[:, None] * CMP_STRIDE
    s = np.arange(n_sb)[None, :] * SEL_BLOCK
    return jnp.asarray(((c <= s + SEL_BLOCK - 1) & (c + CMP_LEN - 1 >= s)), BF16)


def _pad_cols(w, n):
    return jnp.pad(w, ((0, 0), (0, n - w.shape[1])))


def _pad_rows(w, n):
    return jnp.pad(w, ((0, n - w.shape[0]), (0, 0)))


def _rwkv_layer(h, layer, norm_g, mu, wr, wk, wv, wo, w0, w1, w2, a0, a1, a2, g1, g2,
                k_k, k_a, r_k, lnx_w, lnx_b, v_first, v0, v1, v2):
    bf = lambda w: w.astype(BF16)
    xr, xw, xk, xv, xa, xg = rwkv_prep(h, norm_g, mu)
    r = matmul(xr, wr, layer=layer)
    k = matmul(xk, wk, layer=layer)
    v = matmul(xv, wv, layer=layer)
    w_lin = lora(xw, bf(_pad_cols(w1, 128)), bf(_pad_rows(w2, 128)), act="tanh")
    a_lin = lora(xa, bf(a1), bf(a2))
    g = lora(xg, bf(g1), bf(g2), act="sigmoid")
    zeros = jnp.zeros_like(w0)
    if v_first is None:
        v_lin, v0 = None, zeros
    else:
        v_lin = lora(xv, bf(_pad_cols(v1, 128)), bf(_pad_rows(v2, 128)))
    params = jnp.stack([w0, a0, v0, k_k, k_a, r_k.reshape(-1), zeros, zeros])
    m, c, qp, y0, bonus = rwkv_chunks(r, k, v, w_lin, a_lin, v_lin, v_first, params)
    ln_params = jnp.stack([lnx_w, lnx_b] + [zeros] * 6)
    gated = rwkv_state_scan(m, c, qp, y0, bonus, g, ln_params)
    h = matmul(gated, wo, layer=layer, residual=h)
    return h, (v if v_first is None else v_first)


def _mlp(h, layer, norm_g, w_up, w_down):
    hid = norm_matmul(h, norm_g, w_up, layer=layer, act="relu2", out_dtype=BF16)
    return matmul(hid, w_down, layer=layer, residual=h, tm=2048)


def _nsa_shared_kv(h, kv_norm_g, w_kv, cmp_pe, cmp_w1, cmp_b1, cmp_w2):
    t = h.shape[0]
    G, D = NSA_GROUPS, NSA_HD
    kv = norm_matmul(h, kv_norm_g, w_kv, out_dtype=BF16)
    x16 = kv[:, :2 * G * D].reshape(t, 2, G, D).transpose(1, 2, 0, 3).reshape(2, G, t // CMP_STRIDE, CMP_STRIDE * D)
    half = CMP_STRIDE * D
    w1cat = jnp.concatenate([cmp_w1[:, :half], cmp_w1[:, half:]], axis=-1).astype(BF16)
    cvec = jnp.einsum("plk,plkd->pd", cmp_pe, cmp_w1.reshape(2, CMP_LEN, D, D),
                      precision=lax.Precision.HIGHEST) + cmp_b1
    kvc = compress(x16, w1cat, cvec.reshape(2, 1, D), cmp_w2.astype(BF16))
    block_onehot = jnp.asarray(np.arange(t)[:, None] // SEL_BLOCK == np.arange(D)[None, :], BF16)
    return kvc[0], kvc[1], jnp.pad(kv, ((WINDOW, 0), (0, 0))), block_onehot


def _nsa_layer(h, layer, norm_g, wq, wo, kvs, tables):
    t = h.shape[0]
    nq = NSA_HEADS * NSA_HD
    q = norm_matmul(h, norm_g, wq, layer=layer, n=nq, scale=NSA_HD ** -0.5 * LOG2E, out_dtype=BF16)
    gates = norm_matmul(h, norm_g, _pad_cols(wq[layer, :, nq:], 128).astype(BF16), act="sigmoid")
    gates = gates[:, :NSA_HEADS * N_BRANCH].reshape(t, NSA_GROUPS, NSA_REP * N_BRANCH).transpose(1, 0, 2)
    o = nsa_attention(q, gates, *kvs, *tables)
    return matmul(o, wo, layer=layer, residual=h)


def kernel(x, a_norm_g, rw_mu, rw_wr, rw_wk, rw_wv, rw_wo, rw_w0, rw_w1, rw_w2, rw_a0, rw_a1, rw_a2,
           rw_v0, rw_v1, rw_v2, rw_g1, rw_g2, rw_kk, rw_ka, rw_rk, rw_lnx_w, rw_lnx_b, kv_norm_g, w_kv,
           cmp_pe, cmp_w1, cmp_b1, cmp_w2, b_norm_g, nsa_wq, nsa_wo, rel_bias, m_norm_g, mlp_up,
           mlp_down, final_norm_g):
    b, t, d = x.shape
    assert b == 1 and d == D_MODEL and t % SEL_TILE == 0 and t // SEL_BLOCK <= 128 and WINDOW == 4 * QBLK
    h = x.reshape(t, d)
    v_first = None
    kvs = tables = None
    rw_w = (rw_wr, rw_wk, rw_wv, rw_wo)
    for layer in range(DEPTH):
        if layer < N_A_LAYERS:
            i = layer
            vp = (None, None, None) if i == 0 else (rw_v0[i - 1], rw_v1[i - 1], rw_v2[i - 1])
            h, v_first = _rwkv_layer(
                h, i, a_norm_g[i], rw_mu[i], *rw_w, rw_w0[i], rw_w1[i],
                rw_w2[i], rw_a0[i], rw_a1[i], rw_a2[i], rw_g1[i], rw_g2[i], rw_kk[i], rw_ka[i],
                rw_rk[i].reshape(1, RW_HEADS, RW_HEAD), rw_lnx_w[i], rw_lnx_b[i], v_first, *vp)
        else:
            j = layer - N_A_LAYERS
            if j == 0:
                kvs = _nsa_shared_kv(h, kv_norm_g, w_kv, cmp_pe, cmp_w1, cmp_b1, cmp_w2)
                gs3, bs = _nsa_bias_tables(rel_bias)
                tables = (gs3, _overlap_table(t // CMP_STRIDE, 128), bs)
            h = _nsa_layer(h, j, b_norm_g[j], nsa_wq, nsa_wo, kvs, tables)
        h = _mlp(h, layer, m_norm_g[layer], mlp_up, mlp_down)
    return rmsnorm(h, final_norm_g, out_dtype=x.dtype).reshape(b, t, d)
```

```python
import functools
import math

import numpy as np
import jax
import jax.numpy as jnp
from jax import lax
from jax.experimental import pallas as pl
from jax.experimental.pallas import tpu as pltpu

F32 = jnp.float32
BF16 = jnp.bfloat16

D_MODEL = 2048
DEPTH = 4
N_A_LAYERS = 2
RW_HEAD = 64
RW_HEADS = D_MODEL // RW_HEAD
N_SHIFT_MIX = 6
GN_EPS = 64e-5
NSA_HEADS = 16
NSA_GROUPS = 4
NSA_REP = NSA_HEADS // NSA_GROUPS
NSA_HD = D_MODEL // NSA_HEADS
CMP_STRIDE = 16
CMP_LEN = 32
SEL_BLOCK = 64
SEL_TOP = 16
WINDOW = 512
QBLK = 128
N_BRANCH = 3
SEL_FORCE = 1e4
NEG_INF = -1e30
REL_BUCKETS = 32
REL_MAX_DIST = 128
NORM_EPS = 1e-6
LOG2E = math.log2(math.e)

V7X_VMEM_BYTES = 64 * 1024 * 1024
VMEM_LIMIT_BYTES = V7X_VMEM_BYTES - 8 * 1024 * 1024
NSA_VMEM_LIMIT_BYTES = V7X_VMEM_BYTES - 4 * 1024 * 1024
SCAN_CHUNK = 64
SCAN_HEADS = 4
SCAN_WIDTH = SCAN_HEADS * RW_HEAD
SCAN_UNITS = 8
SEL_TILE = 512
NSA_STEP_GROUPS = 2
CMP_BAND = 16
CMP_BAND_LO = 6
CMP_BIAS_K = 32
assert CMP_BAND < CMP_BIAS_K and 4 * CMP_BIAS_K == NSA_HD and NSA_HD == 128


def _cparams(sem, vmem_limit_bytes=VMEM_LIMIT_BYTES):
    return pltpu.CompilerParams(dimension_semantics=sem, vmem_limit_bytes=vmem_limit_bytes)


def _split3(x):
    h = x.astype(BF16)
    r = x - h.astype(F32)
    m = r.astype(BF16)
    l = (r - m.astype(F32)).astype(BF16)
    return h, m, l


def _dot(a, b):
    return jnp.dot(a, b, preferred_element_type=F32)


def _dot_nt(a, b):
    return lax.dot_general(a, b, (((1,), (1,)), ((), ())), preferred_element_type=F32)


def _dot_tn(a, b):
    return lax.dot_general(a, b, (((0,), (0,)), ((), ())), preferred_element_type=F32)


def _dot_exact_lhs(a_bf16, x):
    h, m, l = _split3(x)
    return _dot(a_bf16, h) + _dot(a_bf16, m) + _dot(a_bf16, l)


def _dot_exact_rhs(x, b_bf16):
    h, m, l = _split3(x)
    return _dot(h, b_bf16) + _dot(m, b_bf16) + _dot(l, b_bf16)


def _activate(acc, act, scale):
    if act == "relu2":
        acc = jnp.square(jnp.maximum(acc, 0.0))
    elif act == "tanh":
        acc = jnp.tanh(acc)
    elif act == "sigmoid":
        acc = jax.nn.sigmoid(acc)
    return acc if scale is None else acc * scale


def _matmul_kernel(x_ref, w_ref, *rest, act, scale, has_res, nk):
    if has_res:
        res_ref, o_ref, *scratch = rest
    else:
        o_ref, *scratch = rest

    def epilogue(acc):
        acc = _activate(acc, act, scale)
        if has_res:
            acc = acc + res_ref[...]
        o_ref[...] = acc.astype(o_ref.dtype)

    if nk == 1:
        epilogue(_dot(x_ref[...], w_ref[...].astype(BF16)))
    else:
        acc_ref, = scratch
        k = pl.program_id(2)

        @pl.when(k == 0)
        def _():
            acc_ref[...] = jnp.zeros_like(acc_ref)

        acc_ref[...] += _dot(x_ref[...], w_ref[...].astype(BF16))

        @pl.when(k == nk - 1)
        def _():
            epilogue(acc_ref[...])


def _pick_tile(n, pref):
    t = min(n, pref)
    while n % t:
        t //= 2
    return t


def _weight_spec(w, layer, rows, cols, index):
    if w.ndim == 2:
        return pl.BlockSpec((rows, cols), index)
    return pl.BlockSpec((None, rows, cols), lambda *g: (layer,) + tuple(index(*g)))


def matmul(x, w, *, layer=None, act=None, scale=None, residual=None, out_dtype=F32, tm=1024, tn=1024, tk=2048):
    m, kdim = x.shape
    n = w.shape[-1]
    tm, tn, tk = _pick_tile(m, tm), _pick_tile(n, tn), _pick_tile(kdim, tk)
    nk = kdim // tk
    in_specs = [pl.BlockSpec((tm, tk), lambda j, i, k: (i, k)),
                _weight_spec(w, layer, tk, tn, lambda j, i, k: (k, j))]
    args = [x, w]
    if residual is not None:
        in_specs.append(pl.BlockSpec((tm, tn), lambda j, i, k: (i, j)))
        args.append(residual)
    return pl.pallas_call(
        functools.partial(_matmul_kernel, act=act, scale=scale, has_res=residual is not None, nk=nk),
        out_shape=jax.ShapeDtypeStruct((m, n), out_dtype),
        grid=(n // tn, m // tm, nk),
        in_specs=in_specs,
        out_specs=pl.BlockSpec((tm, tn), lambda j, i, k: (i, j)),
        scratch_shapes=[pltpu.VMEM((tm, tn), F32)] if nk > 1 else [],
        compiler_params=_cparams(("parallel", "parallel", "arbitrary")),
        name="matmul",
    )(*args)


def _lora_kernel(x_ref, w1_ref, w2_ref, o_ref, *, act):
    hidden = _activate(_dot(x_ref[...], w1_ref[...]), act, None).astype(BF16)
    o_ref[...] = _dot(hidden, w2_ref[...])


def lora(x, w1, w2, *, act=None, tm=1024):
    m, kdim = x.shape
    r, n = w2.shape
    tm = _pick_tile(m, tm)
    return pl.pallas_call(
        functools.partial(_lora_kernel, act=act),
        out_shape=jax.ShapeDtypeStruct((m, n), F32),
        grid=(m // tm,),
        in_specs=[pl.BlockSpec((tm, kdim), lambda i: (i, 0)),
                  pl.BlockSpec((kdim, r), lambda i: (0, 0)),
                  pl.BlockSpec((r, n), lambda i: (0, 0))],
        out_specs=pl.BlockSpec((tm, n), lambda i: (i, 0)),
        compiler_params=_cparams(("parallel",)),
        name="lora",
    )(x, w1, w2)


def _rms(x, g):
    return x * lax.rsqrt(jnp.mean(x * x, axis=-1, keepdims=True) + NORM_EPS) * g


def _rmsnorm_kernel(h_ref, g_ref, o_ref):
    o_ref[...] = _rms(h_ref[...], g_ref[...]).astype(o_ref.dtype)


def rmsnorm(h, g, out_dtype=BF16, tm=512):
    t, d = h.shape
    tm = _pick_tile(t, tm)
    return pl.pallas_call(
        _rmsnorm_kernel,
        out_shape=jax.ShapeDtypeStruct((t, d), out_dtype),
        grid=(t // tm,),
        in_specs=[pl.BlockSpec((tm, d), lambda i: (i, 0)), pl.BlockSpec((1, d), lambda i: (0, 0))],
        out_specs=pl.BlockSpec((tm, d), lambda i: (i, 0)),
        compiler_params=_cparams(("parallel",)),
        name="rmsnorm",
    )(h, g.reshape(1, d))


def _norm_matmul_kernel(h_ref, g_ref, w_ref, o_ref, xn_ref, *, act, scale):
    @pl.when(pl.program_id(1) == 0)
    def _():
        xn_ref[...] = _rms(h_ref[...], g_ref[...]).astype(xn_ref.dtype)

    o_ref[...] = _activate(_dot(xn_ref[...], w_ref[...].astype(BF16)), act, scale).astype(o_ref.dtype)


def norm_matmul(h, g, w, *, layer=None, act=None, scale=None, out_dtype=F32, n=None, tm=1024, tn=512):
    m, kdim = h.shape
    n = w.shape[-1] if n is None else n
    tm, tn = _pick_tile(m, tm), _pick_tile(n, tn)
    return pl.pallas_call(
        functools.partial(_norm_matmul_kernel, act=act, scale=scale),
        out_shape=jax.ShapeDtypeStruct((m, n), out_dtype),
        grid=(m // tm, n // tn),
        in_specs=[pl.BlockSpec((tm, kdim), lambda i, j: (i, 0)),
                  pl.BlockSpec((1, kdim), lambda i, j: (0, 0)),
                  _weight_spec(w, layer, kdim, tn, lambda i, j: (0, j))],
        out_specs=pl.BlockSpec((tm, tn), lambda i, j: (i, j)),
        scratch_shapes=[pltpu.VMEM((tm, kdim), BF16)],
        compiler_params=_cparams(("parallel", "arbitrary")),
        name="norm_matmul",
    )(h, g.reshape(1, kdim), w)


def _rwkv_prep_kernel(h_ref, prev_ref, g_ref, mu_ref, *o_refs):
    g = g_ref[...]
    xn = _rms(h_ref[...], g)
    prev = _rms(prev_ref[...], g)[7:8, :] * (pl.program_id(0) > 0).astype(F32)
    row = lax.broadcasted_iota(jnp.int32, xn.shape, 0)
    shifted = jnp.where(row == 0, prev, pltpu.roll(xn, 1, axis=0))
    xx = shifted - xn
    for i, o_ref in enumerate(o_refs):
        o_ref[...] = (xn + xx * mu_ref[i:i + 1, :]).astype(o_ref.dtype)


def rwkv_prep(h, g, mu, tm=256):
    t, d = h.shape
    tm = _pick_tile(t, tm)
    blk = pl.BlockSpec((tm, d), lambda i: (i, 0))
    return pl.pallas_call(
        _rwkv_prep_kernel,
        out_shape=[jax.ShapeDtypeStruct((t, d), BF16)] * N_SHIFT_MIX,
        grid=(t // tm,),
        in_specs=[blk,
                  pl.BlockSpec((8, d), lambda i: (jnp.maximum(i * (tm // 8) - 1, 0), 0)),
                  pl.BlockSpec((1, d), lambda i: (0, 0)),
                  pl.BlockSpec((N_SHIFT_MIX, d), lambda i: (0, 0))],
        out_specs=[blk] * N_SHIFT_MIX,
        compiler_params=_cparams(("parallel",)),
        name="rwkv_prep",
    )(h, h, g.reshape(1, d), mu)


def _head_mask(dtype):
    row = lax.broadcasted_iota(jnp.int32, (SCAN_WIDTH, SCAN_WIDTH), 0)
    col = lax.broadcasted_iota(jnp.int32, (SCAN_WIDTH, SCAN_WIDTH), 1)
    return jnp.where((row >> 6) == (col >> 6), 1.0, 0.0).astype(dtype)


def _block_diag(x, mask):
    return jnp.concatenate([x] * SCAN_HEADS, axis=0) * mask


def _fold_rows(x_full, mask_f32):
    x = x_full * mask_f32
    L = SCAN_CHUNK
    return x[0:L] + x[L:2 * L] + x[2 * L:3 * L] + x[3 * L:4 * L]


def _split2(x):
    hi = x.astype(BF16)
    return hi, (x - hi.astype(F32)).astype(BF16)


def _head_sums(xs, mask_b):
    n = xs[0].shape[0]
    r = _dot(jnp.concatenate([piece for x in xs for piece in _split3(x)], axis=0), mask_b)
    return [r[3 * i * n:(3 * i + 1) * n] + r[(3 * i + 1) * n:(3 * i + 2) * n] + r[(3 * i + 2) * n:(3 * i + 3) * n]
            for i in range(len(xs))]


def _bd_parts(y_split, mask_b):
    hi, lo = y_split
    return _block_diag(hi, mask_b), _block_diag(lo, mask_b)


def _mm_heads(x_split, y_parts):
    x_hi, x_lo = x_split
    y_hi, y_lo = y_parts
    n = x_hi.shape[0]
    both = _dot(jnp.concatenate([x_hi, x_lo], axis=0), y_hi)
    return both[:n] + both[n:] + _dot(x_hi, y_lo)


def _run_interleaved(stage_generators):
    live = list(stage_generators)
    while live:
        still = []
        for gen in live:
            try:
                next(gen)
                still.append(gen)
            except StopIteration:
                pass
        live = still


def _tri_inverse_heads(a, j_row, m_col, mask_b):
    L = SCAN_CHUNK
    eye = jnp.where(m_col == j_row, 1.0, 0.0)
    d = jnp.where((j_row >> 4) == (m_col >> 4), a, 0.0)
    e = a - d
    p = eye + d
    d_split = _split2(d)
    x = _mm_heads(d_split, _bd_parts(d_split, mask_b))
    yield
    for _ in range(2):
        x_split, p_split = _split2(x), _split2(p)
        stacked = tuple(jnp.concatenate([xs, ps], axis=0) for xs, ps in zip(x_split, p_split))
        both = _mm_heads(stacked, _bd_parts(x_split, mask_b))
        x, p = both[:L], p + both[L:]
        yield
    p = p + _mm_heads(_split2(p), _bd_parts(_split2(x), mask_b))
    yield
    p_split = _split2(p)
    n1 = _mm_heads(p_split, _bd_parts(_split2(e), mask_b))
    yield
    n1_split = _split2(n1)
    n2 = _mm_heads(n1_split, _bd_parts(n1_split, mask_b))
    yield
    q = eye + n1
    q = q + _mm_heads(_split2(q), _bd_parts(_split2(n2), mask_b))
    yield
    return _mm_heads(_split2(q), _bd_parts(p_split, mask_b))


def _softplus(x):
    return jnp.maximum(x, 0.0) + jnp.log(1.0 + jnp.exp(-jnp.abs(x)))


def _rwkv_chunk_kernel(*refs, mix_v):
    if mix_v:
        r_ref, k_ref, v_ref, wl_ref, al_ref, vl_ref, vf_ref, par_ref, m_ref, c_ref, qp_ref, y0_ref, bo_ref = refs
    else:
        r_ref, k_ref, v_ref, wl_ref, al_ref, par_ref, m_ref, c_ref, qp_ref, y0_ref, bo_ref = refs
    L, W = SCAN_CHUNK, SCAN_WIDTH
    mask_b = _head_mask(BF16)
    mask_f = _head_mask(F32)
    tri_incl = jnp.where(lax.broadcasted_iota(jnp.int32, (L, L), 1)
                         <= lax.broadcasted_iota(jnp.int32, (L, L), 0), 1.0, 0.0).astype(BF16)
    j_row = lax.broadcasted_iota(jnp.int32, (L, W), 0)
    m_col = lax.broadcasted_iota(jnp.int32, (L, W), 1) & (L - 1)
    strict = m_col < j_row
    incl = m_col <= j_row

    def unit(u):
        cols = slice(u * W, (u + 1) * W)
        par = par_ref[:, cols]
        w0, a0, v0, k_k, k_a, r_k = (par[i:i + 1] for i in range(6))
        r, k, v = r_ref[:, cols], k_ref[:, cols], v_ref[:, cols]
        lw = -jnp.exp(-_softplus(-(w0 + wl_ref[:, cols])) - 0.5)
        a_s = jax.nn.sigmoid(a0 + al_ref[:, cols])
        if mix_v:
            v = v + (vf_ref[:, cols] - v) * jax.nn.sigmoid(v0 + vl_ref[:, cols])
        kk = k * k_k
        k = k * (1.0 + (a_s - 1.0) * k_a)
        kk_sq, rk_sum = _head_sums([kk * kk, r * k * r_k], mask_b)
        c = _dot_exact_lhs(tri_incl, lw)
        yield
        kk = kk * lax.rsqrt(jnp.maximum(kk_sq, 1e-24))
        bo_ref[:, cols] = rk_sum * v
        c_last = c[L - 1:L, :]
        p_inv = jnp.exp(-c)
        p_rest = jnp.exp(c_last - c)
        rt = r * jnp.exp(c)
        at = (-kk * jnp.exp(c - lw)).astype(BF16)
        b = kk * a_s
        vb = v.astype(BF16)
        ar = jnp.concatenate([at, rt.astype(BF16)], axis=0)
        g_b = _dot_nt(ar, _block_diag((b * p_inv).astype(BF16), mask_b))
        g_k = _dot_nt(ar, _block_diag((k * p_inv).astype(BF16), mask_b))
        yield
        a_ab = jnp.where(strict, g_b[:L], 0.0)
        a_ak = jnp.where(strict, g_k[:L], 0.0).astype(BF16)
        a_rb = jnp.where(incl, g_b[L:], 0.0).astype(BF16)
        a_rk = jnp.where(incl, g_k[L:], 0.0).astype(BF16)
        v_bd = _block_diag(vb, mask_b)
        akv = _dot(a_ak, v_bd)
        t_inv = yield from _tri_inverse_heads(a_ab, j_row, m_col, mask_b)
        yield
        t_split = _split2(t_inv)
        w_both = _dot(jnp.concatenate(t_split, axis=0), _block_diag(at, mask_b))
        u0 = _mm_heads(t_split, _bd_parts(_split2(akv), mask_b)).astype(BF16)
        yield
        w = (w_both[:L] + w_both[L:]).astype(BF16)
        qp_ref[:, cols] = (rt + _dot(a_rb, _block_diag(w, mask_b))).astype(qp_ref.dtype)
        y0_ref[:, cols] = _dot(a_rb, _block_diag(u0, mask_b)) + _dot(a_rk, v_bd)
        bh = (b * p_rest).astype(BF16)
        kh = (k * p_rest).astype(BF16)
        decay_diag = jnp.where(m_col == j_row, jnp.exp(c_last), 0.0)
        m_ref[:, cols] = _fold_rows(_dot_tn(bh, w), mask_f) + decay_diag
        c_ref[:, cols] = _fold_rows(_dot_tn(bh, u0) + _dot_tn(kh, vb), mask_f)

    _run_interleaved(unit(u) for u in range(SCAN_UNITS))


def rwkv_chunks(r, k, v, w_lin, a_lin, v_lin, v_first, params):
    t, d = r.shape
    bw = SCAN_UNITS * SCAN_WIDTH
    blk = pl.BlockSpec((SCAN_CHUNK, bw), lambda ci, gi: (ci, gi))
    mix_v = v_lin is not None
    args = [r, k, v, w_lin, a_lin] + ([v_lin, v_first] if mix_v else []) + [params]
    f32 = jax.ShapeDtypeStruct((t, d), F32)
    return pl.pallas_call(
        functools.partial(_rwkv_chunk_kernel, mix_v=mix_v),
        out_shape=[f32, f32, jax.ShapeDtypeStruct((t, d), BF16), f32, f32],
        grid=(t // SCAN_CHUNK, d // bw),
        in_specs=[blk] * (len(args) - 1) + [pl.BlockSpec((8, bw), lambda ci, gi: (0, gi))],
        out_specs=[blk] * 5,
        compiler_params=_cparams(("parallel", "parallel")),
        name="rwkv_chunks",
    )(*args)


def _rwkv_state_kernel(m_ref, c_ref, qp_ref, y0_ref, bo_ref, g_ref, ln_ref, o_ref, st_ref):
    @pl.when(pl.program_id(1) == 0)
    def _():
        st_ref[...] = jnp.zeros_like(st_ref)

    W = SCAN_WIDTH
    mask_b = _head_mask(BF16)
    def unit(u):
        cols = slice(u * W, (u + 1) * W)
        st_parts = _bd_parts(_split2(st_ref[u]), mask_b)
        y = _dot(qp_ref[:, cols], st_parts[0]) + y0_ref[:, cols]
        st_ref[u] = _mm_heads(_split2(m_ref[:, cols]), st_parts) + c_ref[:, cols]
        yield
        mean = _head_sums([y], mask_b)[0] * (1.0 / RW_HEAD)
        yield
        dev = y - mean
        var = _head_sums([dev * dev], mask_b)[0] * (1.0 / RW_HEAD)
        yield
        ln = ln_ref[:, cols]
        out = (dev * lax.rsqrt(var + GN_EPS) * ln[0:1] + ln[1:2] + bo_ref[:, cols]) * g_ref[:, cols]
        o_ref[:, cols] = out.astype(o_ref.dtype)

    _run_interleaved(unit(u) for u in range(SCAN_UNITS))


def rwkv_state_scan(m, c, qp, y0, bonus, g, ln_params):
    t, d = m.shape
    bw = SCAN_UNITS * SCAN_WIDTH
    blk = pl.BlockSpec((SCAN_CHUNK, bw), lambda gi, ci: (ci, gi))
    return pl.pallas_call(
        _rwkv_state_kernel,
        out_shape=jax.ShapeDtypeStruct((t, d), BF16),
        grid=(d // bw, t // SCAN_CHUNK),
        in_specs=[blk] * 6 + [pl.BlockSpec((8, bw), lambda gi, ci: (0, gi))],
        out_specs=blk,
        scratch_shapes=[pltpu.VMEM((SCAN_UNITS, RW_HEAD, SCAN_WIDTH), F32)],
        compiler_params=_cparams(("parallel", "arbitrary")),
        name="rwkv_state_scan",
    )(m, c, qp, y0, bonus, g, ln_params)


def _compress_kernel(x_ref, w1_ref, c_ref, w2_ref, o_ref):
    ab = _dot(x_ref[...], w1_ref[...])
    a = ab[:, :NSA_HD]
    bsh = ab[:, NSA_HD:]
    n16 = a.shape[0]
    row = lax.broadcasted_iota(jnp.int32, a.shape, 0)
    b_next = jnp.where(row == n16 - 1, 0.0, pltpu.roll(bsh, n16 - 1, axis=0))
    hid = a + b_next + c_ref[...]
    hid = jax.nn.gelu(hid, approximate=True)
    o_ref[...] = _dot(hid.astype(BF16), w2_ref[...]).astype(o_ref.dtype)


def compress(x16, w1cat, cvec, w2):
    _, g, n16, kd = x16.shape
    d = NSA_HD
    return pl.pallas_call(
        _compress_kernel,
        out_shape=jax.ShapeDtypeStruct((2, g, n16, d), BF16),
        grid=(2, g),
        in_specs=[pl.BlockSpec((None, None, n16, kd), lambda p, gi: (p, gi, 0, 0)),
                  pl.BlockSpec((None, kd, 2 * d), lambda p, gi: (p, 0, 0)),
                  pl.BlockSpec((None, 1, d), lambda p, gi: (p, 0, 0)),
                  pl.BlockSpec((None, d, d), lambda p, gi: (p, 0, 0))],
        out_specs=pl.BlockSpec((None, None, n16, d), lambda p, gi: (p, gi, 0, 0)),
        compiler_params=_cparams(("parallel", "parallel")),
        name="nsa_compress",
    )(x16, w1cat, cvec, w2)


def _with_ones(v):
    return jnp.concatenate([v, jnp.ones_like(v)], axis=1)


def _nsa_kernel(q_ref, gate_ref, kc_ref, vc_ref, ks_ref, vs_ref, kw_ref, vw_ref, oh_ref,
                gs_ref, ov_ref, bs_ref, o_ref, s_buf, p_buf):
    qi = pl.program_id(1)
    R, Q, D, NG = NSA_REP, QBLK, NSA_HD, NSA_STEP_GROUPS
    n_cp = kc_ref.shape[1]
    n_sb = ov_ref.shape[1]
    n_far = jnp.maximum(qi - 1, 0)
    n0 = pl.multiple_of(jnp.maximum(qi - 1, 0) * Q, Q)
    n_tiles = (n_far + (SEL_TILE // Q - 1)) // (SEL_TILE // Q)
    max_tile = oh_ref.shape[0] // SEL_TILE - 1
    state = [{} for _ in range(NG)]

    def slab(ref, gl, start, n):
        return ref[pl.ds(start, n), gl * D:(gl + 1) * D]

    def sel_keys(gl, start, n):
        return jnp.concatenate([slab(ks_ref, gl, WINDOW + start, n), oh_ref[pl.ds(start, n), :]], axis=1)

    def key_tile(gl, i):
        return sel_keys(gl, pl.multiple_of(i * SEL_TILE, SEL_TILE), SEL_TILE)

    def value_tile(gl, i):
        return _with_ones(slab(vs_ref, gl, pl.multiple_of(WINDOW + i * SEL_TILE, SEL_TILE), SEL_TILE))

    def before_loop(gl):
        st = state[gl]
        q = q_ref[:, gl * R * D:(gl + 1) * R * D]
        qs = jnp.concatenate([q[:, r * D:(r + 1) * D] for r in range(R)], axis=0)
        t_rows = qi * Q + (lax.broadcasted_iota(jnp.int32, (R * Q, 1), 0) & (Q - 1))
        c_idx = lax.broadcasted_iota(jnp.int32, (1, n_cp), 1)
        m_off = qi * (Q // CMP_STRIDE) - lax.broadcasted_iota(jnp.int32, (n_cp, D), 0) + CMP_BAND_LO
        k_idx = lax.broadcasted_iota(jnp.int32, (n_cp, D), 1) & (CMP_BIAS_K - 1)
        in_band = ((k_idx < CMP_BAND) & (m_off == k_idx)) | ((k_idx == CMP_BAND) & (m_off >= CMP_BAND))
        k_c = jnp.concatenate([kc_ref[gl], jnp.where(in_band, 1.0, 0.0).astype(BF16)], axis=1)
        qk_c = _dot_nt(jnp.concatenate([qs, gs_ref[gl]], axis=1), k_c)
        w0 = pl.multiple_of(qi * Q, Q)
        qk_w = _dot_nt(qs, slab(kw_ref, gl, w0, WINDOW + Q))
        yield
        m_c = (c_idx * CMP_STRIDE + (CMP_LEN - 1)) <= t_rows
        s_c = jnp.where(m_c, qk_c, NEG_INF)
        e_c = jnp.exp2(s_c - jnp.max(s_c, axis=-1, keepdims=True))
        acc_c = _dot(e_c.astype(BF16), _with_ones(vc_ref[gl]))
        yield
        inv_l = jnp.where(t_rows >= CMP_LEN - 1, 1.0 / acc_c[:, D:D + 1], 0.0)
        st["o_c"] = acc_c[:, :D] * inv_l
        p_c = e_c * inv_l
        p_sum = p_c[0:Q] + p_c[Q:2 * Q] + p_c[2 * Q:3 * Q] + p_c[3 * Q:4 * Q]
        imp = _dot_exact_rhs(p_sum, ov_ref[...])
        j_io = lax.broadcasted_iota(jnp.int32, (1, WINDOW + Q), 1)
        oldest = jnp.where(lax.broadcasted_iota(jnp.int32, (R * Q, Q), 1) > (t_rows & (Q - 1)), 0.0, NEG_INF)
        bias_w = jnp.concatenate([oldest, jnp.zeros((R * Q, WINDOW - 2 * Q), F32), bs_ref[gl, 0]], axis=1)
        s_w = jnp.where(j_io >= WINDOW - qi * Q, qk_w + bias_w, NEG_INF)
        e_w = jnp.exp2(s_w - jnp.max(s_w, axis=-1, keepdims=True))
        acc_w = _dot(e_w.astype(BF16), _with_ones(slab(vw_ref, gl, w0, WINDOW + Q)))
        yield
        st["o_w"] = acc_w[:, :D] / acc_w[:, D:]
        imp_t = imp.T
        s_row = lax.broadcasted_iota(jnp.int32, (n_sb, Q), 0)
        t_col = qi * Q + lax.broadcasted_iota(jnp.int32, (n_sb, Q), 1)
        forced = (s_row == (t_col >> 6)) | (s_row == 0)
        score = jnp.where(forced, SEL_FORCE, jnp.where(s_row * SEL_BLOCK <= t_col, imp_t, -SEL_FORCE))
        sel_t = jnp.zeros((n_sb, Q), F32)
        s_f = s_row.astype(F32)
        for _ in range(SEL_TOP):
            mx = jnp.max(score, axis=0, keepdims=True)
            first = jnp.min(jnp.where(score == mx, s_f, float(n_sb)), axis=0, keepdims=True)
            pick = s_f == first
            sel_t = jnp.where(pick, 1.0, sel_t)
            score = jnp.where(pick, -3e38, score)
        sel = sel_t.T
        blk = lax.broadcasted_iota(jnp.int32, (Q, n_sb), 1)
        neg_near = jnp.where(sel > 0.5, 0.0, NEG_INF).astype(BF16)
        neg_far = jnp.where((sel > 0.5) & (blk < 2 * n_far), 0.0, NEG_INF).astype(BF16)
        q_near = jnp.concatenate([qs, jnp.concatenate([neg_near] * R, axis=0)], axis=1)
        st["q_far"] = jnp.concatenate([qs, jnp.concatenate([neg_far] * R, axis=0)], axis=1)
        s = _dot_nt(q_near, sel_keys(gl, n0, 2 * Q)) + bs_ref[gl, jnp.where(qi == 0, 1, 0)]
        s_far = _dot_nt(st["q_far"], key_tile(gl, 0))
        s_buf[gl, 0] = s_far
        p_buf[gl, 1] = jnp.zeros(p_buf.shape[2:], p_buf.dtype)
        yield
        st["row_max"] = jnp.max(s_far, axis=-1, keepdims=True)
        st["m"] = jnp.max(s, axis=-1, keepdims=True)
        st["acc"] = _dot(jnp.exp2(s - st["m"]).astype(BF16), _with_ones(slab(vs_ref, gl, WINDOW + n0, 2 * Q)))

    _run_interleaved([before_loop(gl) for gl in range(NG)])

    def half_step(gl, i, slot, carry):
        m_run, acc, row_max = carry
        s_next = _dot_nt(state[gl]["q_far"], key_tile(gl, jnp.minimum(i + 1, max_tile)))
        s_buf[gl, 1 - slot] = s_next
        pv = _dot(p_buf[gl, 1 - slot], value_tile(gl, jnp.maximum(i - 1, 0)))
        m_new = jnp.maximum(m_run, row_max)
        acc = jnp.exp2(m_run - m_new) * (acc + pv)
        p_buf[gl, slot] = jnp.exp2(s_buf[gl, slot] - m_new).astype(p_buf.dtype)
        return m_new, acc, jnp.max(s_next, axis=-1, keepdims=True)

    def far_loop(gl):
        st = state[gl]
        pair_body = lambda j, carry: half_step(gl, 2 * j + 1, 1, half_step(gl, 2 * j, 0, carry))
        return lax.fori_loop(0, n_pairs, pair_body, (st["m"], st["acc"], st["row_max"]))

    n_pairs = (n_tiles + 1) >> 1
    carries = [far_loop(gl) for gl in range(NG)]
    last_tile = jnp.maximum(2 * n_pairs - 1, 0)
    last_pv = [_dot(p_buf[gl, 1], value_tile(gl, last_tile)) for gl in range(NG)]
    for gl, (_, acc_s, _) in enumerate(carries):
        acc_s = acc_s + last_pv[gl]
        o_s = acc_s[:, :D] / acc_s[:, D:]
        o_c, o_w = state[gl]["o_c"], state[gl]["o_w"]
        gates = gate_ref[gl]
        for r in range(R):
            rows = slice(r * Q, (r + 1) * Q)
            o = (gates[:, 3 * r:3 * r + 1] * o_c[rows] + gates[:, 3 * r + 1:3 * r + 2] * o_s[rows]
                 + gates[:, 3 * r + 2:3 * r + 3] * o_w[rows])
            o_ref[:, (gl * R + r) * D:(gl * R + r + 1) * D] = o.astype(o_ref.dtype)


def nsa_attention(q, gates, kc, vc, kv_padded, block_onehot, gs, ov, bs):
    t = q.shape[0]
    g, n_cp, d = kc.shape
    rq = NSA_REP * QBLK
    ng = NSA_STEP_GROUPS
    assert g % ng == 0
    per_g = lambda shape: pl.BlockSpec((ng,) + shape, lambda gi, qi: (gi,) + (0,) * len(shape))
    kv_part = lambda part: pl.BlockSpec((t + WINDOW, ng * d), lambda gi, qi: (0, part * g // ng + gi))
    return pl.pallas_call(
        _nsa_kernel,
        out_shape=jax.ShapeDtypeStruct((t, NSA_HEADS * d), BF16),
        grid=(g // ng, t // QBLK),
        in_specs=[pl.BlockSpec((QBLK, ng * NSA_REP * d), lambda gi, qi: (qi, gi)),
                  pl.BlockSpec((ng, QBLK, NSA_REP * N_BRANCH), lambda gi, qi: (gi, qi, 0)),
                  per_g((n_cp, d)), per_g((n_cp, d)),
                  kv_part(2), kv_part(3), kv_part(4), kv_part(5),
                  pl.BlockSpec((t, d), lambda gi, qi: (0, 0)),
                  per_g((rq, d)),
                  pl.BlockSpec(ov.shape, lambda gi, qi: (0, 0)),
                  per_g((2, rq, 2 * QBLK))],
        out_specs=pl.BlockSpec((QBLK, ng * NSA_REP * d), lambda gi, qi: (qi, gi)),
        scratch_shapes=[pltpu.VMEM((ng, 2, rq, SEL_TILE), F32), pltpu.VMEM((ng, 2, rq, SEL_TILE), BF16)],
        compiler_params=_cparams(("parallel", "arbitrary"), NSA_VMEM_LIMIT_BYTES),
        name="nsa_attention",
    )(q, gates, kc, vc, kv_padded, kv_padded, kv_padded, kv_padded, block_onehot, gs, ov, bs)


def _t5_bucket_np(n):
    n = np.maximum(n, 0)
    max_exact = REL_BUCKETS // 2
    nf = np.maximum(n, 1).astype(np.float64)
    large = max_exact + (np.log(nf / max_exact) / math.log(REL_MAX_DIST / max_exact)
                         * (REL_BUCKETS - max_exact)).astype(np.int64)
    large = np.minimum(large, REL_BUCKETS - 1)
    return np.where(n < max_exact, n, large).astype(np.int32)


def _nsa_bias_tables(rel_bias):
    G, R, Q = NSA_GROUPS, NSA_REP, QBLK
    table = (rel_bias.astype(F32) * LOG2E).reshape(REL_BUCKETS, G, R).transpose(1, 2, 0)
    qq = np.arange(Q)

    def lookup(dist):
        return table[:, :, _t5_bucket_np(dist)].reshape(G, R * Q, dist.shape[1])

    m = np.arange(CMP_BAND) - CMP_BAND_LO
    gs = lookup(qq[:, None] - (CMP_LEN - 1) + CMP_STRIDE * m[None, :])
    far = jnp.broadcast_to(table[:, :, REL_BUCKETS - 1][:, :, None, None], (G, R, Q, 1)).reshape(G, R * Q, 1)
    gs = jnp.concatenate([gs, far, jnp.zeros((G, R * Q, CMP_BIAS_K - CMP_BAND - 1), F32)], axis=-1)
    h, mid, l = _split3(gs)
    gs3 = jnp.concatenate([h, mid, l, jnp.zeros_like(h)], axis=-1)
    def toeplitz(c):
        mm = np.arange(2 * Q)
        v = table[:, :, _t5_bucket_np(np.where(mm < Q, c - mm, c + 2 * Q - mm))]
        rows = jnp.tile(v, (1, 1, Q))[:, :, :Q * (2 * Q - 1)].reshape(G, R, Q, 2 * Q - 1)
        return rows[:, :, :, :Q].reshape(G, R * Q, Q)

    d0 = np.tile(qq[:, None] - qq[None, :], (R, 1))
    causal = jnp.asarray(np.where(d0 >= 0, 0.0, NEG_INF), F32)[None]
    b0 = toeplitz(0) + causal
    b1 = toeplitz(Q)
    far_c = jnp.broadcast_to(far, (G, R * Q, Q))
    neg = jnp.full((G, R * Q, Q), NEG_INF, F32)
    bs = jnp.stack([jnp.concatenate([b1 - far_c, b0 - far_c], -1),
                    jnp.concatenate([b0 - far_c, neg], -1)], axis=1)
    return gs3, bs


def _overlap_table(n_cp, n_sb):
    c = np.arange(n_cp)[:, None] * CMP_STRIDE
    s = np.arange(n_sb)[None, :] * SEL_BLOCK
    return jnp.asarray(((c <= s + SEL_BLOCK - 1) & (c + CMP_LEN - 1 >= s)), BF16)


def _pad_cols(w, n):
    return jnp.pad(w, ((0, 0), (0, n - w.shape[1])))


def _pad_rows(w, n):
    return jnp.pad(w, ((0, n - w.shape[0]), (0, 0)))


def _rwkv_layer(h, layer, norm_g, mu, wr, wk, wv, wo, w0, w1, w2, a0, a1, a2, g1, g2,
                k_k, k_a, r_k, lnx_w, lnx_b, v_first, v0, v1, v2):
    bf = lambda w: w.astype(BF16)
    xr, xw, xk, xv, xa, xg = rwkv_prep(h, norm_g, mu)
    r = matmul(xr, wr, layer=layer)
    k = matmul(xk, wk, layer=layer)
    v = matmul(xv, wv, layer=layer)
    w_lin = lora(xw, bf(_pad_cols(w1, 128)), bf(_pad_rows(w2, 128)), act="tanh")
    a_lin = lora(xa, bf(a1), bf(a2))
    g = lora(xg, bf(g1), bf(g2), act="sigmoid")
    zeros = jnp.zeros_like(w0)
    if v_first is None:
        v_lin, v0 = None, zeros
    else:
        v_lin = lora(xv, bf(_pad_cols(v1, 128)), bf(_pad_rows(v2, 128)))
    params = jnp.stack([w0, a0, v0, k_k, k_a, r_k.reshape(-1), zeros, zeros])
    m, c, qp, y0, bonus = rwkv_chunks(r, k, v, w_lin, a_lin, v_lin, v_first, params)
    ln_params = jnp.stack([lnx_w, lnx_b] + [zeros] * 6)
    gated = rwkv_state_scan(m, c, qp, y0, bonus, g, ln_params)
    h = matmul(gated, wo, layer=layer, residual=h)
    return h, (v if v_first is None else v_first)


def _mlp(h, layer, norm_g, w_up, w_down):
    hid = norm_matmul(h, norm_g, w_up, layer=layer, act="relu2", out_dtype=BF16)
    return matmul(hid, w_down, layer=layer, residual=h, tm=2048, tn=512)


def _nsa_shared_kv(h, kv_norm_g, w_kv, cmp_pe, cmp_w1, cmp_b1, cmp_w2):
    t = h.shape[0]
    G, D = NSA_GROUPS, NSA_HD
    kv = norm_matmul(h, kv_norm_g, w_kv, out_dtype=BF16)
    x16 = kv[:, :2 * G * D].reshape(t, 2, G, D).transpose(1, 2, 0, 3).reshape(2, G, t // CMP_STRIDE, CMP_STRIDE * D)
    half = CMP_STRIDE * D
    w1cat = jnp.concatenate([cmp_w1[:, :half], cmp_w1[:, half:]], axis=-1).astype(BF16)
    cvec = jnp.einsum("plk,plkd->pd", cmp_pe, cmp_w1.reshape(2, CMP_LEN, D, D),
                      precision=lax.Precision.HIGHEST) + cmp_b1
    kvc = compress(x16, w1cat, cvec.reshape(2, 1, D), cmp_w2.astype(BF16))
    block_onehot = jnp.asarray(np.arange(t)[:, None] // SEL_BLOCK == np.arange(D)[None, :], BF16)
    return kvc[0], kvc[1], jnp.pad(kv, ((WINDOW, 0), (0, 0))), block_onehot


def _nsa_layer(h, layer, norm_g, wq, wo, kvs, tables):
    t = h.shape[0]
    nq = NSA_HEADS * NSA_HD
    q = norm_matmul(h, norm_g, wq, layer=layer, n=nq, scale=NSA_HD ** -0.5 * LOG2E, out_dtype=BF16)
    gates = norm_matmul(h, norm_g, _pad_cols(wq[layer, :, nq:], 128).astype(BF16), act="sigmoid")
    gates = gates[:, :NSA_HEADS * N_BRANCH].reshape(t, NSA_GROUPS, NSA_REP * N_BRANCH).transpose(1, 0, 2)
    o = nsa_attention(q, gates, *kvs, *tables)
    return matmul(o, wo, layer=layer, residual=h)


def kernel(x, a_norm_g, rw_mu, rw_wr, rw_wk, rw_wv, rw_wo, rw_w0, rw_w1, rw_w2, rw_a0, rw_a1, rw_a2,
           rw_v0, rw_v1, rw_v2, rw_g1, rw_g2, rw_kk, rw_ka, rw_rk, rw_lnx_w, rw_lnx_b, kv_norm_g, w_kv,
           cmp_pe, cmp_w1, cmp_b1, cmp_w2, b_norm_g, nsa_wq, nsa_wo, rel_bias, m_norm_g, mlp_up,
           mlp_down, final_norm_g):
    b, t, d = x.shape
    assert b == 1 and d == D_MODEL and t % SEL_TILE == 0 and t // SEL_BLOCK <= 128 and WINDOW == 4 * QBLK
    h = x.reshape(t, d)
    v_first = None
    kvs = tables = None
    rw_w = (rw_wr, rw_wk, rw_wv, rw_wo)
    for layer in range(DEPTH):
        if layer < N_A_LAYERS:
            i = layer
            vp = (None, None, None) if i == 0 else (rw_v0[i - 1], rw_v1[i - 1], rw_v2[i - 1])
            h, v_first = _rwkv_layer(
                h, i, a_norm_g[i], rw_mu[i], *rw_w, rw_w0[i], rw_w1[i],
                rw_w2[i], rw_a0[i], rw_a1[i], rw_a2[i], rw_g1[i], rw_g2[i], rw_kk[i], rw_ka[i],
                rw_rk[i].reshape(1, RW_HEADS, RW_HEAD), rw_lnx_w[i], rw_lnx_b[i], v_first, *vp)
        else:
            j = layer - N_A_LAYERS
            if j == 0:
                kvs = _nsa_shared_kv(h, kv_norm_g, w_kv, cmp_pe, cmp_w1, cmp_b1, cmp_w2)
                gs3, bs = _nsa_bias_tables(rel_bias)
                tables = (gs3, _overlap_table(t // CMP_STRIDE, 128), bs)
            h = _nsa_layer(h, j, b_norm_g[j], nsa_wq, nsa_wo, kvs, tables)
        h = _mlp(h, layer, m_norm_g[layer], mlp_up, mlp_down)
    return rmsnorm(h, final_norm_g, out_dtype=x.dtype).reshape(b, t, d)
```

```python
import functools
import math

import numpy as np
import jax
import jax.numpy as jnp
from jax import lax
from jax.experimental import pallas as pl
from jax.experimental.pallas import tpu as pltpu

F32 = jnp.float32
BF16 = jnp.bfloat16

D_MODEL = 2048
DEPTH = 4
N_A_LAYERS = 2
RW_HEAD = 64
RW_HEADS = D_MODEL // RW_HEAD
N_SHIFT_MIX = 6
GN_EPS = 64e-5
NSA_HEADS = 16
NSA_GROUPS = 4
NSA_REP = NSA_HEADS // NSA_GROUPS
NSA_HD = D_MODEL // NSA_HEADS
CMP_STRIDE = 16
CMP_LEN = 32
SEL_BLOCK = 64
SEL_TOP = 16
WINDOW = 512
QBLK = 128
N_BRANCH = 3
SEL_FORCE = 1e4
NEG_INF = -1e30
REL_BUCKETS = 32
REL_MAX_DIST = 128
NORM_EPS = 1e-6
LOG2E = math.log2(math.e)

V7X_VMEM_BYTES = 64 * 1024 * 1024
VMEM_LIMIT_BYTES = V7X_VMEM_BYTES - 8 * 1024 * 1024
NSA_VMEM_LIMIT_BYTES = V7X_VMEM_BYTES - 4 * 1024 * 1024
SCAN_CHUNK = 64
SCAN_HEADS = 4
SCAN_WIDTH = SCAN_HEADS * RW_HEAD
SCAN_UNITS = 8
SEL_TILE = 512
NSA_STEP_GROUPS = 2
CMP_BAND = 16
CMP_BAND_LO = 6
CMP_BIAS_K = 32
assert CMP_BAND < CMP_BIAS_K and 4 * CMP_BIAS_K == NSA_HD and NSA_HD == 128


def _cparams(sem, vmem_limit_bytes=VMEM_LIMIT_BYTES):
    return pltpu.CompilerParams(dimension_semantics=sem, vmem_limit_bytes=vmem_limit_bytes)


def _split3(x):
    h = x.astype(BF16)
    r = x - h.astype(F32)
    m = r.astype(BF16)
    l = (r - m.astype(F32)).astype(BF16)
    return h, m, l


def _dot(a, b):
    return jnp.dot(a, b, preferred_element_type=F32)


def _dot_nt(a, b):
    return lax.dot_general(a, b, (((1,), (1,)), ((), ())), preferred_element_type=F32)


def _dot_tn(a, b):
    return lax.dot_general(a, b, (((0,), (0,)), ((), ())), preferred_element_type=F32)


def _dot_exact_lhs(a_bf16, x):
    h, m, l = _split3(x)
    return _dot(a_bf16, h) + _dot(a_bf16, m) + _dot(a_bf16, l)


def _dot_exact_rhs(x, b_bf16):
    h, m, l = _split3(x)
    return _dot(h, b_bf16) + _dot(m, b_bf16) + _dot(l, b_bf16)


def _activate(acc, act, scale):
    if act == "relu2":
        acc = jnp.square(jnp.maximum(acc, 0.0))
    elif act == "tanh":
        acc = jnp.tanh(acc)
    elif act == "sigmoid":
        acc = jax.nn.sigmoid(acc)
    return acc if scale is None else acc * scale


def _matmul_kernel(x_ref, w_ref, *rest, act, scale, has_res, nk):
    if has_res:
        res_ref, o_ref, *scratch = rest
    else:
        o_ref, *scratch = rest

    def epilogue(acc):
        acc = _activate(acc, act, scale)
        if has_res:
            acc = acc + res_ref[...]
        o_ref[...] = acc.astype(o_ref.dtype)

    if nk == 1:
        epilogue(_dot(x_ref[...], w_ref[...].astype(BF16)))
    else:
        acc_ref, = scratch
        k = pl.program_id(2)

        @pl.when(k == 0)
        def _():
            acc_ref[...] = jnp.zeros_like(acc_ref)

        acc_ref[...] += _dot(x_ref[...], w_ref[...].astype(BF16))

        @pl.when(k == nk - 1)
        def _():
            epilogue(acc_ref[...])


def _pick_tile(n, pref):
    t = min(n, pref)
    while n % t:
        t //= 2
    return t


def _weight_spec(w, layer, rows, cols, index):
    if w.ndim == 2:
        return pl.BlockSpec((rows, cols), index)
    return pl.BlockSpec((None, rows, cols), lambda *g: (layer,) + tuple(index(*g)))


def matmul(x, w, *, layer=None, act=None, scale=None, residual=None, out_dtype=F32, tm=1024, tn=1024, tk=2048):
    m, kdim = x.shape
    n = w.shape[-1]
    tm, tn, tk = _pick_tile(m, tm), _pick_tile(n, tn), _pick_tile(kdim, tk)
    nk = kdim // tk
    in_specs = [pl.BlockSpec((tm, tk), lambda j, i, k: (i, k)),
                _weight_spec(w, layer, tk, tn, lambda j, i, k: (k, j))]
    args = [x, w]
    if residual is not None:
        in_specs.append(pl.BlockSpec((tm, tn), lambda j, i, k: (i, j)))
        args.append(residual)
    return pl.pallas_call(
        functools.partial(_matmul_kernel, act=act, scale=scale, has_res=residual is not None, nk=nk),
        out_shape=jax.ShapeDtypeStruct((m, n), out_dtype),
        grid=(n // tn, m // tm, nk),
        in_specs=in_specs,
        out_specs=pl.BlockSpec((tm, tn), lambda j, i, k: (i, j)),
        scratch_shapes=[pltpu.VMEM((tm, tn), F32)] if nk > 1 else [],
        compiler_params=_cparams(("parallel", "parallel", "arbitrary")),
        name="matmul",
    )(*args)


def _lora_kernel(x_ref, w1_ref, w2_ref, o_ref, *, act):
    hidden = _activate(_dot(x_ref[...], w1_ref[...]), act, None).astype(BF16)
    o_ref[...] = _dot(hidden, w2_ref[...])


def lora(x, w1, w2, *, act=None, tm=1024):
    m, kdim = x.shape
    r, n = w2.shape
    tm = _pick_tile(m, tm)
    return pl.pallas_call(
        functools.partial(_lora_kernel, act=act),
        out_shape=jax.ShapeDtypeStruct((m, n), F32),
        grid=(m // tm,),
        in_specs=[pl.BlockSpec((tm, kdim), lambda i: (i, 0)),
                  pl.BlockSpec((kdim, r), lambda i: (0, 0)),
                  pl.BlockSpec((r, n), lambda i: (0, 0))],
        out_specs=pl.BlockSpec((tm, n), lambda i: (i, 0)),
        compiler_params=_cparams(("parallel",)),
        name="lora",
    )(x, w1, w2)


def _rms(x, g):
    return x * lax.rsqrt(jnp.mean(x * x, axis=-1, keepdims=True) + NORM_EPS) * g


def _rmsnorm_kernel(h_ref, g_ref, o_ref):
    o_ref[...] = _rms(h_ref[...], g_ref[...]).astype(o_ref.dtype)


def rmsnorm(h, g, out_dtype=BF16, tm=512):
    t, d = h.shape
    tm = _pick_tile(t, tm)
    return pl.pallas_call(
        _rmsnorm_kernel,
        out_shape=jax.ShapeDtypeStruct((t, d), out_dtype),
        grid=(t // tm,),
        in_specs=[pl.BlockSpec((tm, d), lambda i: (i, 0)), pl.BlockSpec((1, d), lambda i: (0, 0))],
        out_specs=pl.BlockSpec((tm, d), lambda i: (i, 0)),
        compiler_params=_cparams(("parallel",)),
        name="rmsnorm",
    )(h, g.reshape(1, d))


def _norm_matmul_kernel(h_ref, g_ref, w_ref, o_ref, xn_ref, *, act, scale):
    @pl.when(pl.program_id(1) == 0)
    def _():
        xn_ref[...] = _rms(h_ref[...], g_ref[...]).astype(xn_ref.dtype)

    o_ref[...] = _activate(_dot(xn_ref[...], w_ref[...].astype(BF16)), act, scale).astype(o_ref.dtype)


def norm_matmul(h, g, w, *, layer=None, act=None, scale=None, out_dtype=F32, n=None, tm=1024, tn=1024):
    m, kdim = h.shape
    n = w.shape[-1] if n is None else n
    tm, tn = _pick_tile(m, tm), _pick_tile(n, tn)
    return pl.pallas_call(
        functools.partial(_norm_matmul_kernel, act=act, scale=scale),
        out_shape=jax.ShapeDtypeStruct((m, n), out_dtype),
        grid=(m // tm, n // tn),
        in_specs=[pl.BlockSpec((tm, kdim), lambda i, j: (i, 0)),
                  pl.BlockSpec((1, kdim), lambda i, j: (0, 0)),
                  _weight_spec(w, layer, kdim, tn, lambda i, j: (0, j))],
        out_specs=pl.BlockSpec((tm, tn), lambda i, j: (i, j)),
        scratch_shapes=[pltpu.VMEM((tm, kdim), BF16)],
        compiler_params=_cparams(("parallel", "arbitrary")),
        name="norm_matmul",
    )(h, g.reshape(1, kdim), w)


def _rwkv_prep_kernel(h_ref, prev_ref, g_ref, mu_ref, *o_refs):
    g = g_ref[...]
    xn = _rms(h_ref[...], g)
    prev = _rms(prev_ref[...], g)[7:8, :] * (pl.program_id(0) > 0).astype(F32)
    row = lax.broadcasted_iota(jnp.int32, xn.shape, 0)
    shifted = jnp.where(row == 0, prev, pltpu.roll(xn, 1, axis=0))
    xx = shifted - xn
    for i, o_ref in enumerate(o_refs):
        o_ref[...] = (xn + xx * mu_ref[i:i + 1, :]).astype(o_ref.dtype)


def rwkv_prep(h, g, mu, tm=256):
    t, d = h.shape
    tm = _pick_tile(t, tm)
    blk = pl.BlockSpec((tm, d), lambda i: (i, 0))
    return pl.pallas_call(
        _rwkv_prep_kernel,
        out_shape=[jax.ShapeDtypeStruct((t, d), BF16)] * N_SHIFT_MIX,
        grid=(t // tm,),
        in_specs=[blk,
                  pl.BlockSpec((8, d), lambda i: (jnp.maximum(i * (tm // 8) - 1, 0), 0)),
                  pl.BlockSpec((1, d), lambda i: (0, 0)),
                  pl.BlockSpec((N_SHIFT_MIX, d), lambda i: (0, 0))],
        out_specs=[blk] * N_SHIFT_MIX,
        compiler_params=_cparams(("parallel",)),
        name="rwkv_prep",
    )(h, h, g.reshape(1, d), mu)


def _head_mask(dtype):
    row = lax.broadcasted_iota(jnp.int32, (SCAN_WIDTH, SCAN_WIDTH), 0)
    col = lax.broadcasted_iota(jnp.int32, (SCAN_WIDTH, SCAN_WIDTH), 1)
    return jnp.where((row >> 6) == (col >> 6), 1.0, 0.0).astype(dtype)


def _block_diag(x, mask):
    return jnp.concatenate([x] * SCAN_HEADS, axis=0) * mask


def _fold_rows(x_full, mask_f32):
    x = x_full * mask_f32
    L = SCAN_CHUNK
    return x[0:L] + x[L:2 * L] + x[2 * L:3 * L] + x[3 * L:4 * L]


def _split2(x):
    hi = x.astype(BF16)
    return hi, (x - hi.astype(F32)).astype(BF16)


def _head_sums(xs, mask_b):
    n = xs[0].shape[0]
    r = _dot(jnp.concatenate([piece for x in xs for piece in _split3(x)], axis=0), mask_b)
    return [r[3 * i * n:(3 * i + 1) * n] + r[(3 * i + 1) * n:(3 * i + 2) * n] + r[(3 * i + 2) * n:(3 * i + 3) * n]
            for i in range(len(xs))]


def _bd_parts(y_split, mask_b):
    hi, lo = y_split
    return _block_diag(hi, mask_b), _block_diag(lo, mask_b)


def _mm_heads(x_split, y_parts):
    x_hi, x_lo = x_split
    y_hi, y_lo = y_parts
    n = x_hi.shape[0]
    both = _dot(jnp.concatenate([x_hi, x_lo], axis=0), y_hi)
    return both[:n] + both[n:] + _dot(x_hi, y_lo)


def _run_interleaved(stage_generators):
    live = list(stage_generators)
    while live:
        still = []
        for gen in live:
            try:
                next(gen)
                still.append(gen)
            except StopIteration:
                pass
        live = still


def _tri_inverse_heads(a, j_row, m_col, mask_b):
    L = SCAN_CHUNK
    eye = jnp.where(m_col == j_row, 1.0, 0.0)
    d = jnp.where((j_row >> 4) == (m_col >> 4), a, 0.0)
    e = a - d
    p = eye + d
    d_split = _split2(d)
    x = _mm_heads(d_split, _bd_parts(d_split, mask_b))
    yield
    for _ in range(2):
        x_split, p_split = _split2(x), _split2(p)
        stacked = tuple(jnp.concatenate([xs, ps], axis=0) for xs, ps in zip(x_split, p_split))
        both = _mm_heads(stacked, _bd_parts(x_split, mask_b))
        x, p = both[:L], p + both[L:]
        yield
    p = p + _mm_heads(_split2(p), _bd_parts(_split2(x), mask_b))
    yield
    p_split = _split2(p)
    n1 = _mm_heads(p_split, _bd_parts(_split2(e), mask_b))
    yield
    n1_split = _split2(n1)
    n2 = _mm_heads(n1_split, _bd_parts(n1_split, mask_b))
    yield
    q = eye + n1
    q = q + _mm_heads(_split2(q), _bd_parts(_split2(n2), mask_b))
    yield
    return _mm_heads(_split2(q), _bd_parts(p_split, mask_b))


def _softplus(x):
    return jnp.maximum(x, 0.0) + jnp.log(1.0 + jnp.exp(-jnp.abs(x)))


def _rwkv_chunk_kernel(*refs, mix_v):
    if mix_v:
        r_ref, k_ref, v_ref, wl_ref, al_ref, vl_ref, vf_ref, par_ref, m_ref, c_ref, qp_ref, y0_ref, bo_ref = refs
    else:
        r_ref, k_ref, v_ref, wl_ref, al_ref, par_ref, m_ref, c_ref, qp_ref, y0_ref, bo_ref = refs
    L, W = SCAN_CHUNK, SCAN_WIDTH
    mask_b = _head_mask(BF16)
    mask_f = _head_mask(F32)
    tri_incl = jnp.where(lax.broadcasted_iota(jnp.int32, (L, L), 1)
                         <= lax.broadcasted_iota(jnp.int32, (L, L), 0), 1.0, 0.0).astype(BF16)
    j_row = lax.broadcasted_iota(jnp.int32, (L, W), 0)
    m_col = lax.broadcasted_iota(jnp.int32, (L, W), 1) & (L - 1)
    strict = m_col < j_row
    incl = m_col <= j_row

    def unit(u):
        cols = slice(u * W, (u + 1) * W)
        par = par_ref[:, cols]
        w0, a0, v0, k_k, k_a, r_k = (par[i:i + 1] for i in range(6))
        r, k, v = r_ref[:, cols], k_ref[:, cols], v_ref[:, cols]
        lw = -jnp.exp(-_softplus(-(w0 + wl_ref[:, cols])) - 0.5)
        a_s = jax.nn.sigmoid(a0 + al_ref[:, cols])
        if mix_v:
            v = v + (vf_ref[:, cols] - v) * jax.nn.sigmoid(v0 + vl_ref[:, cols])
        kk = k * k_k
        k = k * (1.0 + (a_s - 1.0) * k_a)
        kk_sq, rk_sum = _head_sums([kk * kk, r * k * r_k], mask_b)
        c = _dot_exact_lhs(tri_incl, lw)
        yield
        kk = kk * lax.rsqrt(jnp.maximum(kk_sq, 1e-24))
        bo_ref[:, cols] = rk_sum * v
        c_last = c[L - 1:L, :]
        p_inv = jnp.exp(-c)
        p_rest = jnp.exp(c_last - c)
        rt = r * jnp.exp(c)
        at = (-kk * jnp.exp(c - lw)).astype(BF16)
        b = kk * a_s
        vb = v.astype(BF16)
        ar = jnp.concatenate([at, rt.astype(BF16)], axis=0)
        g_b = _dot_nt(ar, _block_diag((b * p_inv).astype(BF16), mask_b))
        g_k = _dot_nt(ar, _block_diag((k * p_inv).astype(BF16), mask_b))
        yield
        a_ab = jnp.where(strict, g_b[:L], 0.0)
        a_ak = jnp.where(strict, g_k[:L], 0.0).astype(BF16)
        a_rb = jnp.where(incl, g_b[L:], 0.0).astype(BF16)
        a_rk = jnp.where(incl, g_k[L:], 0.0).astype(BF16)
        v_bd = _block_diag(vb, mask_b)
        akv = _dot(a_ak, v_bd)
        t_inv = yield from _tri_inverse_heads(a_ab, j_row, m_col, mask_b)
        yield
        t_split = _split2(t_inv)
        w_both = _dot(jnp.concatenate(t_split, axis=0), _block_diag(at, mask_b))
        u0 = _mm_heads(t_split, _bd_parts(_split2(akv), mask_b)).astype(BF16)
        yield
        w = (w_both[:L] + w_both[L:]).astype(BF16)
        qp_ref[:, cols] = (rt + _dot(a_rb, _block_diag(w, mask_b))).astype(qp_ref.dtype)
        y0_ref[:, cols] = _dot(a_rb, _block_diag(u0, mask_b)) + _dot(a_rk, v_bd)
        bh = (b * p_rest).astype(BF16)
        kh = (k * p_rest).astype(BF16)
        decay_diag = jnp.where(m_col == j_row, jnp.exp(c_last), 0.0)
        m_ref[:, cols] = _fold_rows(_dot_tn(bh, w), mask_f) + decay_diag
        c_ref[:, cols] = _fold_rows(_dot_tn(bh, u0) + _dot_tn(kh, vb), mask_f)

    _run_interleaved(unit(u) for u in range(SCAN_UNITS))


def rwkv_chunks(r, k, v, w_lin, a_lin, v_lin, v_first, params):
    t, d = r.shape
    bw = SCAN_UNITS * SCAN_WIDTH
    blk = pl.BlockSpec((SCAN_CHUNK, bw), lambda ci, gi: (ci, gi))
    mix_v = v_lin is not None
    args = [r, k, v, w_lin, a_lin] + ([v_lin, v_first] if mix_v else []) + [params]
    f32 = jax.ShapeDtypeStruct((t, d), F32)
    return pl.pallas_call(
        functools.partial(_rwkv_chunk_kernel, mix_v=mix_v),
        out_shape=[f32, f32, jax.ShapeDtypeStruct((t, d), BF16), f32, f32],
        grid=(t // SCAN_CHUNK, d // bw),
        in_specs=[blk] * (len(args) - 1) + [pl.BlockSpec((8, bw), lambda ci, gi: (0, gi))],
        out_specs=[blk] * 5,
        compiler_params=_cparams(("parallel", "parallel")),
        name="rwkv_chunks",
    )(*args)


def _rwkv_state_kernel(m_ref, c_ref, qp_ref, y0_ref, bo_ref, g_ref, ln_ref, o_ref, st_ref):
    @pl.when(pl.program_id(1) == 0)
    def _():
        st_ref[...] = jnp.zeros_like(st_ref)

    W = SCAN_WIDTH
    mask_b = _head_mask(BF16)
    def unit(u):
        cols = slice(u * W, (u + 1) * W)
        st_parts = _bd_parts(_split2(st_ref[u]), mask_b)
        y = _dot(qp_ref[:, cols], st_parts[0]) + y0_ref[:, cols]
        st_ref[u] = _mm_heads(_split2(m_ref[:, cols]), st_parts) + c_ref[:, cols]
        yield
        mean = _head_sums([y], mask_b)[0] * (1.0 / RW_HEAD)
        yield
        dev = y - mean
        var = _head_sums([dev * dev], mask_b)[0] * (1.0 / RW_HEAD)
        yield
        ln = ln_ref[:, cols]
        out = (dev * lax.rsqrt(var + GN_EPS) * ln[0:1] + ln[1:2] + bo_ref[:, cols]) * g_ref[:, cols]
        o_ref[:, cols] = out.astype(o_ref.dtype)

    _run_interleaved(unit(u) for u in range(SCAN_UNITS))


def rwkv_state_scan(m, c, qp, y0, bonus, g, ln_params):
    t, d = m.shape
    bw = SCAN_UNITS * SCAN_WIDTH
    blk = pl.BlockSpec((SCAN_CHUNK, bw), lambda gi, ci: (ci, gi))
    return pl.pallas_call(
        _rwkv_state_kernel,
        out_shape=jax.ShapeDtypeStruct((t, d), BF16),
        grid=(d // bw, t // SCAN_CHUNK),
        in_specs=[blk] * 6 + [pl.BlockSpec((8, bw), lambda gi, ci: (0, gi))],
        out_specs=blk,
        scratch_shapes=[pltpu.VMEM((SCAN_UNITS, RW_HEAD, SCAN_WIDTH), F32)],
        compiler_params=_cparams(("parallel", "arbitrary")),
        name="rwkv_state_scan",
    )(m, c, qp, y0, bonus, g, ln_params)


def _compress_kernel(x_ref, w1_ref, c_ref, w2_ref, o_ref):
    ab = _dot(x_ref[...], w1_ref[...])
    a = ab[:, :NSA_HD]
    bsh = ab[:, NSA_HD:]
    n16 = a.shape[0]
    row = lax.broadcasted_iota(jnp.int32, a.shape, 0)
    b_next = jnp.where(row == n16 - 1, 0.0, pltpu.roll(bsh, n16 - 1, axis=0))
    hid = a + b_next + c_ref[...]
    hid = jax.nn.gelu(hid, approximate=True)
    o_ref[...] = _dot(hid.astype(BF16), w2_ref[...]).astype(o_ref.dtype)


def compress(x16, w1cat, cvec, w2):
    _, g, n16, kd = x16.shape
    d = NSA_HD
    return pl.pallas_call(
        _compress_kernel,
        out_shape=jax.ShapeDtypeStruct((2, g, n16, d), BF16),
        grid=(2, g),
        in_specs=[pl.BlockSpec((None, None, n16, kd), lambda p, gi: (p, gi, 0, 0)),
                  pl.BlockSpec((None, kd, 2 * d), lambda p, gi: (p, 0, 0)),
                  pl.BlockSpec((None, 1, d), lambda p, gi: (p, 0, 0)),
                  pl.BlockSpec((None, d, d), lambda p, gi: (p, 0, 0))],
        out_specs=pl.BlockSpec((None, None, n16, d), lambda p, gi: (p, gi, 0, 0)),
        compiler_params=_cparams(("parallel", "parallel")),
        name="nsa_compress",
    )(x16, w1cat, cvec, w2)


def _with_ones(v):
    return jnp.concatenate([v, jnp.ones_like(v)], axis=1)


def _nsa_kernel(q_ref, gate_ref, kc_ref, vc_ref, ks_ref, vs_ref, kw_ref, vw_ref, oh_ref,
                gs_ref, ov_ref, bs_ref, o_ref, s_buf, p_buf):
    qi = pl.program_id(1)
    R, Q, D, NG = NSA_REP, QBLK, NSA_HD, NSA_STEP_GROUPS
    n_cp = kc_ref.shape[1]
    n_sb = ov_ref.shape[1]
    n_far = jnp.maximum(qi - 1, 0)
    n0 = pl.multiple_of(jnp.maximum(qi - 1, 0) * Q, Q)
    n_tiles = (n_far + (SEL_TILE // Q - 1)) // (SEL_TILE // Q)
    max_tile = oh_ref.shape[0] // SEL_TILE - 1
    state = [{} for _ in range(NG)]

    def slab(ref, gl, start, n):
        return ref[pl.ds(start, n), gl * D:(gl + 1) * D]

    def sel_keys(gl, start, n):
        return jnp.concatenate([slab(ks_ref, gl, WINDOW + start, n), oh_ref[pl.ds(start, n), :]], axis=1)

    def key_tile(gl, i):
        return sel_keys(gl, pl.multiple_of(i * SEL_TILE, SEL_TILE), SEL_TILE)

    def value_tile(gl, i):
        return _with_ones(slab(vs_ref, gl, pl.multiple_of(WINDOW + i * SEL_TILE, SEL_TILE), SEL_TILE))

    def before_loop(gl):
        st = state[gl]
        q = q_ref[:, gl * R * D:(gl + 1) * R * D]
        qs = jnp.concatenate([q[:, r * D:(r + 1) * D] for r in range(R)], axis=0)
        t_rows = qi * Q + (lax.broadcasted_iota(jnp.int32, (R * Q, 1), 0) & (Q - 1))
        c_idx = lax.broadcasted_iota(jnp.int32, (1, n_cp), 1)
        m_off = qi * (Q // CMP_STRIDE) - lax.broadcasted_iota(jnp.int32, (n_cp, D), 0) + CMP_BAND_LO
        k_idx = lax.broadcasted_iota(jnp.int32, (n_cp, D), 1) & (CMP_BIAS_K - 1)
        in_band = ((k_idx < CMP_BAND) & (m_off == k_idx)) | ((k_idx == CMP_BAND) & (m_off >= CMP_BAND))
        k_c = jnp.concatenate([kc_ref[gl], jnp.where(in_band, 1.0, 0.0).astype(BF16)], axis=1)
        qk_c = _dot_nt(jnp.concatenate([qs, gs_ref[gl]], axis=1), k_c)
        w0 = pl.multiple_of(qi * Q, Q)
        qk_w = _dot_nt(qs, slab(kw_ref, gl, w0, WINDOW + Q))
        yield
        m_c = (c_idx * CMP_STRIDE + (CMP_LEN - 1)) <= t_rows
        s_c = jnp.where(m_c, qk_c, NEG_INF)
        e_c = jnp.exp2(s_c - jnp.max(s_c, axis=-1, keepdims=True))
        acc_c = _dot(e_c.astype(BF16), _with_ones(vc_ref[gl]))
        yield
        inv_l = jnp.where(t_rows >= CMP_LEN - 1, 1.0 / acc_c[:, D:D + 1], 0.0)
        st["o_c"] = acc_c[:, :D] * inv_l
        p_c = e_c * inv_l
        p_sum = p_c[0:Q] + p_c[Q:2 * Q] + p_c[2 * Q:3 * Q] + p_c[3 * Q:4 * Q]
        imp = _dot_exact_rhs(p_sum, ov_ref[...])
        j_io = lax.broadcasted_iota(jnp.int32, (1, WINDOW + Q), 1)
        oldest = jnp.where(lax.broadcasted_iota(jnp.int32, (R * Q, Q), 1) > (t_rows & (Q - 1)), 0.0, NEG_INF)
        bias_w = jnp.concatenate([oldest, jnp.zeros((R * Q, WINDOW - 2 * Q), F32), bs_ref[gl, 0]], axis=1)
        s_w = jnp.where(j_io >= WINDOW - qi * Q, qk_w + bias_w, NEG_INF)
        e_w = jnp.exp2(s_w - jnp.max(s_w, axis=-1, keepdims=True))
        acc_w = _dot(e_w.astype(BF16), _with_ones(slab(vw_ref, gl, w0, WINDOW + Q)))
        yield
        st["o_w"] = acc_w[:, :D] / acc_w[:, D:]
        imp_t = imp.T
        s_row = lax.broadcasted_iota(jnp.int32, (n_sb, Q), 0)
        t_col = qi * Q + lax.broadcasted_iota(jnp.int32, (n_sb, Q), 1)
        forced = (s_row == (t_col >> 6)) | (s_row == 0)
        score = jnp.where(forced, SEL_FORCE, jnp.where(s_row * SEL_BLOCK <= t_col, imp_t, -SEL_FORCE))
        sel_t = jnp.zeros((n_sb, Q), F32)
        s_f = s_row.astype(F32)
        for _ in range(SEL_TOP):
            mx = jnp.max(score, axis=0, keepdims=True)
            first = jnp.min(jnp.where(score == mx, s_f, float(n_sb)), axis=0, keepdims=True)
            pick = s_f == first
            sel_t = jnp.where(pick, 1.0, sel_t)
            score = jnp.where(pick, -3e38, score)
        sel = sel_t.T
        blk = lax.broadcasted_iota(jnp.int32, (Q, n_sb), 1)
        neg_near = jnp.where(sel > 0.5, 0.0, NEG_INF).astype(BF16)
        neg_far = jnp.where((sel > 0.5) & (blk < 2 * n_far), 0.0, NEG_INF).astype(BF16)
        q_near = jnp.concatenate([qs, jnp.concatenate([neg_near] * R, axis=0)], axis=1)
        st["q_far"] = jnp.concatenate([qs, jnp.concatenate([neg_far] * R, axis=0)], axis=1)
        s = _dot_nt(q_near, sel_keys(gl, n0, 2 * Q)) + bs_ref[gl, jnp.where(qi == 0, 1, 0)]
        s_far = _dot_nt(st["q_far"], key_tile(gl, 0))
        s_buf[gl, 0] = s_far
        p_buf[gl, 1] = jnp.zeros(p_buf.shape[2:], p_buf.dtype)
        yield
        st["row_max"] = jnp.max(s_far, axis=-1, keepdims=True)
        st["m"] = jnp.max(s, axis=-1, keepdims=True)
        st["acc"] = _dot(jnp.exp2(s - st["m"]).astype(BF16), _with_ones(slab(vs_ref, gl, WINDOW + n0, 2 * Q)))

    _run_interleaved([before_loop(gl) for gl in range(NG)])

    def half_step(gl, i, slot, carry):
        m_run, acc, row_max = carry
        s_next = _dot_nt(state[gl]["q_far"], key_tile(gl, jnp.minimum(i + 1, max_tile)))
        s_buf[gl, 1 - slot] = s_next
        pv = _dot(p_buf[gl, 1 - slot], value_tile(gl, jnp.maximum(i - 1, 0)))
        m_new = jnp.maximum(m_run, row_max)
        acc = jnp.exp2(m_run - m_new) * (acc + pv)
        p_buf[gl, slot] = jnp.exp2(s_buf[gl, slot] - m_new).astype(p_buf.dtype)
        return m_new, acc, jnp.max(s_next, axis=-1, keepdims=True)

    def far_loop(gl):
        st = state[gl]
        pair_body = lambda j, carry: half_step(gl, 2 * j + 1, 1, half_step(gl, 2 * j, 0, carry))
        return lax.fori_loop(0, n_pairs, pair_body, (st["m"], st["acc"], st["row_max"]))

    n_pairs = (n_tiles + 1) >> 1
    carries = [far_loop(gl) for gl in range(NG)]
    last_tile = jnp.maximum(2 * n_pairs - 1, 0)
    last_pv = [_dot(p_buf[gl, 1], value_tile(gl, last_tile)) for gl in range(NG)]
    for gl, (_, acc_s, _) in enumerate(carries):
        acc_s = acc_s + last_pv[gl]
        o_s = acc_s[:, :D] / acc_s[:, D:]
        o_c, o_w = state[gl]["o_c"], state[gl]["o_w"]
        gates = gate_ref[gl]
        for r in range(R):
            rows = slice(r * Q, (r + 1) * Q)
            o = (gates[:, 3 * r:3 * r + 1] * o_c[rows] + gates[:, 3 * r + 1:3 * r + 2] * o_s[rows]
                 + gates[:, 3 * r + 2:3 * r + 3] * o_w[rows])
            o_ref[:, (gl * R + r) * D:(gl * R + r + 1) * D] = o.astype(o_ref.dtype)


def nsa_attention(q, gates, kc, vc, kv_padded, block_onehot, gs, ov, bs):
    t = q.shape[0]
    g, n_cp, d = kc.shape
    rq = NSA_REP * QBLK
    ng = NSA_STEP_GROUPS
    assert g % ng == 0
    per_g = lambda shape: pl.BlockSpec((ng,) + shape, lambda gi, qi: (gi,) + (0,) * len(shape))
    kv_part = lambda part: pl.BlockSpec((t + WINDOW, ng * d), lambda gi, qi: (0, part * g // ng + gi))
    return pl.pallas_call(
        _nsa_kernel,
        out_shape=jax.ShapeDtypeStruct((t, NSA_HEADS * d), BF16),
        grid=(g // ng, t // QBLK),
        in_specs=[pl.BlockSpec((QBLK, ng * NSA_REP * d), lambda gi, qi: (qi, gi)),
                  pl.BlockSpec((ng, QBLK, NSA_REP * N_BRANCH), lambda gi, qi: (gi, qi, 0)),
                  per_g((n_cp, d)), per_g((n_cp, d)),
                  kv_part(2), kv_part(3), kv_part(4), kv_part(5),
                  pl.BlockSpec((t, d), lambda gi, qi: (0, 0)),
                  per_g((rq, d)),
                  pl.BlockSpec(ov.shape, lambda gi, qi: (0, 0)),
                  per_g((2, rq, 2 * QBLK))],
        out_specs=pl.BlockSpec((QBLK, ng * NSA_REP * d), lambda gi, qi: (qi, gi)),
        scratch_shapes=[pltpu.VMEM((ng, 2, rq, SEL_TILE), F32), pltpu.VMEM((ng, 2, rq, SEL_TILE), BF16)],
        compiler_params=_cparams(("parallel", "arbitrary"), NSA_VMEM_LIMIT_BYTES),
        name="nsa_attention",
    )(q, gates, kc, vc, kv_padded, kv_padded, kv_padded, kv_padded, block_onehot, gs, ov, bs)


def _t5_bucket_np(n):
    n = np.maximum(n, 0)
    max_exact = REL_BUCKETS // 2
    nf = np.maximum(n, 1).astype(np.float64)
    large = max_exact + (np.log(nf / max_exact) / math.log(REL_MAX_DIST / max_exact)
                         * (REL_BUCKETS - max_exact)).astype(np.int64)
    large = np.minimum(large, REL_BUCKETS - 1)
    return np.where(n < max_exact, n, large).astype(np.int32)


def _nsa_bias_tables(rel_bias):
    G, R, Q = NSA_GROUPS, NSA_REP, QBLK
    table = (rel_bias.astype(F32) * LOG2E).reshape(REL_BUCKETS, G, R).transpose(1, 2, 0)
    qq = np.arange(Q)

    def lookup(dist):
        return table[:, :, _t5_bucket_np(dist)].reshape(G, R * Q, dist.shape[1])

    m = np.arange(CMP_BAND) - CMP_BAND_LO
    gs = lookup(qq[:, None] - (CMP_LEN - 1) + CMP_STRIDE * m[None, :])
    far = jnp.broadcast_to(table[:, :, REL_BUCKETS - 1][:, :, None, None], (G, R, Q, 1)).reshape(G, R * Q, 1)
    gs = jnp.concatenate([gs, far, jnp.zeros((G, R * Q, CMP_BIAS_K - CMP_BAND - 1), F32)], axis=-1)
    h, mid, l = _split3(gs)
    gs3 = jnp.concatenate([h, mid, l, jnp.zeros_like(h)], axis=-1)
    def toeplitz(c):
        mm = np.arange(2 * Q)
        v = table[:, :, _t5_bucket_np(np.where(mm < Q, c - mm, c + 2 * Q - mm))]
        rows = jnp.tile(v, (1, 1, Q))[:, :, :Q * (2 * Q - 1)].reshape(G, R, Q, 2 * Q - 1)
        return rows[:, :, :, :Q].reshape(G, R * Q, Q)

    d0 = np.tile(qq[:, None] - qq[None, :], (R, 1))
    causal = jnp.asarray(np.where(d0 >= 0, 0.0, NEG_INF), F32)[None]
    b0 = toeplitz(0) + causal
    b1 = toeplitz(Q)
    far_c = jnp.broadcast_to(far, (G, R * Q, Q))
    neg = jnp.full((G, R * Q, Q), NEG_INF, F32)
    bs = jnp.stack([jnp.concatenate([b1 - far_c, b0 - far_c], -1),
                    jnp.concatenate([b0 - far_c, neg], -1)], axis=1)
    return gs3, bs


def _overlap_table(n_cp, n_sb):
    c = np.arange(n_cp)[:, None] * CMP_STRIDE
    s = np.arange(n_sb)[None, :] * SEL_BLOCK
    return jnp.asarray(((c <= s + SEL_BLOCK - 1) & (c + CMP_LEN - 1 >= s)), BF16)


def _pad_cols(w, n):
    return jnp.pad(w, ((0, 0), (0, n - w.shape[1])))


def _pad_rows(w, n):
    return jnp.pad(w, ((0, n - w.shape[0]), (0, 0)))


def _rwkv_layer(h, layer, norm_g, mu, wr, wk, wv, wo, w0, w1, w2, a0, a1, a2, g1, g2,
                k_k, k_a, r_k, lnx_w, lnx_b, v_first, v0, v1, v2):
    bf = lambda w: w.astype(BF16)
    xr, xw, xk, xv, xa, xg = rwkv_prep(h, norm_g, mu)
    r = matmul(xr, wr, layer=layer)
    k = matmul(xk, wk, layer=layer)
    v = matmul(xv, wv, layer=layer)
    w_lin = lora(xw, bf(_pad_cols(w1, 128)), bf(_pad_rows(w2, 128)), act="tanh")
    a_lin = lora(xa, bf(a1), bf(a2))
    g = lora(xg, bf(g1), bf(g2), act="sigmoid")
    zeros = jnp.zeros_like(w0)
    if v_first is None:
        v_lin, v0 = None, zeros
    else:
        v_lin = lora(xv, bf(_pad_cols(v1, 128)), bf(_pad_rows(v2, 128)))
    params = jnp.stack([w0, a0, v0, k_k, k_a, r_k.reshape(-1), zeros, zeros])
    m, c, qp, y0, bonus = rwkv_chunks(r, k, v, w_lin, a_lin, v_lin, v_first, params)
    ln_params = jnp.stack([lnx_w, lnx_b] + [zeros] * 6)
    gated = rwkv_state_scan(m, c, qp, y0, bonus, g, ln_params)
    h = matmul(gated, wo, layer=layer, residual=h)
    return h, (v if v_first is None else v_first)


def _mlp(h, layer, norm_g, w_up, w_down):
    hid = norm_matmul(h, norm_g, w_up, layer=layer, act="relu2", out_dtype=BF16)
    return matmul(hid, w_down, layer=layer, residual=h, tm=2048, tn=512)


def _nsa_shared_kv(h, kv_norm_g, w_kv, cmp_pe, cmp_w1, cmp_b1, cmp_w2):
    t = h.shape[0]
    G, D = NSA_GROUPS, NSA_HD
    kv = norm_matmul(h, kv_norm_g, w_kv, out_dtype=BF16)
    x16 = kv[:, :2 * G * D].reshape(t, 2, G, D).transpose(1, 2, 0, 3).reshape(2, G, t // CMP_STRIDE, CMP_STRIDE * D)
    half = CMP_STRIDE * D
    w1cat = jnp.concatenate([cmp_w1[:, :half], cmp_w1[:, half:]], axis=-1).astype(BF16)
    cvec = jnp.einsum("plk,plkd->pd", cmp_pe, cmp_w1.reshape(2, CMP_LEN, D, D),
                      precision=lax.Precision.HIGHEST) + cmp_b1
    kvc = compress(x16, w1cat, cvec.reshape(2, 1, D), cmp_w2.astype(BF16))
    block_onehot = jnp.asarray(np.arange(t)[:, None] // SEL_BLOCK == np.arange(D)[None, :], BF16)
    return kvc[0], kvc[1], jnp.pad(kv, ((WINDOW, 0), (0, 0))), block_onehot


def _nsa_layer(h, layer, norm_g, wq, wo, kvs, tables):
    t = h.shape[0]
    nq = NSA_HEADS * NSA_HD
    q = norm_matmul(h, norm_g, wq, layer=layer, n=nq, scale=NSA_HD ** -0.5 * LOG2E, out_dtype=BF16)
    gates = norm_matmul(h, norm_g, _pad_cols(wq[layer, :, nq:], 128).astype(BF16), act="sigmoid")
    gates = gates[:, :NSA_HEADS * N_BRANCH].reshape(t, NSA_GROUPS, NSA_REP * N_BRANCH).transpose(1, 0, 2)
    o = nsa_attention(q, gates, *kvs, *tables)
    return matmul(o, wo, layer=layer, residual=h)


def kernel(x, a_norm_g, rw_mu, rw_wr, rw_wk, rw_wv, rw_wo, rw_w0, rw_w1, rw_w2, rw_a0, rw_a1, rw_a2,
           rw_v0, rw_v1, rw_v2, rw_g1, rw_g2, rw_kk, rw_ka, rw_rk, rw_lnx_w, rw_lnx_b, kv_norm_g, w_kv,
           cmp_pe, cmp_w1, cmp_b1, cmp_w2, b_norm_g, nsa_wq, nsa_wo, rel_bias, m_norm_g, mlp_up,
           mlp_down, final_norm_g):
    b, t, d = x.shape
    assert b == 1 and d == D_MODEL and t % SEL_TILE == 0 and t // SEL_BLOCK <= 128 and WINDOW == 4 * QBLK
    h = x.reshape(t, d)
    v_first = None
    kvs = tables = None
    rw_w = (rw_wr, rw_wk, rw_wv, rw_wo)
    for layer in range(DEPTH):
        if layer < N_A_LAYERS:
            i = layer
            vp = (None, None, None) if i == 0 else (rw_v0[i - 1], rw_v1[i - 1], rw_v2[i - 1])
            h, v_first = _rwkv_layer(
                h, i, a_norm_g[i], rw_mu[i], *rw_w, rw_w0[i], rw_w1[i],
                rw_w2[i], rw_a0[i], rw_a1[i], rw_a2[i], rw_g1[i], rw_g2[i], rw_kk[i], rw_ka[i],
                rw_rk[i].reshape(1, RW_HEADS, RW_HEAD), rw_lnx_w[i], rw_lnx_b[i], v_first, *vp)
        else:
            j = layer - N_A_LAYERS
            if j == 0:
                kvs = _nsa_shared_kv(h, kv_norm_g, w_kv, cmp_pe, cmp_w1, cmp_b1, cmp_w2)
                gs3, bs = _nsa_bias_tables(rel_bias)
                tables = (gs3, _overlap_table(t // CMP_STRIDE, 128), bs)
            h = _nsa_layer(h, j, b_norm_g[j], nsa_wq, nsa_wo, kvs, tables)
        h = _mlp(h, layer, m_norm_g[layer], mlp_up, mlp_down)
    return rmsnorm(h, final_norm_g, out_dtype=x.dtype).reshape(b, t, d)
```

```python
import functools
import math

import numpy as np
import jax
import jax.numpy as jnp
from jax import lax
from jax.experimental import pallas as pl
from jax.experimental.pallas import tpu as pltpu

F32 = jnp.float32
BF16 = jnp.bfloat16

D_MODEL = 2048
DEPTH = 4
N_A_LAYERS = 2
RW_HEAD = 64
RW_HEADS = D_MODEL // RW_HEAD
N_SHIFT_MIX = 6
GN_EPS = 64e-5
NSA_HEADS = 16
NSA_GROUPS = 4
NSA_REP = NSA_HEADS // NSA_GROUPS
NSA_HD = D_MODEL // NSA_HEADS
CMP_STRIDE = 16
CMP_LEN = 32
SEL_BLOCK = 64
SEL_TOP = 16
WINDOW = 512
QBLK = 128
N_BRANCH = 3
SEL_FORCE = 1e4
NEG_INF = -1e30
REL_BUCKETS = 32
REL_MAX_DIST = 128
NORM_EPS = 1e-6
LOG2E = math.log2(math.e)

V7X_VMEM_BYTES = 64 * 1024 * 1024
VMEM_LIMIT_BYTES = V7X_VMEM_BYTES - 8 * 1024 * 1024
NSA_VMEM_LIMIT_BYTES = V7X_VMEM_BYTES - 4 * 1024 * 1024
LORA_SPLIT = 4
SCAN_CHUNK = 64
SCAN_HEADS = 4
SCAN_WIDTH = SCAN_HEADS * RW_HEAD
SCAN_UNITS = 8
SEL_TILE = 512
NSA_STEP_GROUPS = 2
CMP_BAND = 16
CMP_BAND_LO = 6
CMP_BIAS_K = 32
assert CMP_BAND < CMP_BIAS_K and 4 * CMP_BIAS_K == NSA_HD and NSA_HD == 128


def _cparams(sem, vmem_limit_bytes=VMEM_LIMIT_BYTES):
    return pltpu.CompilerParams(dimension_semantics=sem, vmem_limit_bytes=vmem_limit_bytes)


def _split3(x):
    h = x.astype(BF16)
    r = x - h.astype(F32)
    m = r.astype(BF16)
    l = (r - m.astype(F32)).astype(BF16)
    return h, m, l


def _dot(a, b):
    return jnp.dot(a, b, preferred_element_type=F32)


def _dot_nt(a, b):
    return lax.dot_general(a, b, (((1,), (1,)), ((), ())), preferred_element_type=F32)


def _dot_tn(a, b):
    return lax.dot_general(a, b, (((0,), (0,)), ((), ())), preferred_element_type=F32)


def _dot_exact_lhs(a_bf16, x):
    h, m, l = _split3(x)
    return _dot(a_bf16, h) + _dot(a_bf16, m) + _dot(a_bf16, l)


def _dot_exact_rhs(x, b_bf16):
    h, m, l = _split3(x)
    return _dot(h, b_bf16) + _dot(m, b_bf16) + _dot(l, b_bf16)


def _activate(acc, act, scale):
    if act == "relu2":
        acc = jnp.square(jnp.maximum(acc, 0.0))
    elif act == "tanh":
        acc = jnp.tanh(acc)
    elif act == "sigmoid":
        acc = jax.nn.sigmoid(acc)
    return acc if scale is None else acc * scale


def _matmul_kernel(x_ref, w_ref, *rest, act, scale, has_res, nk):
    if has_res:
        res_ref, o_ref, *scratch = rest
    else:
        o_ref, *scratch = rest

    def epilogue(acc):
        acc = _activate(acc, act, scale)
        if has_res:
            acc = acc + res_ref[...]
        o_ref[...] = acc.astype(o_ref.dtype)

    if nk == 1:
        epilogue(_dot(x_ref[...], w_ref[...].astype(BF16)))
    else:
        acc_ref, = scratch
        k = pl.program_id(2)

        @pl.when(k == 0)
        def _():
            acc_ref[...] = jnp.zeros_like(acc_ref)

        acc_ref[...] += _dot(x_ref[...], w_ref[...].astype(BF16))

        @pl.when(k == nk - 1)
        def _():
            epilogue(acc_ref[...])


def _pick_tile(n, pref):
    t = min(n, pref)
    while n % t:
        t //= 2
    return t


def _weight_spec(w, layer, rows, cols, index):
    if w.ndim == 2:
        return pl.BlockSpec((rows, cols), index)
    return pl.BlockSpec((None, rows, cols), lambda *g: (layer,) + tuple(index(*g)))


def matmul(x, w, *, layer=None, act=None, scale=None, residual=None, out_dtype=F32, tm=1024, tn=1024, tk=2048):
    m, kdim = x.shape
    n = w.shape[-1]
    tm, tn, tk = _pick_tile(m, tm), _pick_tile(n, tn), _pick_tile(kdim, tk)
    nk = kdim // tk
    in_specs = [pl.BlockSpec((tm, tk), lambda j, i, k: (i, k)),
                _weight_spec(w, layer, tk, tn, lambda j, i, k: (k, j))]
    args = [x, w]
    if residual is not None:
        in_specs.append(pl.BlockSpec((tm, tn), lambda j, i, k: (i, j)))
        args.append(residual)
    return pl.pallas_call(
        functools.partial(_matmul_kernel, act=act, scale=scale, has_res=residual is not None, nk=nk),
        out_shape=jax.ShapeDtypeStruct((m, n), out_dtype),
        grid=(n // tn, m // tm, nk),
        in_specs=in_specs,
        out_specs=pl.BlockSpec((tm, tn), lambda j, i, k: (i, j)),
        scratch_shapes=[pltpu.VMEM((tm, tn), F32)] if nk > 1 else [],
        compiler_params=_cparams(("parallel", "parallel", "arbitrary")),
        name="matmul",
    )(*args)


def _lora_kernel(x_ref, w1_ref, w2_ref, o_ref, *, act):
    rows = x_ref.shape[0] // LORA_SPLIT

    def part(p):
        sl = pl.ds(p * rows, rows)
        first = _dot(x_ref[sl, :], w1_ref[...])
        yield
        o_ref[sl, :] = _dot(_activate(first, act, None).astype(BF16), w2_ref[...])

    _run_interleaved([part(p) for p in range(LORA_SPLIT)])


def lora(x, w1, w2, *, act=None, tm=1024):
    m, kdim = x.shape
    r, n = w2.shape
    tm = _pick_tile(m, tm)
    return pl.pallas_call(
        functools.partial(_lora_kernel, act=act),
        out_shape=jax.ShapeDtypeStruct((m, n), F32),
        grid=(m // tm,),
        in_specs=[pl.BlockSpec((tm, kdim), lambda i: (i, 0)),
                  pl.BlockSpec((kdim, r), lambda i: (0, 0)),
                  pl.BlockSpec((r, n), lambda i: (0, 0))],
        out_specs=pl.BlockSpec((tm, n), lambda i: (i, 0)),
        compiler_params=_cparams(("parallel",)),
        name="lora",
    )(x, w1, w2)


def _rms(x, g):
    return x * lax.rsqrt(jnp.mean(x * x, axis=-1, keepdims=True) + NORM_EPS) * g


def _rmsnorm_kernel(h_ref, g_ref, o_ref):
    o_ref[...] = _rms(h_ref[...], g_ref[...]).astype(o_ref.dtype)


def rmsnorm(h, g, out_dtype=BF16, tm=512):
    t, d = h.shape
    tm = _pick_tile(t, tm)
    return pl.pallas_call(
        _rmsnorm_kernel,
        out_shape=jax.ShapeDtypeStruct((t, d), out_dtype),
        grid=(t // tm,),
        in_specs=[pl.BlockSpec((tm, d), lambda i: (i, 0)), pl.BlockSpec((1, d), lambda i: (0, 0))],
        out_specs=pl.BlockSpec((tm, d), lambda i: (i, 0)),
        compiler_params=_cparams(("parallel",)),
        name="rmsnorm",
    )(h, g.reshape(1, d))


def _norm_matmul_kernel(h_ref, g_ref, w_ref, o_ref, xn_ref, *, act, scale):
    @pl.when(pl.program_id(1) == 0)
    def _():
        xn_ref[...] = _rms(h_ref[...], g_ref[...]).astype(xn_ref.dtype)

    o_ref[...] = _activate(_dot(xn_ref[...], w_ref[...].astype(BF16)), act, scale).astype(o_ref.dtype)


def norm_matmul(h, g, w, *, layer=None, act=None, scale=None, out_dtype=F32, n=None, tm=1024, tn=1024):
    m, kdim = h.shape
    n = w.shape[-1] if n is None else n
    tm, tn = _pick_tile(m, tm), _pick_tile(n, tn)
    return pl.pallas_call(
        functools.partial(_norm_matmul_kernel, act=act, scale=scale),
        out_shape=jax.ShapeDtypeStruct((m, n), out_dtype),
        grid=(m // tm, n // tn),
        in_specs=[pl.BlockSpec((tm, kdim), lambda i, j: (i, 0)),
                  pl.BlockSpec((1, kdim), lambda i, j: (0, 0)),
                  _weight_spec(w, layer, kdim, tn, lambda i, j: (0, j))],
        out_specs=pl.BlockSpec((tm, tn), lambda i, j: (i, j)),
        scratch_shapes=[pltpu.VMEM((tm, kdim), BF16)],
        compiler_params=_cparams(("parallel", "arbitrary")),
        name="norm_matmul",
    )(h, g.reshape(1, kdim), w)


def _rwkv_prep_kernel(h_ref, prev_ref, g_ref, mu_ref, *o_refs):
    g = g_ref[...]
    xn = _rms(h_ref[...], g)
    prev = _rms(prev_ref[...], g)[7:8, :] * (pl.program_id(0) > 0).astype(F32)
    row = lax.broadcasted_iota(jnp.int32, xn.shape, 0)
    shifted = jnp.where(row == 0, prev, pltpu.roll(xn, 1, axis=0))
    xx = shifted - xn
    for i, o_ref in enumerate(o_refs):
        o_ref[...] = (xn + xx * mu_ref[i:i + 1, :]).astype(o_ref.dtype)


def rwkv_prep(h, g, mu, tm=256):
    t, d = h.shape
    tm = _pick_tile(t, tm)
    blk = pl.BlockSpec((tm, d), lambda i: (i, 0))
    return pl.pallas_call(
        _rwkv_prep_kernel,
        out_shape=[jax.ShapeDtypeStruct((t, d), BF16)] * N_SHIFT_MIX,
        grid=(t // tm,),
        in_specs=[blk,
                  pl.BlockSpec((8, d), lambda i: (jnp.maximum(i * (tm // 8) - 1, 0), 0)),
                  pl.BlockSpec((1, d), lambda i: (0, 0)),
                  pl.BlockSpec((N_SHIFT_MIX, d), lambda i: (0, 0))],
        out_specs=[blk] * N_SHIFT_MIX,
        compiler_params=_cparams(("parallel",)),
        name="rwkv_prep",
    )(h, h, g.reshape(1, d), mu)


def _head_mask(dtype):
    row = lax.broadcasted_iota(jnp.int32, (SCAN_WIDTH, SCAN_WIDTH), 0)
    col = lax.broadcasted_iota(jnp.int32, (SCAN_WIDTH, SCAN_WIDTH), 1)
    return jnp.where((row >> 6) == (col >> 6), 1.0, 0.0).astype(dtype)


def _block_diag(x, mask):
    return jnp.concatenate([x] * SCAN_HEADS, axis=0) * mask


def _fold_rows(x_full, mask_f32):
    x = x_full * mask_f32
    L = SCAN_CHUNK
    return x[0:L] + x[L:2 * L] + x[2 * L:3 * L] + x[3 * L:4 * L]


def _split2(x):
    hi = x.astype(BF16)
    return hi, (x - hi.astype(F32)).astype(BF16)


def _head_sums(xs, mask_b):
    n = xs[0].shape[0]
    r = _dot(jnp.concatenate([piece for x in xs for piece in _split3(x)], axis=0), mask_b)
    return [r[3 * i * n:(3 * i + 1) * n] + r[(3 * i + 1) * n:(3 * i + 2) * n] + r[(3 * i + 2) * n:(3 * i + 3) * n]
            for i in range(len(xs))]


def _bd_parts(y_split, mask_b):
    hi, lo = y_split
    return _block_diag(hi, mask_b), _block_diag(lo, mask_b)


def _mm_heads(x_split, y_parts):
    x_hi, x_lo = x_split
    y_hi, y_lo = y_parts
    n = x_hi.shape[0]
    both = _dot(jnp.concatenate([x_hi, x_lo], axis=0), y_hi)
    return both[:n] + both[n:] + _dot(x_hi, y_lo)


def _run_interleaved(stage_generators):
    live = list(stage_generators)
    while live:
        still = []
        for gen in live:
            try:
                next(gen)
                still.append(gen)
            except StopIteration:
                pass
        live = still


def _tri_inverse_heads(a, j_row, m_col, mask_b):
    L = SCAN_CHUNK
    eye = jnp.where(m_col == j_row, 1.0, 0.0)
    d = jnp.where((j_row >> 4) == (m_col >> 4), a, 0.0)
    e = a - d
    p = eye + d
    d_split = _split2(d)
    x = _mm_heads(d_split, _bd_parts(d_split, mask_b))
    yield
    for _ in range(2):
        x_split, p_split = _split2(x), _split2(p)
        stacked = tuple(jnp.concatenate([xs, ps], axis=0) for xs, ps in zip(x_split, p_split))
        both = _mm_heads(stacked, _bd_parts(x_split, mask_b))
        x, p = both[:L], p + both[L:]
        yield
    p = p + _mm_heads(_split2(p), _bd_parts(_split2(x), mask_b))
    yield
    p_split = _split2(p)
    n1 = _mm_heads(p_split, _bd_parts(_split2(e), mask_b))
    yield
    n1_split = _split2(n1)
    n2 = _mm_heads(n1_split, _bd_parts(n1_split, mask_b))
    yield
    q = eye + n1
    q = q + _mm_heads(_split2(q), _bd_parts(_split2(n2), mask_b))
    yield
    return _mm_heads(_split2(q), _bd_parts(p_split, mask_b))


def _softplus(x):
    return jnp.maximum(x, 0.0) + jnp.log(1.0 + jnp.exp(-jnp.abs(x)))


def _rwkv_chunk_kernel(*refs, mix_v):
    if mix_v:
        r_ref, k_ref, v_ref, wl_ref, al_ref, vl_ref, vf_ref, par_ref, m_ref, c_ref, qp_ref, y0_ref, bo_ref = refs
    else:
        r_ref, k_ref, v_ref, wl_ref, al_ref, par_ref, m_ref, c_ref, qp_ref, y0_ref, bo_ref = refs
    L, W = SCAN_CHUNK, SCAN_WIDTH
    mask_b = _head_mask(BF16)
    mask_f = _head_mask(F32)
    tri_incl = jnp.where(lax.broadcasted_iota(jnp.int32, (L, L), 1)
                         <= lax.broadcasted_iota(jnp.int32, (L, L), 0), 1.0, 0.0).astype(BF16)
    j_row = lax.broadcasted_iota(jnp.int32, (L, W), 0)
    m_col = lax.broadcasted_iota(jnp.int32, (L, W), 1) & (L - 1)
    strict = m_col < j_row
    incl = m_col <= j_row

    def unit(u):
        cols = slice(u * W, (u + 1) * W)
        par = par_ref[:, cols]
        w0, a0, v0, k_k, k_a, r_k = (par[i:i + 1] for i in range(6))
        r, k, v = r_ref[:, cols], k_ref[:, cols], v_ref[:, cols]
        lw = -jnp.exp(-_softplus(-(w0 + wl_ref[:, cols])) - 0.5)
        a_s = jax.nn.sigmoid(a0 + al_ref[:, cols])
        if mix_v:
            v = v + (vf_ref[:, cols] - v) * jax.nn.sigmoid(v0 + vl_ref[:, cols])
        kk = k * k_k
        k = k * (1.0 + (a_s - 1.0) * k_a)
        kk_sq, rk_sum = _head_sums([kk * kk, r * k * r_k], mask_b)
        c = _dot_exact_lhs(tri_incl, lw)
        yield
        kk = kk * lax.rsqrt(jnp.maximum(kk_sq, 1e-24))
        bo_ref[:, cols] = rk_sum * v
        c_last = c[L - 1:L, :]
        p_inv = jnp.exp(-c)
        p_rest = jnp.exp(c_last - c)
        rt = r * jnp.exp(c)
        at = (-kk * jnp.exp(c - lw)).astype(BF16)
        b = kk * a_s
        vb = v.astype(BF16)
        ar = jnp.concatenate([at, rt.astype(BF16)], axis=0)
        g_b = _dot_nt(ar, _block_diag((b * p_inv).astype(BF16), mask_b))
        g_k = _dot_nt(ar, _block_diag((k * p_inv).astype(BF16), mask_b))
        yield
        a_ab = jnp.where(strict, g_b[:L], 0.0)
        a_ak = jnp.where(strict, g_k[:L], 0.0).astype(BF16)
        a_rb = jnp.where(incl, g_b[L:], 0.0).astype(BF16)
        a_rk = jnp.where(incl, g_k[L:], 0.0).astype(BF16)
        v_bd = _block_diag(vb, mask_b)
        akv = _dot(a_ak, v_bd)
        t_inv = yield from _tri_inverse_heads(a_ab, j_row, m_col, mask_b)
        yield
        t_split = _split2(t_inv)
        w_both = _dot(jnp.concatenate(t_split, axis=0), _block_diag(at, mask_b))
        u0 = _mm_heads(t_split, _bd_parts(_split2(akv), mask_b)).astype(BF16)
        yield
        w = (w_both[:L] + w_both[L:]).astype(BF16)
        qp_ref[:, cols] = (rt + _dot(a_rb, _block_diag(w, mask_b))).astype(qp_ref.dtype)
        y0_ref[:, cols] = _dot(a_rb, _block_diag(u0, mask_b)) + _dot(a_rk, v_bd)
        bh = (b * p_rest).astype(BF16)
        kh = (k * p_rest).astype(BF16)
        decay_diag = jnp.where(m_col == j_row, jnp.exp(c_last), 0.0)
        m_ref[:, cols] = _fold_rows(_dot_tn(bh, w), mask_f) + decay_diag
        c_ref[:, cols] = _fold_rows(_dot_tn(bh, u0) + _dot_tn(kh, vb), mask_f)

    _run_interleaved(unit(u) for u in range(SCAN_UNITS))


def rwkv_chunks(r, k, v, w_lin, a_lin, v_lin, v_first, params):
    t, d = r.shape
    bw = SCAN_UNITS * SCAN_WIDTH
    blk = pl.BlockSpec((SCAN_CHUNK, bw), lambda ci, gi: (ci, gi))
    mix_v = v_lin is not None
    args = [r, k, v, w_lin, a_lin] + ([v_lin, v_first] if mix_v else []) + [params]
    f32 = jax.ShapeDtypeStruct((t, d), F32)
    return pl.pallas_call(
        functools.partial(_rwkv_chunk_kernel, mix_v=mix_v),
        out_shape=[f32, f32, jax.ShapeDtypeStruct((t, d), BF16), f32, f32],
        grid=(t // SCAN_CHUNK, d // bw),
        in_specs=[blk] * (len(args) - 1) + [pl.BlockSpec((8, bw), lambda ci, gi: (0, gi))],
        out_specs=[blk] * 5,
        compiler_params=_cparams(("parallel", "parallel")),
        name="rwkv_chunks",
    )(*args)


def _rwkv_state_kernel(m_ref, c_ref, qp_ref, y0_ref, bo_ref, g_ref, ln_ref, o_ref, st_ref):
    @pl.when(pl.program_id(1) == 0)
    def _():
        st_ref[...] = jnp.zeros_like(st_ref)

    W = SCAN_WIDTH
    mask_b = _head_mask(BF16)
    def unit(u):
        cols = slice(u * W, (u + 1) * W)
        st_parts = _bd_parts(_split2(st_ref[u]), mask_b)
        y = _dot(qp_ref[:, cols], st_parts[0]) + y0_ref[:, cols]
        st_ref[u] = _mm_heads(_split2(m_ref[:, cols]), st_parts) + c_ref[:, cols]
        yield
        mean = _head_sums([y], mask_b)[0] * (1.0 / RW_HEAD)
        yield
        dev = y - mean
        var = _head_sums([dev * dev], mask_b)[0] * (1.0 / RW_HEAD)
        yield
        ln = ln_ref[:, cols]
        out = (dev * lax.rsqrt(var + GN_EPS) * ln[0:1] + ln[1:2] + bo_ref[:, cols]) * g_ref[:, cols]
        o_ref[:, cols] = out.astype(o_ref.dtype)

    _run_interleaved(unit(u) for u in range(SCAN_UNITS))


def rwkv_state_scan(m, c, qp, y0, bonus, g, ln_params):
    t, d = m.shape
    bw = SCAN_UNITS * SCAN_WIDTH
    blk = pl.BlockSpec((SCAN_CHUNK, bw), lambda gi, ci: (ci, gi))
    return pl.pallas_call(
        _rwkv_state_kernel,
        out_shape=jax.ShapeDtypeStruct((t, d), BF16),
        grid=(d // bw, t // SCAN_CHUNK),
        in_specs=[blk] * 6 + [pl.BlockSpec((8, bw), lambda gi, ci: (0, gi))],
        out_specs=blk,
        scratch_shapes=[pltpu.VMEM((SCAN_UNITS, RW_HEAD, SCAN_WIDTH), F32)],
        compiler_params=_cparams(("parallel", "arbitrary")),
        name="rwkv_state_scan",
    )(m, c, qp, y0, bonus, g, ln_params)


def _compress_kernel(x_ref, w1_ref, c_ref, w2_ref, o_ref):
    ab = _dot(x_ref[...], w1_ref[...])
    a = ab[:, :NSA_HD]
    bsh = ab[:, NSA_HD:]
    n16 = a.shape[0]
    row = lax.broadcasted_iota(jnp.int32, a.shape, 0)
    b_next = jnp.where(row == n16 - 1, 0.0, pltpu.roll(bsh, n16 - 1, axis=0))
    hid = a + b_next + c_ref[...]
    hid = jax.nn.gelu(hid, approximate=True)
    o_ref[...] = _dot(hid.astype(BF16), w2_ref[...]).astype(o_ref.dtype)


def compress(x16, w1cat, cvec, w2):
    _, g, n16, kd = x16.shape
    d = NSA_HD
    return pl.pallas_call(
        _compress_kernel,
        out_shape=jax.ShapeDtypeStruct((2, g, n16, d), BF16),
        grid=(2, g),
        in_specs=[pl.BlockSpec((None, None, n16, kd), lambda p, gi: (p, gi, 0, 0)),
                  pl.BlockSpec((None, kd, 2 * d), lambda p, gi: (p, 0, 0)),
                  pl.BlockSpec((None, 1, d), lambda p, gi: (p, 0, 0)),
                  pl.BlockSpec((None, d, d), lambda p, gi: (p, 0, 0))],
        out_specs=pl.BlockSpec((None, None, n16, d), lambda p, gi: (p, gi, 0, 0)),
        compiler_params=_cparams(("parallel", "parallel")),
        name="nsa_compress",
    )(x16, w1cat, cvec, w2)


def _with_ones(v):
    return jnp.concatenate([v, jnp.ones_like(v)], axis=1)


def _nsa_kernel(q_ref, gate_ref, kc_ref, vc_ref, ks_ref, vs_ref, kw_ref, vw_ref, oh_ref,
                gs_ref, ov_ref, bs_ref, o_ref, s_buf, p_buf):
    qi = pl.program_id(1)
    R, Q, D, NG = NSA_REP, QBLK, NSA_HD, NSA_STEP_GROUPS
    n_cp = kc_ref.shape[1]
    n_sb = ov_ref.shape[1]
    n_far = jnp.maximum(qi - 1, 0)
    n0 = pl.multiple_of(jnp.maximum(qi - 1, 0) * Q, Q)
    n_tiles = (n_far + (SEL_TILE // Q - 1)) // (SEL_TILE // Q)
    max_tile = oh_ref.shape[0] // SEL_TILE - 1
    state = [{} for _ in range(NG)]

    def slab(ref, gl, start, n):
        return ref[pl.ds(start, n), gl * D:(gl + 1) * D]

    def sel_keys(gl, start, n):
        return jnp.concatenate([slab(ks_ref, gl, WINDOW + start, n), oh_ref[pl.ds(start, n), :]], axis=1)

    def key_tile(gl, i):
        return sel_keys(gl, pl.multiple_of(i * SEL_TILE, SEL_TILE), SEL_TILE)

    def value_tile(gl, i):
        return _with_ones(slab(vs_ref, gl, pl.multiple_of(WINDOW + i * SEL_TILE, SEL_TILE), SEL_TILE))

    def before_loop(gl):
        st = state[gl]
        q = q_ref[:, gl * R * D:(gl + 1) * R * D]
        qs = jnp.concatenate([q[:, r * D:(r + 1) * D] for r in range(R)], axis=0)
        t_rows = qi * Q + (lax.broadcasted_iota(jnp.int32, (R * Q, 1), 0) & (Q - 1))
        c_idx = lax.broadcasted_iota(jnp.int32, (1, n_cp), 1)
        m_off = qi * (Q // CMP_STRIDE) - lax.broadcasted_iota(jnp.int32, (n_cp, D), 0) + CMP_BAND_LO
        k_idx = lax.broadcasted_iota(jnp.int32, (n_cp, D), 1) & (CMP_BIAS_K - 1)
        in_band = ((k_idx < CMP_BAND) & (m_off == k_idx)) | ((k_idx == CMP_BAND) & (m_off >= CMP_BAND))
        k_c = jnp.concatenate([kc_ref[gl], jnp.where(in_band, 1.0, 0.0).astype(BF16)], axis=1)
        qk_c = _dot_nt(jnp.concatenate([qs, gs_ref[gl]], axis=1), k_c)
        w0 = pl.multiple_of(qi * Q, Q)
        qk_w = _dot_nt(qs, slab(kw_ref, gl, w0, WINDOW + Q))
        yield
        m_c = (c_idx * CMP_STRIDE + (CMP_LEN - 1)) <= t_rows
        s_c = jnp.where(m_c, qk_c, NEG_INF)
        e_c = jnp.exp2(s_c - jnp.max(s_c, axis=-1, keepdims=True))
        acc_c = _dot(e_c.astype(BF16), _with_ones(vc_ref[gl]))
        yield
        inv_l = jnp.where(t_rows >= CMP_LEN - 1, 1.0 / acc_c[:, D:D + 1], 0.0)
        st["o_c"] = acc_c[:, :D] * inv_l
        p_c = e_c * inv_l
        p_sum = p_c[0:Q] + p_c[Q:2 * Q] + p_c[2 * Q:3 * Q] + p_c[3 * Q:4 * Q]
        imp = _dot_exact_rhs(p_sum, ov_ref[...])
        j_io = lax.broadcasted_iota(jnp.int32, (1, WINDOW + Q), 1)
        oldest = jnp.where(lax.broadcasted_iota(jnp.int32, (R * Q, Q), 1) > (t_rows & (Q - 1)), 0.0, NEG_INF)
        bias_w = jnp.concatenate([oldest, jnp.zeros((R * Q, WINDOW - 2 * Q), F32), bs_ref[gl, 0]], axis=1)
        s_w = jnp.where(j_io >= WINDOW - qi * Q, qk_w + bias_w, NEG_INF)
        e_w = jnp.exp2(s_w - jnp.max(s_w, axis=-1, keepdims=True))
        acc_w = _dot(e_w.astype(BF16), _with_ones(slab(vw_ref, gl, w0, WINDOW + Q)))
        yield
        st["o_w"] = acc_w[:, :D] / acc_w[:, D:]
        imp_t = imp.T
        s_row = lax.broadcasted_iota(jnp.int32, (n_sb, Q), 0)
        t_col = qi * Q + lax.broadcasted_iota(jnp.int32, (n_sb, Q), 1)
        forced = (s_row == (t_col >> 6)) | (s_row == 0)
        score = jnp.where(forced, SEL_FORCE, jnp.where(s_row * SEL_BLOCK <= t_col, imp_t, -SEL_FORCE))
        sel_t = jnp.zeros((n_sb, Q), F32)
        s_f = s_row.astype(F32)
        for _ in range(SEL_TOP):
            mx = jnp.max(score, axis=0, keepdims=True)
            first = jnp.min(jnp.where(score == mx, s_f, float(n_sb)), axis=0, keepdims=True)
            pick = s_f == first
            sel_t = jnp.where(pick, 1.0, sel_t)
            score = jnp.where(pick, -3e38, score)
        sel = sel_t.T
        blk = lax.broadcasted_iota(jnp.int32, (Q, n_sb), 1)
        neg_near = jnp.where(sel > 0.5, 0.0, NEG_INF).astype(BF16)
        neg_far = jnp.where((sel > 0.5) & (blk < 2 * n_far), 0.0, NEG_INF).astype(BF16)
        q_near = jnp.concatenate([qs, jnp.concatenate([neg_near] * R, axis=0)], axis=1)
        st["q_far"] = jnp.concatenate([qs, jnp.concatenate([neg_far] * R, axis=0)], axis=1)
        s = _dot_nt(q_near, sel_keys(gl, n0, 2 * Q)) + bs_ref[gl, jnp.where(qi == 0, 1, 0)]
        s_far = _dot_nt(st["q_far"], key_tile(gl, 0))
        s_buf[gl, 0] = s_far
        p_buf[gl, 1] = jnp.zeros(p_buf.shape[2:], p_buf.dtype)
        yield
        st["row_max"] = jnp.max(s_far, axis=-1, keepdims=True)
        st["m"] = jnp.max(s, axis=-1, keepdims=True)
        st["acc"] = _dot(jnp.exp2(s - st["m"]).astype(BF16), _with_ones(slab(vs_ref, gl, WINDOW + n0, 2 * Q)))

    _run_interleaved([before_loop(gl) for gl in range(NG)])

    def half_step(gl, i, slot, carry):
        m_run, acc, row_max = carry
        s_next = _dot_nt(state[gl]["q_far"], key_tile(gl, jnp.minimum(i + 1, max_tile)))
        s_buf[gl, 1 - slot] = s_next
        pv = _dot(p_buf[gl, 1 - slot], value_tile(gl, jnp.maximum(i - 1, 0)))
        m_new = jnp.maximum(m_run, row_max)
        acc = jnp.exp2(m_run - m_new) * (acc + pv)
        p_buf[gl, slot] = jnp.exp2(s_buf[gl, slot] - m_new).astype(p_buf.dtype)
        return m_new, acc, jnp.max(s_next, axis=-1, keepdims=True)

    def far_loop(gl):
        st = state[gl]
        pair_body = lambda j, carry: half_step(gl, 2 * j + 1, 1, half_step(gl, 2 * j, 0, carry))
        return lax.fori_loop(0, n_pairs, pair_body, (st["m"], st["acc"], st["row_max"]))

    n_pairs = (n_tiles + 1) >> 1
    carries = [far_loop(gl) for gl in range(NG)]
    last_tile = jnp.maximum(2 * n_pairs - 1, 0)
    last_pv = [_dot(p_buf[gl, 1], value_tile(gl, last_tile)) for gl in range(NG)]
    for gl, (_, acc_s, _) in enumerate(carries):
        acc_s = acc_s + last_pv[gl]
        o_s = acc_s[:, :D] / acc_s[:, D:]
        o_c, o_w = state[gl]["o_c"], state[gl]["o_w"]
        gates = gate_ref[gl]
        for r in range(R):
            rows = slice(r * Q, (r + 1) * Q)
            o = (gates[:, 3 * r:3 * r + 1] * o_c[rows] + gates[:, 3 * r + 1:3 * r + 2] * o_s[rows]
                 + gates[:, 3 * r + 2:3 * r + 3] * o_w[rows])
            o_ref[:, (gl * R + r) * D:(gl * R + r + 1) * D] = o.astype(o_ref.dtype)


def nsa_attention(q, gates, kc, vc, kv_padded, block_onehot, gs, ov, bs):
    t = q.shape[0]
    g, n_cp, d = kc.shape
    rq = NSA_REP * QBLK
    ng = NSA_STEP_GROUPS
    assert g % ng == 0
    per_g = lambda shape: pl.BlockSpec((ng,) + shape, lambda gi, qi: (gi,) + (0,) * len(shape))
    kv_part = lambda part: pl.BlockSpec((t + WINDOW, ng * d), lambda gi, qi: (0, part * g // ng + gi))
    return pl.pallas_call(
        _nsa_kernel,
        out_shape=jax.ShapeDtypeStruct((t, NSA_HEADS * d), BF16),
        grid=(g // ng, t // QBLK),
        in_specs=[pl.BlockSpec((QBLK, ng * NSA_REP * d), lambda gi, qi: (qi, gi)),
                  pl.BlockSpec((ng, QBLK, NSA_REP * N_BRANCH), lambda gi, qi: (gi, qi, 0)),
                  per_g((n_cp, d)), per_g((n_cp, d)),
                  kv_part(2), kv_part(3), kv_part(4), kv_part(5),
                  pl.BlockSpec((t, d), lambda gi, qi: (0, 0)),
                  per_g((rq, d)),
                  pl.BlockSpec(ov.shape, lambda gi, qi: (0, 0)),
                  per_g((2, rq, 2 * QBLK))],
        out_specs=pl.BlockSpec((QBLK, ng * NSA_REP * d), lambda gi, qi: (qi, gi)),
        scratch_shapes=[pltpu.VMEM((ng, 2, rq, SEL_TILE), F32), pltpu.VMEM((ng, 2, rq, SEL_TILE), BF16)],
        compiler_params=_cparams(("parallel", "arbitrary"), NSA_VMEM_LIMIT_BYTES),
        name="nsa_attention",
    )(q, gates, kc, vc, kv_padded, kv_padded, kv_padded, kv_padded, block_onehot, gs, ov, bs)


def _t5_bucket_np(n):
    n = np.maximum(n, 0)
    max_exact = REL_BUCKETS // 2
    nf = np.maximum(n, 1).astype(np.float64)
    large = max_exact + (np.log(nf / max_exact) / math.log(REL_MAX_DIST / max_exact)
                         * (REL_BUCKETS - max_exact)).astype(np.int64)
    large = np.minimum(large, REL_BUCKETS - 1)
    return np.where(n < max_exact, n, large).astype(np.int32)


def _nsa_bias_tables(rel_bias):
    G, R, Q = NSA_GROUPS, NSA_REP, QBLK
    table = (rel_bias.astype(F32) * LOG2E).reshape(REL_BUCKETS, G, R).transpose(1, 2, 0)
    qq = np.arange(Q)

    def lookup(dist):
        return table[:, :, _t5_bucket_np(dist)].reshape(G, R * Q, dist.shape[1])

    m = np.arange(CMP_BAND) - CMP_BAND_LO
    gs = lookup(qq[:, None] - (CMP_LEN - 1) + CMP_STRIDE * m[None, :])
    far = jnp.broadcast_to(table[:, :, REL_BUCKETS - 1][:, :, None, None], (G, R, Q, 1)).reshape(G, R * Q, 1)
    gs = jnp.concatenate([gs, far, jnp.zeros((G, R * Q, CMP_BIAS_K - CMP_BAND - 1), F32)], axis=-1)
    h, mid, l = _split3(gs)
    gs3 = jnp.concatenate([h, mid, l, jnp.zeros_like(h)], axis=-1)
    def toeplitz(c):
        mm = np.arange(2 * Q)
        v = table[:, :, _t5_bucket_np(np.where(mm < Q, c - mm, c + 2 * Q - mm))]
        rows = jnp.tile(v, (1, 1, Q))[:, :, :Q * (2 * Q - 1)].reshape(G, R, Q, 2 * Q - 1)
        return rows[:, :, :, :Q].reshape(G, R * Q, Q)

    d0 = np.tile(qq[:, None] - qq[None, :], (R, 1))
    causal = jnp.asarray(np.where(d0 >= 0, 0.0, NEG_INF), F32)[None]
    b0 = toeplitz(0) + causal
    b1 = toeplitz(Q)
    far_c = jnp.broadcast_to(far, (G, R * Q, Q))
    neg = jnp.full((G, R * Q, Q), NEG_INF, F32)
    bs = jnp.stack([jnp.concatenate([b1 - far_c, b0 - far_c], -1),
                    jnp.concatenate([b0 - far_c, neg], -1)], axis=1)
    return gs3, bs


def _overlap_table(n_cp, n_sb):
    c = np.arange(n_cp)[:, None] * CMP_STRIDE
    s = np.arange(n_sb)[None, :] * SEL_BLOCK
    return jnp.asarray(((c <= s + SEL_BLOCK - 1) & (c + CMP_LEN - 1 >= s)), BF16)


def _pad_cols(w, n):
    return jnp.pad(w, ((0, 0), (0, n - w.shape[1])))


def _pad_rows(w, n):
    return jnp.pad(w, ((0, n - w.shape[0]), (0, 0)))


def _rwkv_layer(h, layer, norm_g, mu, wr, wk, wv, wo, w0, w1, w2, a0, a1, a2, g1, g2,
                k_k, k_a, r_k, lnx_w, lnx_b, v_first, v0, v1, v2):
    bf = lambda w: w.astype(BF16)
    xr, xw, xk, xv, xa, xg = rwkv_prep(h, norm_g, mu)
    r = matmul(xr, wr, layer=layer)
    k = matmul(xk, wk, layer=layer)
    v = matmul(xv, wv, layer=layer)
    w_lin = lora(xw, bf(_pad_cols(w1, 128)), bf(_pad_rows(w2, 128)), act="tanh")
    a_lin = lora(xa, bf(a1), bf(a2))
    g = lora(xg, bf(g1), bf(g2), act="sigmoid")
    zeros = jnp.zeros_like(w0)
    if v_first is None:
        v_lin, v0 = None, zeros
    else:
        v_lin = lora(xv, bf(_pad_cols(v1, 128)), bf(_pad_rows(v2, 128)))
    params = jnp.stack([w0, a0, v0, k_k, k_a, r_k.reshape(-1), zeros, zeros])
    m, c, qp, y0, bonus = rwkv_chunks(r, k, v, w_lin, a_lin, v_lin, v_first, params)
    ln_params = jnp.stack([lnx_w, lnx_b] + [zeros] * 6)
    gated = rwkv_state_scan(m, c, qp, y0, bonus, g, ln_params)
    h = matmul(gated, wo, layer=layer, residual=h)
    return h, (v if v_first is None else v_first)


def _mlp(h, layer, norm_g, w_up, w_down):
    hid = norm_matmul(h, norm_g, w_up, layer=layer, act="relu2", out_dtype=BF16)
    return matmul(hid, w_down, layer=layer, residual=h, tm=2048, tn=512)


def _nsa_shared_kv(h, kv_norm_g, w_kv, cmp_pe, cmp_w1, cmp_b1, cmp_w2):
    t = h.shape[0]
    G, D = NSA_GROUPS, NSA_HD
    kv = norm_matmul(h, kv_norm_g, w_kv, out_dtype=BF16)
    x16 = kv[:, :2 * G * D].reshape(t, 2, G, D).transpose(1, 2, 0, 3).reshape(2, G, t // CMP_STRIDE, CMP_STRIDE * D)
    half = CMP_STRIDE * D
    w1cat = jnp.concatenate([cmp_w1[:, :half], cmp_w1[:, half:]], axis=-1).astype(BF16)
    cvec = jnp.einsum("plk,plkd->pd", cmp_pe, cmp_w1.reshape(2, CMP_LEN, D, D),
                      precision=lax.Precision.HIGHEST) + cmp_b1
    kvc = compress(x16, w1cat, cvec.reshape(2, 1, D), cmp_w2.astype(BF16))
    block_onehot = jnp.asarray(np.arange(t)[:, None] // SEL_BLOCK == np.arange(D)[None, :], BF16)
    return kvc[0], kvc[1], jnp.pad(kv, ((WINDOW, 0), (0, 0))), block_onehot


def _nsa_layer(h, layer, norm_g, wq, wo, kvs, tables):
    t = h.shape[0]
    nq = NSA_HEADS * NSA_HD
    q = norm_matmul(h, norm_g, wq, layer=layer, n=nq, scale=NSA_HD ** -0.5 * LOG2E, out_dtype=BF16)
    gates = norm_matmul(h, norm_g, _pad_cols(wq[layer, :, nq:], 128).astype(BF16), act="sigmoid")
    gates = gates[:, :NSA_HEADS * N_BRANCH].reshape(t, NSA_GROUPS, NSA_REP * N_BRANCH).transpose(1, 0, 2)
    o = nsa_attention(q, gates, *kvs, *tables)
    return matmul(o, wo, layer=layer, residual=h)


def kernel(x, a_norm_g, rw_mu, rw_wr, rw_wk, rw_wv, rw_wo, rw_w0, rw_w1, rw_w2, rw_a0, rw_a1, rw_a2,
           rw_v0, rw_v1, rw_v2, rw_g1, rw_g2, rw_kk, rw_ka, rw_rk, rw_lnx_w, rw_lnx_b, kv_norm_g, w_kv,
           cmp_pe, cmp_w1, cmp_b1, cmp_w2, b_norm_g, nsa_wq, nsa_wo, rel_bias, m_norm_g, mlp_up,
           mlp_down, final_norm_g):
    b, t, d = x.shape
    assert b == 1 and d == D_MODEL and t % SEL_TILE == 0 and t // SEL_BLOCK <= 128 and WINDOW == 4 * QBLK
    h = x.reshape(t, d)
    v_first = None
    kvs = tables = None
    rw_w = (rw_wr, rw_wk, rw_wv, rw_wo)
    for layer in range(DEPTH):
        if layer < N_A_LAYERS:
            i = layer
            vp = (None, None, None) if i == 0 else (rw_v0[i - 1], rw_v1[i - 1], rw_v2[i - 1])
            h, v_first = _rwkv_layer(
                h, i, a_norm_g[i], rw_mu[i], *rw_w, rw_w0[i], rw_w1[i],
                rw_w2[i], rw_a0[i], rw_a1[i], rw_a2[i], rw_g1[i], rw_g2[i], rw_kk[i], rw_ka[i],
                rw_rk[i].reshape(1, RW_HEADS, RW_HEAD), rw_lnx_w[i], rw_lnx_b[i], v_first, *vp)
        else:
            j = layer - N_A_LAYERS
            if j == 0:
                kvs = _nsa_shared_kv(h, kv_norm_g, w_kv, cmp_pe, cmp_w1, cmp_b1, cmp_w2)
                gs3, bs = _nsa_bias_tables(rel_bias)
                tables = (gs3, _overlap_table(t // CMP_STRIDE, 128), bs)
            h = _nsa_layer(h, j, b_norm_g[j], nsa_wq, nsa_wo, kvs, tables)
        h = _mlp(h, layer, m_norm_g[layer], mlp_up, mlp_down)
    return rmsnorm(h, final_norm_g, out_dtype=x.dtype).reshape(b, t, d)
```
